```python
import math
import jax
import jax.numpy as jnp
from jax import lax
import numpy as np

D_MODEL = 1024
BATCH = 8
SEQ = 2048
DEPTH = 2
DEC_BATCH = 32
DEC_SEQ = 16
PAST_LEN = 4096

CHUNK = 64
N_HEADS = 8
N_KV_HEADS = 2
HEAD_DIM = 64
GQA_GROUP = N_HEADS // N_KV_HEADS
ROT_DIM = HEAD_DIM // 4
ROPE_THETA = 500000.0
WINDOW = 128
WIN_CHUNKS = WINDOW // CHUNK
SSM_HEADS = 16
SSM_HEAD_DIM = 64
SSM_INNER = SSM_HEADS * SSM_HEAD_DIM
SSM_GROUPS = 2
SSM_STATE = 128
SSM_CHUNK = 64
CONV_WIDTH = 4
CONV_DIM = SSM_INNER + 2 * SSM_GROUPS * SSM_STATE
GM_WIDTH = 512
GM_GROUPS = 4
GM_GROUP_DIM = GM_WIDTH // GM_GROUPS
GM_CHUNK = 128
D_FF = 4 * D_MODEL
N_BRANCH = 3
Q_W = N_HEADS * HEAD_DIM
KV_W = N_KV_HEADS * HEAD_DIM
N_IN = Q_W + 2 * KV_W + SSM_INNER + CONV_DIM + SSM_HEADS + 2 * GM_WIDTH + N_BRANCH * D_MODEL
EPS = 1e-6

kernel_name = 'hybrid_streaming_encoder_step'


def rms_norm(x, g):
    xf = x.astype(jnp.float32)
    y = xf * lax.rsqrt(jnp.mean(xf * xf, axis=-1, keepdims=True) + EPS)
    return (y * g.astype(jnp.float32)).astype(x.dtype)


def layer_norm(x, g, b):
    xf = x.astype(jnp.float32)
    xc = xf - jnp.mean(xf, axis=-1, keepdims=True)
    y = xc * lax.rsqrt(jnp.mean(xc * xc, axis=-1, keepdims=True) + EPS)
    return (y * g.astype(jnp.float32) + b.astype(jnp.float32)).astype(x.dtype)


def partial_rope(x, pos):
    half = ROT_DIM // 2
    inv_freq = ROPE_THETA ** (-jnp.arange(half, dtype=jnp.float32) * (2.0 / ROT_DIM))
    ang = pos.astype(jnp.float32)[:, None] * inv_freq[None, :]
    cos = jnp.cos(ang)[None, :, None, :]
    sin = jnp.sin(ang)[None, :, None, :]
    xf = x.astype(jnp.float32)
    x1 = xf[..., :half]
    x2 = xf[..., half:ROT_DIM]
    out = jnp.concatenate([x1 * cos - x2 * sin, x2 * cos + x1 * sin, xf[..., ROT_DIM:]], axis=-1)
    return out.astype(x.dtype)


def split_projection(proj):
    sizes = (Q_W, KV_W, KV_W, SSM_INNER, CONV_DIM, SSM_HEADS, GM_WIDTH, GM_WIDTH, N_BRANCH * D_MODEL)
    idx = [int(i) for i in np.cumsum(np.array(sizes))[:-1]]
    return jnp.split(proj, idx, axis=-1)


def sink_attend(q, k, v, valid, sinks):
    s = jnp.einsum('bcqkgd,bcnkd->bckgqn', q.astype(jnp.float32), k.astype(jnp.float32)) * (HEAD_DIM ** -0.5)
    s = jnp.where(valid[:, None, None, None, :], s, -jnp.inf)
    sink = sinks.astype(jnp.float32).reshape(N_KV_HEADS, GQA_GROUP)[:, :, None]
    m = jnp.maximum(jnp.max(s, axis=-1), sink)
    p = jnp.exp(s - m[..., None])
    denom = jnp.sum(p, axis=-1) + jnp.exp(sink - m)
    o = jnp.einsum('bckgqn,bcnkd->bcqkgd', p, v.astype(jnp.float32))
    o = o / jnp.moveaxis(denom, -1, 2)[..., None]
    return o.astype(q.dtype)


def attn_prompt(q, k, v, sinks):
    bsz, s = q.shape[0], q.shape[1]
    nc = s // CHUNK
    qb = q.reshape(bsz, nc, CHUNK, N_KV_HEADS, GQA_GROUP, HEAD_DIM)

    def band(t):
        tp = jnp.pad(t, ((0, 0), (WINDOW, 0), (0, 0), (0, 0)))
        tp = tp.reshape(bsz, nc + WIN_CHUNKS, CHUNK, N_KV_HEADS, HEAD_DIM)
        return jnp.concatenate([tp[:, j:j + nc] for j in range(WIN_CHUNKS + 1)], axis=2)

    kb = band(k)
    vb = band(v)
    slot_chunk = jnp.repeat(jnp.arange(WIN_CHUNKS + 1), CHUNK)
    valid = (jnp.arange(nc)[:, None] + slot_chunk[None, :] - WIN_CHUNKS) >= 0
    return sink_attend(qb, kb, vb, valid, sinks).reshape(bsz, s, Q_W)


def attn_sample(q, k, v, cache_k, cache_v, sinks):
    bsz, s = q.shape[0], q.shape[1]
    qb = q.reshape(bsz, 1, s, N_KV_HEADS, GQA_GROUP, HEAD_DIM)
    kb = jnp.concatenate([cache_k.astype(k.dtype), k], axis=1)[:, None]
    vb = jnp.concatenate([cache_v.astype(v.dtype), v], axis=1)[:, None]
    valid = jnp.ones((1, kb.shape[2]), dtype=bool)
    return sink_attend(qb, kb, vb, valid, sinks).reshape(bsz, s, Q_W)


def causal_conv(xbc, conv_state, w, b):
    s = xbc.shape[1]
    xp = jnp.concatenate([conv_state.astype(xbc.dtype), xbc], axis=1)
    out = b
    for i in range(CONV_WIDTH):
        out = out + xp[:, i:i + s] * w[i]
    return jax.nn.silu(out), xp[:, xp.shape[1] - (CONV_WIDTH - 1):]


def ssd_scan(x, dt, a, bm, cm, h0):
    bsz, s = x.shape[0], x.shape[1]
    l = min(SSM_CHUNK, s)
    nc = s // l
    hg = SSM_HEADS // SSM_GROUPS
    xdt = (x * dt[..., None]).reshape(bsz, nc, l, SSM_GROUPS, hg, SSM_HEAD_DIM)
    cum = jnp.cumsum((dt * a).reshape(bsz, nc, l, SSM_GROUPS, hg), axis=2)
    bc = bm.reshape(bsz, nc, l, SSM_GROUPS, SSM_STATE)
    cc = cm.reshape(bsz, nc, l, SSM_GROUPS, SSM_STATE)
    causal = jnp.tril(jnp.ones((l, l), dtype=bool))[:, :, None, None]
    diff = cum[:, :, :, None] - cum[:, :, None, :]
    decay = jnp.exp(jnp.where(causal, diff, -jnp.inf))
    cb = jnp.einsum('bctgn,bcsgn->bctsg', cc, bc)
    y_intra = jnp.einsum('bctsg,bctsgk,bcsgkp->bctgkp', cb, decay, xdt)
    decay_end = jnp.exp(cum[:, :, -1:] - cum)
    chunk_state = jnp.einsum('bcsgn,bcsgk,bcsgkp->bcgkpn', bc, decay_end, xdt)
    chunk_decay = jnp.exp(cum[:, :, -1])

    def step(h, inp):
        dec, st = inp
        return dec[..., None, None] * h + st, h

    h_last, h_in = lax.scan(step, h0.reshape(bsz, SSM_GROUPS, hg, SSM_HEAD_DIM, SSM_STATE),
                            (jnp.moveaxis(chunk_decay, 1, 0), jnp.moveaxis(chunk_state, 1, 0)))
    h_in = jnp.moveaxis(h_in, 0, 1)
    y_inter = jnp.einsum('bctgn,bcgkpn->bctgkp', cc, h_in) * jnp.exp(cum)[..., None]
    y = (y_intra + y_inter).reshape(bsz, s, SSM_HEADS, SSM_HEAD_DIM)
    return y, h_last.reshape(bsz, SSM_HEADS, SSM_HEAD_DIM, SSM_STATE)


def mamba_branch(xbc, z, dt_raw, conv_state, ssm_state, p):
    xbc_act, new_conv = causal_conv(xbc, conv_state, p['conv_w'], p['conv_b'])
    bsz, s = xbc.shape[0], xbc.shape[1]
    xs, bm, cm = jnp.split(xbc_act.astype(jnp.float32), [SSM_INNER, SSM_INNER + SSM_GROUPS * SSM_STATE], axis=-1)
    xs = xs.reshape(bsz, s, SSM_HEADS, SSM_HEAD_DIM)
    dt = jax.nn.softplus(dt_raw.astype(jnp.float32) + p['dt_bias'].astype(jnp.float32))
    a = -jnp.exp(p['a_log'].astype(jnp.float32))
    y, h_last = ssd_scan(xs, dt, a,
                         bm.reshape(bsz, s, SSM_GROUPS, SSM_STATE),
                         cm.reshape(bsz, s, SSM_GROUPS, SSM_STATE),
                         ssm_state.astype(jnp.float32))
    y = y + p['d_skip'].astype(jnp.float32)[:, None] * xs
    y = rms_norm(y.reshape(bsz, s, SSM_INNER) * jax.nn.silu(z.astype(jnp.float32)), p['ssm_norm_w'])
    return y.astype(z.dtype), new_conv, h_last.astype(ssm_state.dtype)


def gmlp_branch(u, v, p):
    bsz, s = u.shape[0], u.shape[1]
    l = min(GM_CHUNK, s)
    nc = s // l
    u = jax.nn.gelu(u)
    v = layer_norm(jax.nn.gelu(v), p['gm_ln_g'], p['gm_ln_b'])
    vg = v.reshape(bsz, nc, l, GM_GROUPS, GM_GROUP_DIM)
    w = jnp.where(jnp.tril(jnp.ones((l, l), dtype=bool)), p['gm_w_s'][:, :l, :l], 0)
    mixed = jnp.einsum('gts,bcsgd->bctgd', w, vg) + jnp.transpose(p['gm_b_s'][:, :l])[:, :, None]
    return u * mixed.reshape(bsz, s, GM_WIDTH).astype(u.dtype), v


def trunk_layer(x, c, pos, conv_state, ssm_state, kv_cache, p):
    bsz, s = x.shape[0], x.shape[1]
    mod = (jax.nn.silu(c) @ p['w_ada'] + p['b_ada']).reshape(bsz, 6, D_MODEL)
    shift1, scale1, gate1, shift2, scale2, gate2 = [mod[:, i][:, None, :] for i in range(6)]
    h = rms_norm(x, p['g_mix']) * (1 + scale1) + shift1
    q, k, v, z, xbc, dt_raw, gu, gv, gates = split_projection(h @ p['w_in'])
    q = partial_rope(q.reshape(bsz, s, N_HEADS, HEAD_DIM), pos)
    k = partial_rope(k.reshape(bsz, s, N_KV_HEADS, HEAD_DIM), pos)
    v = v.reshape(bsz, s, N_KV_HEADS, HEAD_DIM)
    if kv_cache is None:
        attn = attn_prompt(q, k, v, p['sinks'])
        keep_k, keep_v = k[:, s - WINDOW:], v[:, s - WINDOW:]
    else:
        attn = attn_sample(q, k, v, kv_cache[0], kv_cache[1], p['sinks'])
        keep_k, keep_v = k, v
    ssm_out, new_conv, h_last = mamba_branch(xbc, z, dt_raw, conv_state, ssm_state, p)
    gm_out, gm_v = gmlp_branch(gu, gv, p)
    g = jax.nn.sigmoid(gates.astype(jnp.float32)).astype(x.dtype).reshape(bsz, s, N_BRANCH, D_MODEL)
    merged = (g[:, :, 0] * (attn @ p['w_attn_o'])
              + g[:, :, 1] * (ssm_out @ p['w_ssm_o'])
              + g[:, :, 2] * (gm_out @ p['w_gm_o']))
    x = x + gate1 * (merged @ p['w_out'])
    h2 = rms_norm(x, p['g_ff']) * (1 + scale2) + shift2
    x = x + gate2 * (jnp.square(jax.nn.relu(h2 @ p['w_ff1'])) @ p['w_ff2'])
    return x, keep_k, keep_v, new_conv, h_last, gm_v


def setup_inputs(seed: int = 0) -> dict:
    key = jax.random.key(seed)
    keys = jax.random.split(key, 40)
    counter = [0]

    def nxt():
        k = keys[counter[0]]
        counter[0] += 1
        return k

    def nrm(shape, scale):
        return jax.random.normal(nxt(), shape, jnp.float32) * scale

    attn_rows = min(WINDOW, PAST_LEN)
    x_prompt = nrm((BATCH, SEQ, D_MODEL), 1.0)
    x_sample = nrm((DEC_BATCH, DEC_SEQ, D_MODEL), 1.0)
    c_prompt = nrm((BATCH, D_MODEL), 1.0)
    c_sample = nrm((DEC_BATCH, D_MODEL), 1.0)
    cache_attn_k = nrm((DEPTH, DEC_BATCH, attn_rows, N_KV_HEADS, HEAD_DIM), 1.0)
    cache_attn_v = nrm((DEPTH, DEC_BATCH, attn_rows, N_KV_HEADS, HEAD_DIM), 1.0)
    state_ssm = nrm((DEPTH, DEC_BATCH, SSM_HEADS, SSM_HEAD_DIM, SSM_STATE), 0.1)
    state_conv = nrm((DEPTH, DEC_BATCH, CONV_WIDTH - 1, CONV_DIM), 1.0)
    dt0 = jnp.exp(jax.random.uniform(nxt(), (DEPTH, SSM_HEADS), jnp.float32, math.log(1e-3), math.log(1e-1)))
    a0 = jax.random.uniform(nxt(), (DEPTH, SSM_HEADS), jnp.float32, 1.0, 16.0)
    return {
        'x_prompt': x_prompt,
        'x_sample': x_sample,
        'c_prompt': c_prompt,
        'c_sample': c_sample,
        'cache_attn_k': cache_attn_k,
        'cache_attn_v': cache_attn_v,
        'state_ssm': state_ssm,
        'state_conv': state_conv,
        'w_ada': nrm((DEPTH, D_MODEL, 6 * D_MODEL), 0.5 * D_MODEL ** -0.5),
        'b_ada': nrm((DEPTH, 6 * D_MODEL), 0.01),
        'g_mix': 1.0 + nrm((DEPTH, D_MODEL), 0.05),
        'w_in': nrm((DEPTH, D_MODEL, N_IN), D_MODEL ** -0.5),
        'sinks': nrm((DEPTH, N_HEADS), 0.5),
        'conv_w': nrm((DEPTH, CONV_WIDTH, CONV_DIM), CONV_WIDTH ** -0.5),
        'conv_b': nrm((DEPTH, CONV_DIM), 0.01),
        'dt_bias': dt0 + jnp.log(-jnp.expm1(-dt0)),
        'a_log': jnp.log(a0),
        'd_skip': 1.0 + nrm((DEPTH, SSM_HEADS), 0.1),
        'ssm_norm_w': 1.0 + nrm((DEPTH, SSM_INNER), 0.05),
        'gm_ln_g': 1.0 + nrm((DEPTH, GM_WIDTH), 0.05),
        'gm_ln_b': nrm((DEPTH, GM_WIDTH), 0.01),
        'gm_w_s': nrm((DEPTH, GM_GROUPS, GM_CHUNK, GM_CHUNK), GM_CHUNK ** -0.5),
        'gm_b_s': 1.0 + nrm((DEPTH, GM_GROUPS, GM_CHUNK), 0.1),
        'w_attn_o': nrm((DEPTH, Q_W, D_MODEL), Q_W ** -0.5),
        'w_ssm_o': nrm((DEPTH, SSM_INNER, D_MODEL), SSM_INNER ** -0.5),
        'w_gm_o': nrm((DEPTH, GM_WIDTH, D_MODEL), GM_WIDTH ** -0.5),
        'w_out': nrm((DEPTH, D_MODEL, D_MODEL), D_MODEL ** -0.5),
        'g_ff': 1.0 + nrm((DEPTH, D_MODEL), 0.05),
        'w_ff1': nrm((DEPTH, D_MODEL, D_FF), D_MODEL ** -0.5),
        'w_ff2': nrm((DEPTH, D_FF, D_MODEL), D_FF ** -0.5),
        'g_final': 1.0 + nrm((D_MODEL,), 0.05),
    }


def reference(x_prompt, x_sample, c_prompt, c_sample, cache_attn_k, cache_attn_v, state_ssm, state_conv,
              w_ada, b_ada, g_mix, w_in, sinks, conv_w, conv_b, dt_bias, a_log, d_skip, ssm_norm_w,
              gm_ln_g, gm_ln_b, gm_w_s, gm_b_s, w_attn_o, w_ssm_o, w_gm_o, w_out, g_ff, w_ff1, w_ff2,
              g_final):
    bp, sp = x_prompt.shape[0], x_prompt.shape[1]
    pos_p = jnp.arange(sp)
    pos_s = PAST_LEN + jnp.arange(x_sample.shape[1])
    xp = x_prompt
    xs = x_sample
    kp_l, vp_l, sp_l, cp_l = [], [], [], []
    ks_l, vs_l, ss_l, cs_l, gs_l = [], [], [], [], []
    for l in range(DEPTH):
        p = {
            'w_ada': w_ada[l], 'b_ada': b_ada[l], 'g_mix': g_mix[l], 'w_in': w_in[l], 'sinks': sinks[l],
            'conv_w': conv_w[l], 'conv_b': conv_b[l], 'dt_bias': dt_bias[l], 'a_log': a_log[l],
            'd_skip': d_skip[l], 'ssm_norm_w': ssm_norm_w[l], 'gm_ln_g': gm_ln_g[l], 'gm_ln_b': gm_ln_b[l],
            'gm_w_s': gm_w_s[l], 'gm_b_s': gm_b_s[l], 'w_attn_o': w_attn_o[l], 'w_ssm_o': w_ssm_o[l],
            'w_gm_o': w_gm_o[l], 'w_out': w_out[l], 'g_ff': g_ff[l], 'w_ff1': w_ff1[l], 'w_ff2': w_ff2[l],
        }
        zero_conv = jnp.zeros((bp, CONV_WIDTH - 1, CONV_DIM), x_prompt.dtype)
        zero_ssm = jnp.zeros((bp, SSM_HEADS, SSM_HEAD_DIM, SSM_STATE), jnp.float32)
        xp, kp, vp, convp, ssmp, _ = trunk_layer(xp, c_prompt, pos_p, zero_conv, zero_ssm, None, p)
        xs, ksn, vsn, convs, ssms, gvs = trunk_layer(xs, c_sample, pos_s, state_conv[l], state_ssm[l],
                                                     (cache_attn_k[l], cache_attn_v[l]), p)
        kp_l.append(kp)
        vp_l.append(vp)
        sp_l.append(ssmp)
        cp_l.append(convp)
        ks_l.append(ksn)
        vs_l.append(vsn)
        ss_l.append(ssms)
        cs_l.append(convs)
        gs_l.append(gvs)
    y_prompt = rms_norm(xp, g_final)
    y_sample = rms_norm(xs, g_final)
    return (y_prompt, y_sample,
            jnp.stack(kp_l), jnp.stack(vp_l), jnp.stack(sp_l), jnp.stack(cp_l),
            jnp.stack(ks_l), jnp.stack(vs_l), jnp.stack(ss_l), jnp.stack(cs_l), jnp.stack(gs_l))
```

```python
import functools

import jax
import jax.numpy as jnp
from jax import lax
from jax.experimental import pallas as pl
from jax.experimental.pallas import tpu as pltpu

F32 = jnp.float32
BF16 = jnp.bfloat16

D_MODEL = 1024
DEPTH = 2
CHUNK = 64
N_HEADS = 8
N_KV_HEADS = 2
HEAD_DIM = 64
GQA_GROUP = N_HEADS // N_KV_HEADS
ROT_DIM = HEAD_DIM // 4
ROPE_THETA = 500000.0
WINDOW = 128
SSM_HEADS = 16
SSM_HEAD_DIM = 64
SSM_INNER = SSM_HEADS * SSM_HEAD_DIM
SSM_GROUPS = 2
SSM_STATE = 128
SSM_CHUNK = 64
CONV_WIDTH = 4
CONV_DIM = SSM_INNER + 2 * SSM_GROUPS * SSM_STATE
GM_WIDTH = 512
GM_GROUPS = 4
GM_GROUP_DIM = GM_WIDTH // GM_GROUPS
GM_CHUNK = 128
D_FF = 4 * D_MODEL
Q_W = N_HEADS * HEAD_DIM
KV_W = N_KV_HEADS * HEAD_DIM
PAST_LEN = 4096
EPS = 1e-6

LANES = 128
HEADS_PER_GROUP = SSM_HEADS // SSM_GROUPS
GROUP_INNER = HEADS_PER_GROUP * SSM_HEAD_DIM

C_Q = (0, Q_W)
C_K = (C_Q[1], C_Q[1] + KV_W)
C_V = (C_K[1], C_K[1] + KV_W)
C_Z = (C_V[1], C_V[1] + SSM_INNER)
C_XBC = (C_Z[1], C_Z[1] + CONV_DIM)
C_GU = (C_XBC[1], C_XBC[1] + GM_WIDTH)
C_GV = (C_GU[1], C_GU[1] + GM_WIDTH)
C_G0 = (C_GV[1], C_GV[1] + D_MODEL)
C_G1 = (C_G0[1], C_G0[1] + D_MODEL)
C_G2 = (C_G1[1], C_G1[1] + D_MODEL)
C_DT = (C_G2[1], C_G2[1] + LANES)
N_INP = C_DT[1]

PROMPT_TS = 256
SAMPLE_NB = 8
FFN_ROWS = 512
VMEM_LIMIT = 56 * 1024 * 1024


def _dot(a, b):
    return jnp.dot(a, b, preferred_element_type=F32)


def _dot_nt(a, b):
    return lax.dot_general(a, b, (((1,), (1,)), ((), ())), preferred_element_type=F32)


def _dot_tn(a, b):
    return lax.dot_general(a, b, (((0,), (0,)), ((), ())), preferred_element_type=F32)


def _split3(x):
    hi = x.astype(BF16)
    r = x - hi.astype(F32)
    mid = r.astype(BF16)
    lo = (r - mid.astype(F32)).astype(BF16)
    return hi, mid, lo


def _silu(x):
    return x * jax.nn.sigmoid(x)


def _gelu(x):
    return 0.5 * x * (1.0 + jnp.tanh(0.7978845608028654 * (x + 0.044715 * (x * x * x))))


def _softplus(x):
    return jnp.maximum(x, 0.0) + jnp.log1p(jnp.exp(-jnp.abs(x)))


def _rms(x, g):
    return x * lax.rsqrt(jnp.mean(x * x, axis=-1, keepdims=True) + EPS) * g


def _rope(x, cos, sin_a, sin_b):
    return x * cos + pltpu.roll(x, LANES - ROT_DIM // 2, 1) * sin_a + pltpu.roll(x, ROT_DIM // 2, 1) * sin_b


def _proj(h, win_ref, cols):
    return _dot(h, win_ref[:, cols[0]:cols[1]])


def _modnorm(x3, g, mod3):
    nb, t, d = x3.shape
    h3 = _rms(x3, g) * (1.0 + mod3[:, 1:2, :]) + mod3[:, 0:1, :]
    return h3.reshape(nb * t, d).astype(BF16)


def _sink_column(sinks_ref, kv, rows_per_head):
    r = lax.broadcasted_iota(jnp.int32, (GQA_GROUP * rows_per_head, 1), 0)
    col = jnp.full((GQA_GROUP * rows_per_head, 1), sinks_ref[kv * GQA_GROUP], F32)
    for i in range(1, GQA_GROUP):
        col = jnp.where(r >= i * rows_per_head, sinks_ref[kv * GQA_GROUP + i], col)
    return col


def _sink_softmax_pv(scores, values, sink_col):
    m = sink_col
    for s in scores:
        m = jnp.maximum(m, jnp.max(s, axis=-1, keepdims=True))
    denom = jnp.exp(sink_col - m)
    o = None
    for s, v in zip(scores, values):
        p = jnp.exp(s - m)
        denom = denom + jnp.sum(p, axis=-1, keepdims=True)
        pv = _dot(p.astype(BF16), v)
        o = pv if o is None else o + pv
    return o / denom


def _stack_heads(q, kv, rows):
    return jnp.concatenate(
        [q[:, (kv * GQA_GROUP + i) * HEAD_DIM:(kv * GQA_GROUP + i + 1) * HEAD_DIM] for i in range(GQA_GROUP)], axis=0)


def _unstack_heads(o, rows):
    return jnp.concatenate([o[i * rows:(i + 1) * rows, :] for i in range(GQA_GROUP)], axis=1)


def _ssd_chunk(r0, L, xact_ref, dt_ref, st_ref, cumt_ref, y_ref, a_row, e_ref, dskip_e):
    rows = pl.ds(r0, L)
    xs = xact_ref[rows, 0:SSM_INNER]
    bm = xact_ref[rows, SSM_INNER:SSM_INNER + SSM_GROUPS * SSM_STATE].astype(BF16)
    cm = xact_ref[rows, SSM_INNER + SSM_GROUPS * SSM_STATE:CONV_DIM].astype(BF16)
    dt = dt_ref[rows, :]
    dta = dt * a_row
    ri = lax.broadcasted_iota(jnp.int32, (L, L), 0)
    ci = lax.broadcasted_iota(jnp.int32, (L, L), 1)
    causal = ri >= ci
    tri = jnp.where(causal, 1.0, 0.0).astype(BF16)
    hi, mid, lo = _split3(dta)
    cum = _dot(tri, hi) + _dot(tri, mid) + _dot(tri, lo)
    triu = jnp.where(ri <= ci, 1.0, 0.0).astype(BF16)
    cumt_ref[...] = _dot_tn(hi, triu) + _dot_tn(mid, triu) + _dot_tn(lo, triu)
    both = jnp.concatenate([dt, cum], axis=0)
    bh, bmid, bl = _split3(both)
    e = e_ref[...]
    both_e = _dot(bh, e) + _dot(bmid, e) + _dot(bl, e)
    dt_e = both_e[0:L]
    cum_e = both_e[L:2 * L]
    last_e = cum_e[L - 1:L, :]
    xdt = xs * dt_e
    xdt_b = xdt.astype(BF16)
    y_inter = []
    for g in range(SSM_GROUPS):
        bm_g = bm[:, g * SSM_STATE:(g + 1) * SSM_STATE]
        cm_g = cm[:, g * SSM_STATE:(g + 1) * SSM_STATE]
        cb = _dot_nt(cm_g, bm_g)
        for k in range(HEADS_PER_GROUP):
            h = g * HEADS_PER_GROUP + k
            lo_l = h * SSM_HEAD_DIM
            diff = cum_e[:, lo_l:lo_l + L] - cumt_ref[h:h + 1, :]
            decay = jnp.exp(jnp.where(causal, diff, -jnp.inf))
            m_h = (cb * decay).astype(BF16)
            y_ref[rows, lo_l:lo_l + SSM_HEAD_DIM] = _dot(m_h, xdt_b[:, lo_l:lo_l + SSM_HEAD_DIM])
        st_g = st_ref[:, g * GROUP_INNER:(g + 1) * GROUP_INNER].astype(BF16)
        y_inter.append(_dot(cm_g, st_g))
    y_inter = jnp.concatenate(y_inter, axis=1)
    y_ref[rows, :] = y_ref[rows, :] + y_inter * jnp.exp(cum_e) + dskip_e * xs
    wx = (xdt * jnp.exp(last_e - cum_e)).astype(BF16)
    cdec = jnp.exp(last_e)
    for g in range(SSM_GROUPS):
        gs = slice(g * GROUP_INNER, (g + 1) * GROUP_INNER)
        bm_g = bm[:, g * SSM_STATE:(g + 1) * SSM_STATE]
        st_ref[:, gs] = st_ref[:, gs] * cdec[:, gs] + _dot_tn(bm_g, wx[:, gs])


def _gmlp_chunk(vn, u, gws_ref, gbst_ref, L):
    ri = lax.broadcasted_iota(jnp.int32, (L, L), 0)
    ci = lax.broadcasted_iota(jnp.int32, (L, L), 1)
    outs = []
    for g in range(GM_GROUPS):
        w = jnp.where(ri >= ci, gws_ref[g, 0:L, 0:L], 0.0).astype(BF16)
        v_g = vn[:, g * GM_GROUP_DIM:(g + 1) * GM_GROUP_DIM].astype(BF16)
        outs.append(_dot(w, v_g) + gbst_ref[0:L, g:g + 1])
    return u * jnp.concatenate(outs, axis=1)


def _layer_norm(x, g, b):
    xc = x - jnp.mean(x, axis=-1, keepdims=True)
    return xc * lax.rsqrt(jnp.mean(xc * xc, axis=-1, keepdims=True) + EPS) * g + b


def _merge_out(x3, mod3, h, win_ref, a, b, c, wout_ref):
    merged = jax.nn.sigmoid(_proj(h, win_ref, C_G0)) * a
    merged = merged + jax.nn.sigmoid(_proj(h, win_ref, C_G1)) * b
    merged = merged + jax.nn.sigmoid(_proj(h, win_ref, C_G2)) * c
    o = _dot(merged.astype(BF16), wout_ref[...])
    nb, t, d = x3.shape
    return x3 + mod3[:, 2:3, :] * o.reshape(nb, t, d)


def _prompt_kernel(x_ref, mod_ref, gmix_ref, win_ref, cos_ref, sa_ref, sb_ref, sinks_ref,
                   convw_ref, convb_ref, dtb_ref, alog_ref, dskip_ref, ssmnw_ref, glng_ref, glnb_ref,
                   gws_ref, gbst_ref, wao_ref, wso_ref, wgo_ref, wout_ref, e_ref,
                   xo_ref, ko_ref, vo_ref, sto_ref, cvo_ref,
                   khist, vhist, xp, xact, dt_s, cumt_s, st_s, y_s):
    ts = PROMPT_TS
    s = pl.program_id(1)
    last = pl.num_programs(1) - 1

    @pl.when(s == 0)
    def _():
        khist[0:WINDOW, :] = jnp.zeros((WINDOW, KV_W), F32)
        vhist[0:WINDOW, :] = jnp.zeros((WINDOW, KV_W), F32)
        xp[0:8, :] = jnp.zeros((8, CONV_DIM), F32)
        st_s[...] = jnp.zeros_like(st_s)

    x3 = x_ref[...]
    mod3 = mod_ref[...]
    h = _modnorm(x3, gmix_ref[...], mod3)

    cos, sa, sb = cos_ref[...], sa_ref[...], sb_ref[...]
    q = _proj(h, win_ref, C_Q) * (HEAD_DIM ** -0.5)
    q = jnp.concatenate([_rope(q[:, i * LANES:(i + 1) * LANES], cos, sa, sb) for i in range(Q_W // LANES)], axis=1)
    q = q.astype(BF16)
    khist[WINDOW:WINDOW + ts, :] = _rope(_proj(h, win_ref, C_K), cos, sa, sb)
    vhist[WINDOW:WINDOW + ts, :] = _proj(h, win_ref, C_V)
    n_keys = WINDOW + CHUNK
    attn_rows = []
    for c in range(ts // CHUNK):
        per_kv = []
        for kv in range(N_KV_HEADS):
            kk = khist[c * CHUNK:c * CHUNK + n_keys, kv * HEAD_DIM:(kv + 1) * HEAD_DIM].astype(BF16)
            vv = vhist[c * CHUNK:c * CHUNK + n_keys, kv * HEAD_DIM:(kv + 1) * HEAD_DIM].astype(BF16)
            qg = _stack_heads(q[c * CHUNK:(c + 1) * CHUNK, :], kv, CHUNK)
            sc = _dot_nt(qg, kk)
            if c * CHUNK < WINDOW:
                col = lax.broadcasted_iota(jnp.int32, sc.shape, 1)
                sc = jnp.where(col >= WINDOW - c * CHUNK - s * ts, sc, -jnp.inf)
            o = _sink_softmax_pv([sc], [vv], _sink_column(sinks_ref, kv, CHUNK))
            per_kv.append(_unstack_heads(o, CHUNK))
        attn_rows.append(jnp.concatenate(per_kv, axis=1))
    attn = jnp.concatenate(attn_rows, axis=0).astype(BF16)
    a_out = _dot(attn, wao_ref[...])

    @pl.when(s == last)
    def _():
        ko_ref[0] = khist[ts:ts + WINDOW, :]
        vo_ref[0] = vhist[ts:ts + WINDOW, :]

    khist[0:WINDOW, :] = khist[ts:ts + WINDOW, :]
    vhist[0:WINDOW, :] = vhist[ts:ts + WINDOW, :]

    xp[8:8 + ts, :] = _proj(h, win_ref, C_XBC)
    acc = convb_ref[...] + xp[8:8 + ts, :] * convw_ref[CONV_WIDTH - 1:CONV_WIDTH, :]
    for j in range(1, CONV_WIDTH):
        acc = acc + xp[8 - j:8 - j + ts, :] * convw_ref[CONV_WIDTH - 1 - j:CONV_WIDTH - j, :]
    xact[...] = _silu(acc)
    tail = xp[ts + 8 - (CONV_WIDTH - 1):ts + 8, :]

    @pl.when(s == last)
    def _():
        cvo_ref[0] = tail

    xp[8 - (CONV_WIDTH - 1):8, :] = tail
    dt_s[...] = _softplus(_proj(h, win_ref, C_DT) + dtb_ref[...])
    a_row = -jnp.exp(alog_ref[...])
    dskip_e = dskip_ref[...]

    def chunk_body(c, carry):
        _ssd_chunk(pl.multiple_of(c * SSM_CHUNK, SSM_CHUNK), SSM_CHUNK, xact, dt_s, st_s, cumt_s, y_s,
                   a_row, e_ref, dskip_e)
        return carry

    lax.fori_loop(0, ts // SSM_CHUNK, chunk_body, 0)

    @pl.when(s == last)
    def _():
        sto_ref[0] = st_s[...].T

    z = _proj(h, win_ref, C_Z)
    ssm = _rms(y_s[...] * _silu(z), ssmnw_ref[...]).astype(BF16)
    b_out = _dot(ssm, wso_ref[...])

    u = _gelu(_proj(h, win_ref, C_GU))
    vn = _layer_norm(_gelu(_proj(h, win_ref, C_GV)), glng_ref[...], glnb_ref[...])
    gm = jnp.concatenate(
        [_gmlp_chunk(vn[c * GM_CHUNK:(c + 1) * GM_CHUNK], u[c * GM_CHUNK:(c + 1) * GM_CHUNK], gws_ref, gbst_ref,
                     GM_CHUNK) for c in range(ts // GM_CHUNK)], axis=0)
    c_out = _dot(gm.astype(BF16), wgo_ref[...])

    xo_ref[...] = _merge_out(x3, mod3, h, win_ref, a_out, b_out, c_out, wout_ref)


def _sample_kernel(x_ref, mod_ref, gmix_ref, win_ref, cos_ref, sa_ref, sb_ref, sinks_ref,
                   ck_ref, cv_ref, h0_ref, cs_ref,
                   convw_ref, convb_ref, dtb_ref, alog_ref, dskip_ref, ssmnw_ref, glng_ref, glnb_ref,
                   gws_ref, gbst_ref, wao_ref, wso_ref, wgo_ref, wout_ref, e_ref,
                   xo_ref, ko_ref, vo_ref, sto_ref, cvo_ref, gvo_ref,
                   q_s, k_s, v_s, xp, xact, dt_s, cumt_s, st_s, y_s, attn_s, vn_s, u_s, gm_s):
    nb = SAMPLE_NB
    t = x_ref.shape[1]
    m = nb * t
    x3 = x_ref[...]
    mod3 = mod_ref[...]
    h = _modnorm(x3, gmix_ref[...], mod3)

    cos, sa, sb = cos_ref[...], sa_ref[...], sb_ref[...]
    q = _proj(h, win_ref, C_Q) * (HEAD_DIM ** -0.5)
    q_s[...] = jnp.concatenate(
        [_rope(q[:, i * LANES:(i + 1) * LANES], cos, sa, sb) for i in range(Q_W // LANES)], axis=1)
    k_new = _rope(_proj(h, win_ref, C_K), cos, sa, sb)
    v_new = _proj(h, win_ref, C_V)
    k_s[...] = k_new
    v_s[...] = v_new
    ko_ref[...] = k_new.reshape(nb, t, KV_W)
    vo_ref[...] = v_new.reshape(nb, t, KV_W)

    xp[:, 8 - (CONV_WIDTH - 1):8, :] = cs_ref[...]
    xp[:, 8:8 + t, :] = _proj(h, win_ref, C_XBC).reshape(nb, t, CONV_DIM)
    acc = convb_ref[...] + xp[:, 8:8 + t, :] * convw_ref[CONV_WIDTH - 1:CONV_WIDTH, :]
    for j in range(1, CONV_WIDTH):
        acc = acc + xp[:, 8 - j:8 - j + t, :] * convw_ref[CONV_WIDTH - 1 - j:CONV_WIDTH - j, :]
    xact[...] = _silu(acc).reshape(m, CONV_DIM)
    cvo_ref[...] = xp[:, t + 8 - (CONV_WIDTH - 1):t + 8, :]
    dt_s[...] = _softplus(_proj(h, win_ref, C_DT) + dtb_ref[...])
    a_row = -jnp.exp(alog_ref[...])
    dskip_e = dskip_ref[...]

    u_s[...] = _gelu(_proj(h, win_ref, C_GU))
    vn = _layer_norm(_gelu(_proj(h, win_ref, C_GV)), glng_ref[...], glnb_ref[...])
    vn_s[...] = vn
    gvo_ref[...] = vn.reshape(nb, t, GM_WIDTH)

    def seq_body(i, carry):
        r0 = pl.multiple_of(i * t, t)
        rows = pl.ds(r0, t)
        qi = q_s[rows, :].astype(BF16)
        per_kv = []
        for kv in range(N_KV_HEADS):
            hs = slice(kv * HEAD_DIM, (kv + 1) * HEAD_DIM)
            qg = _stack_heads(qi, kv, t)
            ck = ck_ref[i][:, hs].astype(BF16)
            cv = cv_ref[i][:, hs].astype(BF16)
            kn = k_s[rows, hs].astype(BF16)
            vnew = v_s[rows, hs].astype(BF16)
            o = _sink_softmax_pv([_dot_nt(qg, ck), _dot_nt(qg, kn)], [cv, vnew], _sink_column(sinks_ref, kv, t))
            per_kv.append(_unstack_heads(o, t))
        attn_s[rows, :] = jnp.concatenate(per_kv, axis=1)
        st_s[...] = h0_ref[i].T
        _ssd_chunk(r0, t, xact, dt_s, st_s, cumt_s, y_s, a_row, e_ref, dskip_e)
        sto_ref[i] = st_s[...].T
        gm_s[rows, :] = _gmlp_chunk(vn_s[rows, :], u_s[rows, :], gws_ref, gbst_ref, t)
        return carry

    lax.fori_loop(0, nb, seq_body, 0)

    a_out = _dot(attn_s[...].astype(BF16), wao_ref[...])
    z = _proj(h, win_ref, C_Z)
    ssm = _rms(y_s[...] * _silu(z), ssmnw_ref[...]).astype(BF16)
    b_out = _dot(ssm, wso_ref[...])
    c_out = _dot(gm_s[...].astype(BF16), wgo_ref[...])
    xo_ref[...] = _merge_out(x3, mod3, h, win_ref, a_out, b_out, c_out, wout_ref)


def _ffn_kernel(x_ref, mod_ref, gff_ref, w1_ref, w2_ref, gfin_ref, o_ref, *, final_norm):
    x3 = x_ref[...]
    mod3 = mod_ref[...]
    nb, t, d = x3.shape
    h3 = _rms(x3, gff_ref[...]) * (1.0 + mod3[:, 4:5, :]) + mod3[:, 3:4, :]
    h = h3.reshape(nb * t, d).astype(BF16)
    a = jnp.maximum(_dot(h, w1_ref[...]), 0.0)
    y = _dot((a * a).astype(BF16), w2_ref[...])
    out = x3 + mod3[:, 5:6, :] * y.reshape(nb, t, d)
    if final_norm:
        out = _rms(out, gfin_ref[...])
    o_ref[...] = out


def _ada_kernel(c_ref, w_ref, b_ref, o_ref):
    c = c_ref[...]
    o_ref[0] = _dot(_silu(c).astype(BF16), w_ref[0].astype(BF16)) + b_ref[0]


def _const_spec(shape):
    return pl.BlockSpec(shape, lambda *_: (0,) * len(shape), pipeline_mode=pl.Buffered(1))


def _smem_spec():
    return pl.BlockSpec(memory_space=pltpu.SMEM)


def _params(n_grid):
    return pltpu.CompilerParams(dimension_semantics=("arbitrary",) * n_grid, vmem_limit_bytes=VMEM_LIMIT)


def _ada_call(c_all, w_ada, b_ada):
    rows = c_all.shape[0]
    tn = D_MODEL
    n_tiles = w_ada.shape[2] // tn
    return pl.pallas_call(
        _ada_kernel,
        grid=(DEPTH, n_tiles),
        in_specs=[pl.BlockSpec((rows, D_MODEL), lambda l, n: (0, 0)),
                  pl.BlockSpec((1, D_MODEL, tn), lambda l, n: (l, 0, n)),
                  pl.BlockSpec((1, 1, tn), lambda l, n: (l, 0, n))],
        out_specs=pl.BlockSpec((1, rows, tn), lambda l, n: (l, 0, n)),
        out_shape=jax.ShapeDtypeStruct((DEPTH, rows, w_ada.shape[2]), F32),
        compiler_params=_params(2),
        name="ada_mod",
    )(c_all, w_ada, b_ada.reshape(DEPTH, 1, -1))


def _layer_weight_specs():
    return [
        _const_spec((CONV_WIDTH, CONV_DIM)), _const_spec((1, CONV_DIM)), _const_spec((1, LANES)),
        _const_spec((1, LANES)), _const_spec((1, SSM_INNER)), _const_spec((1, SSM_INNER)),
        _const_spec((1, GM_WIDTH)), _const_spec((1, GM_WIDTH)),
        _const_spec((GM_GROUPS, GM_CHUNK, GM_CHUNK)), _const_spec((GM_CHUNK, GM_GROUPS)),
        _const_spec((Q_W, D_MODEL)), _const_spec((SSM_INNER, D_MODEL)), _const_spec((GM_WIDTH, D_MODEL)),
        _const_spec((D_MODEL, D_MODEL)), _const_spec((LANES, SSM_INNER)),
    ]


def _layer_weight_args(p):
    return (p["conv_w"], p["conv_b"], p["dt_bias"], p["a_log"], p["d_skip"], p["ssm_norm_w"], p["gm_ln_g"],
            p["gm_ln_b"], p["gm_w_s"], p["gm_b_st"], p["w_attn_o"], p["w_ssm_o"], p["w_gm_o"], p["w_out"], p["expand"])


def _prompt_mixer(x, mod, p, rope):
    bsz, seq, d = x.shape
    ts = PROMPT_TS
    tab = pl.BlockSpec((ts, LANES), lambda b, s: (s, 0))
    in_specs = [
        pl.BlockSpec((1, ts, d), lambda b, s: (b, s, 0)),
        pl.BlockSpec((1, 6, d), lambda b, s: (b, 0, 0)),
        _const_spec((1, d)), _const_spec((d, N_INP)),
        tab, tab, tab, _smem_spec(),
    ] + _layer_weight_specs()
    out_shape = (
        jax.ShapeDtypeStruct((bsz, seq, d), F32),
        jax.ShapeDtypeStruct((bsz, WINDOW, KV_W), F32),
        jax.ShapeDtypeStruct((bsz, WINDOW, KV_W), F32),
        jax.ShapeDtypeStruct((bsz, SSM_INNER, SSM_STATE), F32),
        jax.ShapeDtypeStruct((bsz, CONV_WIDTH - 1, CONV_DIM), F32),
    )
    out_specs = (
        pl.BlockSpec((1, ts, d), lambda b, s: (b, s, 0)),
        pl.BlockSpec((1, WINDOW, KV_W), lambda b, s: (b, 0, 0)),
        pl.BlockSpec((1, WINDOW, KV_W), lambda b, s: (b, 0, 0)),
        pl.BlockSpec((1, SSM_INNER, SSM_STATE), lambda b, s: (b, 0, 0)),
        pl.BlockSpec((1, CONV_WIDTH - 1, CONV_DIM), lambda b, s: (b, 0, 0)),
    )
    scratch = [
        pltpu.VMEM((WINDOW + ts, KV_W), F32), pltpu.VMEM((WINDOW + ts, KV_W), F32),
        pltpu.VMEM((8 + ts, CONV_DIM), F32), pltpu.VMEM((ts, CONV_DIM), F32),
        pltpu.VMEM((ts, LANES), F32), pltpu.VMEM((LANES, SSM_CHUNK), F32),
        pltpu.VMEM((SSM_STATE, SSM_INNER), F32), pltpu.VMEM((ts, SSM_INNER), F32),
    ]
    return pl.pallas_call(
        _prompt_kernel, grid=(bsz, seq // ts), in_specs=in_specs, out_specs=out_specs, out_shape=out_shape,
        scratch_shapes=scratch, compiler_params=_params(2), name="prompt_mixer",
    )(x, mod, p["g_mix"], p["w_in"], *rope, p["sinks"], *_layer_weight_args(p))


def _sample_mixer(x, mod, cache_k, cache_v, h0, conv_state, p, rope):
    bsz, t, d = x.shape
    nb = SAMPLE_NB
    m = nb * t

    def blk(*tail):
        return pl.BlockSpec((nb,) + tail, lambda i: (i,) + (0,) * len(tail))

    in_specs = [
        blk(t, d), blk(6, d), _const_spec((1, d)), _const_spec((d, N_INP)),
        _const_spec((m, LANES)), _const_spec((m, LANES)), _const_spec((m, LANES)), _smem_spec(),
        blk(WINDOW, KV_W), blk(WINDOW, KV_W), blk(SSM_INNER, SSM_STATE), blk(CONV_WIDTH - 1, CONV_DIM),
    ] + _layer_weight_specs()
    out_shape = (
        jax.ShapeDtypeStruct((bsz, t, d), F32),
        jax.ShapeDtypeStruct((bsz, t, KV_W), F32),
        jax.ShapeDtypeStruct((bsz, t, KV_W), F32),
        jax.ShapeDtypeStruct((bsz, SSM_INNER, SSM_STATE), F32),
        jax.ShapeDtypeStruct((bsz, CONV_WIDTH - 1, CONV_DIM), F32),
        jax.ShapeDtypeStruct((bsz, t, GM_WIDTH), F32),
    )
    out_specs = (blk(t, d), blk(t, KV_W), blk(t, KV_W), blk(SSM_INNER, SSM_STATE),
                 blk(CONV_WIDTH - 1, CONV_DIM), blk(t, GM_WIDTH))
    scratch = [
        pltpu.VMEM((m, Q_W), F32), pltpu.VMEM((m, KV_W), F32), pltpu.VMEM((m, KV_W), F32),
        pltpu.VMEM((nb, 8 + t, CONV_DIM), F32), pltpu.VMEM((m, CONV_DIM), F32),
        pltpu.VMEM((m, LANES), F32), pltpu.VMEM((LANES, t), F32),
        pltpu.VMEM((SSM_STATE, SSM_INNER), F32), pltpu.VMEM((m, SSM_INNER), F32),
        pltpu.VMEM((m, Q_W), F32), pltpu.VMEM((m, GM_WIDTH), F32), pltpu.VMEM((m, GM_WIDTH), F32),
        pltpu.VMEM((m, GM_WIDTH), F32),
    ]
    return pl.pallas_call(
        _sample_kernel, grid=(bsz // nb,), in_specs=in_specs, out_specs=out_specs, out_shape=out_shape,
        scratch_shapes=scratch, compiler_params=_params(1), name="sample_mixer",
    )(x, mod, p["g_mix"], p["w_in"], *rope, p["sinks"], cache_k, cache_v, h0, conv_state, *_layer_weight_args(p))


def _ffn(x, mod, p, g_final, final_norm, name):
    bsz, t, d = x.shape
    if t >= FFN_ROWS:
        nb, tt = 1, FFN_ROWS
    else:
        nb, tt = FFN_ROWS // t, t
    grid = (bsz // nb, t // tt)
    return pl.pallas_call(
        functools.partial(_ffn_kernel, final_norm=final_norm),
        grid=grid,
        in_specs=[pl.BlockSpec((nb, tt, d), lambda b, s: (b, s, 0)),
                  pl.BlockSpec((nb, 6, d), lambda b, s: (b, 0, 0)),
                  _const_spec((1, d)), _const_spec((d, D_FF)), _const_spec((D_FF, d)), _const_spec((1, d))],
        out_specs=pl.BlockSpec((nb, tt, d), lambda b, s: (b, s, 0)),
        out_shape=jax.ShapeDtypeStruct((bsz, t, d), F32),
        compiler_params=_params(2), name=name,
    )(x, mod, p["g_ff"], p["w_ff1"], p["w_ff2"], g_final)


def _rope_tables(pos):
    half = ROT_DIM // 2
    inv_freq = ROPE_THETA ** (-jnp.arange(half, dtype=F32) * (2.0 / ROT_DIM))
    ang = pos.astype(F32)[:, None] * inv_freq[None, :]
    cos, sin = jnp.cos(ang), jnp.sin(ang)
    n = pos.shape[0]
    ones = jnp.ones((n, HEAD_DIM - ROT_DIM), F32)
    zeros = jnp.zeros((n, HEAD_DIM - ROT_DIM), F32)
    zh = jnp.zeros((n, half), F32)
    cos_t = jnp.concatenate([cos, cos, ones], axis=1)
    sin_a = jnp.concatenate([-sin, zh, zeros], axis=1)
    sin_b = jnp.concatenate([zh, sin, zeros], axis=1)
    rep = LANES // HEAD_DIM
    return tuple(jnp.tile(a, (1, rep)) for a in (cos_t, sin_a, sin_b))


def _layer_params(l, w_in, g_mix, sinks, conv_w, conv_b, dt_bias, a_log, d_skip, ssm_norm_w, gm_ln_g, gm_ln_b,
                  gm_w_s, gm_b_s, w_attn_o, w_ssm_o, w_gm_o, w_out, g_ff, w_ff1, w_ff2):
    dt_lo = Q_W + 2 * KV_W + SSM_INNER + CONV_DIM
    w = w_in[l]
    w_r = jnp.concatenate([w[:, :dt_lo], w[:, dt_lo + SSM_HEADS:], w[:, dt_lo:dt_lo + SSM_HEADS],
                           jnp.zeros((D_MODEL, LANES - SSM_HEADS), w.dtype)], axis=1).astype(BF16)
    pad = jnp.zeros((LANES - SSM_HEADS,), F32)
    expand = (jnp.arange(SSM_INNER)[None, :] // SSM_HEAD_DIM == jnp.arange(LANES)[:, None]).astype(BF16)
    return {
        "w_in": w_r, "g_mix": g_mix[l][None], "sinks": sinks[l],
        "conv_w": conv_w[l], "conv_b": conv_b[l][None],
        "dt_bias": jnp.concatenate([dt_bias[l], pad])[None], "a_log": jnp.concatenate([a_log[l], pad])[None],
        "d_skip": jnp.repeat(d_skip[l], SSM_HEAD_DIM)[None], "ssm_norm_w": ssm_norm_w[l][None],
        "gm_ln_g": gm_ln_g[l][None], "gm_ln_b": gm_ln_b[l][None],
        "gm_w_s": gm_w_s[l], "gm_b_st": jnp.transpose(gm_b_s[l]),
        "w_attn_o": w_attn_o[l].astype(BF16), "w_ssm_o": w_ssm_o[l].astype(BF16),
        "w_gm_o": w_gm_o[l].astype(BF16), "w_out": w_out[l].astype(BF16), "expand": expand,
        "g_ff": g_ff[l][None], "w_ff1": w_ff1[l].astype(BF16), "w_ff2": w_ff2[l].astype(BF16),
    }


def kernel(x_prompt, x_sample, c_prompt, c_sample, cache_attn_k, cache_attn_v, state_ssm, state_conv, w_ada, b_ada, g_mix, w_in, sinks, conv_w, conv_b, dt_bias, a_log, d_skip, ssm_norm_w, gm_ln_g, gm_ln_b, gm_w_s, gm_b_s, w_attn_o, w_ssm_o, w_gm_o, w_out, g_ff, w_ff1, w_ff2, g_final):
    bp, sp, d = x_prompt.shape
    bs, ss, _ = x_sample.shape
    mod = _ada_call(jnp.concatenate([c_prompt, c_sample], axis=0), w_ada, b_ada)
    rope_p = _rope_tables(jnp.arange(sp))
    rope_s = tuple(jnp.tile(a, (SAMPLE_NB, 1)) for a in _rope_tables(PAST_LEN + jnp.arange(ss)))
    g_fin = g_final[None]
    xp, xs = x_prompt, x_sample
    outs = [[] for _ in range(9)]
    for l in range(DEPTH):
        p = _layer_params(l, w_in, g_mix, sinks, conv_w, conv_b, dt_bias, a_log, d_skip, ssm_norm_w, gm_ln_g,
                          gm_ln_b, gm_w_s, gm_b_s, w_attn_o, w_ssm_o, w_gm_o, w_out, g_ff, w_ff1, w_ff2)
        mod_p = mod[l, :bp].reshape(bp, 6, d)
        mod_s = mod[l, bp:].reshape(bs, 6, d)
        final = l == DEPTH - 1
        xp, kp, vp, ssm_p, conv_p = _prompt_mixer(xp, mod_p, p, rope_p)
        xp = _ffn(xp, mod_p, p, g_fin, final, "prompt_ffn")
        ck = cache_attn_k[l].reshape(bs, WINDOW, KV_W)
        cv = cache_attn_v[l].reshape(bs, WINDOW, KV_W)
        h0 = state_ssm[l].reshape(bs, SSM_INNER, SSM_STATE)
        xs, ks, vs, ssm_s, conv_s, gv_s = _sample_mixer(xs, mod_s, ck, cv, h0, state_conv[l], p, rope_s)
        xs = _ffn(xs, mod_s, p, g_fin, final, "sample_ffn")
        for acc, val in zip(outs, (
                kp.reshape(bp, WINDOW, N_KV_HEADS, HEAD_DIM), vp.reshape(bp, WINDOW, N_KV_HEADS, HEAD_DIM),
                ssm_p.reshape(bp, SSM_HEADS, SSM_HEAD_DIM, SSM_STATE), conv_p,
                ks.reshape(bs, ss, N_KV_HEADS, HEAD_DIM), vs.reshape(bs, ss, N_KV_HEADS, HEAD_DIM),
                ssm_s.reshape(bs, SSM_HEADS, SSM_HEAD_DIM, SSM_STATE), conv_s, gv_s)):
            acc.append(val)
    return (xp, xs) + tuple(jnp.stack(o) for o in outs)
```

```python
import functools

import jax
import jax.numpy as jnp
from jax import lax
from jax.experimental import pallas as pl
from jax.experimental.pallas import tpu as pltpu

F32 = jnp.float32
BF16 = jnp.bfloat16

D_MODEL = 1024
DEPTH = 2
CHUNK = 64
N_HEADS = 8
N_KV_HEADS = 2
HEAD_DIM = 64
GQA_GROUP = N_HEADS // N_KV_HEADS
ROT_DIM = HEAD_DIM // 4
ROPE_THETA = 500000.0
WINDOW = 128
SSM_HEADS = 16
SSM_HEAD_DIM = 64
SSM_INNER = SSM_HEADS * SSM_HEAD_DIM
SSM_GROUPS = 2
SSM_STATE = 128
SSM_CHUNK = 64
CONV_WIDTH = 4
CONV_DIM = SSM_INNER + 2 * SSM_GROUPS * SSM_STATE
GM_WIDTH = 512
GM_GROUPS = 4
GM_GROUP_DIM = GM_WIDTH // GM_GROUPS
GM_CHUNK = 128
D_FF = 4 * D_MODEL
Q_W = N_HEADS * HEAD_DIM
KV_W = N_KV_HEADS * HEAD_DIM
PAST_LEN = 4096
EPS = 1e-6

LANES = 128
HEADS_PER_GROUP = SSM_HEADS // SSM_GROUPS
GROUP_INNER = HEADS_PER_GROUP * SSM_HEAD_DIM

C_Q = (0, Q_W)
C_K = (C_Q[1], C_Q[1] + KV_W)
C_V = (C_K[1], C_K[1] + KV_W)
C_Z = (C_V[1], C_V[1] + SSM_INNER)
C_XBC = (C_Z[1], C_Z[1] + CONV_DIM)
C_GU = (C_XBC[1], C_XBC[1] + GM_WIDTH)
C_GV = (C_GU[1], C_GU[1] + GM_WIDTH)
C_G0 = (C_GV[1], C_GV[1] + D_MODEL)
C_G1 = (C_G0[1], C_G0[1] + D_MODEL)
C_G2 = (C_G1[1], C_G1[1] + D_MODEL)
C_DT = (C_G2[1], C_G2[1] + LANES)
N_INP = C_DT[1]

PROMPT_TS = 256
PROJ_BLOCK = 256
SAMPLE_NB = 8
FFN_ROWS = 512
VMEM_LIMIT = 56 * 1024 * 1024


def _dot(a, b):
    return jnp.dot(a, b, preferred_element_type=F32)


def _dot_nt(a, b):
    return lax.dot_general(a, b, (((1,), (1,)), ((), ())), preferred_element_type=F32)


def _dot_tn(a, b):
    return lax.dot_general(a, b, (((0,), (0,)), ((), ())), preferred_element_type=F32)


def _split3(x):
    hi = x.astype(BF16)
    r = x - hi.astype(F32)
    mid = r.astype(BF16)
    lo = (r - mid.astype(F32)).astype(BF16)
    return hi, mid, lo


def _sigmoid(x):
    return 0.5 * (1.0 + jnp.tanh(0.5 * x))


def _silu(x):
    return x * _sigmoid(x)


def _gelu(x):
    return 0.5 * x * (1.0 + jnp.tanh(0.7978845608028654 * (x + 0.044715 * (x * x * x))))


def _softplus(x):
    return jnp.maximum(x, 0.0) + jnp.log1p(jnp.exp(-jnp.abs(x)))


def _rms(x, g):
    return x * lax.rsqrt(jnp.mean(x * x, axis=-1, keepdims=True) + EPS) * g


def _rope(x, cos, sin_a, sin_b):
    return x * cos + pltpu.roll(x, LANES - ROT_DIM // 2, 1) * sin_a + pltpu.roll(x, ROT_DIM // 2, 1) * sin_b


def _proj(h, win_ref, cols):
    return _dot(h, win_ref[:, cols[0]:cols[1]])


def _modnorm(x3, g, mod3):
    nb, t, d = x3.shape
    h3 = _rms(x3, g) * (1.0 + mod3[:, 1:2, :]) + mod3[:, 0:1, :]
    return h3.reshape(nb * t, d).astype(BF16)


def _sink_column(sinks_ref, kv, rows_per_head):
    r = lax.broadcasted_iota(jnp.int32, (GQA_GROUP * rows_per_head, 1), 0)
    col = jnp.full((GQA_GROUP * rows_per_head, 1), sinks_ref[kv * GQA_GROUP], F32)
    for i in range(1, GQA_GROUP):
        col = jnp.where(r >= i * rows_per_head, sinks_ref[kv * GQA_GROUP + i], col)
    return col


def _sink_row(sinks_ref, kv, cols_per_head):
    c = lax.broadcasted_iota(jnp.int32, (1, GQA_GROUP * cols_per_head), 1)
    row = jnp.full((1, GQA_GROUP * cols_per_head), sinks_ref[kv * GQA_GROUP], F32)
    for i in range(1, GQA_GROUP):
        row = jnp.where(c >= i * cols_per_head, sinks_ref[kv * GQA_GROUP + i], row)
    return row


def _sink_softmax_pv(scores, values, sink_col):
    m = sink_col
    for s in scores:
        m = jnp.maximum(m, jnp.max(s, axis=-1, keepdims=True))
    denom = jnp.exp(sink_col - m)
    o = None
    for s, v in zip(scores, values):
        p = jnp.exp(s - m)
        denom = denom + jnp.sum(p, axis=-1, keepdims=True)
        pv = _dot(p.astype(BF16), v)
        o = pv if o is None else o + pv
    return o / denom


def _stack_heads(q, kv, rows):
    return jnp.concatenate(
        [q[:, (kv * GQA_GROUP + i) * HEAD_DIM:(kv * GQA_GROUP + i + 1) * HEAD_DIM] for i in range(GQA_GROUP)], axis=0)


def _unstack_heads(o, rows):
    return jnp.concatenate([o[i * rows:(i + 1) * rows, :] for i in range(GQA_GROUP)], axis=1)


def _ssd_chunk(r0, L, xact_ref, dt_ref, st_ref, cumt_ref, y_ref, a_row, e_ref, dskip_e):
    rows = pl.ds(r0, L)
    xs = xact_ref[rows, 0:SSM_INNER]
    bm = xact_ref[rows, SSM_INNER:SSM_INNER + SSM_GROUPS * SSM_STATE].astype(BF16)
    cm = xact_ref[rows, SSM_INNER + SSM_GROUPS * SSM_STATE:CONV_DIM].astype(BF16)
    dt = dt_ref[rows, :]
    dta = dt * a_row
    ri = lax.broadcasted_iota(jnp.int32, (L, L), 0)
    ci = lax.broadcasted_iota(jnp.int32, (L, L), 1)
    causal = ri >= ci
    tri = jnp.where(causal, 1.0, 0.0).astype(BF16)
    hi, mid, lo = _split3(dta)
    cum = _dot(tri, hi) + _dot(tri, mid) + _dot(tri, lo)
    triu = jnp.where(ri <= ci, 1.0, 0.0).astype(BF16)
    cumt_ref[...] = _dot_tn(hi, triu) + _dot_tn(mid, triu) + _dot_tn(lo, triu)
    both = jnp.concatenate([dt, cum], axis=0)
    bh, bmid, bl = _split3(both)
    e = e_ref[...]
    both_e = _dot(bh, e) + _dot(bmid, e) + _dot(bl, e)
    dt_e = both_e[0:L]
    cum_e = both_e[L:2 * L]
    last_e = cum_e[L - 1:L, :]
    xdt = xs * dt_e
    xdt_b = xdt.astype(BF16)
    y_inter = []
    for g in range(SSM_GROUPS):
        bm_g = bm[:, g * SSM_STATE:(g + 1) * SSM_STATE]
        cm_g = cm[:, g * SSM_STATE:(g + 1) * SSM_STATE]
        cb = _dot_nt(cm_g, bm_g)
        for k in range(HEADS_PER_GROUP):
            h = g * HEADS_PER_GROUP + k
            lo_l = h * SSM_HEAD_DIM
            diff = cum_e[:, lo_l:lo_l + L] - cumt_ref[h:h + 1, :]
            decay = jnp.exp(jnp.where(causal, diff, -jnp.inf))
            m_h = (cb * decay).astype(BF16)
            y_ref[rows, lo_l:lo_l + SSM_HEAD_DIM] = _dot(m_h, xdt_b[:, lo_l:lo_l + SSM_HEAD_DIM])
        st_g = st_ref[:, g * GROUP_INNER:(g + 1) * GROUP_INNER].astype(BF16)
        y_inter.append(_dot(cm_g, st_g))
    y_inter = jnp.concatenate(y_inter, axis=1)
    y_ref[rows, :] = y_ref[rows, :] + y_inter * jnp.exp(cum_e) + dskip_e * xs
    wx = (xdt * jnp.exp(last_e - cum_e)).astype(BF16)
    cdec = jnp.exp(last_e)
    for g in range(SSM_GROUPS):
        gs = slice(g * GROUP_INNER, (g + 1) * GROUP_INNER)
        bm_g = bm[:, g * SSM_STATE:(g + 1) * SSM_STATE]
        st_ref[:, gs] = st_ref[:, gs] * cdec[:, gs] + _dot_tn(bm_g, wx[:, gs])


def _gmlp_chunk(vn, u, gws_ref, gbst_ref, L):
    ri = lax.broadcasted_iota(jnp.int32, (L, L), 0)
    ci = lax.broadcasted_iota(jnp.int32, (L, L), 1)
    outs = []
    for g in range(GM_GROUPS):
        w = jnp.where(ri >= ci, gws_ref[g, 0:L, 0:L], 0.0).astype(BF16)
        v_g = vn[:, g * GM_GROUP_DIM:(g + 1) * GM_GROUP_DIM].astype(BF16)
        outs.append(_dot(w, v_g) + gbst_ref[0:L, g:g + 1])
    return u * jnp.concatenate(outs, axis=1)


def _layer_norm(x, g, b):
    xc = x - jnp.mean(x, axis=-1, keepdims=True)
    return xc * lax.rsqrt(jnp.mean(xc * xc, axis=-1, keepdims=True) + EPS) * g + b


def _merge_out(x3, mod3, gates, a, b, c, wout_ref):
    merged = gates[0] * a + gates[1] * b + gates[2] * c
    o = _dot(merged.astype(BF16), wout_ref[...])
    nb, t, d = x3.shape
    return x3 + mod3[:, 2:3, :] * o.reshape(nb, t, d)


def _prompt_attention(q, khist, vhist, sinks_ref, pos0):
    ts = q.shape[0]
    n_keys = WINDOW + CHUNK
    groups = [(c, kv) for c in range(ts // CHUNK) for kv in range(N_KV_HEADS)]
    sink_rows = [_sink_row(sinks_ref, kv, CHUNK) for kv in range(N_KV_HEADS)]
    key_i = lax.broadcasted_iota(jnp.int32, (n_keys, GQA_GROUP * CHUNK), 0)
    scores = []
    for c, kv in groups:
        k_g = khist[c * CHUNK:c * CHUNK + n_keys, kv * HEAD_DIM:(kv + 1) * HEAD_DIM].astype(BF16)
        sc = _dot_nt(k_g, _stack_heads(q[c * CHUNK:(c + 1) * CHUNK, :], kv, CHUNK))
        if c * CHUNK < WINDOW:
            sc = jnp.where(key_i >= WINDOW - c * CHUNK - pos0, sc, -jnp.inf)
        scores.append(sc)
    probs = []
    for (c, kv), sc in zip(groups, scores):
        m = jnp.maximum(jnp.max(sc, axis=0, keepdims=True), sink_rows[kv])
        p = jnp.exp(sc - m)
        denom = jnp.sum(p, axis=0, keepdims=True) + jnp.exp(sink_rows[kv] - m)
        probs.append((p * (1.0 / denom)).astype(BF16))
    outs = []
    for (c, kv), p in zip(groups, probs):
        v_g = vhist[c * CHUNK:c * CHUNK + n_keys, kv * HEAD_DIM:(kv + 1) * HEAD_DIM].astype(BF16)
        outs.append(_unstack_heads(_dot_tn(p, v_g), CHUNK))
    rows = [jnp.concatenate(outs[c * N_KV_HEADS:(c + 1) * N_KV_HEADS], axis=1) for c in range(ts // CHUNK)]
    return jnp.concatenate(rows, axis=0)


def _prompt_ssd(xact, dt, st_ref, a_row, e, dskip_e):
    ts = xact.shape[0]
    L = SSM_CHUNK
    nc = ts // L
    xs = xact[:, 0:SSM_INNER]
    bm = xact[:, SSM_INNER:SSM_INNER + SSM_GROUPS * SSM_STATE].astype(BF16)
    cm = xact[:, SSM_INNER + SSM_GROUPS * SSM_STATE:CONV_DIM].astype(BF16)
    ri = lax.broadcasted_iota(jnp.int32, (ts, ts), 0)
    ci = lax.broadcasted_iota(jnp.int32, (ts, ts), 1)
    lag = ri - ci
    tri = jnp.where((lag >= 0) & (lag <= (ri & (L - 1))), 1.0, 0.0).astype(BF16)
    hi, mid, lo = _split3(dt * a_row)
    cum = _dot(tri, hi) + _dot(tri, mid) + _dot(tri, lo)
    hi, mid, lo = _split3(cum)
    cum_e = (_dot(hi, e) + _dot(mid, e) + _dot(lo, e)).reshape(nc, L, SSM_INNER)
    dt_hi = dt.astype(BF16)
    dt_lo = (dt - dt_hi.astype(F32)).astype(BF16)
    dt_e = _dot(dt_hi, e) + _dot(dt_lo, e)
    t3 = lax.broadcasted_iota(jnp.int32, cum_e.shape, 1)
    slot3 = lax.broadcasted_iota(jnp.int32, cum_e.shape, 2) & (SSM_HEAD_DIM - 1)
    cum_src = jnp.sum(jnp.where(t3 == slot3, cum_e, 0.0), axis=1, keepdims=True)
    decay = jnp.exp(jnp.where(t3 >= slot3, cum_e - cum_src, -jnp.inf))
    last = cum_e[:, L - 1:L, :]
    exp_cum = jnp.exp(cum_e)
    chunk_decay = jnp.exp(last)
    xdt = xs * dt_e
    wx = (xdt.reshape(nc, L, SSM_INNER) * jnp.exp(last - cum_e)).astype(BF16)
    first_head = (lax.broadcasted_iota(jnp.int32, xdt.shape, 1) & (2 * SSM_HEAD_DIM - 1)) < SSM_HEAD_DIM
    xdt_a = jnp.where(first_head, xdt, 0.0).astype(BF16)
    xdt_b = jnp.where(first_head, 0.0, xdt).astype(BF16)
    st = st_ref[...]
    ys = []
    for c in range(nc):
        rows = slice(c * L, (c + 1) * L)
        cb = jnp.concatenate(
            [_dot_nt(cm[rows, g * SSM_STATE:(g + 1) * SSM_STATE],
                     jnp.concatenate([bm[rows, g * SSM_STATE:(g + 1) * SSM_STATE]] * HEADS_PER_GROUP, axis=0))
             for g in range(SSM_GROUPS)], axis=1)
        m_full = (cb * decay[c]).astype(BF16)
        y_intra = []
        for k in range(SSM_HEADS // 2):
            ls = slice(k * 2 * SSM_HEAD_DIM, (k + 1) * 2 * SSM_HEAD_DIM)
            block_diag = jnp.concatenate([xdt_a[rows, ls], xdt_b[rows, ls]], axis=0)
            y_intra.append(_dot(m_full[:, ls], block_diag))
        st_b = st.astype(BF16)
        y_inter = jnp.concatenate(
            [_dot(cm[rows, g * SSM_STATE:(g + 1) * SSM_STATE], st_b[:, g * GROUP_INNER:(g + 1) * GROUP_INNER])
             for g in range(SSM_GROUPS)], axis=1)
        ys.append(jnp.concatenate(y_intra, axis=1) + y_inter * exp_cum[c] + dskip_e * xs[rows])
        st = st * chunk_decay[c] + jnp.concatenate(
            [_dot_tn(bm[rows, g * SSM_STATE:(g + 1) * SSM_STATE], wx[c][:, g * GROUP_INNER:(g + 1) * GROUP_INNER])
             for g in range(SSM_GROUPS)], axis=1)
    st_ref[...] = st
    return jnp.concatenate(ys, axis=0)


def _prompt_kernel(x_ref, mod_ref, gmix_ref, win_ref, cos_ref, sa_ref, sb_ref, sinks_ref,
                   convw_ref, convb_ref, dtb_ref, alog_ref, dskip_ref, ssmnw_ref, glng_ref, glnb_ref,
                   gws_ref, gbst_ref, wao_ref, wso_ref, wgo_ref, wout_ref, e_ref,
                   xo_ref, ko_ref, vo_ref, sto_ref, cvo_ref,
                   khist, vhist, xp, st_s):
    ts = PROMPT_TS
    s = pl.program_id(1)
    last = pl.num_programs(1) - 1

    @pl.when(s == 0)
    def _():
        khist[0:WINDOW, :] = jnp.zeros((WINDOW, KV_W), F32)
        vhist[0:WINDOW, :] = jnp.zeros((WINDOW, KV_W), F32)
        xp[0:8, :] = jnp.zeros((8, CONV_DIM), F32)
        st_s[...] = jnp.zeros_like(st_s)

    x3 = x_ref[...]
    mod3 = mod_ref[...]
    h = _modnorm(x3, gmix_ref[...], mod3)

    cos, sa, sb = cos_ref[...], sa_ref[...], sb_ref[...]
    res = {}

    def rope_block(raw):
        return [_rope(raw[:, i * LANES:(i + 1) * LANES], cos, sa, sb) for i in range(raw.shape[1] // LANES)]

    def q_block(raw):
        res.setdefault("q", []).extend(p.astype(BF16) for p in rope_block(raw * (HEAD_DIM ** -0.5)))

    def kv_block(raw):
        khist[WINDOW:WINDOW + ts, :] = rope_block(raw[:, 0:KV_W])[0]
        vhist[WINDOW:WINDOW + ts, :] = raw[:, KV_W:2 * KV_W]

    def conv_block(lo, raw):
        cs = slice(lo, lo + PROJ_BLOCK)
        xp[8:8 + ts, cs] = raw
        acc = convb_ref[:, cs] + raw * convw_ref[CONV_WIDTH - 1:CONV_WIDTH, cs]
        for j in range(1, CONV_WIDTH):
            acc = acc + xp[8 - j:8 - j + ts, cs] * convw_ref[CONV_WIDTH - 1 - j:CONV_WIDTH - j, cs]
        res.setdefault("xact", []).append(_silu(acc))

    def collect(key, fn):
        return lambda raw: res.setdefault(key, []).append(fn(raw))

    stages = [(C_Q[0] + i * PROJ_BLOCK, PROJ_BLOCK, q_block) for i in range(Q_W // PROJ_BLOCK)]
    stages.append((C_K[0], 2 * KV_W, kv_block))
    conv_stages = [(C_XBC[0] + lo, PROJ_BLOCK, functools.partial(conv_block, lo))
                   for lo in range(0, CONV_DIM, PROJ_BLOCK)]
    gate_stages = [(C_G0[0] + lo, PROJ_BLOCK, collect("gate", _sigmoid)) for lo in range(0, 3 * D_MODEL, PROJ_BLOCK)]
    for i, st in enumerate(conv_stages):
        stages += [st, gate_stages[i]]
    stages.append((C_DT[0], LANES, collect("dt", lambda raw: _softplus(raw + dtb_ref[...]))))
    stages += [(C_Z[0] + lo, PROJ_BLOCK, collect("sz", _silu)) for lo in range(0, SSM_INNER, PROJ_BLOCK)]
    stages += [(C_GU[0] + lo, PROJ_BLOCK, collect("u", _gelu)) for lo in range(0, GM_WIDTH, PROJ_BLOCK)]
    stages += [(C_GV[0] + lo, PROJ_BLOCK, collect("gv", _gelu)) for lo in range(0, GM_WIDTH, PROJ_BLOCK)]
    stages += gate_stages[len(conv_stages):]
    pending = None
    for lo, width, consume in stages:
        raw = _dot(h, win_ref[:, lo:lo + width])
        if pending is not None:
            pending[0](pending[1])
        pending = (consume, raw)
    pending[0](pending[1])
    q = jnp.concatenate(res["q"], axis=1)
    xact = jnp.concatenate(res["xact"], axis=1)
    dt = res["dt"][0]
    sz = jnp.concatenate(res["sz"], axis=1)
    u = jnp.concatenate(res["u"], axis=1)
    gv = jnp.concatenate(res["gv"], axis=1)
    n_gate = D_MODEL // PROJ_BLOCK
    gates = [jnp.concatenate(res["gate"][i * n_gate:(i + 1) * n_gate], axis=1) for i in range(3)]

    a_out = _dot(_prompt_attention(q, khist, vhist, sinks_ref, s * ts).astype(BF16), wao_ref[...])

    @pl.when(s == last)
    def _():
        ko_ref[0] = khist[ts:ts + WINDOW, :]
        vo_ref[0] = vhist[ts:ts + WINDOW, :]

    khist[0:WINDOW, :] = khist[ts:ts + WINDOW, :]
    vhist[0:WINDOW, :] = vhist[ts:ts + WINDOW, :]

    tail = xp[ts + 8 - (CONV_WIDTH - 1):ts + 8, :]

    @pl.when(s == last)
    def _():
        cvo_ref[0] = tail

    xp[8 - (CONV_WIDTH - 1):8, :] = tail
    y = _prompt_ssd(xact, dt, st_s, -jnp.exp(alog_ref[...]), e_ref[...], dskip_ref[...])

    @pl.when(s == last)
    def _():
        sto_ref[0] = st_s[...].T

    ssm = _rms(y * sz, ssmnw_ref[...]).astype(BF16)
    b_out = _dot(ssm, wso_ref[...])

    vn = _layer_norm(gv, glng_ref[...], glnb_ref[...])
    gm = jnp.concatenate(
        [_gmlp_chunk(vn[c * GM_CHUNK:(c + 1) * GM_CHUNK], u[c * GM_CHUNK:(c + 1) * GM_CHUNK], gws_ref, gbst_ref,
                     GM_CHUNK) for c in range(ts // GM_CHUNK)], axis=0)
    c_out = _dot(gm.astype(BF16), wgo_ref[...])

    xo_ref[...] = _merge_out(x3, mod3, gates, a_out, b_out, c_out, wout_ref)


def _sample_kernel(x_ref, mod_ref, gmix_ref, win_ref, cos_ref, sa_ref, sb_ref, sinks_ref,
                   ck_ref, cv_ref, h0_ref, cs_ref,
                   convw_ref, convb_ref, dtb_ref, alog_ref, dskip_ref, ssmnw_ref, glng_ref, glnb_ref,
                   gws_ref, gbst_ref, wao_ref, wso_ref, wgo_ref, wout_ref, e_ref,
                   xo_ref, ko_ref, vo_ref, sto_ref, cvo_ref, gvo_ref,
                   q_s, k_s, v_s, xp, xact, dt_s, cumt_s, st_s, y_s, attn_s, vn_s, u_s, gm_s):
    nb = SAMPLE_NB
    t = x_ref.shape[1]
    m = nb * t
    x3 = x_ref[...]
    mod3 = mod_ref[...]
    h = _modnorm(x3, gmix_ref[...], mod3)

    cos, sa, sb = cos_ref[...], sa_ref[...], sb_ref[...]
    q = _proj(h, win_ref, C_Q) * (HEAD_DIM ** -0.5)
    q_s[...] = jnp.concatenate(
        [_rope(q[:, i * LANES:(i + 1) * LANES], cos, sa, sb) for i in range(Q_W // LANES)], axis=1)
    k_new = _rope(_proj(h, win_ref, C_K), cos, sa, sb)
    v_new = _proj(h, win_ref, C_V)
    k_s[...] = k_new
    v_s[...] = v_new
    ko_ref[...] = k_new.reshape(nb, t, KV_W)
    vo_ref[...] = v_new.reshape(nb, t, KV_W)

    xp[:, 8 - (CONV_WIDTH - 1):8, :] = cs_ref[...]
    xp[:, 8:8 + t, :] = _proj(h, win_ref, C_XBC).reshape(nb, t, CONV_DIM)
    acc = convb_ref[...] + xp[:, 8:8 + t, :] * convw_ref[CONV_WIDTH - 1:CONV_WIDTH, :]
    for j in range(1, CONV_WIDTH):
        acc = acc + xp[:, 8 - j:8 - j + t, :] * convw_ref[CONV_WIDTH - 1 - j:CONV_WIDTH - j, :]
    xact[...] = _silu(acc).reshape(m, CONV_DIM)
    cvo_ref[...] = xp[:, t + 8 - (CONV_WIDTH - 1):t + 8, :]
    dt_s[...] = _softplus(_proj(h, win_ref, C_DT) + dtb_ref[...])
    a_row = -jnp.exp(alog_ref[...])
    dskip_e = dskip_ref[...]

    u_s[...] = _gelu(_proj(h, win_ref, C_GU))
    vn = _layer_norm(_gelu(_proj(h, win_ref, C_GV)), glng_ref[...], glnb_ref[...])
    vn_s[...] = vn
    gvo_ref[...] = vn.reshape(nb, t, GM_WIDTH)

    def seq_body(i, carry):
        r0 = pl.multiple_of(i * t, t)
        rows = pl.ds(r0, t)
        qi = q_s[rows, :].astype(BF16)
        per_kv = []
        for kv in range(N_KV_HEADS):
            hs = slice(kv * HEAD_DIM, (kv + 1) * HEAD_DIM)
            qg = _stack_heads(qi, kv, t)
            ck = ck_ref[i][:, hs].astype(BF16)
            cv = cv_ref[i][:, hs].astype(BF16)
            kn = k_s[rows, hs].astype(BF16)
            vnew = v_s[rows, hs].astype(BF16)
            o = _sink_softmax_pv([_dot_nt(qg, ck), _dot_nt(qg, kn)], [cv, vnew], _sink_column(sinks_ref, kv, t))
            per_kv.append(_unstack_heads(o, t))
        attn_s[rows, :] = jnp.concatenate(per_kv, axis=1)
        st_s[...] = h0_ref[i].T
        _ssd_chunk(r0, t, xact, dt_s, st_s, cumt_s, y_s, a_row, e_ref, dskip_e)
        sto_ref[i] = st_s[...].T
        gm_s[rows, :] = _gmlp_chunk(vn_s[rows, :], u_s[rows, :], gws_ref, gbst_ref, t)
        return carry

    lax.fori_loop(0, nb, seq_body, 0)

    a_out = _dot(attn_s[...].astype(BF16), wao_ref[...])
    z = _proj(h, win_ref, C_Z)
    ssm = _rms(y_s[...] * _silu(z), ssmnw_ref[...]).astype(BF16)
    b_out = _dot(ssm, wso_ref[...])
    c_out = _dot(gm_s[...].astype(BF16), wgo_ref[...])
    gates = [_sigmoid(_proj(h, win_ref, cols)) for cols in (C_G0, C_G1, C_G2)]
    xo_ref[...] = _merge_out(x3, mod3, gates, a_out, b_out, c_out, wout_ref)


def _ffn_kernel(x_ref, mod_ref, gff_ref, w1_ref, w2_ref, gfin_ref, o_ref, *, final_norm):
    x3 = x_ref[...]
    mod3 = mod_ref[...]
    nb, t, d = x3.shape
    h3 = _rms(x3, gff_ref[...]) * (1.0 + mod3[:, 4:5, :]) + mod3[:, 3:4, :]
    h = h3.reshape(nb * t, d).astype(BF16)
    a = jnp.maximum(_dot(h, w1_ref[...]), 0.0)
    y = _dot((a * a).astype(BF16), w2_ref[...])
    out = x3 + mod3[:, 5:6, :] * y.reshape(nb, t, d)
    if final_norm:
        out = _rms(out, gfin_ref[...])
    o_ref[...] = out


def _ada_kernel(c_ref, w_ref, b_ref, o_ref):
    c = c_ref[...]
    o_ref[0] = _dot(_silu(c).astype(BF16), w_ref[0].astype(BF16)) + b_ref[0]


def _const_spec(shape):
    return pl.BlockSpec(shape, lambda *_: (0,) * len(shape), pipeline_mode=pl.Buffered(1))


def _smem_spec():
    return pl.BlockSpec(memory_space=pltpu.SMEM)


def _params(n_grid):
    return pltpu.CompilerParams(dimension_semantics=("arbitrary",) * n_grid, vmem_limit_bytes=VMEM_LIMIT)


def _ada_call(c_all, w_ada, b_ada):
    rows = c_all.shape[0]
    tn = D_MODEL
    n_tiles = w_ada.shape[2] // tn
    return pl.pallas_call(
        _ada_kernel,
        grid=(DEPTH, n_tiles),
        in_specs=[pl.BlockSpec((rows, D_MODEL), lambda l, n: (0, 0)),
                  pl.BlockSpec((1, D_MODEL, tn), lambda l, n: (l, 0, n)),
                  pl.BlockSpec((1, 1, tn), lambda l, n: (l, 0, n))],
        out_specs=pl.BlockSpec((1, rows, tn), lambda l, n: (l, 0, n)),
        out_shape=jax.ShapeDtypeStruct((DEPTH, rows, w_ada.shape[2]), F32),
        compiler_params=_params(2),
        name="ada_mod",
    )(c_all, w_ada, b_ada.reshape(DEPTH, 1, -1))


def _layer_weight_specs():
    return [
        _const_spec((CONV_WIDTH, CONV_DIM)), _const_spec((1, CONV_DIM)), _const_spec((1, LANES)),
        _const_spec((1, LANES)), _const_spec((1, SSM_INNER)), _const_spec((1, SSM_INNER)),
        _const_spec((1, GM_WIDTH)), _const_spec((1, GM_WIDTH)),
        _const_spec((GM_GROUPS, GM_CHUNK, GM_CHUNK)), _const_spec((GM_CHUNK, GM_GROUPS)),
        _const_spec((Q_W, D_MODEL)), _const_spec((SSM_INNER, D_MODEL)), _const_spec((GM_WIDTH, D_MODEL)),
        _const_spec((D_MODEL, D_MODEL)), _const_spec((LANES, SSM_INNER)),
    ]


def _layer_weight_args(p):
    return (p["conv_w"], p["conv_b"], p["dt_bias"], p["a_log"], p["d_skip"], p["ssm_norm_w"], p["gm_ln_g"],
            p["gm_ln_b"], p["gm_w_s"], p["gm_b_st"], p["w_attn_o"], p["w_ssm_o"], p["w_gm_o"], p["w_out"], p["expand"])


def _prompt_mixer(x, mod, p, rope):
    bsz, seq, d = x.shape
    ts = PROMPT_TS
    tab = pl.BlockSpec((ts, LANES), lambda b, s: (s, 0))
    in_specs = [
        pl.BlockSpec((1, ts, d), lambda b, s: (b, s, 0)),
        pl.BlockSpec((1, 6, d), lambda b, s: (b, 0, 0)),
        _const_spec((1, d)), _const_spec((d, N_INP)),
        tab, tab, tab, _smem_spec(),
    ] + _layer_weight_specs()
    out_shape = (
        jax.ShapeDtypeStruct((bsz, seq, d), F32),
        jax.ShapeDtypeStruct((bsz, WINDOW, KV_W), F32),
        jax.ShapeDtypeStruct((bsz, WINDOW, KV_W), F32),
        jax.ShapeDtypeStruct((bsz, SSM_INNER, SSM_STATE), F32),
        jax.ShapeDtypeStruct((bsz, CONV_WIDTH - 1, CONV_DIM), F32),
    )
    out_specs = (
        pl.BlockSpec((1, ts, d), lambda b, s: (b, s, 0)),
        pl.BlockSpec((1, WINDOW, KV_W), lambda b, s: (b, 0, 0)),
        pl.BlockSpec((1, WINDOW, KV_W), lambda b, s: (b, 0, 0)),
        pl.BlockSpec((1, SSM_INNER, SSM_STATE), lambda b, s: (b, 0, 0)),
        pl.BlockSpec((1, CONV_WIDTH - 1, CONV_DIM), lambda b, s: (b, 0, 0)),
    )
    scratch = [
        pltpu.VMEM((WINDOW + ts, KV_W), F32), pltpu.VMEM((WINDOW + ts, KV_W), F32),
        pltpu.VMEM((8 + ts, CONV_DIM), F32), pltpu.VMEM((SSM_STATE, SSM_INNER), F32),
    ]
    return pl.pallas_call(
        _prompt_kernel, grid=(bsz, seq // ts), in_specs=in_specs, out_specs=out_specs, out_shape=out_shape,
        scratch_shapes=scratch, compiler_params=_params(2), name="prompt_mixer",
    )(x, mod, p["g_mix"], p["w_in"], *rope, p["sinks"], *_layer_weight_args(p))


def _sample_mixer(x, mod, cache_k, cache_v, h0, conv_state, p, rope):
    bsz, t, d = x.shape
    nb = SAMPLE_NB
    m = nb * t

    def blk(*tail):
        return pl.BlockSpec((nb,) + tail, lambda i: (i,) + (0,) * len(tail))

    in_specs = [
        blk(t, d), blk(6, d), _const_spec((1, d)), _const_spec((d, N_INP)),
        _const_spec((m, LANES)), _const_spec((m, LANES)), _const_spec((m, LANES)), _smem_spec(),
        blk(WINDOW, KV_W), blk(WINDOW, KV_W), blk(SSM_INNER, SSM_STATE), blk(CONV_WIDTH - 1, CONV_DIM),
    ] + _layer_weight_specs()
    out_shape = (
        jax.ShapeDtypeStruct((bsz, t, d), F32),
        jax.ShapeDtypeStruct((bsz, t, KV_W), F32),
        jax.ShapeDtypeStruct((bsz, t, KV_W), F32),
        jax.ShapeDtypeStruct((bsz, SSM_INNER, SSM_STATE), F32),
        jax.ShapeDtypeStruct((bsz, CONV_WIDTH - 1, CONV_DIM), F32),
        jax.ShapeDtypeStruct((bsz, t, GM_WIDTH), F32),
    )
    out_specs = (blk(t, d), blk(t, KV_W), blk(t, KV_W), blk(SSM_INNER, SSM_STATE),
                 blk(CONV_WIDTH - 1, CONV_DIM), blk(t, GM_WIDTH))
    scratch = [
        pltpu.VMEM((m, Q_W), F32), pltpu.VMEM((m, KV_W), F32), pltpu.VMEM((m, KV_W), F32),
        pltpu.VMEM((nb, 8 + t, CONV_DIM), F32), pltpu.VMEM((m, CONV_DIM), F32),
        pltpu.VMEM((m, LANES), F32), pltpu.VMEM((LANES, t), F32),
        pltpu.VMEM((SSM_STATE, SSM_INNER), F32), pltpu.VMEM((m, SSM_INNER), F32),
        pltpu.VMEM((m, Q_W), F32), pltpu.VMEM((m, GM_WIDTH), F32), pltpu.VMEM((m, GM_WIDTH), F32),
        pltpu.VMEM((m, GM_WIDTH), F32),
    ]
    return pl.pallas_call(
        _sample_kernel, grid=(bsz // nb,), in_specs=in_specs, out_specs=out_specs, out_shape=out_shape,
        scratch_shapes=scratch, compiler_params=_params(1), name="sample_mixer",
    )(x, mod, p["g_mix"], p["w_in"], *rope, p["sinks"], cache_k, cache_v, h0, conv_state, *_layer_weight_args(p))


def _ffn(x, mod, p, g_final, final_norm, name):
    bsz, t, d = x.shape
    if t >= FFN_ROWS:
        nb, tt = 1, FFN_ROWS
    else:
        nb, tt = FFN_ROWS // t, t
    grid = (bsz // nb, t // tt)
    return pl.pallas_call(
        functools.partial(_ffn_kernel, final_norm=final_norm),
        grid=grid,
        in_specs=[pl.BlockSpec((nb, tt, d), lambda b, s: (b, s, 0)),
                  pl.BlockSpec((nb, 6, d), lambda b, s: (b, 0, 0)),
                  _const_spec((1, d)), _const_spec((d, D_FF)), _const_spec((D_FF, d)), _const_spec((1, d))],
        out_specs=pl.BlockSpec((nb, tt, d), lambda b, s: (b, s, 0)),
        out_shape=jax.ShapeDtypeStruct((bsz, t, d), F32),
        compiler_params=_params(2), name=name,
    )(x, mod, p["g_ff"], p["w_ff1"], p["w_ff2"], g_final)


def _rope_tables(pos):
    half = ROT_DIM // 2
    inv_freq = ROPE_THETA ** (-jnp.arange(half, dtype=F32) * (2.0 / ROT_DIM))
    ang = pos.astype(F32)[:, None] * inv_freq[None, :]
    cos, sin = jnp.cos(ang), jnp.sin(ang)
    n = pos.shape[0]
    ones = jnp.ones((n, HEAD_DIM - ROT_DIM), F32)
    zeros = jnp.zeros((n, HEAD_DIM - ROT_DIM), F32)
    zh = jnp.zeros((n, half), F32)
    cos_t = jnp.concatenate([cos, cos, ones], axis=1)
    sin_a = jnp.concatenate([-sin, zh, zeros], axis=1)
    sin_b = jnp.concatenate([zh, sin, zeros], axis=1)
    rep = LANES // HEAD_DIM
    return tuple(jnp.tile(a, (1, rep)) for a in (cos_t, sin_a, sin_b))


def _layer_params(l, w_in, g_mix, sinks, conv_w, conv_b, dt_bias, a_log, d_skip, ssm_norm_w, gm_ln_g, gm_ln_b,
                  gm_w_s, gm_b_s, w_attn_o, w_ssm_o, w_gm_o, w_out, g_ff, w_ff1, w_ff2):
    dt_lo = Q_W + 2 * KV_W + SSM_INNER + CONV_DIM
    w = w_in[l]
    w_r = jnp.concatenate([w[:, :dt_lo], w[:, dt_lo + SSM_HEADS:], w[:, dt_lo:dt_lo + SSM_HEADS],
                           jnp.zeros((D_MODEL, LANES - SSM_HEADS), w.dtype)], axis=1).astype(BF16)
    pad = jnp.zeros((LANES - SSM_HEADS,), F32)
    expand = (jnp.arange(SSM_INNER)[None, :] // SSM_HEAD_DIM == jnp.arange(LANES)[:, None]).astype(BF16)
    return {
        "w_in": w_r, "g_mix": g_mix[l][None], "sinks": sinks[l],
        "conv_w": conv_w[l], "conv_b": conv_b[l][None],
        "dt_bias": jnp.concatenate([dt_bias[l], pad])[None], "a_log": jnp.concatenate([a_log[l], pad])[None],
        "d_skip": jnp.repeat(d_skip[l], SSM_HEAD_DIM)[None], "ssm_norm_w": ssm_norm_w[l][None],
        "gm_ln_g": gm_ln_g[l][None], "gm_ln_b": gm_ln_b[l][None],
        "gm_w_s": gm_w_s[l], "gm_b_st": jnp.transpose(gm_b_s[l]),
        "w_attn_o": w_attn_o[l].astype(BF16), "w_ssm_o": w_ssm_o[l].astype(BF16),
        "w_gm_o": w_gm_o[l].astype(BF16), "w_out": w_out[l].astype(BF16), "expand": expand,
        "g_ff": g_ff[l][None], "w_ff1": w_ff1[l].astype(BF16), "w_ff2": w_ff2[l].astype(BF16),
    }


def kernel(x_prompt, x_sample, c_prompt, c_sample, cache_attn_k, cache_attn_v, state_ssm, state_conv, w_ada, b_ada, g_mix, w_in, sinks, conv_w, conv_b, dt_bias, a_log, d_skip, ssm_norm_w, gm_ln_g, gm_ln_b, gm_w_s, gm_b_s, w_attn_o, w_ssm_o, w_gm_o, w_out, g_ff, w_ff1, w_ff2, g_final):
    bp, sp, d = x_prompt.shape
    bs, ss, _ = x_sample.shape
    mod = _ada_call(jnp.concatenate([c_prompt, c_sample], axis=0), w_ada, b_ada)
    rope_p = _rope_tables(jnp.arange(sp))
    rope_s = tuple(jnp.tile(a, (SAMPLE_NB, 1)) for a in _rope_tables(PAST_LEN + jnp.arange(ss)))
    g_fin = g_final[None]
    xp, xs = x_prompt, x_sample
    outs = [[] for _ in range(9)]
    for l in range(DEPTH):
        p = _layer_params(l, w_in, g_mix, sinks, conv_w, conv_b, dt_bias, a_log, d_skip, ssm_norm_w, gm_ln_g,
                          gm_ln_b, gm_w_s, gm_b_s, w_attn_o, w_ssm_o, w_gm_o, w_out, g_ff, w_ff1, w_ff2)
        mod_p = mod[l, :bp].reshape(bp, 6, d)
        mod_s = mod[l, bp:].reshape(bs, 6, d)
        final = l == DEPTH - 1
        xp, kp, vp, ssm_p, conv_p = _prompt_mixer(xp, mod_p, p, rope_p)
        xp = _ffn(xp, mod_p, p, g_fin, final, "prompt_ffn")
        ck = cache_attn_k[l].reshape(bs, WINDOW, KV_W)
        cv = cache_attn_v[l].reshape(bs, WINDOW, KV_W)
        h0 = state_ssm[l].reshape(bs, SSM_INNER, SSM_STATE)
        xs, ks, vs, ssm_s, conv_s, gv_s = _sample_mixer(xs, mod_s, ck, cv, h0, state_conv[l], p, rope_s)
        xs = _ffn(xs, mod_s, p, g_fin, final, "sample_ffn")
        for acc, val in zip(outs, (
                kp.reshape(bp, WINDOW, N_KV_HEADS, HEAD_DIM), vp.reshape(bp, WINDOW, N_KV_HEADS, HEAD_DIM),
                ssm_p.reshape(bp, SSM_HEADS, SSM_HEAD_DIM, SSM_STATE), conv_p,
                ks.reshape(bs, ss, N_KV_HEADS, HEAD_DIM), vs.reshape(bs, ss, N_KV_HEADS, HEAD_DIM),
                ssm_s.reshape(bs, SSM_HEADS, SSM_HEAD_DIM, SSM_STATE), conv_s, gv_s)):
            acc.append(val)
    return (xp, xs) + tuple(jnp.stack(o) for o in outs)
```

```python
import functools

import jax
import jax.numpy as jnp
from jax import lax
from jax.experimental import pallas as pl
from jax.experimental.pallas import tpu as pltpu

F32 = jnp.float32
BF16 = jnp.bfloat16

D_MODEL = 1024
DEPTH = 2
CHUNK = 64
N_HEADS = 8
N_KV_HEADS = 2
HEAD_DIM = 64
GQA_GROUP = N_HEADS // N_KV_HEADS
ROT_DIM = HEAD_DIM // 4
ROPE_THETA = 500000.0
WINDOW = 128
SSM_HEADS = 16
SSM_HEAD_DIM = 64
SSM_INNER = SSM_HEADS * SSM_HEAD_DIM
SSM_GROUPS = 2
SSM_STATE = 128
SSM_CHUNK = 64
CONV_WIDTH = 4
CONV_DIM = SSM_INNER + 2 * SSM_GROUPS * SSM_STATE
GM_WIDTH = 512
GM_GROUPS = 4
GM_GROUP_DIM = GM_WIDTH // GM_GROUPS
GM_CHUNK = 128
D_FF = 4 * D_MODEL
Q_W = N_HEADS * HEAD_DIM
KV_W = N_KV_HEADS * HEAD_DIM
PAST_LEN = 4096
EPS = 1e-6

LANES = 128
HEADS_PER_GROUP = SSM_HEADS // SSM_GROUPS
GROUP_INNER = HEADS_PER_GROUP * SSM_HEAD_DIM

C_Q = (0, Q_W)
C_K = (C_Q[1], C_Q[1] + KV_W)
C_V = (C_K[1], C_K[1] + KV_W)
C_Z = (C_V[1], C_V[1] + SSM_INNER)
C_XBC = (C_Z[1], C_Z[1] + CONV_DIM)
C_GU = (C_XBC[1], C_XBC[1] + GM_WIDTH)
C_GV = (C_GU[1], C_GU[1] + GM_WIDTH)
C_G0 = (C_GV[1], C_GV[1] + D_MODEL)
C_G1 = (C_G0[1], C_G0[1] + D_MODEL)
C_G2 = (C_G1[1], C_G1[1] + D_MODEL)
C_DT = (C_G2[1], C_G2[1] + LANES)
N_INP = C_DT[1]

PROMPT_TS = 256
PROJ_BLOCK = 256
SAMPLE_NB = 8
FFN_ROWS = 512
VMEM_LIMIT = 56 * 1024 * 1024


def _dot(a, b):
    return jnp.dot(a, b, preferred_element_type=F32)


def _dot_nt(a, b):
    return lax.dot_general(a, b, (((1,), (1,)), ((), ())), preferred_element_type=F32)


def _dot_tn(a, b):
    return lax.dot_general(a, b, (((0,), (0,)), ((), ())), preferred_element_type=F32)


def _split3(x):
    hi = x.astype(BF16)
    r = x - hi.astype(F32)
    mid = r.astype(BF16)
    lo = (r - mid.astype(F32)).astype(BF16)
    return hi, mid, lo


def _sigmoid(x):
    return 0.5 * (1.0 + jnp.tanh(0.5 * x))


def _silu(x):
    return x * _sigmoid(x)


def _gelu(x):
    return 0.5 * x * (1.0 + jnp.tanh(0.7978845608028654 * (x + 0.044715 * (x * x * x))))


def _softplus(x):
    return jnp.maximum(x, 0.0) + jnp.log1p(jnp.exp(-jnp.abs(x)))


def _rms(x, g):
    return x * lax.rsqrt(jnp.mean(x * x, axis=-1, keepdims=True) + EPS) * g


def _rope(x, cos, sin_a, sin_b):
    return x * cos + pltpu.roll(x, LANES - ROT_DIM // 2, 1) * sin_a + pltpu.roll(x, ROT_DIM // 2, 1) * sin_b


def _proj(h, win_ref, cols):
    return _dot(h, win_ref[:, cols[0]:cols[1]])


def _modnorm(x3, g, mod3):
    nb, t, d = x3.shape
    h3 = _rms(x3, g) * (1.0 + mod3[:, 1:2, :]) + mod3[:, 0:1, :]
    return h3.reshape(nb * t, d).astype(BF16)


def _sink_column(sinks_ref, kv, rows_per_head):
    r = lax.broadcasted_iota(jnp.int32, (GQA_GROUP * rows_per_head, 1), 0)
    col = jnp.full((GQA_GROUP * rows_per_head, 1), sinks_ref[kv * GQA_GROUP], F32)
    for i in range(1, GQA_GROUP):
        col = jnp.where(r >= i * rows_per_head, sinks_ref[kv * GQA_GROUP + i], col)
    return col


def _sink_row(sinks_ref, kv, cols_per_head):
    c = lax.broadcasted_iota(jnp.int32, (1, GQA_GROUP * cols_per_head), 1)
    row = jnp.full((1, GQA_GROUP * cols_per_head), sinks_ref[kv * GQA_GROUP], F32)
    for i in range(1, GQA_GROUP):
        row = jnp.where(c >= i * cols_per_head, sinks_ref[kv * GQA_GROUP + i], row)
    return row


def _sink_softmax_pv(scores, values, sink_col):
    m = sink_col
    for s in scores:
        m = jnp.maximum(m, jnp.max(s, axis=-1, keepdims=True))
    denom = jnp.exp(sink_col - m)
    o = None
    for s, v in zip(scores, values):
        p = jnp.exp(s - m)
        denom = denom + jnp.sum(p, axis=-1, keepdims=True)
        pv = _dot(p.astype(BF16), v)
        o = pv if o is None else o + pv
    return o / denom


def _stack_heads(q, kv, rows):
    return jnp.concatenate(
        [q[:, (kv * GQA_GROUP + i) * HEAD_DIM:(kv * GQA_GROUP + i + 1) * HEAD_DIM] for i in range(GQA_GROUP)], axis=0)


def _unstack_heads(o, rows):
    return jnp.concatenate([o[i * rows:(i + 1) * rows, :] for i in range(GQA_GROUP)], axis=1)


def _ssd_chunk(r0, L, xact_ref, dt_ref, st_ref, cumt_ref, y_ref, a_row, e_ref, dskip_e):
    rows = pl.ds(r0, L)
    xs = xact_ref[rows, 0:SSM_INNER]
    bm = xact_ref[rows, SSM_INNER:SSM_INNER + SSM_GROUPS * SSM_STATE].astype(BF16)
    cm = xact_ref[rows, SSM_INNER + SSM_GROUPS * SSM_STATE:CONV_DIM].astype(BF16)
    dt = dt_ref[rows, :]
    dta = dt * a_row
    ri = lax.broadcasted_iota(jnp.int32, (L, L), 0)
    ci = lax.broadcasted_iota(jnp.int32, (L, L), 1)
    causal = ri >= ci
    tri = jnp.where(causal, 1.0, 0.0).astype(BF16)
    hi, mid, lo = _split3(dta)
    cum = _dot(tri, hi) + _dot(tri, mid) + _dot(tri, lo)
    triu = jnp.where(ri <= ci, 1.0, 0.0).astype(BF16)
    cumt_ref[...] = _dot_tn(hi, triu) + _dot_tn(mid, triu) + _dot_tn(lo, triu)
    both = jnp.concatenate([dt, cum], axis=0)
    bh, bmid, bl = _split3(both)
    e = e_ref[...]
    both_e = _dot(bh, e) + _dot(bmid, e) + _dot(bl, e)
    dt_e = both_e[0:L]
    cum_e = both_e[L:2 * L]
    last_e = cum_e[L - 1:L, :]
    xdt = xs * dt_e
    xdt_b = xdt.astype(BF16)
    y_inter = []
    for g in range(SSM_GROUPS):
        bm_g = bm[:, g * SSM_STATE:(g + 1) * SSM_STATE]
        cm_g = cm[:, g * SSM_STATE:(g + 1) * SSM_STATE]
        cb = _dot_nt(cm_g, bm_g)
        for k in range(HEADS_PER_GROUP):
            h = g * HEADS_PER_GROUP + k
            lo_l = h * SSM_HEAD_DIM
            diff = cum_e[:, lo_l:lo_l + L] - cumt_ref[h:h + 1, :]
            decay = jnp.exp(jnp.where(causal, diff, -jnp.inf))
            m_h = (cb * decay).astype(BF16)
            y_ref[rows, lo_l:lo_l + SSM_HEAD_DIM] = _dot(m_h, xdt_b[:, lo_l:lo_l + SSM_HEAD_DIM])
        st_g = st_ref[:, g * GROUP_INNER:(g + 1) * GROUP_INNER].astype(BF16)
        y_inter.append(_dot(cm_g, st_g))
    y_inter = jnp.concatenate(y_inter, axis=1)
    y_ref[rows, :] = y_ref[rows, :] + y_inter * jnp.exp(cum_e) + dskip_e * xs
    wx = (xdt * jnp.exp(last_e - cum_e)).astype(BF16)
    cdec = jnp.exp(last_e)
    for g in range(SSM_GROUPS):
        gs = slice(g * GROUP_INNER, (g + 1) * GROUP_INNER)
        bm_g = bm[:, g * SSM_STATE:(g + 1) * SSM_STATE]
        st_ref[:, gs] = st_ref[:, gs] * cdec[:, gs] + _dot_tn(bm_g, wx[:, gs])


def _gmlp_chunk(vn, u, gws_ref, gbst_ref, L):
    ri = lax.broadcasted_iota(jnp.int32, (L, L), 0)
    ci = lax.broadcasted_iota(jnp.int32, (L, L), 1)
    outs = []
    for g in range(GM_GROUPS):
        w = jnp.where(ri >= ci, gws_ref[g, 0:L, 0:L], 0.0).astype(BF16)
        v_g = vn[:, g * GM_GROUP_DIM:(g + 1) * GM_GROUP_DIM].astype(BF16)
        outs.append(_dot(w, v_g) + gbst_ref[0:L, g:g + 1])
    return u * jnp.concatenate(outs, axis=1)


def _layer_norm(x, g, b):
    xc = x - jnp.mean(x, axis=-1, keepdims=True)
    return xc * lax.rsqrt(jnp.mean(xc * xc, axis=-1, keepdims=True) + EPS) * g + b


def _merge_out(x3, mod3, gates, a, b, c, wout_ref):
    merged = gates[0] * a + gates[1] * b + gates[2] * c
    o = _dot(merged.astype(BF16), wout_ref[...])
    nb, t, d = x3.shape
    return x3 + mod3[:, 2:3, :] * o.reshape(nb, t, d)


def _prompt_attention(q, khist, vhist, sinks_ref, pos0):
    ts = q.shape[0]
    n_keys = WINDOW + CHUNK
    groups = [(c, kv) for c in range(ts // CHUNK) for kv in range(N_KV_HEADS)]
    sink_rows = [_sink_row(sinks_ref, kv, CHUNK) for kv in range(N_KV_HEADS)]
    key_i = lax.broadcasted_iota(jnp.int32, (n_keys, GQA_GROUP * CHUNK), 0)
    scores = []
    for c, kv in groups:
        k_g = khist[c * CHUNK:c * CHUNK + n_keys, kv * HEAD_DIM:(kv + 1) * HEAD_DIM].astype(BF16)
        sc = _dot_nt(k_g, _stack_heads(q[c * CHUNK:(c + 1) * CHUNK, :], kv, CHUNK))
        if c * CHUNK < WINDOW:
            sc = jnp.where(key_i >= WINDOW - c * CHUNK - pos0, sc, -jnp.inf)
        scores.append(sc)
    probs = []
    for (c, kv), sc in zip(groups, scores):
        m = jnp.maximum(jnp.max(sc, axis=0, keepdims=True), sink_rows[kv])
        p = jnp.exp(sc - m)
        denom = jnp.sum(p, axis=0, keepdims=True) + jnp.exp(sink_rows[kv] - m)
        probs.append((p * (1.0 / denom)).astype(BF16))
    outs = []
    for (c, kv), p in zip(groups, probs):
        v_g = vhist[c * CHUNK:c * CHUNK + n_keys, kv * HEAD_DIM:(kv + 1) * HEAD_DIM].astype(BF16)
        outs.append(_unstack_heads(_dot_tn(p, v_g), CHUNK))
    rows = [jnp.concatenate(outs[c * N_KV_HEADS:(c + 1) * N_KV_HEADS], axis=1) for c in range(ts // CHUNK)]
    return jnp.concatenate(rows, axis=0)


def _prompt_ssd(xact, dt, st_ref, a_row, e, dskip_e):
    ts = xact.shape[0]
    L = SSM_CHUNK
    nc = ts // L
    xs = xact[:, 0:SSM_INNER]
    bm = xact[:, SSM_INNER:SSM_INNER + SSM_GROUPS * SSM_STATE].astype(BF16)
    cm = xact[:, SSM_INNER + SSM_GROUPS * SSM_STATE:CONV_DIM].astype(BF16)
    ri = lax.broadcasted_iota(jnp.int32, (ts, ts), 0)
    ci = lax.broadcasted_iota(jnp.int32, (ts, ts), 1)
    lag = ri - ci
    tri = jnp.where((lag >= 0) & (lag <= (ri & (L - 1))), 1.0, 0.0).astype(BF16)
    hi, mid, lo = _split3(dt * a_row)
    cum = _dot(tri, hi) + _dot(tri, mid) + _dot(tri, lo)
    hi, mid, lo = _split3(cum)
    cum_e = (_dot(hi, e) + _dot(mid, e) + _dot(lo, e)).reshape(nc, L, SSM_INNER)
    dt_hi = dt.astype(BF16)
    dt_lo = (dt - dt_hi.astype(F32)).astype(BF16)
    dt_e = _dot(dt_hi, e) + _dot(dt_lo, e)
    t3 = lax.broadcasted_iota(jnp.int32, cum_e.shape, 1)
    slot3 = lax.broadcasted_iota(jnp.int32, cum_e.shape, 2) & (SSM_HEAD_DIM - 1)
    cum_src = jnp.sum(jnp.where(t3 == slot3, cum_e, 0.0), axis=1, keepdims=True)
    decay = jnp.exp(jnp.where(t3 >= slot3, cum_e - cum_src, -jnp.inf))
    last = cum_e[:, L - 1:L, :]
    exp_cum = jnp.exp(cum_e)
    chunk_decay = jnp.exp(last)
    xdt = xs * dt_e
    wx = (xdt.reshape(nc, L, SSM_INNER) * jnp.exp(last - cum_e)).astype(BF16)
    first_head = (lax.broadcasted_iota(jnp.int32, xdt.shape, 1) & (2 * SSM_HEAD_DIM - 1)) < SSM_HEAD_DIM
    xdt_a = jnp.where(first_head, xdt, 0.0).astype(BF16)
    xdt_b = jnp.where(first_head, 0.0, xdt).astype(BF16)
    st = st_ref[...]
    ys = []
    for c in range(nc):
        rows = slice(c * L, (c + 1) * L)
        cb = jnp.concatenate(
            [_dot_nt(cm[rows, g * SSM_STATE:(g + 1) * SSM_STATE],
                     jnp.concatenate([bm[rows, g * SSM_STATE:(g + 1) * SSM_STATE]] * HEADS_PER_GROUP, axis=0))
             for g in range(SSM_GROUPS)], axis=1)
        m_full = (cb * decay[c]).astype(BF16)
        y_intra = []
        for k in range(SSM_HEADS // 2):
            ls = slice(k * 2 * SSM_HEAD_DIM, (k + 1) * 2 * SSM_HEAD_DIM)
            block_diag = jnp.concatenate([xdt_a[rows, ls], xdt_b[rows, ls]], axis=0)
            y_intra.append(_dot(m_full[:, ls], block_diag))
        st_b = st.astype(BF16)
        y_inter = jnp.concatenate(
            [_dot(cm[rows, g * SSM_STATE:(g + 1) * SSM_STATE], st_b[:, g * GROUP_INNER:(g + 1) * GROUP_INNER])
             for g in range(SSM_GROUPS)], axis=1)
        ys.append(jnp.concatenate(y_intra, axis=1) + y_inter * exp_cum[c] + dskip_e * xs[rows])
        st = st * chunk_decay[c] + jnp.concatenate(
            [_dot_tn(bm[rows, g * SSM_STATE:(g + 1) * SSM_STATE], wx[c][:, g * GROUP_INNER:(g + 1) * GROUP_INNER])
             for g in range(SSM_GROUPS)], axis=1)
    st_ref[...] = st
    return jnp.concatenate(ys, axis=0)


def _prompt_kernel(x_ref, mod_ref, gmix_ref, win_ref, cos_ref, sa_ref, sb_ref, sinks_ref,
                   convw_ref, convb_ref, dtb_ref, alog_ref, dskip_ref, ssmnw_ref, glng_ref, glnb_ref,
                   gws_ref, gbst_ref, wao_ref, wso_ref, wgo_ref, wout_ref, e_ref,
                   xo_ref, ko_ref, vo_ref, sto_ref, cvo_ref,
                   khist, vhist, xp, st_s):
    ts = PROMPT_TS
    s = pl.program_id(1)
    last = pl.num_programs(1) - 1

    @pl.when(s == 0)
    def _():
        khist[0:WINDOW, :] = jnp.zeros((WINDOW, KV_W), F32)
        vhist[0:WINDOW, :] = jnp.zeros((WINDOW, KV_W), F32)
        xp[0:8, :] = jnp.zeros((8, CONV_DIM), F32)
        st_s[...] = jnp.zeros_like(st_s)

    x3 = x_ref[...]
    mod3 = mod_ref[...]
    h = _modnorm(x3, gmix_ref[...], mod3)

    cos, sa, sb = cos_ref[...], sa_ref[...], sb_ref[...]
    res = {}

    def rope_block(raw):
        return [_rope(raw[:, i * LANES:(i + 1) * LANES], cos, sa, sb) for i in range(raw.shape[1] // LANES)]

    def q_block(raw):
        res.setdefault("q", []).extend(p.astype(BF16) for p in rope_block(raw * (HEAD_DIM ** -0.5)))

    def kv_block(raw):
        khist[WINDOW:WINDOW + ts, :] = rope_block(raw[:, 0:KV_W])[0]
        vhist[WINDOW:WINDOW + ts, :] = raw[:, KV_W:2 * KV_W]

    def conv_block(lo, raw):
        cs = slice(lo, lo + PROJ_BLOCK)
        xp[8:8 + ts, cs] = raw
        acc = convb_ref[:, cs] + raw * convw_ref[CONV_WIDTH - 1:CONV_WIDTH, cs]
        for j in range(1, CONV_WIDTH):
            acc = acc + xp[8 - j:8 - j + ts, cs] * convw_ref[CONV_WIDTH - 1 - j:CONV_WIDTH - j, cs]
        res.setdefault("xact", []).append(_silu(acc))

    def collect(key, fn):
        return lambda raw: res.setdefault(key, []).append(fn(raw))

    stages = [(C_Q[0] + i * PROJ_BLOCK, PROJ_BLOCK, q_block) for i in range(Q_W // PROJ_BLOCK)]
    stages.append((C_K[0], 2 * KV_W, kv_block))
    conv_stages = [(C_XBC[0] + lo, PROJ_BLOCK, functools.partial(conv_block, lo))
                   for lo in range(0, CONV_DIM, PROJ_BLOCK)]
    gate_stages = [(C_G0[0] + lo, PROJ_BLOCK, collect("gate", _sigmoid)) for lo in range(0, 3 * D_MODEL, PROJ_BLOCK)]
    for i, st in enumerate(conv_stages):
        stages += [st, gate_stages[i]]
    stages.append((C_DT[0], LANES, collect("dt", lambda raw: _softplus(raw + dtb_ref[...]))))
    stages += [(C_Z[0] + lo, PROJ_BLOCK, collect("sz", _silu)) for lo in range(0, SSM_INNER, PROJ_BLOCK)]
    stages += [(C_GU[0] + lo, PROJ_BLOCK, collect("u", _gelu)) for lo in range(0, GM_WIDTH, PROJ_BLOCK)]
    stages += [(C_GV[0] + lo, PROJ_BLOCK, collect("gv", _gelu)) for lo in range(0, GM_WIDTH, PROJ_BLOCK)]
    stages += gate_stages[len(conv_stages):]
    pending = None
    for lo, width, consume in stages:
        raw = _dot(h, win_ref[:, lo:lo + width])
        if pending is not None:
            pending[0](pending[1])
        pending = (consume, raw)
    pending[0](pending[1])
    q = jnp.concatenate(res["q"], axis=1)
    xact = jnp.concatenate(res["xact"], axis=1)
    dt = res["dt"][0]
    sz = jnp.concatenate(res["sz"], axis=1)
    u = jnp.concatenate(res["u"], axis=1)
    gv = jnp.concatenate(res["gv"], axis=1)
    n_gate = D_MODEL // PROJ_BLOCK
    gates = [jnp.concatenate(res["gate"][i * n_gate:(i + 1) * n_gate], axis=1) for i in range(3)]

    a_out = _dot(_prompt_attention(q, khist, vhist, sinks_ref, s * ts).astype(BF16), wao_ref[...])

    @pl.when(s == last)
    def _():
        ko_ref[0] = khist[ts:ts + WINDOW, :]
        vo_ref[0] = vhist[ts:ts + WINDOW, :]

    khist[0:WINDOW, :] = khist[ts:ts + WINDOW, :]
    vhist[0:WINDOW, :] = vhist[ts:ts + WINDOW, :]

    tail = xp[ts + 8 - (CONV_WIDTH - 1):ts + 8, :]

    @pl.when(s == last)
    def _():
        cvo_ref[0] = tail

    xp[8 - (CONV_WIDTH - 1):8, :] = tail
    y = _prompt_ssd(xact, dt, st_s, -jnp.exp(alog_ref[...]), e_ref[...], dskip_ref[...])

    @pl.when(s == last)
    def _():
        sto_ref[0] = st_s[...].T

    ssm = _rms(y * sz, ssmnw_ref[...]).astype(BF16)
    b_out = _dot(ssm, wso_ref[...])

    vn = _layer_norm(gv, glng_ref[...], glnb_ref[...])
    gm = jnp.concatenate(
        [_gmlp_chunk(vn[c * GM_CHUNK:(c + 1) * GM_CHUNK], u[c * GM_CHUNK:(c + 1) * GM_CHUNK], gws_ref, gbst_ref,
                     GM_CHUNK) for c in range(ts // GM_CHUNK)], axis=0)
    c_out = _dot(gm.astype(BF16), wgo_ref[...])

    xo_ref[...] = _merge_out(x3, mod3, gates, a_out, b_out, c_out, wout_ref)


def _sample_kernel(x_ref, mod_ref, gmix_ref, win_ref, cos_ref, sa_ref, sb_ref, sinks_ref,
                   ck_ref, cv_ref, h0_ref, cs_ref,
                   convw_ref, convb_ref, dtb_ref, alog_ref, dskip_ref, ssmnw_ref, glng_ref, glnb_ref,
                   gws_ref, gbst_ref, wao_ref, wso_ref, wgo_ref, wout_ref, e_ref,
                   xo_ref, ko_ref, vo_ref, sto_ref, cvo_ref, gvo_ref,
                   q_s, k_s, v_s, xp, xact, dt_s, cumt_s, st_s, y_s, attn_s, vn_s, u_s, gm_s):
    nb = SAMPLE_NB
    t = x_ref.shape[1]
    m = nb * t
    x3 = x_ref[...]
    mod3 = mod_ref[...]
    h = _modnorm(x3, gmix_ref[...], mod3)

    cos, sa, sb = cos_ref[...], sa_ref[...], sb_ref[...]
    q = _proj(h, win_ref, C_Q) * (HEAD_DIM ** -0.5)
    q_s[...] = jnp.concatenate(
        [_rope(q[:, i * LANES:(i + 1) * LANES], cos, sa, sb) for i in range(Q_W // LANES)], axis=1)
    k_new = _rope(_proj(h, win_ref, C_K), cos, sa, sb)
    v_new = _proj(h, win_ref, C_V)
    k_s[...] = k_new
    v_s[...] = v_new
    ko_ref[...] = k_new.reshape(nb, t, KV_W)
    vo_ref[...] = v_new.reshape(nb, t, KV_W)

    xp[:, 8 - (CONV_WIDTH - 1):8, :] = cs_ref[...]
    xp[:, 8:8 + t, :] = _proj(h, win_ref, C_XBC).reshape(nb, t, CONV_DIM)
    acc = convb_ref[...] + xp[:, 8:8 + t, :] * convw_ref[CONV_WIDTH - 1:CONV_WIDTH, :]
    for j in range(1, CONV_WIDTH):
        acc = acc + xp[:, 8 - j:8 - j + t, :] * convw_ref[CONV_WIDTH - 1 - j:CONV_WIDTH - j, :]
    xact[...] = _silu(acc).reshape(m, CONV_DIM)
    cvo_ref[...] = xp[:, t + 8 - (CONV_WIDTH - 1):t + 8, :]
    dt_s[...] = _softplus(_proj(h, win_ref, C_DT) + dtb_ref[...])
    a_row = -jnp.exp(alog_ref[...])
    dskip_e = dskip_ref[...]

    u_s[...] = _gelu(_proj(h, win_ref, C_GU))
    vn = _layer_norm(_gelu(_proj(h, win_ref, C_GV)), glng_ref[...], glnb_ref[...])
    vn_s[...] = vn
    gvo_ref[...] = vn.reshape(nb, t, GM_WIDTH)

    def seq_body(i, carry):
        r0 = pl.multiple_of(i * t, t)
        rows = pl.ds(r0, t)
        qi = q_s[rows, :].astype(BF16)
        per_kv = []
        for kv in range(N_KV_HEADS):
            hs = slice(kv * HEAD_DIM, (kv + 1) * HEAD_DIM)
            qg = _stack_heads(qi, kv, t)
            ck = ck_ref[i][:, hs].astype(BF16)
            cv = cv_ref[i][:, hs].astype(BF16)
            kn = k_s[rows, hs].astype(BF16)
            vnew = v_s[rows, hs].astype(BF16)
            o = _sink_softmax_pv([_dot_nt(qg, ck), _dot_nt(qg, kn)], [cv, vnew], _sink_column(sinks_ref, kv, t))
            per_kv.append(_unstack_heads(o, t))
        attn_s[rows, :] = jnp.concatenate(per_kv, axis=1)
        st_s[...] = h0_ref[i].T
        _ssd_chunk(r0, t, xact, dt_s, st_s, cumt_s, y_s, a_row, e_ref, dskip_e)
        sto_ref[i] = st_s[...].T
        gm_s[rows, :] = _gmlp_chunk(vn_s[rows, :], u_s[rows, :], gws_ref, gbst_ref, t)
        return carry

    lax.fori_loop(0, nb, seq_body, 0)

    a_out = _dot(attn_s[...].astype(BF16), wao_ref[...])
    z = _proj(h, win_ref, C_Z)
    ssm = _rms(y_s[...] * _silu(z), ssmnw_ref[...]).astype(BF16)
    b_out = _dot(ssm, wso_ref[...])
    c_out = _dot(gm_s[...].astype(BF16), wgo_ref[...])
    gates = [_sigmoid(_proj(h, win_ref, cols)) for cols in (C_G0, C_G1, C_G2)]
    xo_ref[...] = _merge_out(x3, mod3, gates, a_out, b_out, c_out, wout_ref)


def _ffn_kernel(x_ref, mod_ref, gff_ref, w1_ref, w2_ref, gfin_ref, o_ref, *, final_norm):
    x3 = x_ref[...]
    mod3 = mod_ref[...]
    nb, t, d = x3.shape
    h3 = _rms(x3, gff_ref[...]) * (1.0 + mod3[:, 4:5, :]) + mod3[:, 3:4, :]
    h = h3.reshape(nb * t, d).astype(BF16)
    a = jnp.maximum(_dot(h, w1_ref[...]), 0.0)
    y = _dot((a * a).astype(BF16), w2_ref[...])
    out = x3 + mod3[:, 5:6, :] * y.reshape(nb, t, d)
    if final_norm:
        out = _rms(out, gfin_ref[...])
    o_ref[...] = out


def _ada_kernel(c_ref, w_ref, b_ref, op_ref, os_ref):
    mod = _dot(_silu(c_ref[...]).astype(BF16), w_ref[...].astype(BF16)) + b_ref[...]
    n_prompt = op_ref.shape[0]
    op_ref[...] = mod[0:n_prompt]
    os_ref[...] = mod[n_prompt:]


def _with_ignored_inputs(body, n_in, n_ignored):
    if n_ignored == 0:
        return body
    return lambda *refs: body(*refs[:n_in], *refs[n_in + n_ignored:])


def _const_spec(shape):
    return pl.BlockSpec(shape, lambda *_: (0,) * len(shape), pipeline_mode=pl.Buffered(1))


def _layer_spec(l, shape):
    return pl.BlockSpec((None,) + shape, lambda *_: (l,) + (0,) * len(shape), pipeline_mode=pl.Buffered(1))


def _any_spec():
    return pl.BlockSpec(memory_space=pl.ANY)


def _smem_spec():
    return pl.BlockSpec(memory_space=pltpu.SMEM)


def _params(n_grid):
    return pltpu.CompilerParams(dimension_semantics=("arbitrary",) * n_grid, vmem_limit_bytes=VMEM_LIMIT)


def _ada_call(c_prompt, c_sample, w_ada, b_ada):
    n_p, n_s = c_prompt.shape[0], c_sample.shape[0]
    c_all = jnp.concatenate([c_prompt, c_sample], axis=0)
    tn = D_MODEL
    n_tiles = w_ada.shape[2] // tn
    mod_p, mod_s = pl.pallas_call(
        _ada_kernel,
        grid=(DEPTH, n_tiles),
        in_specs=[pl.BlockSpec((n_p + n_s, D_MODEL), lambda l, n: (0, 0)),
                  pl.BlockSpec((None, D_MODEL, tn), lambda l, n: (l, 0, n)),
                  pl.BlockSpec((None, 1, tn), lambda l, n: (l, 0, n))],
        out_specs=(pl.BlockSpec((None, n_p, tn), lambda l, n: (l, 0, n)),
                   pl.BlockSpec((None, n_s, tn), lambda l, n: (l, 0, n))),
        out_shape=(jax.ShapeDtypeStruct((DEPTH, n_p, w_ada.shape[2]), F32),
                   jax.ShapeDtypeStruct((DEPTH, n_s, w_ada.shape[2]), F32)),
        compiler_params=_params(2),
        name="ada_mod",
    )(c_all, w_ada, b_ada.reshape(DEPTH, 1, -1))
    return mod_p.reshape(DEPTH, n_p, 6, D_MODEL), mod_s.reshape(DEPTH, n_s, 6, D_MODEL)


def _layer_weight_specs(l):
    return [
        _layer_spec(l, (CONV_WIDTH, CONV_DIM)), _layer_spec(l, (1, CONV_DIM)), _layer_spec(l, (1, LANES)),
        _layer_spec(l, (1, LANES)), _layer_spec(l, (1, SSM_INNER)), _layer_spec(l, (1, SSM_INNER)),
        _layer_spec(l, (1, GM_WIDTH)), _layer_spec(l, (1, GM_WIDTH)),
        _layer_spec(l, (GM_GROUPS, GM_CHUNK, GM_CHUNK)), _layer_spec(l, (GM_CHUNK, GM_GROUPS)),
        _layer_spec(l, (Q_W, D_MODEL)), _layer_spec(l, (SSM_INNER, D_MODEL)), _layer_spec(l, (GM_WIDTH, D_MODEL)),
        _layer_spec(l, (D_MODEL, D_MODEL)), _const_spec((LANES, SSM_INNER)),
    ]


def _layer_weight_args(p):
    return (p["conv_w"], p["conv_b"], p["dt_bias"], p["a_log"], p["d_skip"], p["ssm_norm_w"], p["gm_ln_g"],
            p["gm_ln_b"], p["gm_w_s"], p["gm_b_st"], p["w_attn_o"], p["w_ssm_o"], p["w_gm_o"], p["w_out"], p["expand"])


def _prompt_mixer(l, x, mod, p, rope, prev):
    bsz, seq, d = x.shape
    ts = PROMPT_TS
    tab = pl.BlockSpec((ts, LANES), lambda b, s: (s, 0))
    in_specs = [
        pl.BlockSpec((1, ts, d), lambda b, s: (b, s, 0)),
        pl.BlockSpec((None, 1, 6, d), lambda b, s: (l, b, 0, 0)),
        _layer_spec(l, (1, d)), _layer_spec(l, (d, N_INP)),
        tab, tab, tab, _smem_spec(),
    ] + _layer_weight_specs(l)
    args = (x, mod, p["g_mix"], p["w_in"], *rope, p["sinks"][l], *_layer_weight_args(p))
    state_tails = ((WINDOW, KV_W), (WINDOW, KV_W), (SSM_INNER, SSM_STATE), (CONV_WIDTH - 1, CONV_DIM))
    out_shape = (jax.ShapeDtypeStruct((bsz, seq, d), F32),) + tuple(
        jax.ShapeDtypeStruct((DEPTH, bsz) + tail, F32) for tail in state_tails)
    out_specs = (pl.BlockSpec((1, ts, d), lambda b, s: (b, s, 0)),) + tuple(
        pl.BlockSpec((None, 1) + tail, lambda b, s: (l, b, 0, 0)) for tail in state_tails)
    scratch = [
        pltpu.VMEM((WINDOW + ts, KV_W), F32), pltpu.VMEM((WINDOW + ts, KV_W), F32),
        pltpu.VMEM((8 + ts, CONV_DIM), F32), pltpu.VMEM((SSM_STATE, SSM_INNER), F32),
    ]
    n_in = len(args)
    prev = () if prev is None else tuple(prev)
    return pl.pallas_call(
        _with_ignored_inputs(_prompt_kernel, n_in, len(prev)),
        grid=(bsz, seq // ts), in_specs=in_specs + [_any_spec()] * len(prev), out_specs=out_specs,
        out_shape=out_shape, scratch_shapes=scratch, compiler_params=_params(2), name="prompt_mixer",
        input_output_aliases={n_in + i: 1 + i for i in range(len(prev))},
    )(*args, *prev)


def _sample_mixer(l, x, mod, cache_k, cache_v, h0, conv_state, p, rope, prev):
    bsz, t, d = x.shape
    nb = SAMPLE_NB
    m = nb * t

    def blk(*tail):
        return pl.BlockSpec((nb,) + tail, lambda i: (i,) + (0,) * len(tail))

    def lblk(*tail):
        return pl.BlockSpec((None, nb) + tail, lambda i: (l, i) + (0,) * len(tail))

    in_specs = [
        blk(t, d), lblk(6, d), _layer_spec(l, (1, d)), _layer_spec(l, (d, N_INP)),
        _const_spec((m, LANES)), _const_spec((m, LANES)), _const_spec((m, LANES)), _smem_spec(),
        lblk(WINDOW, KV_W), lblk(WINDOW, KV_W), lblk(SSM_INNER, SSM_STATE), lblk(CONV_WIDTH - 1, CONV_DIM),
    ] + _layer_weight_specs(l)
    args = (x, mod, p["g_mix"], p["w_in"], *rope, p["sinks"][l], cache_k, cache_v, h0, conv_state,
            *_layer_weight_args(p))
    state_tails = ((t, KV_W), (t, KV_W), (SSM_INNER, SSM_STATE), (CONV_WIDTH - 1, CONV_DIM), (t, GM_WIDTH))
    out_shape = (jax.ShapeDtypeStruct((bsz, t, d), F32),) + tuple(
        jax.ShapeDtypeStruct((DEPTH, bsz) + tail, F32) for tail in state_tails)
    out_specs = (blk(t, d),) + tuple(lblk(*tail) for tail in state_tails)
    scratch = [
        pltpu.VMEM((m, Q_W), F32), pltpu.VMEM((m, KV_W), F32), pltpu.VMEM((m, KV_W), F32),
        pltpu.VMEM((nb, 8 + t, CONV_DIM), F32), pltpu.VMEM((m, CONV_DIM), F32),
        pltpu.VMEM((m, LANES), F32), pltpu.VMEM((LANES, t), F32),
        pltpu.VMEM((SSM_STATE, SSM_INNER), F32), pltpu.VMEM((m, SSM_INNER), F32),
        pltpu.VMEM((m, Q_W), F32), pltpu.VMEM((m, GM_WIDTH), F32), pltpu.VMEM((m, GM_WIDTH), F32),
        pltpu.VMEM((m, GM_WIDTH), F32),
    ]
    n_in = len(args)
    prev = () if prev is None else tuple(prev)
    return pl.pallas_call(
        _with_ignored_inputs(_sample_kernel, n_in, len(prev)),
        grid=(bsz // nb,), in_specs=in_specs + [_any_spec()] * len(prev), out_specs=out_specs,
        out_shape=out_shape, scratch_shapes=scratch, compiler_params=_params(1), name="sample_mixer",
        input_output_aliases={n_in + i: 1 + i for i in range(len(prev))},
    )(*args, *prev)


def _ffn(l, x, mod, p, g_final, final_norm, name):
    bsz, t, d = x.shape
    if t >= FFN_ROWS:
        nb, tt = 1, FFN_ROWS
    else:
        nb, tt = FFN_ROWS // t, t
    grid = (bsz // nb, t // tt)
    return pl.pallas_call(
        functools.partial(_ffn_kernel, final_norm=final_norm),
        grid=grid,
        in_specs=[pl.BlockSpec((nb, tt, d), lambda b, s: (b, s, 0)),
                  pl.BlockSpec((None, nb, 6, d), lambda b, s: (l, b, 0, 0)),
                  _layer_spec(l, (1, d)), _layer_spec(l, (d, D_FF)), _layer_spec(l, (D_FF, d)), _const_spec((1, d))],
        out_specs=pl.BlockSpec((nb, tt, d), lambda b, s: (b, s, 0)),
        out_shape=jax.ShapeDtypeStruct((bsz, t, d), F32),
        compiler_params=_params(2), name=name,
    )(x, mod, p["g_ff"], p["w_ff1"], p["w_ff2"], g_final)


def _rope_tables(pos):
    half = ROT_DIM // 2
    inv_freq = ROPE_THETA ** (-jnp.arange(half, dtype=F32) * (2.0 / ROT_DIM))
    ang = pos.astype(F32)[:, None] * inv_freq[None, :]
    cos, sin = jnp.cos(ang), jnp.sin(ang)
    n = pos.shape[0]
    ones = jnp.ones((n, HEAD_DIM - ROT_DIM), F32)
    zeros = jnp.zeros((n, HEAD_DIM - ROT_DIM), F32)
    zh = jnp.zeros((n, half), F32)
    cos_t = jnp.concatenate([cos, cos, ones], axis=1)
    sin_a = jnp.concatenate([-sin, zh, zeros], axis=1)
    sin_b = jnp.concatenate([zh, sin, zeros], axis=1)
    rep = LANES // HEAD_DIM
    return tuple(jnp.tile(a, (1, rep)) for a in (cos_t, sin_a, sin_b))


def _stacked_params(w_in, g_mix, sinks, conv_w, conv_b, dt_bias, a_log, d_skip, ssm_norm_w, gm_ln_g, gm_ln_b,
                    gm_w_s, gm_b_s, w_attn_o, w_ssm_o, w_gm_o, w_out, g_ff, w_ff1, w_ff2):
    dt_lo = Q_W + 2 * KV_W + SSM_INNER + CONV_DIM
    w_r = jnp.concatenate([w_in[:, :, :dt_lo], w_in[:, :, dt_lo + SSM_HEADS:], w_in[:, :, dt_lo:dt_lo + SSM_HEADS],
                           jnp.zeros((DEPTH, D_MODEL, LANES - SSM_HEADS), w_in.dtype)], axis=2).astype(BF16)
    pad = jnp.zeros((DEPTH, LANES - SSM_HEADS), F32)
    expand = (jnp.arange(SSM_INNER)[None, :] // SSM_HEAD_DIM == jnp.arange(LANES)[:, None]).astype(BF16)

    def row(a):
        return a[:, None, :]

    return {
        "w_in": w_r, "g_mix": row(g_mix), "sinks": sinks,
        "conv_w": conv_w, "conv_b": row(conv_b),
        "dt_bias": row(jnp.concatenate([dt_bias, pad], axis=1)), "a_log": row(jnp.concatenate([a_log, pad], axis=1)),
        "d_skip": row(jnp.repeat(d_skip, SSM_HEAD_DIM, axis=1)), "ssm_norm_w": row(ssm_norm_w),
        "gm_ln_g": row(gm_ln_g), "gm_ln_b": row(gm_ln_b),
        "gm_w_s": gm_w_s, "gm_b_st": jnp.swapaxes(gm_b_s, 1, 2),
        "w_attn_o": w_attn_o.astype(BF16), "w_ssm_o": w_ssm_o.astype(BF16),
        "w_gm_o": w_gm_o.astype(BF16), "w_out": w_out.astype(BF16), "expand": expand,
        "g_ff": row(g_ff), "w_ff1": w_ff1.astype(BF16), "w_ff2": w_ff2.astype(BF16),
    }


def kernel(x_prompt, x_sample, c_prompt, c_sample, cache_attn_k, cache_attn_v, state_ssm, state_conv, w_ada, b_ada, g_mix, w_in, sinks, conv_w, conv_b, dt_bias, a_log, d_skip, ssm_norm_w, gm_ln_g, gm_ln_b, gm_w_s, gm_b_s, w_attn_o, w_ssm_o, w_gm_o, w_out, g_ff, w_ff1, w_ff2, g_final):
    bp, sp, d = x_prompt.shape
    bs, ss, _ = x_sample.shape
    mod_p, mod_s = _ada_call(c_prompt, c_sample, w_ada, b_ada)
    rope_p = _rope_tables(jnp.arange(sp))
    rope_s = tuple(jnp.tile(a, (SAMPLE_NB, 1)) for a in _rope_tables(PAST_LEN + jnp.arange(ss)))
    g_fin = g_final[None]
    p = _stacked_params(w_in, g_mix, sinks, conv_w, conv_b, dt_bias, a_log, d_skip, ssm_norm_w, gm_ln_g,
                        gm_ln_b, gm_w_s, gm_b_s, w_attn_o, w_ssm_o, w_gm_o, w_out, g_ff, w_ff1, w_ff2)
    ck = cache_attn_k.reshape(DEPTH, bs, WINDOW, KV_W)
    cv = cache_attn_v.reshape(DEPTH, bs, WINDOW, KV_W)
    h0 = state_ssm.reshape(DEPTH, bs, SSM_INNER, SSM_STATE)
    xp, xs = x_prompt, x_sample
    state_p = state_s = None
    for l in range(DEPTH):
        final = l == DEPTH - 1
        xp, *state_p = _prompt_mixer(l, xp, mod_p, p, rope_p, state_p)
        xp = _ffn(l, xp, mod_p, p, g_fin, final, "prompt_ffn")
        xs, *state_s = _sample_mixer(l, xs, mod_s, ck, cv, h0, state_conv, p, rope_s, state_s)
        xs = _ffn(l, xs, mod_s, p, g_fin, final, "sample_ffn")
    kp, vp, ssm_p, conv_p = state_p
    ks, vs, ssm_s, conv_s, gv_s = state_s
    return (xp, xs,
            kp.reshape(DEPTH, bp, WINDOW, N_KV_HEADS, HEAD_DIM), vp.reshape(DEPTH, bp, WINDOW, N_KV_HEADS, HEAD_DIM),
            ssm_p.reshape(DEPTH, bp, SSM_HEADS, SSM_HEAD_DIM, SSM_STATE), conv_p,
            ks.reshape(DEPTH, bs, ss, N_KV_HEADS, HEAD_DIM), vs.reshape(DEPTH, bs, ss, N_KV_HEADS, HEAD_DIM),
            ssm_s.reshape(DEPTH, bs, SSM_HEADS, SSM_HEAD_DIM, SSM_STATE), conv_s, gv_s)
```

```python
import functools

import jax
import jax.numpy as jnp
from jax import lax
from jax.experimental import pallas as pl
from jax.experimental.pallas import tpu as pltpu

F32 = jnp.float32
BF16 = jnp.bfloat16

D_MODEL = 1024
DEPTH = 2
CHUNK = 64
N_HEADS = 8
N_KV_HEADS = 2
HEAD_DIM = 64
GQA_GROUP = N_HEADS // N_KV_HEADS
ROT_DIM = HEAD_DIM // 4
ROPE_THETA = 500000.0
WINDOW = 128
SSM_HEADS = 16
SSM_HEAD_DIM = 64
SSM_INNER = SSM_HEADS * SSM_HEAD_DIM
SSM_GROUPS = 2
SSM_STATE = 128
SSM_CHUNK = 64
CONV_WIDTH = 4
CONV_DIM = SSM_INNER + 2 * SSM_GROUPS * SSM_STATE
GM_WIDTH = 512
GM_GROUPS = 4
GM_GROUP_DIM = GM_WIDTH // GM_GROUPS
GM_CHUNK = 128
D_FF = 4 * D_MODEL
Q_W = N_HEADS * HEAD_DIM
KV_W = N_KV_HEADS * HEAD_DIM
PAST_LEN = 4096
EPS = 1e-6

LANES = 128
HEADS_PER_GROUP = SSM_HEADS // SSM_GROUPS
GROUP_INNER = HEADS_PER_GROUP * SSM_HEAD_DIM

C_Q = (0, 0, Q_W)
C_K = (0, C_Q[2], C_Q[2] + KV_W)
C_V = (0, C_K[2], C_K[2] + KV_W)
C_Z = (0, C_V[2], C_V[2] + SSM_INNER)
C_XBC = (0, C_Z[2], C_Z[2] + CONV_DIM)
N_IN_A = C_XBC[2]
C_GU = (1, 0, GM_WIDTH)
C_GV = (1, C_GU[2], C_GU[2] + GM_WIDTH)
C_G0 = (1, C_GV[2], C_GV[2] + D_MODEL)
C_G1 = (1, C_G0[2], C_G0[2] + D_MODEL)
C_G2 = (1, C_G1[2], C_G1[2] + D_MODEL)
N_IN_B = C_G2[2]
C_DT = (2, 0, LANES)

PROMPT_TS = 256
PROJ_BLOCK = 256
SAMPLE_NB = 8
FFN_ROWS = 512
VMEM_LIMIT = 56 * 1024 * 1024


def _dot(a, b):
    return jnp.dot(a, b, preferred_element_type=F32)


def _dot_nt(a, b):
    return lax.dot_general(a, b, (((1,), (1,)), ((), ())), preferred_element_type=F32)


def _dot_tn(a, b):
    return lax.dot_general(a, b, (((0,), (0,)), ((), ())), preferred_element_type=F32)


def _split3(x):
    hi = x.astype(BF16)
    r = x - hi.astype(F32)
    mid = r.astype(BF16)
    lo = (r - mid.astype(F32)).astype(BF16)
    return hi, mid, lo


def _sigmoid(x):
    return 0.5 * (1.0 + jnp.tanh(0.5 * x))


def _silu(x):
    return x * _sigmoid(x)


def _gelu(x):
    return 0.5 * x * (1.0 + jnp.tanh(0.7978845608028654 * (x + 0.044715 * (x * x * x))))


def _softplus(x):
    return jnp.maximum(x, 0.0) + jnp.log1p(jnp.exp(-jnp.abs(x)))


def _rms(x, g):
    return x * lax.rsqrt(jnp.mean(x * x, axis=-1, keepdims=True) + EPS) * g


def _rope(x, cos, sin_a, sin_b):
    return x * cos + pltpu.roll(x, LANES - ROT_DIM // 2, 1) * sin_a + pltpu.roll(x, ROT_DIM // 2, 1) * sin_b


def _proj(h, win_refs, cols):
    return _dot(h, win_refs[cols[0]][:, cols[1]:cols[2]])


def _modnorm(x3, g, mod3):
    nb, t, d = x3.shape
    h3 = _rms(x3, g) * (1.0 + mod3[:, 1:2, :]) + mod3[:, 0:1, :]
    return h3.reshape(nb * t, d).astype(BF16)


def _sink_column(sinks_ref, kv, rows_per_head):
    r = lax.broadcasted_iota(jnp.int32, (GQA_GROUP * rows_per_head, 1), 0)
    col = jnp.full((GQA_GROUP * rows_per_head, 1), sinks_ref[kv * GQA_GROUP], F32)
    for i in range(1, GQA_GROUP):
        col = jnp.where(r >= i * rows_per_head, sinks_ref[kv * GQA_GROUP + i], col)
    return col


def _sink_row(sinks_ref, kv, cols_per_head):
    c = lax.broadcasted_iota(jnp.int32, (1, GQA_GROUP * cols_per_head), 1)
    row = jnp.full((1, GQA_GROUP * cols_per_head), sinks_ref[kv * GQA_GROUP], F32)
    for i in range(1, GQA_GROUP):
        row = jnp.where(c >= i * cols_per_head, sinks_ref[kv * GQA_GROUP + i], row)
    return row


def _sink_softmax_pv(scores, values, sink_col):
    m = sink_col
    for s in scores:
        m = jnp.maximum(m, jnp.max(s, axis=-1, keepdims=True))
    denom = jnp.exp(sink_col - m)
    o = None
    for s, v in zip(scores, values):
        p = jnp.exp(s - m)
        denom = denom + jnp.sum(p, axis=-1, keepdims=True)
        pv = _dot(p.astype(BF16), v)
        o = pv if o is None else o + pv
    return o / denom


def _stack_heads(q, kv, rows):
    return jnp.concatenate(
        [q[:, (kv * GQA_GROUP + i) * HEAD_DIM:(kv * GQA_GROUP + i + 1) * HEAD_DIM] for i in range(GQA_GROUP)], axis=0)


def _unstack_heads(o, rows):
    return jnp.concatenate([o[i * rows:(i + 1) * rows, :] for i in range(GQA_GROUP)], axis=1)


def _ssd_chunk(r0, L, xact_ref, dt_ref, st_ref, cumt_ref, y_ref, a_row, e_ref, dskip_e):
    rows = pl.ds(r0, L)
    xs = xact_ref[rows, 0:SSM_INNER]
    bm = xact_ref[rows, SSM_INNER:SSM_INNER + SSM_GROUPS * SSM_STATE].astype(BF16)
    cm = xact_ref[rows, SSM_INNER + SSM_GROUPS * SSM_STATE:CONV_DIM].astype(BF16)
    dt = dt_ref[rows, :]
    dta = dt * a_row
    ri = lax.broadcasted_iota(jnp.int32, (L, L), 0)
    ci = lax.broadcasted_iota(jnp.int32, (L, L), 1)
    causal = ri >= ci
    tri = jnp.where(causal, 1.0, 0.0).astype(BF16)
    hi, mid, lo = _split3(dta)
    cum = _dot(tri, hi) + _dot(tri, mid) + _dot(tri, lo)
    triu = jnp.where(ri <= ci, 1.0, 0.0).astype(BF16)
    cumt_ref[...] = _dot_tn(hi, triu) + _dot_tn(mid, triu) + _dot_tn(lo, triu)
    both = jnp.concatenate([dt, cum], axis=0)
    bh, bmid, bl = _split3(both)
    e = e_ref[...]
    both_e = _dot(bh, e) + _dot(bmid, e) + _dot(bl, e)
    dt_e = both_e[0:L]
    cum_e = both_e[L:2 * L]
    last_e = cum_e[L - 1:L, :]
    xdt = xs * dt_e
    xdt_b = xdt.astype(BF16)
    y_inter = []
    for g in range(SSM_GROUPS):
        bm_g = bm[:, g * SSM_STATE:(g + 1) * SSM_STATE]
        cm_g = cm[:, g * SSM_STATE:(g + 1) * SSM_STATE]
        cb = _dot_nt(cm_g, bm_g)
        for k in range(HEADS_PER_GROUP):
            h = g * HEADS_PER_GROUP + k
            lo_l = h * SSM_HEAD_DIM
            diff = cum_e[:, lo_l:lo_l + L] - cumt_ref[h:h + 1, :]
            decay = jnp.exp(jnp.where(causal, diff, -jnp.inf))
            m_h = (cb * decay).astype(BF16)
            y_ref[rows, lo_l:lo_l + SSM_HEAD_DIM] = _dot(m_h, xdt_b[:, lo_l:lo_l + SSM_HEAD_DIM])
        st_g = st_ref[:, g * GROUP_INNER:(g + 1) * GROUP_INNER].astype(BF16)
        y_inter.append(_dot(cm_g, st_g))
    y_inter = jnp.concatenate(y_inter, axis=1)
    y_ref[rows, :] = y_ref[rows, :] + y_inter * jnp.exp(cum_e) + dskip_e * xs
    wx = (xdt * jnp.exp(last_e - cum_e)).astype(BF16)
    cdec = jnp.exp(last_e)
    for g in range(SSM_GROUPS):
        gs = slice(g * GROUP_INNER, (g + 1) * GROUP_INNER)
        bm_g = bm[:, g * SSM_STATE:(g + 1) * SSM_STATE]
        st_ref[:, gs] = st_ref[:, gs] * cdec[:, gs] + _dot_tn(bm_g, wx[:, gs])


def _gmlp_chunk(vn, u, gws_ref, gbst_ref, L):
    ri = lax.broadcasted_iota(jnp.int32, (L, L), 0)
    ci = lax.broadcasted_iota(jnp.int32, (L, L), 1)
    outs = []
    for g in range(GM_GROUPS):
        w = jnp.where(ri >= ci, gws_ref[g, 0:L, 0:L], 0.0).astype(BF16)
        v_g = vn[:, g * GM_GROUP_DIM:(g + 1) * GM_GROUP_DIM].astype(BF16)
        outs.append(_dot(w, v_g) + gbst_ref[0:L, g:g + 1])
    return u * jnp.concatenate(outs, axis=1)


def _layer_norm(x, g, b):
    xc = x - jnp.mean(x, axis=-1, keepdims=True)
    return xc * lax.rsqrt(jnp.mean(xc * xc, axis=-1, keepdims=True) + EPS) * g + b


def _merge_out(x3, mod3, gates, a, b, c, wout_ref):
    merged = gates[0] * a + gates[1] * b + gates[2] * c
    o = _dot(merged.astype(BF16), wout_ref[...])
    nb, t, d = x3.shape
    return x3 + mod3[:, 2:3, :] * o.reshape(nb, t, d)


def _attn_groups(ts):
    return [(c, kv) for c in range(ts // CHUNK) for kv in range(N_KV_HEADS)]


def _attn_scores(q, khist, pos0):
    ts = q.shape[0]
    n_keys = WINDOW + CHUNK
    key_i = lax.broadcasted_iota(jnp.int32, (n_keys, GQA_GROUP * CHUNK), 0)
    scores = []
    for c, kv in _attn_groups(ts):
        k_g = khist[c * CHUNK:c * CHUNK + n_keys, kv * HEAD_DIM:(kv + 1) * HEAD_DIM].astype(BF16)
        sc = _dot_nt(k_g, _stack_heads(q[c * CHUNK:(c + 1) * CHUNK, :], kv, CHUNK))
        if c * CHUNK < WINDOW:
            sc = jnp.where(key_i >= WINDOW - c * CHUNK - pos0, sc, -jnp.inf)
        scores.append(sc)
    return scores


def _attn_probs(scores, sinks_ref, ts):
    sink_rows = [_sink_row(sinks_ref, kv, CHUNK) for kv in range(N_KV_HEADS)]
    probs = []
    for (c, kv), sc in zip(_attn_groups(ts), scores):
        m = jnp.maximum(jnp.max(sc, axis=0, keepdims=True), sink_rows[kv])
        p = jnp.exp(sc - m)
        denom = jnp.sum(p, axis=0, keepdims=True) + jnp.exp(sink_rows[kv] - m)
        probs.append((p * (1.0 / denom)).astype(BF16))
    return probs


def _attn_values(probs, vhist, ts):
    n_keys = WINDOW + CHUNK
    outs = []
    for (c, kv), p in zip(_attn_groups(ts), probs):
        v_g = vhist[c * CHUNK:c * CHUNK + n_keys, kv * HEAD_DIM:(kv + 1) * HEAD_DIM].astype(BF16)
        outs.append(_unstack_heads(_dot_tn(p, v_g), CHUNK))
    rows = [jnp.concatenate(outs[c * N_KV_HEADS:(c + 1) * N_KV_HEADS], axis=1) for c in range(ts // CHUNK)]
    return jnp.concatenate(rows, axis=0).astype(BF16)


def _ssd_cumsum(dt, a_row, e):
    ts = dt.shape[0]
    L = SSM_CHUNK
    ri = lax.broadcasted_iota(jnp.int32, (ts, ts), 0)
    ci = lax.broadcasted_iota(jnp.int32, (ts, ts), 1)
    lag = ri - ci
    tri = jnp.where((lag >= 0) & (lag <= (ri & (L - 1))), 1.0, 0.0).astype(BF16)
    hi, mid, lo = _split3(dt * a_row)
    cum = _dot(tri, hi) + _dot(tri, mid) + _dot(tri, lo)
    hi, mid, lo = _split3(cum)
    cum_e = _dot(hi, e) + _dot(mid, e) + _dot(lo, e)
    dt_hi = dt.astype(BF16)
    dt_lo = (dt - dt_hi.astype(F32)).astype(BF16)
    dt_e = _dot(dt_hi, e) + _dot(dt_lo, e)
    return cum_e, dt_e


def _ssd_decays(xact, cum_e, dt_e):
    ts = xact.shape[0]
    L = SSM_CHUNK
    nc = ts // L
    xs = xact[:, 0:SSM_INNER]
    cum_e = cum_e.reshape(nc, L, SSM_INNER)
    t3 = lax.broadcasted_iota(jnp.int32, cum_e.shape, 1)
    slot3 = lax.broadcasted_iota(jnp.int32, cum_e.shape, 2) & (SSM_HEAD_DIM - 1)
    cum_src = jnp.sum(jnp.where(t3 == slot3, cum_e, 0.0), axis=1, keepdims=True)
    decay = jnp.exp(jnp.where(t3 >= slot3, cum_e - cum_src, -jnp.inf))
    last = cum_e[:, L - 1:L, :]
    exp_cum = jnp.exp(cum_e)
    chunk_decay = jnp.exp(last)
    xdt = xs * dt_e
    wx = (xdt.reshape(nc, L, SSM_INNER) * jnp.exp(last - cum_e)).astype(BF16)
    first_head = (lax.broadcasted_iota(jnp.int32, xdt.shape, 1) & (2 * SSM_HEAD_DIM - 1)) < SSM_HEAD_DIM
    return {
        "xs": xs,
        "bm": xact[:, SSM_INNER:SSM_INNER + SSM_GROUPS * SSM_STATE].astype(BF16),
        "cm": xact[:, SSM_INNER + SSM_GROUPS * SSM_STATE:CONV_DIM].astype(BF16),
        "decay": decay, "exp_cum": exp_cum, "chunk_decay": chunk_decay, "wx": wx,
        "xdt_a": jnp.where(first_head, xdt, 0.0).astype(BF16),
        "xdt_b": jnp.where(first_head, 0.0, xdt).astype(BF16),
    }


def _ssd_chunk_step(c, ctx, st, dskip_e):
    L = SSM_CHUNK
    rows = slice(c * L, (c + 1) * L)
    bm, cm = ctx["bm"], ctx["cm"]
    cb = jnp.concatenate(
        [_dot_nt(cm[rows, g * SSM_STATE:(g + 1) * SSM_STATE],
                 jnp.concatenate([bm[rows, g * SSM_STATE:(g + 1) * SSM_STATE]] * HEADS_PER_GROUP, axis=0))
         for g in range(SSM_GROUPS)], axis=1)
    m_full = (cb * ctx["decay"][c]).astype(BF16)
    y_intra = []
    for k in range(SSM_HEADS // 2):
        ls = slice(k * 2 * SSM_HEAD_DIM, (k + 1) * 2 * SSM_HEAD_DIM)
        block_diag = jnp.concatenate([ctx["xdt_a"][rows, ls], ctx["xdt_b"][rows, ls]], axis=0)
        y_intra.append(_dot(m_full[:, ls], block_diag))
    st_b = st.astype(BF16)
    y_inter = jnp.concatenate(
        [_dot(cm[rows, g * SSM_STATE:(g + 1) * SSM_STATE], st_b[:, g * GROUP_INNER:(g + 1) * GROUP_INNER])
         for g in range(SSM_GROUPS)], axis=1)
    y = jnp.concatenate(y_intra, axis=1) + y_inter * ctx["exp_cum"][c] + dskip_e * ctx["xs"][rows]
    st = st * ctx["chunk_decay"][c] + jnp.concatenate(
        [_dot_tn(bm[rows, g * SSM_STATE:(g + 1) * SSM_STATE], ctx["wx"][c][:, g * GROUP_INNER:(g + 1) * GROUP_INNER])
         for g in range(SSM_GROUPS)], axis=1)
    return y, st


class _Stagger:
    def __init__(self):
        self._pending = None

    def push(self, produce, consume):
        val = produce()
        self.flush()
        self._pending = (consume, val)

    def flush(self):
        if self._pending is not None:
            consume, val = self._pending
            self._pending = None
            consume(val)


def _prompt_kernel(x_ref, mod_ref, gmix_ref, wa_ref, wb_ref, wdt_ref, cos_ref, sa_ref, sb_ref, sinks_ref,
                   convw_ref, convb_ref, dtb_ref, alog_ref, dskip_ref, ssmnw_ref, glng_ref, glnb_ref,
                   gws_ref, gbst_ref, wao_ref, wso_ref, wgo_ref, wout_ref, e_ref,
                   xo_ref, ko_ref, vo_ref, sto_ref, cvo_ref,
                   khist, vhist, xp, st_s):
    ts = PROMPT_TS
    s = pl.program_id(1)
    last = pl.num_programs(1) - 1

    @pl.when(s == 0)
    def _():
        khist[0:WINDOW, :] = jnp.zeros((WINDOW, KV_W), F32)
        vhist[0:WINDOW, :] = jnp.zeros((WINDOW, KV_W), F32)
        xp[0:8, :] = jnp.zeros((8, CONV_DIM), F32)
        st_s[...] = jnp.zeros_like(st_s)

    x3 = x_ref[...]
    mod3 = mod_ref[...]
    h = _modnorm(x3, gmix_ref[...], mod3)

    cos, sa, sb = cos_ref[...], sa_ref[...], sb_ref[...]
    res = {}

    def rope_block(raw):
        return [_rope(raw[:, i * LANES:(i + 1) * LANES], cos, sa, sb) for i in range(raw.shape[1] // LANES)]

    def q_block(raw):
        res.setdefault("q", []).extend(p.astype(BF16) for p in rope_block(raw * (HEAD_DIM ** -0.5)))

    def kv_block(raw):
        khist[WINDOW:WINDOW + ts, :] = rope_block(raw[:, 0:KV_W])[0]
        vhist[WINDOW:WINDOW + ts, :] = raw[:, KV_W:2 * KV_W]

    def conv_block(lo, raw):
        cs = slice(lo, lo + PROJ_BLOCK)
        xp[8:8 + ts, cs] = raw
        acc = convb_ref[:, cs] + raw * convw_ref[CONV_WIDTH - 1:CONV_WIDTH, cs]
        for j in range(1, CONV_WIDTH):
            acc = acc + xp[8 - j:8 - j + ts, cs] * convw_ref[CONV_WIDTH - 1 - j:CONV_WIDTH - j, cs]
        res.setdefault("xact", []).append(_silu(acc))

    def collect(key, fn):
        return lambda raw: res.setdefault(key, []).append(fn(raw))

    wrefs = (wa_ref, wb_ref, wdt_ref)
    pipe = _Stagger()

    def push_proj(cols, off, width, consume):
        ref, lo = wrefs[cols[0]], cols[1] + off
        pipe.push(lambda: _dot(h, ref[:, lo:lo + width]), consume)

    def blocks(cols, consume):
        return [functools.partial(push_proj, cols, off, PROJ_BLOCK, consume)
                for off in range(0, cols[2] - cols[1], PROJ_BLOCK)]

    conv_blocks = [functools.partial(push_proj, C_XBC, off, PROJ_BLOCK, functools.partial(conv_block, off))
                   for off in range(0, CONV_DIM, PROJ_BLOCK)]
    gate_blocks = iter(blocks(C_G0, collect("g0", _sigmoid)) + blocks(C_G1, collect("g1", _sigmoid))
                       + blocks(C_G2, collect("g2", _sigmoid)))
    z_blocks = blocks(C_Z, collect("sz", _silu))
    gm_blocks = blocks(C_GU, collect("u", _gelu)) + blocks(C_GV, collect("gv", _gelu))

    def cat(key):
        return jnp.concatenate(res[key], axis=1)

    for blk in blocks(C_Q, q_block):
        blk()
    push_proj(C_K, 0, 2 * KV_W, kv_block)
    conv_blocks[0]()
    next(gate_blocks)()
    conv_blocks[1]()
    scores = _attn_scores(cat("q"), khist, s * ts)
    conv_blocks[2]()
    probs = _attn_probs(scores, sinks_ref, ts)
    next(gate_blocks)()
    conv_blocks[3]()
    attn = _attn_values(probs, vhist, ts)
    conv_blocks[4]()
    next(gate_blocks)()
    conv_blocks[5]()
    push_proj(C_DT, 0, LANES, collect("dt", lambda raw: _softplus(raw + dtb_ref[...])))
    next(gate_blocks)()
    a_out = _dot(attn, wao_ref[...])
    z_blocks[0]()
    cum_e, dt_e = _ssd_cumsum(res["dt"][0], -jnp.exp(alog_ref[...]), e_ref[...])
    z_blocks[1]()
    merged = cat("g0") * a_out
    ctx = _ssd_decays(cat("xact"), cum_e, dt_e)
    z_blocks[2]()
    z_blocks[3]()
    st = st_s[...]
    ys = []
    for c in range(ts // SSM_CHUNK):
        y_c, st = _ssd_chunk_step(c, ctx, st, dskip_ref[...])
        ys.append(y_c)
        gm_blocks[c]()
    st_s[...] = st
    next(gate_blocks)()
    vn = _layer_norm(cat("gv"), glng_ref[...], glnb_ref[...])
    u = cat("u")
    next(gate_blocks)()
    ssm = _rms(jnp.concatenate(ys, axis=0) * cat("sz"), ssmnw_ref[...]).astype(BF16)
    gm = jnp.concatenate(
        [_gmlp_chunk(vn[c * GM_CHUNK:(c + 1) * GM_CHUNK], u[c * GM_CHUNK:(c + 1) * GM_CHUNK], gws_ref, gbst_ref,
                     GM_CHUNK) for c in range(ts // GM_CHUNK)], axis=0).astype(BF16)
    next(gate_blocks)()
    b_out = _dot(ssm, wso_ref[...])
    next(gate_blocks)()
    pipe.flush()
    merged = merged + cat("g1") * b_out
    c_out = _dot(gm, wgo_ref[...])
    for blk in gate_blocks:
        blk()
    pipe.flush()
    merged = merged + cat("g2") * c_out
    o = _dot(merged.astype(BF16), wout_ref[...])
    xo_ref[...] = x3 + mod3[:, 2:3, :] * o.reshape(x3.shape)

    khist[0:WINDOW, :] = khist[ts:ts + WINDOW, :]
    vhist[0:WINDOW, :] = vhist[ts:ts + WINDOW, :]
    tail = xp[ts + 8 - (CONV_WIDTH - 1):ts + 8, :]
    xp[8 - (CONV_WIDTH - 1):8, :] = tail

    @pl.when(s == last)
    def _():
        ko_ref[0] = khist[ts:ts + WINDOW, :]
        vo_ref[0] = vhist[ts:ts + WINDOW, :]
        cvo_ref[0] = tail
        sto_ref[0] = st_s[...].T


def _sample_kernel(x_ref, mod_ref, gmix_ref, wa_ref, wb_ref, wdt_ref, cos_ref, sa_ref, sb_ref, sinks_ref,
                   ck_ref, cv_ref, h0_ref, cs_ref,
                   convw_ref, convb_ref, dtb_ref, alog_ref, dskip_ref, ssmnw_ref, glng_ref, glnb_ref,
                   gws_ref, gbst_ref, wao_ref, wso_ref, wgo_ref, wout_ref, e_ref,
                   xo_ref, ko_ref, vo_ref, sto_ref, cvo_ref, gvo_ref,
                   q_s, k_s, v_s, xp, xact, dt_s, cumt_s, st_s, y_s, attn_s, vn_s, u_s, gm_s):
    nb = SAMPLE_NB
    t = x_ref.shape[1]
    m = nb * t
    win_ref = (wa_ref, wb_ref, wdt_ref)
    x3 = x_ref[...]
    mod3 = mod_ref[...]
    h = _modnorm(x3, gmix_ref[...], mod3)

    cos, sa, sb = cos_ref[...], sa_ref[...], sb_ref[...]
    q = _proj(h, win_ref, C_Q) * (HEAD_DIM ** -0.5)
    q_s[...] = jnp.concatenate(
        [_rope(q[:, i * LANES:(i + 1) * LANES], cos, sa, sb) for i in range(Q_W // LANES)], axis=1)
    k_new = _rope(_proj(h, win_ref, C_K), cos, sa, sb)
    v_new = _proj(h, win_ref, C_V)
    k_s[...] = k_new
    v_s[...] = v_new
    ko_ref[...] = k_new.reshape(nb, t, KV_W)
    vo_ref[...] = v_new.reshape(nb, t, KV_W)

    xp[:, 8 - (CONV_WIDTH - 1):8, :] = cs_ref[...]
    xp[:, 8:8 + t, :] = _proj(h, win_ref, C_XBC).reshape(nb, t, CONV_DIM)
    acc = convb_ref[...] + xp[:, 8:8 + t, :] * convw_ref[CONV_WIDTH - 1:CONV_WIDTH, :]
    for j in range(1, CONV_WIDTH):
        acc = acc + xp[:, 8 - j:8 - j + t, :] * convw_ref[CONV_WIDTH - 1 - j:CONV_WIDTH - j, :]
    xact[...] = _silu(acc).reshape(m, CONV_DIM)
    cvo_ref[...] = xp[:, t + 8 - (CONV_WIDTH - 1):t + 8, :]
    dt_s[...] = _softplus(_proj(h, win_ref, C_DT) + dtb_ref[...])
    a_row = -jnp.exp(alog_ref[...])
    dskip_e = dskip_ref[...]

    u_s[...] = _gelu(_proj(h, win_ref, C_GU))
    vn = _layer_norm(_gelu(_proj(h, win_ref, C_GV)), glng_ref[...], glnb_ref[...])
    vn_s[...] = vn
    gvo_ref[...] = vn.reshape(nb, t, GM_WIDTH)

    def seq_body(i, carry):
        r0 = pl.multiple_of(i * t, t)
        rows = pl.ds(r0, t)
        qi = q_s[rows, :].astype(BF16)
        per_kv = []
        for kv in range(N_KV_HEADS):
            hs = slice(kv * HEAD_DIM, (kv + 1) * HEAD_DIM)
            qg = _stack_heads(qi, kv, t)
            ck = ck_ref[i][:, hs].astype(BF16)
            cv = cv_ref[i][:, hs].astype(BF16)
            kn = k_s[rows, hs].astype(BF16)
            vnew = v_s[rows, hs].astype(BF16)
            o = _sink_softmax_pv([_dot_nt(qg, ck), _dot_nt(qg, kn)], [cv, vnew], _sink_column(sinks_ref, kv, t))
            per_kv.append(_unstack_heads(o, t))
        attn_s[rows, :] = jnp.concatenate(per_kv, axis=1)
        st_s[...] = h0_ref[i].T
        _ssd_chunk(r0, t, xact, dt_s, st_s, cumt_s, y_s, a_row, e_ref, dskip_e)
        sto_ref[i] = st_s[...].T
        gm_s[rows, :] = _gmlp_chunk(vn_s[rows, :], u_s[rows, :], gws_ref, gbst_ref, t)
        return carry

    lax.fori_loop(0, nb, seq_body, 0)

    a_out = _dot(attn_s[...].astype(BF16), wao_ref[...])
    z = _proj(h, win_ref, C_Z)
    ssm = _rms(y_s[...] * _silu(z), ssmnw_ref[...]).astype(BF16)
    b_out = _dot(ssm, wso_ref[...])
    c_out = _dot(gm_s[...].astype(BF16), wgo_ref[...])
    gates = [_sigmoid(_proj(h, win_ref, cols)) for cols in (C_G0, C_G1, C_G2)]
    xo_ref[...] = _merge_out(x3, mod3, gates, a_out, b_out, c_out, wout_ref)


def _ffn_kernel(x_ref, mod_ref, gff_ref, w1_ref, w2_ref, gfin_ref, o_ref, *, final_norm):
    x3 = x_ref[...]
    mod3 = mod_ref[...]
    nb, t, d = x3.shape
    h3 = _rms(x3, gff_ref[...]) * (1.0 + mod3[:, 4:5, :]) + mod3[:, 3:4, :]
    h = h3.reshape(nb * t, d).astype(BF16)
    a = jnp.maximum(_dot(h, w1_ref[...]), 0.0)
    y = _dot((a * a).astype(BF16), w2_ref[...])
    out = x3 + mod3[:, 5:6, :] * y.reshape(nb, t, d)
    if final_norm:
        out = _rms(out, gfin_ref[...])
    o_ref[...] = out


def _ada_kernel(c_ref, w_ref, b_ref, op_ref, os_ref):
    mod = _dot(_silu(c_ref[...]).astype(BF16), w_ref[...].astype(BF16)) + b_ref[...]
    n_prompt = op_ref.shape[0]
    op_ref[...] = mod[0:n_prompt]
    os_ref[...] = mod[n_prompt:]


def _with_ignored_inputs(body, n_in, n_ignored):
    if n_ignored == 0:
        return body
    return lambda *refs: body(*refs[:n_in], *refs[n_in + n_ignored:])


def _const_spec(shape):
    return pl.BlockSpec(shape, lambda *_: (0,) * len(shape), pipeline_mode=pl.Buffered(1))


def _layer_spec(l, shape):
    return pl.BlockSpec((None,) + shape, lambda *_: (l,) + (0,) * len(shape), pipeline_mode=pl.Buffered(1))


def _any_spec():
    return pl.BlockSpec(memory_space=pl.ANY)


def _smem_spec():
    return pl.BlockSpec(memory_space=pltpu.SMEM)


def _params(n_grid):
    return pltpu.CompilerParams(dimension_semantics=("arbitrary",) * n_grid, vmem_limit_bytes=VMEM_LIMIT)


def _ada_call(c_prompt, c_sample, w_ada, b_ada):
    n_p, n_s = c_prompt.shape[0], c_sample.shape[0]
    c_all = jnp.concatenate([c_prompt, c_sample], axis=0)
    tn = D_MODEL
    n_tiles = w_ada.shape[2] // tn
    mod_p, mod_s = pl.pallas_call(
        _ada_kernel,
        grid=(DEPTH, n_tiles),
        in_specs=[pl.BlockSpec((n_p + n_s, D_MODEL), lambda l, n: (0, 0)),
                  pl.BlockSpec((None, D_MODEL, tn), lambda l, n: (l, 0, n)),
                  pl.BlockSpec((None, 1, tn), lambda l, n: (l, 0, n))],
        out_specs=(pl.BlockSpec((None, n_p, tn), lambda l, n: (l, 0, n)),
                   pl.BlockSpec((None, n_s, tn), lambda l, n: (l, 0, n))),
        out_shape=(jax.ShapeDtypeStruct((DEPTH, n_p, w_ada.shape[2]), F32),
                   jax.ShapeDtypeStruct((DEPTH, n_s, w_ada.shape[2]), F32)),
        compiler_params=_params(2),
        name="ada_mod",
    )(c_all, w_ada, b_ada.reshape(DEPTH, 1, -1))
    return mod_p.reshape(DEPTH, n_p, 6, D_MODEL), mod_s.reshape(DEPTH, n_s, 6, D_MODEL)


def _w_in_specs(l):
    return [_layer_spec(l, (D_MODEL, N_IN_A)), _layer_spec(l, (D_MODEL, N_IN_B)), _layer_spec(l, (D_MODEL, LANES))]


def _layer_weight_specs(l):
    return [
        _layer_spec(l, (CONV_WIDTH, CONV_DIM)), _layer_spec(l, (1, CONV_DIM)), _layer_spec(l, (1, LANES)),
        _layer_spec(l, (1, LANES)), _layer_spec(l, (1, SSM_INNER)), _layer_spec(l, (1, SSM_INNER)),
        _layer_spec(l, (1, GM_WIDTH)), _layer_spec(l, (1, GM_WIDTH)),
        _layer_spec(l, (GM_GROUPS, GM_CHUNK, GM_CHUNK)), _layer_spec(l, (GM_CHUNK, GM_GROUPS)),
        _layer_spec(l, (Q_W, D_MODEL)), _layer_spec(l, (SSM_INNER, D_MODEL)), _layer_spec(l, (GM_WIDTH, D_MODEL)),
        _layer_spec(l, (D_MODEL, D_MODEL)), _const_spec((LANES, SSM_INNER)),
    ]


def _layer_weight_args(p):
    return (p["conv_w"], p["conv_b"], p["dt_bias"], p["a_log"], p["d_skip"], p["ssm_norm_w"], p["gm_ln_g"],
            p["gm_ln_b"], p["gm_w_s"], p["gm_b_st"], p["w_attn_o"], p["w_ssm_o"], p["w_gm_o"], p["w_out"], p["expand"])


def _prompt_mixer(l, x, mod, p, rope, prev):
    bsz, seq, d = x.shape
    ts = PROMPT_TS
    tab = pl.BlockSpec((ts, LANES), lambda b, s: (s, 0))
    in_specs = [
        pl.BlockSpec((1, ts, d), lambda b, s: (b, s, 0)),
        pl.BlockSpec((None, 1, 6, d), lambda b, s: (l, b, 0, 0)),
        _layer_spec(l, (1, d)), *_w_in_specs(l),
        tab, tab, tab, _smem_spec(),
    ] + _layer_weight_specs(l)
    args = (x, mod, p["g_mix"], *p["w_in"], *rope, p["sinks"][l], *_layer_weight_args(p))
    state_tails = ((WINDOW, KV_W), (WINDOW, KV_W), (SSM_INNER, SSM_STATE), (CONV_WIDTH - 1, CONV_DIM))
    out_shape = (jax.ShapeDtypeStruct((bsz, seq, d), F32),) + tuple(
        jax.ShapeDtypeStruct((DEPTH, bsz) + tail, F32) for tail in state_tails)
    out_specs = (pl.BlockSpec((1, ts, d), lambda b, s: (b, s, 0)),) + tuple(
        pl.BlockSpec((None, 1) + tail, lambda b, s: (l, b, 0, 0)) for tail in state_tails)
    scratch = [
        pltpu.VMEM((WINDOW + ts, KV_W), F32), pltpu.VMEM((WINDOW + ts, KV_W), F32),
        pltpu.VMEM((8 + ts, CONV_DIM), F32), pltpu.VMEM((SSM_STATE, SSM_INNER), F32),
    ]
    n_in = len(args)
    prev = () if prev is None else tuple(prev)
    return pl.pallas_call(
        _with_ignored_inputs(_prompt_kernel, n_in, len(prev)),
        grid=(bsz, seq // ts), in_specs=in_specs + [_any_spec()] * len(prev), out_specs=out_specs,
        out_shape=out_shape, scratch_shapes=scratch, compiler_params=_params(2), name="prompt_mixer",
        input_output_aliases={n_in + i: 1 + i for i in range(len(prev))},
    )(*args, *prev)


def _sample_mixer(l, x, mod, cache_k, cache_v, h0, conv_state, p, rope, prev):
    bsz, t, d = x.shape
    nb = SAMPLE_NB
    m = nb * t

    def blk(*tail):
        return pl.BlockSpec((nb,) + tail, lambda i: (i,) + (0,) * len(tail))

    def lblk(*tail):
        return pl.BlockSpec((None, nb) + tail, lambda i: (l, i) + (0,) * len(tail))

    in_specs = [
        blk(t, d), lblk(6, d), _layer_spec(l, (1, d)), *_w_in_specs(l),
        _const_spec((m, LANES)), _const_spec((m, LANES)), _const_spec((m, LANES)), _smem_spec(),
        lblk(WINDOW, KV_W), lblk(WINDOW, KV_W), lblk(SSM_INNER, SSM_STATE), lblk(CONV_WIDTH - 1, CONV_DIM),
    ] + _layer_weight_specs(l)
    args = (x, mod, p["g_mix"], *p["w_in"], *rope, p["sinks"][l], cache_k, cache_v, h0, conv_state,
            *_layer_weight_args(p))
    state_tails = ((t, KV_W), (t, KV_W), (SSM_INNER, SSM_STATE), (CONV_WIDTH - 1, CONV_DIM), (t, GM_WIDTH))
    out_shape = (jax.ShapeDtypeStruct((bsz, t, d), F32),) + tuple(
        jax.ShapeDtypeStruct((DEPTH, bsz) + tail, F32) for tail in state_tails)
    out_specs = (blk(t, d),) + tuple(lblk(*tail) for tail in state_tails)
    scratch = [
        pltpu.VMEM((m, Q_W), F32), pltpu.VMEM((m, KV_W), F32), pltpu.VMEM((m, KV_W), F32),
        pltpu.VMEM((nb, 8 + t, CONV_DIM), F32), pltpu.VMEM((m, CONV_DIM), F32),
        pltpu.VMEM((m, LANES), F32), pltpu.VMEM((LANES, t), F32),
        pltpu.VMEM((SSM_STATE, SSM_INNER), F32), pltpu.VMEM((m, SSM_INNER), F32),
        pltpu.VMEM((m, Q_W), F32), pltpu.VMEM((m, GM_WIDTH), F32), pltpu.VMEM((m, GM_WIDTH), F32),
        pltpu.VMEM((m, GM_WIDTH), F32),
    ]
    n_in = len(args)
    prev = () if prev is None else tuple(prev)
    return pl.pallas_call(
        _with_ignored_inputs(_sample_kernel, n_in, len(prev)),
        grid=(bsz // nb,), in_specs=in_specs + [_any_spec()] * len(prev), out_specs=out_specs,
        out_shape=out_shape, scratch_shapes=scratch, compiler_params=_params(1), name="sample_mixer",
        input_output_aliases={n_in + i: 1 + i for i in range(len(prev))},
    )(*args, *prev)


def _ffn(l, x, mod, p, g_final, final_norm, name):
    bsz, t, d = x.shape
    if t >= FFN_ROWS:
        nb, tt = 1, FFN_ROWS
    else:
        nb, tt = FFN_ROWS // t, t
    grid = (bsz // nb, t // tt)
    return pl.pallas_call(
        functools.partial(_ffn_kernel, final_norm=final_norm),
        grid=grid,
        in_specs=[pl.BlockSpec((nb, tt, d), lambda b, s: (b, s, 0)),
                  pl.BlockSpec((None, nb, 6, d), lambda b, s: (l, b, 0, 0)),
                  _layer_spec(l, (1, d)), _layer_spec(l, (d, D_FF)), _layer_spec(l, (D_FF, d)), _const_spec((1, d))],
        out_specs=pl.BlockSpec((nb, tt, d), lambda b, s: (b, s, 0)),
        out_shape=jax.ShapeDtypeStruct((bsz, t, d), F32),
        compiler_params=_params(2), name=name,
    )(x, mod, p["g_ff"], p["w_ff1"], p["w_ff2"], g_final)


def _rope_tables(pos):
    half = ROT_DIM // 2
    inv_freq = ROPE_THETA ** (-jnp.arange(half, dtype=F32) * (2.0 / ROT_DIM))
    ang = pos.astype(F32)[:, None] * inv_freq[None, :]
    cos, sin = jnp.cos(ang), jnp.sin(ang)
    n = pos.shape[0]
    ones = jnp.ones((n, HEAD_DIM - ROT_DIM), F32)
    zeros = jnp.zeros((n, HEAD_DIM - ROT_DIM), F32)
    zh = jnp.zeros((n, half), F32)
    cos_t = jnp.concatenate([cos, cos, ones], axis=1)
    sin_a = jnp.concatenate([-sin, zh, zeros], axis=1)
    sin_b = jnp.concatenate([zh, sin, zeros], axis=1)
    rep = LANES // HEAD_DIM
    return tuple(jnp.tile(a, (1, rep)) for a in (cos_t, sin_a, sin_b))


def _stacked_params(w_in, g_mix, sinks, conv_w, conv_b, dt_bias, a_log, d_skip, ssm_norm_w, gm_ln_g, gm_ln_b,
                    gm_w_s, gm_b_s, w_attn_o, w_ssm_o, w_gm_o, w_out, g_ff, w_ff1, w_ff2):
    w_r = (w_in[:, :, :N_IN_A].astype(BF16), w_in[:, :, N_IN_A + SSM_HEADS:].astype(BF16),
           jnp.pad(w_in[:, :, N_IN_A:N_IN_A + SSM_HEADS], ((0, 0), (0, 0), (0, LANES - SSM_HEADS))).astype(BF16))
    pad = jnp.zeros((DEPTH, LANES - SSM_HEADS), F32)
    expand = (jnp.arange(SSM_INNER)[None, :] // SSM_HEAD_DIM == jnp.arange(LANES)[:, None]).astype(BF16)

    def row(a):
        return a[:, None, :]

    return {
        "w_in": w_r, "g_mix": row(g_mix), "sinks": sinks,
        "conv_w": conv_w, "conv_b": row(conv_b),
        "dt_bias": row(jnp.concatenate([dt_bias, pad], axis=1)), "a_log": row(jnp.concatenate([a_log, pad], axis=1)),
        "d_skip": row(jnp.repeat(d_skip, SSM_HEAD_DIM, axis=1)), "ssm_norm_w": row(ssm_norm_w),
        "gm_ln_g": row(gm_ln_g), "gm_ln_b": row(gm_ln_b),
        "gm_w_s": gm_w_s, "gm_b_st": jnp.swapaxes(gm_b_s, 1, 2),
        "w_attn_o": w_attn_o.astype(BF16), "w_ssm_o": w_ssm_o.astype(BF16),
        "w_gm_o": w_gm_o.astype(BF16), "w_out": w_out.astype(BF16), "expand": expand,
        "g_ff": row(g_ff), "w_ff1": w_ff1.astype(BF16), "w_ff2": w_ff2.astype(BF16),
    }


def kernel(x_prompt, x_sample, c_prompt, c_sample, cache_attn_k, cache_attn_v, state_ssm, state_conv, w_ada, b_ada, g_mix, w_in, sinks, conv_w, conv_b, dt_bias, a_log, d_skip, ssm_norm_w, gm_ln_g, gm_ln_b, gm_w_s, gm_b_s, w_attn_o, w_ssm_o, w_gm_o, w_out, g_ff, w_ff1, w_ff2, g_final):
    bp, sp, d = x_prompt.shape
    bs, ss, _ = x_sample.shape
    mod_p, mod_s = _ada_call(c_prompt, c_sample, w_ada, b_ada)
    rope_p = _rope_tables(jnp.arange(sp))
    rope_s = tuple(jnp.tile(a, (SAMPLE_NB, 1)) for a in _rope_tables(PAST_LEN + jnp.arange(ss)))
    g_fin = g_final[None]
    p = _stacked_params(w_in, g_mix, sinks, conv_w, conv_b, dt_bias, a_log, d_skip, ssm_norm_w, gm_ln_g,
                        gm_ln_b, gm_w_s, gm_b_s, w_attn_o, w_ssm_o, w_gm_o, w_out, g_ff, w_ff1, w_ff2)
    ck = cache_attn_k.reshape(DEPTH, bs, WINDOW, KV_W)
    cv = cache_attn_v.reshape(DEPTH, bs, WINDOW, KV_W)
    h0 = state_ssm.reshape(DEPTH, bs, SSM_INNER, SSM_STATE)
    xp, xs = x_prompt, x_sample
    state_p = state_s = None
    for l in range(DEPTH):
        final = l == DEPTH - 1
        xp, *state_p = _prompt_mixer(l, xp, mod_p, p, rope_p, state_p)
        xp = _ffn(l, xp, mod_p, p, g_fin, final, "prompt_ffn")
        xs, *state_s = _sample_mixer(l, xs, mod_s, ck, cv, h0, state_conv, p, rope_s, state_s)
        xs = _ffn(l, xs, mod_s, p, g_fin, final, "sample_ffn")
    kp, vp, ssm_p, conv_p = state_p
    ks, vs, ssm_s, conv_s, gv_s = state_s
    return (xp, xs,
            kp.reshape(DEPTH, bp, WINDOW, N_KV_HEADS, HEAD_DIM), vp.reshape(DEPTH, bp, WINDOW, N_KV_HEADS, HEAD_DIM),
            ssm_p.reshape(DEPTH, bp, SSM_HEADS, SSM_HEAD_DIM, SSM_STATE), conv_p,
            ks.reshape(DEPTH, bs, ss, N_KV_HEADS, HEAD_DIM), vs.reshape(DEPTH, bs, ss, N_KV_HEADS, HEAD_DIM),
            ssm_s.reshape(DEPTH, bs, SSM_HEADS, SSM_HEAD_DIM, SSM_STATE), conv_s, gv_s)
```

```python
import functools

import jax
import jax.numpy as jnp
from jax import lax
from jax.experimental import pallas as pl
from jax.experimental.pallas import tpu as pltpu

F32 = jnp.float32
BF16 = jnp.bfloat16

D_MODEL = 1024
DEPTH = 2
CHUNK = 64
N_HEADS = 8
N_KV_HEADS = 2
HEAD_DIM = 64
GQA_GROUP = N_HEADS // N_KV_HEADS
ROT_DIM = HEAD_DIM // 4
ROPE_THETA = 500000.0
WINDOW = 128
SSM_HEADS = 16
SSM_HEAD_DIM = 64
SSM_INNER = SSM_HEADS * SSM_HEAD_DIM
SSM_GROUPS = 2
SSM_STATE = 128
SSM_CHUNK = 64
CONV_WIDTH = 4
CONV_DIM = SSM_INNER + 2 * SSM_GROUPS * SSM_STATE
GM_WIDTH = 512
GM_GROUPS = 4
GM_GROUP_DIM = GM_WIDTH // GM_GROUPS
GM_CHUNK = 128
D_FF = 4 * D_MODEL
Q_W = N_HEADS * HEAD_DIM
KV_W = N_KV_HEADS * HEAD_DIM
PAST_LEN = 4096
EPS = 1e-6

LANES = 128
HEADS_PER_GROUP = SSM_HEADS // SSM_GROUPS
GROUP_INNER = HEADS_PER_GROUP * SSM_HEAD_DIM

C_Q = (0, 0, Q_W)
C_K = (0, C_Q[2], C_Q[2] + KV_W)
C_V = (0, C_K[2], C_K[2] + KV_W)
C_Z = (0, C_V[2], C_V[2] + SSM_INNER)
C_XBC = (0, C_Z[2], C_Z[2] + CONV_DIM)
N_IN_A = C_XBC[2]
C_GU = (1, 0, GM_WIDTH)
C_GV = (1, C_GU[2], C_GU[2] + GM_WIDTH)
C_G0 = (1, C_GV[2], C_GV[2] + D_MODEL)
C_G1 = (1, C_G0[2], C_G0[2] + D_MODEL)
C_G2 = (1, C_G1[2], C_G1[2] + D_MODEL)
N_IN_B = C_G2[2]
C_DT = (2, 0, LANES)

PROMPT_TS = 256
PROJ_BLOCK = 256
SAMPLE_NB = 8
FFN_ROWS = 512
W_SPLIT_ROWS = 256
VMEM_LIMIT = 56 * 1024 * 1024


def _log2(n):
    assert n & (n - 1) == 0, n
    return n.bit_length() - 1


def _dot(a, b):
    return jnp.dot(a, b, preferred_element_type=F32)


def _dot_nt(a, b):
    return lax.dot_general(a, b, (((1,), (1,)), ((), ())), preferred_element_type=F32)


def _dot_tn(a, b):
    return lax.dot_general(a, b, (((0,), (0,)), ((), ())), preferred_element_type=F32)


def _split3(x):
    hi = x.astype(BF16)
    r = x - hi.astype(F32)
    mid = r.astype(BF16)
    lo = (r - mid.astype(F32)).astype(BF16)
    return hi, mid, lo


def _sigmoid(x):
    return 0.5 * (1.0 + jnp.tanh(0.5 * x))


def _silu(x):
    return x * _sigmoid(x)


def _silu_of_half(xh):
    return xh * (1.0 + jnp.tanh(xh))


def _twice_sigmoid_of_half(xh):
    return 1.0 + jnp.tanh(xh)


def _gelu(x):
    return 0.5 * x * (1.0 + jnp.tanh(0.7978845608028654 * (x + 0.044715 * (x * x * x))))


def _softplus(x):
    return jnp.maximum(x, 0.0) + jnp.log1p(jnp.exp(-jnp.abs(x)))


def _rms(x, g):
    return x * lax.rsqrt(jnp.mean(x * x, axis=-1, keepdims=True) + EPS) * g


def _rope(x, cos, sin_a, sin_b):
    return x * cos + pltpu.roll(x, LANES - ROT_DIM // 2, 1) * sin_a + pltpu.roll(x, ROT_DIM // 2, 1) * sin_b


def _proj(h, win_refs, cols):
    return _dot(h, win_refs[cols[0]][:, cols[1]:cols[2]])


def _modnorm(x3, g, mod3):
    nb, t, d = x3.shape
    h3 = _rms(x3, g) * (1.0 + mod3[:, 1:2, :]) + mod3[:, 0:1, :]
    return h3.reshape(nb * t, d).astype(BF16)


def _sink_row(sinks_ref, kv, cols_per_head):
    c = lax.broadcasted_iota(jnp.int32, (1, GQA_GROUP * cols_per_head), 1)
    row = jnp.full((1, GQA_GROUP * cols_per_head), sinks_ref[kv * GQA_GROUP], F32)
    for i in range(1, GQA_GROUP):
        row = jnp.where(c >= i * cols_per_head, sinks_ref[kv * GQA_GROUP + i], row)
    return row


def _stack_heads(q, kv, rows):
    return jnp.concatenate(
        [q[:, (kv * GQA_GROUP + i) * HEAD_DIM:(kv * GQA_GROUP + i + 1) * HEAD_DIM] for i in range(GQA_GROUP)], axis=0)


def _unstack_heads(o, rows):
    return jnp.concatenate([o[i * rows:(i + 1) * rows, :] for i in range(GQA_GROUP)], axis=1)


def _gmlp_chunk(vn, u, gws_ref, gbst_ref, L):
    ri = lax.broadcasted_iota(jnp.int32, (L, L), 0)
    ci = lax.broadcasted_iota(jnp.int32, (L, L), 1)
    outs = []
    for g in range(GM_GROUPS):
        w = jnp.where(ri >= ci, gws_ref[g, 0:L, 0:L], 0.0).astype(BF16)
        v_g = vn[:, g * GM_GROUP_DIM:(g + 1) * GM_GROUP_DIM].astype(BF16)
        outs.append(_dot(w, v_g) + gbst_ref[0:L, g:g + 1])
    return u * jnp.concatenate(outs, axis=1)


def _layer_norm(x, g, b):
    xc = x - jnp.mean(x, axis=-1, keepdims=True)
    return xc * lax.rsqrt(jnp.mean(xc * xc, axis=-1, keepdims=True) + EPS) * g + b


def _merge_out(x3, mod3, gates, a, b, c, wout_ref):
    merged = gates[0] * a + gates[1] * b + gates[2] * c
    o = _dot(merged.astype(BF16), wout_ref[...])
    nb, t, d = x3.shape
    return x3 + mod3[:, 2:3, :] * o.reshape(nb, t, d)


def _attn_groups(ts):
    return [(c, kv) for c in range(ts // CHUNK) for kv in range(N_KV_HEADS)]


def _attn_scores(q, khist, pos0):
    ts = q.shape[0]
    n_keys = WINDOW + CHUNK
    key_i = lax.broadcasted_iota(jnp.int32, (n_keys, GQA_GROUP * CHUNK), 0)
    scores = []
    for c, kv in _attn_groups(ts):
        k_g = khist[c * CHUNK:c * CHUNK + n_keys, kv * HEAD_DIM:(kv + 1) * HEAD_DIM].astype(BF16)
        sc = _dot_nt(k_g, _stack_heads(q[c * CHUNK:(c + 1) * CHUNK, :], kv, CHUNK))
        if c * CHUNK < WINDOW:
            sc = jnp.where(key_i >= WINDOW - c * CHUNK - pos0, sc, -jnp.inf)
        scores.append(sc)
    return scores


def _attn_probs(scores, sinks_ref, ts):
    sink_rows = [_sink_row(sinks_ref, kv, CHUNK) for kv in range(N_KV_HEADS)]
    probs = []
    for (c, kv), sc in zip(_attn_groups(ts), scores):
        m = jnp.maximum(jnp.max(sc, axis=0, keepdims=True), sink_rows[kv])
        p = jnp.exp(sc - m)
        denom = jnp.sum(p, axis=0, keepdims=True) + jnp.exp(sink_rows[kv] - m)
        probs.append((p * (1.0 / denom)).astype(BF16))
    return probs


def _attn_values(probs, vhist, ts):
    n_keys = WINDOW + CHUNK
    outs = []
    for (c, kv), p in zip(_attn_groups(ts), probs):
        v_g = vhist[c * CHUNK:c * CHUNK + n_keys, kv * HEAD_DIM:(kv + 1) * HEAD_DIM].astype(BF16)
        outs.append(_unstack_heads(_dot_tn(p, v_g), CHUNK))
    rows = [jnp.concatenate(outs[c * N_KV_HEADS:(c + 1) * N_KV_HEADS], axis=1) for c in range(ts // CHUNK)]
    return jnp.concatenate(rows, axis=0).astype(BF16)


def _ssd_cumsum(dt, a_row, e):
    ts = dt.shape[0]
    L = SSM_CHUNK
    ri = lax.broadcasted_iota(jnp.int32, (ts, ts), 0)
    ci = lax.broadcasted_iota(jnp.int32, (ts, ts), 1)
    lag = ri - ci
    tri = jnp.where((lag >= 0) & (lag <= (ri & (L - 1))), 1.0, 0.0).astype(BF16)
    hi, mid, lo = _split3(dt * a_row)
    cum = _dot(tri, hi) + _dot(tri, mid) + _dot(tri, lo)
    hi, mid, lo = _split3(cum)
    cum_e = _dot(hi, e) + _dot(mid, e) + _dot(lo, e)
    dt_hi = dt.astype(BF16)
    dt_lo = (dt - dt_hi.astype(F32)).astype(BF16)
    dt_e = _dot(dt_hi, e) + _dot(dt_lo, e)
    return cum_e, dt_e


def _ssd_decays(xact, cum_e, dt_e):
    ts = xact.shape[0]
    L = SSM_CHUNK
    nc = ts // L
    xs = xact[:, 0:SSM_INNER]
    cum_e = cum_e.reshape(nc, L, SSM_INNER)
    t3 = lax.broadcasted_iota(jnp.int32, cum_e.shape, 1)
    slot3 = lax.broadcasted_iota(jnp.int32, cum_e.shape, 2) & (SSM_HEAD_DIM - 1)
    cum_src = jnp.sum(jnp.where(t3 == slot3, cum_e, 0.0), axis=1, keepdims=True)
    decay = jnp.exp(jnp.where(t3 >= slot3, cum_e - cum_src, -jnp.inf))
    last = cum_e[:, L - 1:L, :]
    exp_cum = jnp.exp(cum_e)
    chunk_decay = jnp.exp(last)
    xdt = xs * dt_e
    wx = (xdt.reshape(nc, L, SSM_INNER) * jnp.exp(last - cum_e)).astype(BF16)
    first_head = (lax.broadcasted_iota(jnp.int32, xdt.shape, 1) & (2 * SSM_HEAD_DIM - 1)) < SSM_HEAD_DIM
    return {
        "xs": xs,
        "bm": xact[:, SSM_INNER:SSM_INNER + SSM_GROUPS * SSM_STATE].astype(BF16),
        "cm": xact[:, SSM_INNER + SSM_GROUPS * SSM_STATE:CONV_DIM].astype(BF16),
        "decay": decay, "exp_cum": exp_cum, "chunk_decay": chunk_decay, "wx": wx,
        "xdt_a": jnp.where(first_head, xdt, 0.0).astype(BF16),
        "xdt_b": jnp.where(first_head, 0.0, xdt).astype(BF16),
    }


def _ssd_chunk_step(c, ctx, st, dskip_e):
    L = SSM_CHUNK
    rows = slice(c * L, (c + 1) * L)
    bm, cm = ctx["bm"], ctx["cm"]
    cb = [_dot_nt(cm[rows, g * SSM_STATE:(g + 1) * SSM_STATE],
                  jnp.concatenate([bm[rows, g * SSM_STATE:(g + 1) * SSM_STATE]] * 2, axis=0))
          for g in range(SSM_GROUPS)]
    y_intra = []
    for k in range(SSM_HEADS // 2):
        ls = slice(k * 2 * SSM_HEAD_DIM, (k + 1) * 2 * SSM_HEAD_DIM)
        m_pair = (cb[2 * k // HEADS_PER_GROUP] * ctx["decay"][c][:, ls]).astype(BF16)
        block_diag = jnp.concatenate([ctx["xdt_a"][rows, ls], ctx["xdt_b"][rows, ls]], axis=0)
        y_intra.append(_dot(m_pair, block_diag))
    st_b = st.astype(BF16)
    y_inter = jnp.concatenate(
        [_dot(cm[rows, g * SSM_STATE:(g + 1) * SSM_STATE], st_b[:, g * GROUP_INNER:(g + 1) * GROUP_INNER])
         for g in range(SSM_GROUPS)], axis=1)
    y = jnp.concatenate(y_intra, axis=1) + y_inter * ctx["exp_cum"][c] + dskip_e * ctx["xs"][rows]
    st = st * ctx["chunk_decay"][c] + jnp.concatenate(
        [_dot_tn(bm[rows, g * SSM_STATE:(g + 1) * SSM_STATE], ctx["wx"][c][:, g * GROUP_INNER:(g + 1) * GROUP_INNER])
         for g in range(SSM_GROUPS)], axis=1)
    return y, st


class _Stagger:
    def __init__(self):
        self._pending = None

    def push(self, produce, consume):
        val = produce()
        self.flush()
        self._pending = (consume, val)

    def flush(self):
        if self._pending is not None:
            consume, val = self._pending
            self._pending = None
            consume(val)


def _prompt_kernel(x_ref, mod_ref, gmix_ref, wa_ref, wb_ref, wdt_ref, cos_ref, sa_ref, sb_ref, sinks_ref,
                   convw_ref, convb_ref, dtb_ref, alog_ref, dskip_ref, ssmnw_ref, glng_ref, glnb_ref,
                   gws_ref, gbst_ref, wao_ref, wso_ref, wgo_ref, wout_ref, e_ref,
                   xo_ref, ko_ref, vo_ref, sto_ref, cvo_ref,
                   khist, vhist, xp, st_s):
    ts = PROMPT_TS
    s = pl.program_id(1)
    last = pl.num_programs(1) - 1

    @pl.when(s == 0)
    def _():
        khist[0:WINDOW, :] = jnp.zeros((WINDOW, KV_W), F32)
        vhist[0:WINDOW, :] = jnp.zeros((WINDOW, KV_W), F32)
        xp[...] = jnp.zeros_like(xp)
        st_s[...] = jnp.zeros_like(st_s)

    x3 = x_ref[...]
    mod3 = mod_ref[...]
    h = _modnorm(x3, gmix_ref[...], mod3)

    cos, sa, sb = cos_ref[...], sa_ref[...], sb_ref[...]
    res = {}

    def rope_block(raw):
        return [_rope(raw[:, i * LANES:(i + 1) * LANES], cos, sa, sb) for i in range(raw.shape[1] // LANES)]

    def q_block(raw):
        res.setdefault("q", []).extend(p.astype(BF16) for p in rope_block(raw * (HEAD_DIM ** -0.5)))

    def kv_block(raw):
        khist[WINDOW:WINDOW + ts, :] = rope_block(raw[:, 0:KV_W])[0]
        vhist[WINDOW:WINDOW + ts, :] = raw[:, KV_W:2 * KV_W]

    def conv_block(lo, raw):
        cs = slice(lo, lo + PROJ_BLOCK)
        ext = jnp.concatenate([xp[:, cs], raw], axis=0)
        acc = convb_ref[:, cs] + raw * convw_ref[CONV_WIDTH - 1:CONV_WIDTH, cs]
        for j in range(1, CONV_WIDTH):
            acc = acc + pltpu.roll(ext, j, 0)[8:8 + ts] * convw_ref[CONV_WIDTH - 1 - j:CONV_WIDTH - j, cs]
        xp[:, cs] = raw[ts - 8:ts]
        res.setdefault("xact", []).append(_silu_of_half(acc))

    def collect(key, fn):
        return lambda raw: res.setdefault(key, []).append(fn(raw))

    wrefs = (wa_ref, wb_ref, wdt_ref)
    pipe = _Stagger()

    def push_proj(cols, off, width, consume):
        ref, lo = wrefs[cols[0]], cols[1] + off
        pipe.push(lambda: _dot(h, ref[:, lo:lo + width]), consume)

    def blocks(cols, consume):
        return [functools.partial(push_proj, cols, off, PROJ_BLOCK, consume)
                for off in range(0, cols[2] - cols[1], PROJ_BLOCK)]

    conv_blocks = [functools.partial(push_proj, C_XBC, off, PROJ_BLOCK, functools.partial(conv_block, off))
                   for off in range(0, CONV_DIM, PROJ_BLOCK)]
    gate_blocks = iter(blocks(C_G0, collect("g0", _twice_sigmoid_of_half))
                       + blocks(C_G1, collect("g1", _twice_sigmoid_of_half))
                       + blocks(C_G2, collect("g2", _twice_sigmoid_of_half)))
    z_blocks = blocks(C_Z, collect("sz", _silu_of_half))
    gm_blocks = blocks(C_GU, collect("u", _gelu)) + blocks(C_GV, collect("gv", _gelu))

    def cat(key):
        return jnp.concatenate(res[key], axis=1)

    for blk in blocks(C_Q, q_block):
        blk()
    push_proj(C_K, 0, 2 * KV_W, kv_block)
    conv_blocks[0]()
    next(gate_blocks)()
    conv_blocks[1]()
    scores = _attn_scores(cat("q"), khist, s * ts)
    conv_blocks[2]()
    probs = _attn_probs(scores, sinks_ref, ts)
    next(gate_blocks)()
    conv_blocks[3]()
    attn = _attn_values(probs, vhist, ts)
    conv_blocks[4]()
    next(gate_blocks)()
    conv_blocks[5]()
    push_proj(C_DT, 0, LANES, collect("dt", lambda raw: _softplus(raw + dtb_ref[...])))
    next(gate_blocks)()
    a_out = _dot(attn, wao_ref[...])
    z_blocks[0]()
    cum_e, dt_e = _ssd_cumsum(res["dt"][0], -jnp.exp(alog_ref[...]), e_ref[...])
    z_blocks[1]()
    merged = cat("g0") * a_out
    ctx = _ssd_decays(cat("xact"), cum_e, dt_e)
    z_blocks[2]()
    z_blocks[3]()
    st = st_s[...]
    ys = []
    for c in range(ts // SSM_CHUNK):
        y_c, st = _ssd_chunk_step(c, ctx, st, dskip_ref[...])
        ys.append(y_c)
        gm_blocks[c]()
    st_s[...] = st
    next(gate_blocks)()
    vn = _layer_norm(cat("gv"), glng_ref[...], glnb_ref[...])
    u = cat("u")
    next(gate_blocks)()
    ssm = _rms(jnp.concatenate(ys, axis=0) * cat("sz"), ssmnw_ref[...]).astype(BF16)
    gm = jnp.concatenate(
        [_gmlp_chunk(vn[c * GM_CHUNK:(c + 1) * GM_CHUNK], u[c * GM_CHUNK:(c + 1) * GM_CHUNK], gws_ref, gbst_ref,
                     GM_CHUNK) for c in range(ts // GM_CHUNK)], axis=0).astype(BF16)
    next(gate_blocks)()
    b_out = _dot(ssm, wso_ref[...])
    next(gate_blocks)()
    pipe.flush()
    merged = merged + cat("g1") * b_out
    c_out = _dot(gm, wgo_ref[...])
    for blk in gate_blocks:
        blk()
    pipe.flush()
    merged = merged + cat("g2") * c_out
    o = _dot(merged.astype(BF16), wout_ref[...])
    xo_ref[...] = x3 + mod3[:, 2:3, :] * o.reshape(x3.shape)

    khist[0:WINDOW, :] = khist[ts:ts + WINDOW, :]
    vhist[0:WINDOW, :] = vhist[ts:ts + WINDOW, :]

    @pl.when(s == last)
    def _():
        ko_ref[0] = khist[ts:ts + WINDOW, :]
        vo_ref[0] = vhist[ts:ts + WINDOW, :]
        cvo_ref[0] = xp[8 - (CONV_WIDTH - 1):8, :]
        sto_ref[0] = st_s[...].T


def _sample_attention(q, k_new, v_new, cache_k, cache_v, sinks_ref):
    t = q.shape[0]
    outs = []
    for kv in range(N_KV_HEADS):
        hs = slice(kv * HEAD_DIM, (kv + 1) * HEAD_DIM)
        qg = _stack_heads(q, kv, t)
        s_c = _dot_nt(cache_k[:, hs].astype(BF16), qg)
        s_n = _dot_nt(k_new[:, hs].astype(BF16), qg)
        sink = _sink_row(sinks_ref, kv, t)
        m = jnp.maximum(jnp.maximum(jnp.max(s_c, axis=0, keepdims=True), jnp.max(s_n, axis=0, keepdims=True)), sink)
        p_c = jnp.exp(s_c - m)
        p_n = jnp.exp(s_n - m)
        inv = 1.0 / (jnp.sum(p_c, axis=0, keepdims=True) + jnp.sum(p_n, axis=0, keepdims=True) + jnp.exp(sink - m))
        o = (_dot_tn((p_c * inv).astype(BF16), cache_v[:, hs].astype(BF16))
             + _dot_tn((p_n * inv).astype(BF16), v_new[:, hs].astype(BF16)))
        outs.append(_unstack_heads(o, t))
    return jnp.concatenate(outs, axis=1)


def _sample_ssd_setup(xact, dt, a_row, e, L):
    m = xact.shape[0]
    n_seq = m // L
    xs = xact[:, 0:SSM_INNER]
    ri = lax.broadcasted_iota(jnp.int32, (m, m), 0)
    lag = ri - lax.broadcasted_iota(jnp.int32, (m, m), 1)
    tri = jnp.where((lag >= 0) & (lag <= (ri & (L - 1))), 1.0, 0.0).astype(BF16)
    hi, mid, lo = _split3(dt * a_row)
    cum = _dot(tri, hi) + _dot(tri, mid) + _dot(tri, lo)
    hi, mid, lo = _split3(cum)
    cum_e = (_dot(hi, e) + _dot(mid, e) + _dot(lo, e)).reshape(n_seq, L, SSM_INNER)
    head_of_lane = jnp.right_shift(lax.broadcasted_iota(jnp.int32, (LANES, SSM_HEADS * L), 1), _log2(L))
    e_slots = jnp.where(head_of_lane == lax.broadcasted_iota(jnp.int32, (LANES, SSM_HEADS * L), 0), 1.0, 0.0)
    e_slots = e_slots.astype(BF16)
    cum_s = (_dot(hi, e_slots) + _dot(mid, e_slots) + _dot(lo, e_slots)).reshape(n_seq, L, SSM_HEADS * L)
    dt_hi = dt.astype(BF16)
    dt_lo = (dt - dt_hi.astype(F32)).astype(BF16)
    dt_e = _dot(dt_hi, e) + _dot(dt_lo, e)
    t3 = lax.broadcasted_iota(jnp.int32, cum_s.shape, 1)
    slot3 = lax.broadcasted_iota(jnp.int32, cum_s.shape, 2) & (L - 1)
    cum_src = jnp.sum(jnp.where(t3 == slot3, cum_s, 0.0), axis=1, keepdims=True)
    decay = jnp.exp(jnp.where(t3 >= slot3, cum_s - cum_src, -jnp.inf))
    last = cum_e[:, L - 1:L, :]
    xdt = xs * dt_e
    wx = (xdt.reshape(n_seq, L, SSM_INNER) * jnp.exp(last - cum_e)).astype(BF16)
    per_head = jnp.exp(cum.reshape(n_seq, L, LANES)[:, L - 1:L, :])
    rep = jnp.broadcast_to(per_head, (n_seq, SSM_HEADS, LANES)).reshape(n_seq * SSM_HEADS, LANES)
    own_lane = (lax.broadcasted_iota(jnp.int32, rep.shape, 1)
                == (lax.broadcasted_iota(jnp.int32, rep.shape, 0) & (SSM_HEADS - 1)))
    hi, mid, lo = _split3(jnp.where(own_lane, rep, 0.0))
    ones = jnp.ones((LANES, LANES), BF16)
    splat = (_dot(hi, ones) + _dot(mid, ones) + _dot(lo, ones)).reshape(n_seq, SSM_HEADS, LANES)
    return {
        "L": L, "xs": xs,
        "bm": xact[:, SSM_INNER:SSM_INNER + SSM_GROUPS * SSM_STATE].astype(BF16),
        "cm": xact[:, SSM_INNER + SSM_GROUPS * SSM_STATE:CONV_DIM].astype(BF16),
        "decay": decay, "exp_cum": jnp.exp(cum_e), "wx": wx, "xdt": xdt.astype(BF16), "state_decay": splat,
    }


def _sample_ssd_seq(i, s, h0, dskip_e):
    L = s["L"]
    rows = slice(i * L, (i + 1) * L)
    h0_b = h0.astype(BF16)
    ri = jnp.right_shift(lax.broadcasted_iota(jnp.int32, (HEADS_PER_GROUP * L, GROUP_INNER), 0), _log2(L))
    li = jnp.right_shift(lax.broadcasted_iota(jnp.int32, (HEADS_PER_GROUP * L, GROUP_INNER), 1), _log2(SSM_HEAD_DIM))
    y_parts, upd = [], []
    for g in range(SSM_GROUPS):
        gn = slice(g * SSM_STATE, (g + 1) * SSM_STATE)
        gp = slice(g * GROUP_INNER, (g + 1) * GROUP_INNER)
        gs = slice(g * HEADS_PER_GROUP * L, (g + 1) * HEADS_PER_GROUP * L)
        bm_g, cm_g = s["bm"][rows, gn], s["cm"][rows, gn]
        cb = _dot_nt(cm_g, jnp.concatenate([bm_g] * HEADS_PER_GROUP, axis=0))
        m_g = (cb * s["decay"][i][:, gs]).astype(BF16)
        tiled = jnp.concatenate([s["xdt"][rows, gp]] * HEADS_PER_GROUP, axis=0)
        block_diag = jnp.where(ri == li, tiled, jnp.zeros_like(tiled))
        y_g = _dot(m_g, block_diag) + _dot_nt(cm_g, h0_b[gp, :]) * s["exp_cum"][i][:, gp]
        y_parts.append(y_g)
        upd.append(_dot_tn(s["wx"][i][:, gp], bm_g))
    y = jnp.concatenate(y_parts, axis=1) + dskip_e * s["xs"][rows]
    decayed = h0.reshape(SSM_HEADS, SSM_HEAD_DIM, SSM_STATE) * s["state_decay"][i][:, None, :]
    return y, decayed.reshape(SSM_INNER, SSM_STATE) + jnp.concatenate(upd, axis=0)


def _sample_kernel(x_ref, mod_ref, gmix_ref, wa_ref, wb_ref, wdt_ref, cos_ref, sa_ref, sb_ref, sinks_ref,
                   ck_ref, cv_ref, h0_ref, cs_ref,
                   convw_ref, convb_ref, dtb_ref, alog_ref, dskip_ref, ssmnw_ref, glng_ref, glnb_ref,
                   gws_ref, gbst_ref, wao_ref, wso_ref, wgo_ref, wout_ref, e_ref,
                   xo_ref, ko_ref, vo_ref, sto_ref, cvo_ref, gvo_ref,
                   xp):
    nb = SAMPLE_NB
    t = x_ref.shape[1]
    m = nb * t
    win_ref = (wa_ref, wb_ref, wdt_ref)
    x3 = x_ref[...]
    mod3 = mod_ref[...]
    h = _modnorm(x3, gmix_ref[...], mod3)

    cos, sa, sb = cos_ref[...], sa_ref[...], sb_ref[...]
    q = _proj(h, win_ref, C_Q) * (HEAD_DIM ** -0.5)
    q = jnp.concatenate(
        [_rope(q[:, i * LANES:(i + 1) * LANES], cos, sa, sb) for i in range(Q_W // LANES)], axis=1).astype(BF16)
    k_new = _rope(_proj(h, win_ref, C_K), cos, sa, sb)
    v_new = _proj(h, win_ref, C_V)
    ko_ref[...] = k_new.reshape(nb, t, KV_W)
    vo_ref[...] = v_new.reshape(nb, t, KV_W)

    xp[:, 8 - (CONV_WIDTH - 1):8, :] = cs_ref[...]
    xp[:, 8:8 + t, :] = _proj(h, win_ref, C_XBC).reshape(nb, t, CONV_DIM)
    acc = convb_ref[...] + xp[:, 8:8 + t, :] * convw_ref[CONV_WIDTH - 1:CONV_WIDTH, :]
    for j in range(1, CONV_WIDTH):
        acc = acc + xp[:, 8 - j:8 - j + t, :] * convw_ref[CONV_WIDTH - 1 - j:CONV_WIDTH - j, :]
    xact = _silu_of_half(acc).reshape(m, CONV_DIM)
    cvo_ref[...] = xp[:, t + 8 - (CONV_WIDTH - 1):t + 8, :]
    dt = _softplus(_proj(h, win_ref, C_DT) + dtb_ref[...])
    dskip_e = dskip_ref[...]

    u = _gelu(_proj(h, win_ref, C_GU))
    vn = _layer_norm(_gelu(_proj(h, win_ref, C_GV)), glng_ref[...], glnb_ref[...])
    gvo_ref[...] = vn.reshape(nb, t, GM_WIDTH)

    ssd = _sample_ssd_setup(xact, dt, -jnp.exp(alog_ref[...]), e_ref[...], t)
    attn_rows, y_rows, gm_rows = [], [], []
    for i in range(nb):
        rows = slice(i * t, (i + 1) * t)
        attn_rows.append(_sample_attention(q[rows], k_new[rows], v_new[rows], ck_ref[i], cv_ref[i], sinks_ref))
        y_i, sto_ref[i] = _sample_ssd_seq(i, ssd, h0_ref[i], dskip_e)
        y_rows.append(y_i)
        gm_rows.append(_gmlp_chunk(vn[rows], u[rows], gws_ref, gbst_ref, t))

    a_out = _dot(jnp.concatenate(attn_rows, axis=0).astype(BF16), wao_ref[...])
    z = _proj(h, win_ref, C_Z)
    ssm = _rms(jnp.concatenate(y_rows, axis=0) * _silu_of_half(z), ssmnw_ref[...]).astype(BF16)
    b_out = _dot(ssm, wso_ref[...])
    c_out = _dot(jnp.concatenate(gm_rows, axis=0).astype(BF16), wgo_ref[...])
    gates = [_twice_sigmoid_of_half(_proj(h, win_ref, cols)) for cols in (C_G0, C_G1, C_G2)]
    xo_ref[...] = _merge_out(x3, mod3, gates, a_out, b_out, c_out, wout_ref)


def _ffn_kernel(x_ref, mod_ref, gff_ref, w1_ref, w2_ref, gfin_ref, o_ref, *, final_norm):
    x3 = x_ref[...]
    mod3 = mod_ref[...]
    nb, t, d = x3.shape
    h3 = _rms(x3, gff_ref[...]) * (1.0 + mod3[:, 4:5, :]) + mod3[:, 3:4, :]
    h = h3.reshape(nb * t, d).astype(BF16)
    a = jnp.maximum(_dot(h, w1_ref[...]), 0.0)
    y = _dot((a * a).astype(BF16), w2_ref[...])
    out = x3 + mod3[:, 5:6, :] * y.reshape(nb, t, d)
    if final_norm:
        out = _rms(out, gfin_ref[...])
    o_ref[...] = out


def _ada_kernel(c_ref, w_ref, b_ref, op_ref, os_ref):
    mod = _dot(_silu(c_ref[...]).astype(BF16), w_ref[...].astype(BF16)) + b_ref[...]
    n_prompt = op_ref.shape[0]
    op_ref[...] = mod[0:n_prompt]
    os_ref[...] = mod[n_prompt:]


def _w_in_split_kernel(w_ref, a_ref, b_ref, dt_ref):
    col_a = lax.broadcasted_iota(jnp.int32, (1, N_IN_A), 1)
    half_a = jnp.where((col_a >= C_Z[1]) & (col_a < C_Z[2]), 0.5, 1.0)
    col_b = lax.broadcasted_iota(jnp.int32, (1, N_IN_B), 1)
    half_b = jnp.where(col_b >= C_G0[1], 0.5, 1.0)
    a_ref[...] = (w_ref[:, 0:N_IN_A] * half_a).astype(BF16)
    b_ref[...] = (w_ref[:, N_IN_A + SSM_HEADS:N_IN_A + SSM_HEADS + N_IN_B] * half_b).astype(BF16)
    dt = w_ref[:, N_IN_A:N_IN_A + SSM_HEADS]
    dt_ref[...] = jnp.concatenate([dt, jnp.zeros((dt.shape[0], LANES - SSM_HEADS), dt.dtype)], axis=1).astype(BF16)


def _w_in_split(w_in):
    rows = W_SPLIT_ROWS
    n_in = w_in.shape[2]
    return pl.pallas_call(
        _w_in_split_kernel,
        grid=(DEPTH, D_MODEL // rows),
        in_specs=[pl.BlockSpec((None, rows, n_in), lambda l, r: (l, r, 0))],
        out_specs=tuple(pl.BlockSpec((None, rows, n), lambda l, r: (l, r, 0)) for n in (N_IN_A, N_IN_B, LANES)),
        out_shape=tuple(jax.ShapeDtypeStruct((DEPTH, D_MODEL, n), BF16) for n in (N_IN_A, N_IN_B, LANES)),
        compiler_params=_params(2),
        name="w_in_split",
    )(w_in)


def _with_ignored_inputs(body, n_in, n_ignored):
    if n_ignored == 0:
        return body
    return lambda *refs: body(*refs[:n_in], *refs[n_in + n_ignored:])


def _const_spec(shape):
    return pl.BlockSpec(shape, lambda *_: (0,) * len(shape), pipeline_mode=pl.Buffered(1))


def _layer_spec(l, shape):
    return pl.BlockSpec((None,) + shape, lambda *_: (l,) + (0,) * len(shape), pipeline_mode=pl.Buffered(1))


def _any_spec():
    return pl.BlockSpec(memory_space=pl.ANY)


def _smem_spec():
    return pl.BlockSpec(memory_space=pltpu.SMEM)


def _params(n_grid):
    return pltpu.CompilerParams(dimension_semantics=("arbitrary",) * n_grid, vmem_limit_bytes=VMEM_LIMIT)


def _ada_call(c_prompt, c_sample, w_ada, b_ada):
    n_p, n_s = c_prompt.shape[0], c_sample.shape[0]
    c_all = jnp.concatenate([c_prompt, c_sample], axis=0)
    tn = D_MODEL
    n_tiles = w_ada.shape[2] // tn
    mod_p, mod_s = pl.pallas_call(
        _ada_kernel,
        grid=(DEPTH, n_tiles),
        in_specs=[pl.BlockSpec((n_p + n_s, D_MODEL), lambda l, n: (0, 0)),
                  pl.BlockSpec((None, D_MODEL, tn), lambda l, n: (l, 0, n)),
                  pl.BlockSpec((None, 1, tn), lambda l, n: (l, 0, n))],
        out_specs=(pl.BlockSpec((None, n_p, tn), lambda l, n: (l, 0, n)),
                   pl.BlockSpec((None, n_s, tn), lambda l, n: (l, 0, n))),
        out_shape=(jax.ShapeDtypeStruct((DEPTH, n_p, w_ada.shape[2]), F32),
                   jax.ShapeDtypeStruct((DEPTH, n_s, w_ada.shape[2]), F32)),
        compiler_params=_params(2),
        name="ada_mod",
    )(c_all, w_ada, b_ada.reshape(DEPTH, 1, -1))
    return mod_p.reshape(DEPTH, n_p, 6, D_MODEL), mod_s.reshape(DEPTH, n_s, 6, D_MODEL)


def _w_in_specs(l):
    return [_layer_spec(l, (D_MODEL, N_IN_A)), _layer_spec(l, (D_MODEL, N_IN_B)), _layer_spec(l, (D_MODEL, LANES))]


def _layer_weight_specs(l):
    return [
        _layer_spec(l, (CONV_WIDTH, CONV_DIM)), _layer_spec(l, (1, CONV_DIM)), _layer_spec(l, (1, LANES)),
        _layer_spec(l, (1, LANES)), _layer_spec(l, (1, SSM_INNER)), _layer_spec(l, (1, SSM_INNER)),
        _layer_spec(l, (1, GM_WIDTH)), _layer_spec(l, (1, GM_WIDTH)),
        _layer_spec(l, (GM_GROUPS, GM_CHUNK, GM_CHUNK)), _layer_spec(l, (GM_CHUNK, GM_GROUPS)),
        _layer_spec(l, (Q_W, D_MODEL)), _layer_spec(l, (SSM_INNER, D_MODEL)), _layer_spec(l, (GM_WIDTH, D_MODEL)),
        _layer_spec(l, (D_MODEL, D_MODEL)), _const_spec((LANES, SSM_INNER)),
    ]


def _layer_weight_args(p):
    return (p["conv_w"], p["conv_b"], p["dt_bias"], p["a_log"], p["d_skip"], p["ssm_norm_w"], p["gm_ln_g"],
            p["gm_ln_b"], p["gm_w_s"], p["gm_b_st"], p["w_attn_o"], p["w_ssm_o"], p["w_gm_o"], p["w_out"], p["expand"])


def _prompt_mixer(l, x, mod, p, rope, prev):
    bsz, seq, d = x.shape
    ts = PROMPT_TS
    tab = pl.BlockSpec((ts, LANES), lambda b, s: (s, 0))
    in_specs = [
        pl.BlockSpec((1, ts, d), lambda b, s: (b, s, 0)),
        pl.BlockSpec((None, 1, 6, d), lambda b, s: (l, b, 0, 0)),
        _layer_spec(l, (1, d)), *_w_in_specs(l),
        tab, tab, tab, _smem_spec(),
    ] + _layer_weight_specs(l)
    args = (x, mod, p["g_mix"], *p["w_in"], *rope, p["sinks"][l], *_layer_weight_args(p))
    state_tails = ((WINDOW, KV_W), (WINDOW, KV_W), (SSM_INNER, SSM_STATE), (CONV_WIDTH - 1, CONV_DIM))
    out_shape = (jax.ShapeDtypeStruct((bsz, seq, d), F32),) + tuple(
        jax.ShapeDtypeStruct((DEPTH, bsz) + tail, F32) for tail in state_tails)
    out_specs = (pl.BlockSpec((1, ts, d), lambda b, s: (b, s, 0)),) + tuple(
        pl.BlockSpec((None, 1) + tail, lambda b, s: (l, b, 0, 0)) for tail in state_tails)
    scratch = [
        pltpu.VMEM((WINDOW + ts, KV_W), F32), pltpu.VMEM((WINDOW + ts, KV_W), F32),
        pltpu.VMEM((8, CONV_DIM), F32), pltpu.VMEM((SSM_STATE, SSM_INNER), F32),
    ]
    n_in = len(args)
    prev = () if prev is None else tuple(prev)
    return pl.pallas_call(
        _with_ignored_inputs(_prompt_kernel, n_in, len(prev)),
        grid=(bsz, seq // ts), in_specs=in_specs + [_any_spec()] * len(prev), out_specs=out_specs,
        out_shape=out_shape, scratch_shapes=scratch, compiler_params=_params(2), name="prompt_mixer",
        input_output_aliases={n_in + i: 1 + i for i in range(len(prev))},
    )(*args, *prev)


def _sample_mixer(l, x, mod, cache_k, cache_v, h0, conv_state, p, rope, prev):
    bsz, t, d = x.shape
    nb = SAMPLE_NB
    m = nb * t

    def blk(*tail):
        return pl.BlockSpec((nb,) + tail, lambda i: (i,) + (0,) * len(tail))

    def lblk(*tail):
        return pl.BlockSpec((None, nb) + tail, lambda i: (l, i) + (0,) * len(tail))

    in_specs = [
        blk(t, d), lblk(6, d), _layer_spec(l, (1, d)), *_w_in_specs(l),
        _const_spec((m, LANES)), _const_spec((m, LANES)), _const_spec((m, LANES)), _smem_spec(),
        lblk(WINDOW, KV_W), lblk(WINDOW, KV_W), lblk(SSM_INNER, SSM_STATE), lblk(CONV_WIDTH - 1, CONV_DIM),
    ] + _layer_weight_specs(l)
    args = (x, mod, p["g_mix"], *p["w_in"], *rope, p["sinks"][l], cache_k, cache_v, h0, conv_state,
            *_layer_weight_args(p))
    state_tails = ((t, KV_W), (t, KV_W), (SSM_INNER, SSM_STATE), (CONV_WIDTH - 1, CONV_DIM), (t, GM_WIDTH))
    out_shape = (jax.ShapeDtypeStruct((bsz, t, d), F32),) + tuple(
        jax.ShapeDtypeStruct((DEPTH, bsz) + tail, F32) for tail in state_tails)
    out_specs = (blk(t, d),) + tuple(lblk(*tail) for tail in state_tails)
    scratch = [pltpu.VMEM((nb, 8 + t, CONV_DIM), F32)]
    n_in = len(args)
    prev = () if prev is None else tuple(prev)
    return pl.pallas_call(
        _with_ignored_inputs(_sample_kernel, n_in, len(prev)),
        grid=(bsz // nb,), in_specs=in_specs + [_any_spec()] * len(prev), out_specs=out_specs,
        out_shape=out_shape, scratch_shapes=scratch, compiler_params=_params(1), name="sample_mixer",
        input_output_aliases={n_in + i: 1 + i for i in range(len(prev))},
    )(*args, *prev)


def _ffn(l, x, mod, p, g_final, final_norm, name):
    bsz, t, d = x.shape
    if t >= FFN_ROWS:
        nb, tt = 1, FFN_ROWS
    else:
        nb, tt = FFN_ROWS // t, t
    grid = (bsz // nb, t // tt)
    return pl.pallas_call(
        functools.partial(_ffn_kernel, final_norm=final_norm),
        grid=grid,
        in_specs=[pl.BlockSpec((nb, tt, d), lambda b, s: (b, s, 0)),
                  pl.BlockSpec((None, nb, 6, d), lambda b, s: (l, b, 0, 0)),
                  _layer_spec(l, (1, d)), _layer_spec(l, (d, D_FF)), _layer_spec(l, (D_FF, d)), _const_spec((1, d))],
        out_specs=pl.BlockSpec((nb, tt, d), lambda b, s: (b, s, 0)),
        out_shape=jax.ShapeDtypeStruct((bsz, t, d), F32),
        compiler_params=_params(2), name=name,
    )(x, mod, p["g_ff"], p["w_ff1"], p["w_ff2"], g_final)


def _rope_tables(pos):
    half = ROT_DIM // 2
    inv_freq = ROPE_THETA ** (-jnp.arange(half, dtype=F32) * (2.0 / ROT_DIM))
    ang = pos.astype(F32)[:, None] * inv_freq[None, :]
    cos, sin = jnp.cos(ang), jnp.sin(ang)
    n = pos.shape[0]
    ones = jnp.ones((n, HEAD_DIM - ROT_DIM), F32)
    zeros = jnp.zeros((n, HEAD_DIM - ROT_DIM), F32)
    zh = jnp.zeros((n, half), F32)
    cos_t = jnp.concatenate([cos, cos, ones], axis=1)
    sin_a = jnp.concatenate([-sin, zh, zeros], axis=1)
    sin_b = jnp.concatenate([zh, sin, zeros], axis=1)
    rep = LANES // HEAD_DIM
    return tuple(jnp.tile(a, (1, rep)) for a in (cos_t, sin_a, sin_b))


def _stacked_params(w_in, g_mix, sinks, conv_w, conv_b, dt_bias, a_log, d_skip, ssm_norm_w, gm_ln_g, gm_ln_b,
                    gm_w_s, gm_b_s, w_attn_o, w_ssm_o, w_gm_o, w_out, g_ff, w_ff1, w_ff2):
    w_r = _w_in_split(w_in)
    pad = jnp.zeros((DEPTH, LANES - SSM_HEADS), F32)
    expand = (jnp.arange(SSM_INNER)[None, :] // SSM_HEAD_DIM == jnp.arange(LANES)[:, None]).astype(BF16)

    def row(a):
        return a[:, None, :]

    return {
        "w_in": w_r, "g_mix": row(g_mix), "sinks": sinks,
        "conv_w": 0.5 * conv_w, "conv_b": row(0.5 * conv_b),
        "dt_bias": row(jnp.concatenate([dt_bias, pad], axis=1)), "a_log": row(jnp.concatenate([a_log, pad], axis=1)),
        "d_skip": row(jnp.repeat(d_skip, SSM_HEAD_DIM, axis=1)), "ssm_norm_w": row(ssm_norm_w),
        "gm_ln_g": row(gm_ln_g), "gm_ln_b": row(gm_ln_b),
        "gm_w_s": gm_w_s, "gm_b_st": jnp.swapaxes(gm_b_s, 1, 2),
        "w_attn_o": (0.5 * w_attn_o).astype(BF16), "w_ssm_o": (0.5 * w_ssm_o).astype(BF16),
        "w_gm_o": (0.5 * w_gm_o).astype(BF16), "w_out": w_out.astype(BF16), "expand": expand,
        "g_ff": row(g_ff), "w_ff1": w_ff1.astype(BF16), "w_ff2": w_ff2.astype(BF16),
    }


def kernel(x_prompt, x_sample, c_prompt, c_sample, cache_attn_k, cache_attn_v, state_ssm, state_conv, w_ada, b_ada, g_mix, w_in, sinks, conv_w, conv_b, dt_bias, a_log, d_skip, ssm_norm_w, gm_ln_g, gm_ln_b, gm_w_s, gm_b_s, w_attn_o, w_ssm_o, w_gm_o, w_out, g_ff, w_ff1, w_ff2, g_final):
    bp, sp, d = x_prompt.shape
    bs, ss, _ = x_sample.shape
    mod_p, mod_s = _ada_call(c_prompt, c_sample, w_ada, b_ada)
    rope_p = _rope_tables(jnp.arange(sp))
    rope_s = tuple(jnp.tile(a, (SAMPLE_NB, 1)) for a in _rope_tables(PAST_LEN + jnp.arange(ss)))
    g_fin = g_final[None]
    p = _stacked_params(w_in, g_mix, sinks, conv_w, conv_b, dt_bias, a_log, d_skip, ssm_norm_w, gm_ln_g,
                        gm_ln_b, gm_w_s, gm_b_s, w_attn_o, w_ssm_o, w_gm_o, w_out, g_ff, w_ff1, w_ff2)
    ck = cache_attn_k.reshape(DEPTH, bs, WINDOW, KV_W)
    cv = cache_attn_v.reshape(DEPTH, bs, WINDOW, KV_W)
    h0 = state_ssm.reshape(DEPTH, bs, SSM_INNER, SSM_STATE)
    xp, xs = x_prompt, x_sample
    state_p = state_s = None
    for l in range(DEPTH):
        final = l == DEPTH - 1
        xp, *state_p = _prompt_mixer(l, xp, mod_p, p, rope_p, state_p)
        xp = _ffn(l, xp, mod_p, p, g_fin, final, "prompt_ffn")
        xs, *state_s = _sample_mixer(l, xs, mod_s, ck, cv, h0, state_conv, p, rope_s, state_s)
        xs = _ffn(l, xs, mod_s, p, g_fin, final, "sample_ffn")
    kp, vp, ssm_p, conv_p = state_p
    ks, vs, ssm_s, conv_s, gv_s = state_s
    return (xp, xs,
            kp.reshape(DEPTH, bp, WINDOW, N_KV_HEADS, HEAD_DIM), vp.reshape(DEPTH, bp, WINDOW, N_KV_HEADS, HEAD_DIM),
            ssm_p.reshape(DEPTH, bp, SSM_HEADS, SSM_HEAD_DIM, SSM_STATE), conv_p,
            ks.reshape(DEPTH, bs, ss, N_KV_HEADS, HEAD_DIM), vs.reshape(DEPTH, bs, ss, N_KV_HEADS, HEAD_DIM),
            ssm_s.reshape(DEPTH, bs, SSM_HEADS, SSM_HEAD_DIM, SSM_STATE), conv_s, gv_s)
```

```python
import functools

import jax
import jax.numpy as jnp
from jax import lax
from jax.experimental import pallas as pl
from jax.experimental.pallas import tpu as pltpu

F32 = jnp.float32
BF16 = jnp.bfloat16

D_MODEL = 1024
DEPTH = 2
CHUNK = 64
N_HEADS = 8
N_KV_HEADS = 2
HEAD_DIM = 64
GQA_GROUP = N_HEADS // N_KV_HEADS
ROT_DIM = HEAD_DIM // 4
ROPE_THETA = 500000.0
WINDOW = 128
SSM_HEADS = 16
SSM_HEAD_DIM = 64
SSM_INNER = SSM_HEADS * SSM_HEAD_DIM
SSM_GROUPS = 2
SSM_STATE = 128
SSM_CHUNK = 64
CONV_WIDTH = 4
CONV_DIM = SSM_INNER + 2 * SSM_GROUPS * SSM_STATE
GM_WIDTH = 512
GM_GROUPS = 4
GM_GROUP_DIM = GM_WIDTH // GM_GROUPS
GM_CHUNK = 128
D_FF = 4 * D_MODEL
Q_W = N_HEADS * HEAD_DIM
KV_W = N_KV_HEADS * HEAD_DIM
PAST_LEN = 4096
EPS = 1e-6

LANES = 128
HEADS_PER_GROUP = SSM_HEADS // SSM_GROUPS
GROUP_INNER = HEADS_PER_GROUP * SSM_HEAD_DIM

C_Q = (0, 0, Q_W)
C_K = (0, C_Q[2], C_Q[2] + KV_W)
C_V = (0, C_K[2], C_K[2] + KV_W)
C_Z = (0, C_V[2], C_V[2] + SSM_INNER)
C_XBC = (0, C_Z[2], C_Z[2] + CONV_DIM)
N_IN_A = C_XBC[2]
C_GU = (1, 0, GM_WIDTH)
C_GV = (1, C_GU[2], C_GU[2] + GM_WIDTH)
C_G0 = (1, C_GV[2], C_GV[2] + D_MODEL)
C_G1 = (1, C_G0[2], C_G0[2] + D_MODEL)
C_G2 = (1, C_G1[2], C_G1[2] + D_MODEL)
N_IN_B = C_G2[2]
C_DT = (2, 0, LANES)

PROMPT_TS = 256
PROJ_BLOCK = 256
SAMPLE_NB = 8
FFN_ROWS = 512
W_SPLIT_COLS = 256
VMEM_LIMIT = 56 * 1024 * 1024


def _log2(n):
    assert n & (n - 1) == 0, n
    return n.bit_length() - 1


def _dot(a, b):
    return jnp.dot(a, b, preferred_element_type=F32)


def _dot_nt(a, b):
    return lax.dot_general(a, b, (((1,), (1,)), ((), ())), preferred_element_type=F32)


def _dot_tn(a, b):
    return lax.dot_general(a, b, (((0,), (0,)), ((), ())), preferred_element_type=F32)


def _split3(x):
    hi = x.astype(BF16)
    r = x - hi.astype(F32)
    mid = r.astype(BF16)
    lo = (r - mid.astype(F32)).astype(BF16)
    return hi, mid, lo


def _sigmoid(x):
    return 0.5 * (1.0 + jnp.tanh(0.5 * x))


def _silu(x):
    return x * _sigmoid(x)


def _silu_of_half(xh):
    return xh * (1.0 + jnp.tanh(xh))


def _twice_sigmoid_of_half(xh):
    return 1.0 + jnp.tanh(xh)


def _gelu(x):
    return 0.5 * x * (1.0 + jnp.tanh(0.7978845608028654 * (x + 0.044715 * (x * x * x))))


def _softplus(x):
    return jnp.maximum(x, 0.0) + jnp.log1p(jnp.exp(-jnp.abs(x)))


def _rms(x, g):
    return x * lax.rsqrt(jnp.mean(x * x, axis=-1, keepdims=True) + EPS) * g


def _rope(x, cos, sin_a, sin_b):
    return x * cos + pltpu.roll(x, LANES - ROT_DIM // 2, 1) * sin_a + pltpu.roll(x, ROT_DIM // 2, 1) * sin_b


def _proj(h, win_refs, cols):
    return _dot(h, win_refs[cols[0]][:, cols[1]:cols[2]])


def _modnorm(x3, g, mod3):
    nb, t, d = x3.shape
    h3 = _rms(x3, g) * (1.0 + mod3[:, 1:2, :]) + mod3[:, 0:1, :]
    return h3.reshape(nb * t, d).astype(BF16)


def _sink_row(sinks_ref, kv, cols_per_head):
    c = lax.broadcasted_iota(jnp.int32, (1, GQA_GROUP * cols_per_head), 1)
    row = jnp.full((1, GQA_GROUP * cols_per_head), sinks_ref[kv * GQA_GROUP], F32)
    for i in range(1, GQA_GROUP):
        row = jnp.where(c >= i * cols_per_head, sinks_ref[kv * GQA_GROUP + i], row)
    return row


def _stack_heads(q, kv, rows):
    return jnp.concatenate(
        [q[:, (kv * GQA_GROUP + i) * HEAD_DIM:(kv * GQA_GROUP + i + 1) * HEAD_DIM] for i in range(GQA_GROUP)], axis=0)


def _unstack_heads(o, rows):
    return jnp.concatenate([o[i * rows:(i + 1) * rows, :] for i in range(GQA_GROUP)], axis=1)


def _gmlp_chunk(vn, u, gws_ref, gbst_ref, L):
    ri = lax.broadcasted_iota(jnp.int32, (L, L), 0)
    ci = lax.broadcasted_iota(jnp.int32, (L, L), 1)
    outs = []
    for g in range(GM_GROUPS):
        w = jnp.where(ri >= ci, gws_ref[g, 0:L, 0:L], 0.0).astype(BF16)
        v_g = vn[:, g * GM_GROUP_DIM:(g + 1) * GM_GROUP_DIM].astype(BF16)
        outs.append(_dot(w, v_g) + gbst_ref[0:L, g:g + 1])
    return u * jnp.concatenate(outs, axis=1)


def _layer_norm(x, g, b):
    xc = x - jnp.mean(x, axis=-1, keepdims=True)
    return xc * lax.rsqrt(jnp.mean(xc * xc, axis=-1, keepdims=True) + EPS) * g + b


def _merge_out(x3, mod3, gates, a, b, c, wout_ref):
    merged = gates[0] * a + gates[1] * b + gates[2] * c
    o = _dot(merged.astype(BF16), wout_ref[...])
    nb, t, d = x3.shape
    return x3 + mod3[:, 2:3, :] * o.reshape(nb, t, d)


def _attn_groups(ts):
    return [(c, kv) for c in range(ts // CHUNK) for kv in range(N_KV_HEADS)]


def _attn_scores(q, khist, pos0):
    ts = q.shape[0]
    n_keys = WINDOW + CHUNK
    key_i = lax.broadcasted_iota(jnp.int32, (n_keys, GQA_GROUP * CHUNK), 0)
    scores = []
    for c, kv in _attn_groups(ts):
        k_g = khist[c * CHUNK:c * CHUNK + n_keys, kv * HEAD_DIM:(kv + 1) * HEAD_DIM].astype(BF16)
        sc = _dot_nt(k_g, _stack_heads(q[c * CHUNK:(c + 1) * CHUNK, :], kv, CHUNK))
        if c * CHUNK < WINDOW:
            sc = jnp.where(key_i >= WINDOW - c * CHUNK - pos0, sc, -jnp.inf)
        scores.append(sc)
    return scores


def _attn_probs(scores, sinks_ref, ts):
    sink_rows = [_sink_row(sinks_ref, kv, CHUNK) for kv in range(N_KV_HEADS)]
    probs = []
    for (c, kv), sc in zip(_attn_groups(ts), scores):
        m = jnp.maximum(jnp.max(sc, axis=0, keepdims=True), sink_rows[kv])
        p = jnp.exp(sc - m)
        denom = jnp.sum(p, axis=0, keepdims=True) + jnp.exp(sink_rows[kv] - m)
        probs.append((p * (1.0 / denom)).astype(BF16))
    return probs


def _attn_values(probs, vhist, ts):
    n_keys = WINDOW + CHUNK
    outs = []
    for (c, kv), p in zip(_attn_groups(ts), probs):
        v_g = vhist[c * CHUNK:c * CHUNK + n_keys, kv * HEAD_DIM:(kv + 1) * HEAD_DIM].astype(BF16)
        outs.append(_unstack_heads(_dot_tn(p, v_g), CHUNK))
    rows = [jnp.concatenate(outs[c * N_KV_HEADS:(c + 1) * N_KV_HEADS], axis=1) for c in range(ts // CHUNK)]
    return jnp.concatenate(rows, axis=0).astype(BF16)


def _ssd_cumsum(dt, a_row, e):
    ts = dt.shape[0]
    L = SSM_CHUNK
    ri = lax.broadcasted_iota(jnp.int32, (ts, ts), 0)
    ci = lax.broadcasted_iota(jnp.int32, (ts, ts), 1)
    lag = ri - ci
    tri = jnp.where((lag >= 0) & (lag <= (ri & (L - 1))), 1.0, 0.0).astype(BF16)
    hi, mid, lo = _split3(dt * a_row)
    cum = _dot(tri, hi) + _dot(tri, mid) + _dot(tri, lo)
    hi, mid, lo = _split3(cum)
    cum_e = _dot(hi, e) + _dot(mid, e) + _dot(lo, e)
    dt_hi = dt.astype(BF16)
    dt_lo = (dt - dt_hi.astype(F32)).astype(BF16)
    dt_e = _dot(dt_hi, e) + _dot(dt_lo, e)
    return cum_e, dt_e


def _ssd_decays(xact, cum_e, dt_e):
    ts = xact.shape[0]
    L = SSM_CHUNK
    nc = ts // L
    xs = xact[:, 0:SSM_INNER]
    cum_e = cum_e.reshape(nc, L, SSM_INNER)
    t3 = lax.broadcasted_iota(jnp.int32, cum_e.shape, 1)
    slot3 = lax.broadcasted_iota(jnp.int32, cum_e.shape, 2) & (SSM_HEAD_DIM - 1)
    cum_src = jnp.sum(jnp.where(t3 == slot3, cum_e, 0.0), axis=1, keepdims=True)
    decay = jnp.exp(jnp.where(t3 >= slot3, cum_e - cum_src, -jnp.inf))
    last = cum_e[:, L - 1:L, :]
    exp_cum = jnp.exp(cum_e)
    chunk_decay = jnp.exp(last)
    xdt = xs * dt_e
    wx = (xdt.reshape(nc, L, SSM_INNER) * jnp.exp(last - cum_e)).astype(BF16)
    first_head = (lax.broadcasted_iota(jnp.int32, xdt.shape, 1) & (2 * SSM_HEAD_DIM - 1)) < SSM_HEAD_DIM
    return {
        "xs": xs,
        "bm": xact[:, SSM_INNER:SSM_INNER + SSM_GROUPS * SSM_STATE].astype(BF16),
        "cm": xact[:, SSM_INNER + SSM_GROUPS * SSM_STATE:CONV_DIM].astype(BF16),
        "decay": decay, "exp_cum": exp_cum, "chunk_decay": chunk_decay, "wx": wx,
        "xdt_a": jnp.where(first_head, xdt, 0.0).astype(BF16),
        "xdt_b": jnp.where(first_head, 0.0, xdt).astype(BF16),
    }


def _ssd_chunk_step(c, ctx, st, dskip_e):
    L = SSM_CHUNK
    rows = slice(c * L, (c + 1) * L)
    bm, cm = ctx["bm"], ctx["cm"]
    cb = [_dot_nt(cm[rows, g * SSM_STATE:(g + 1) * SSM_STATE],
                  jnp.concatenate([bm[rows, g * SSM_STATE:(g + 1) * SSM_STATE]] * 2, axis=0))
          for g in range(SSM_GROUPS)]
    y_intra = []
    for k in range(SSM_HEADS // 2):
        ls = slice(k * 2 * SSM_HEAD_DIM, (k + 1) * 2 * SSM_HEAD_DIM)
        m_pair = (cb[2 * k // HEADS_PER_GROUP] * ctx["decay"][c][:, ls]).astype(BF16)
        block_diag = jnp.concatenate([ctx["xdt_a"][rows, ls], ctx["xdt_b"][rows, ls]], axis=0)
        y_intra.append(_dot(m_pair, block_diag))
    st_b = st.astype(BF16)
    y_inter = jnp.concatenate(
        [_dot(cm[rows, g * SSM_STATE:(g + 1) * SSM_STATE], st_b[:, g * GROUP_INNER:(g + 1) * GROUP_INNER])
         for g in range(SSM_GROUPS)], axis=1)
    y = jnp.concatenate(y_intra, axis=1) + y_inter * ctx["exp_cum"][c] + dskip_e * ctx["xs"][rows]
    st = st * ctx["chunk_decay"][c] + jnp.concatenate(
        [_dot_tn(bm[rows, g * SSM_STATE:(g + 1) * SSM_STATE], ctx["wx"][c][:, g * GROUP_INNER:(g + 1) * GROUP_INNER])
         for g in range(SSM_GROUPS)], axis=1)
    return y, st


class _Stagger:
    def __init__(self):
        self._pending = None

    def push(self, produce, consume):
        val = produce()
        self.flush()
        self._pending = (consume, val)

    def flush(self):
        if self._pending is not None:
            consume, val = self._pending
            self._pending = None
            consume(val)


def _prompt_kernel(x_ref, mod_ref, gmix_ref, wa_ref, wb_ref, wdt_ref, cos_ref, sa_ref, sb_ref, sinks_ref,
                   convw_ref, convb_ref, dtb_ref, alog_ref, dskip_ref, ssmnw_ref, glng_ref, glnb_ref,
                   gws_ref, gbst_ref, wao_ref, wso_ref, wgo_ref, wout_ref, e_ref,
                   xo_ref, ko_ref, vo_ref, sto_ref, cvo_ref,
                   khist, vhist, xp, st_s):
    ts = PROMPT_TS
    s = pl.program_id(1)
    last = pl.num_programs(1) - 1

    @pl.when(s == 0)
    def _():
        khist[0:WINDOW, :] = jnp.zeros((WINDOW, KV_W), F32)
        vhist[0:WINDOW, :] = jnp.zeros((WINDOW, KV_W), F32)
        xp[...] = jnp.zeros_like(xp)
        st_s[...] = jnp.zeros_like(st_s)

    x3 = x_ref[...]
    mod3 = mod_ref[...]
    h = _modnorm(x3, gmix_ref[...], mod3)

    cos, sa, sb = cos_ref[...], sa_ref[...], sb_ref[...]
    res = {}

    def rope_block(raw):
        return [_rope(raw[:, i * LANES:(i + 1) * LANES], cos, sa, sb) for i in range(raw.shape[1] // LANES)]

    def q_block(raw):
        res.setdefault("q", []).extend(p.astype(BF16) for p in rope_block(raw * (HEAD_DIM ** -0.5)))

    def kv_block(raw):
        khist[WINDOW:WINDOW + ts, :] = rope_block(raw[:, 0:KV_W])[0]
        vhist[WINDOW:WINDOW + ts, :] = raw[:, KV_W:2 * KV_W]

    def conv_block(lo, raw):
        cs = slice(lo, lo + PROJ_BLOCK)
        ext = jnp.concatenate([xp[:, cs], raw], axis=0)
        acc = convb_ref[:, cs] + raw * convw_ref[CONV_WIDTH - 1:CONV_WIDTH, cs]
        for j in range(1, CONV_WIDTH):
            acc = acc + pltpu.roll(ext, j, 0)[8:8 + ts] * convw_ref[CONV_WIDTH - 1 - j:CONV_WIDTH - j, cs]
        xp[:, cs] = raw[ts - 8:ts]
        res.setdefault("xact", []).append(_silu_of_half(acc))

    def collect(key, fn):
        return lambda raw: res.setdefault(key, []).append(fn(raw))

    wrefs = (wa_ref, wb_ref, wdt_ref)
    pipe = _Stagger()

    def push_proj(cols, off, width, consume):
        ref, lo = wrefs[cols[0]], cols[1] + off
        pipe.push(lambda: _dot(h, ref[:, lo:lo + width]), consume)

    def blocks(cols, consume):
        return [functools.partial(push_proj, cols, off, PROJ_BLOCK, consume)
                for off in range(0, cols[2] - cols[1], PROJ_BLOCK)]

    conv_blocks = [functools.partial(push_proj, C_XBC, off, PROJ_BLOCK, functools.partial(conv_block, off))
                   for off in range(0, CONV_DIM, PROJ_BLOCK)]
    gate_blocks = iter(blocks(C_G0, collect("g0", _twice_sigmoid_of_half))
                       + blocks(C_G1, collect("g1", _twice_sigmoid_of_half))
                       + blocks(C_G2, collect("g2", _twice_sigmoid_of_half)))
    z_blocks = blocks(C_Z, collect("sz", _silu_of_half))
    gm_blocks = blocks(C_GU, collect("u", _gelu)) + blocks(C_GV, collect("gv", _gelu))

    def cat(key):
        return jnp.concatenate(res[key], axis=1)

    for blk in blocks(C_Q, q_block):
        blk()
    push_proj(C_K, 0, 2 * KV_W, kv_block)
    conv_blocks[0]()
    next(gate_blocks)()
    conv_blocks[1]()
    scores = _attn_scores(cat("q"), khist, s * ts)
    conv_blocks[2]()
    probs = _attn_probs(scores, sinks_ref, ts)
    next(gate_blocks)()
    conv_blocks[3]()
    attn = _attn_values(probs, vhist, ts)
    conv_blocks[4]()
    next(gate_blocks)()
    conv_blocks[5]()
    push_proj(C_DT, 0, LANES, collect("dt", lambda raw: _softplus(raw + dtb_ref[...])))
    next(gate_blocks)()
    a_out = _dot(attn, wao_ref[...])
    z_blocks[0]()
    cum_e, dt_e = _ssd_cumsum(res["dt"][0], -jnp.exp(alog_ref[...]), e_ref[...])
    z_blocks[1]()
    merged = cat("g0") * a_out
    ctx = _ssd_decays(cat("xact"), cum_e, dt_e)
    z_blocks[2]()
    z_blocks[3]()
    st = st_s[...]
    ys = []
    for c in range(ts // SSM_CHUNK):
        y_c, st = _ssd_chunk_step(c, ctx, st, dskip_ref[...])
        ys.append(y_c)
        gm_blocks[c]()
    st_s[...] = st
    next(gate_blocks)()
    vn = _layer_norm(cat("gv"), glng_ref[...], glnb_ref[...])
    u = cat("u")
    next(gate_blocks)()
    ssm = _rms(jnp.concatenate(ys, axis=0) * cat("sz"), ssmnw_ref[...]).astype(BF16)
    gm = jnp.concatenate(
        [_gmlp_chunk(vn[c * GM_CHUNK:(c + 1) * GM_CHUNK], u[c * GM_CHUNK:(c + 1) * GM_CHUNK], gws_ref, gbst_ref,
                     GM_CHUNK) for c in range(ts // GM_CHUNK)], axis=0).astype(BF16)
    next(gate_blocks)()
    b_out = _dot(ssm, wso_ref[...])
    next(gate_blocks)()
    pipe.flush()
    merged = merged + cat("g1") * b_out
    c_out = _dot(gm, wgo_ref[...])
    for blk in gate_blocks:
        blk()
    pipe.flush()
    merged = merged + cat("g2") * c_out
    o = _dot(merged.astype(BF16), wout_ref[...])
    xo_ref[...] = x3 + mod3[:, 2:3, :] * o.reshape(x3.shape)

    khist[0:WINDOW, :] = khist[ts:ts + WINDOW, :]
    vhist[0:WINDOW, :] = vhist[ts:ts + WINDOW, :]

    @pl.when(s == last)
    def _():
        ko_ref[0] = khist[ts:ts + WINDOW, :]
        vo_ref[0] = vhist[ts:ts + WINDOW, :]
        cvo_ref[0] = xp[8 - (CONV_WIDTH - 1):8, :]
        sto_ref[0] = st_s[...].T


def _sample_attention(q, k_new, v_new, cache_k, cache_v, sinks_ref):
    t = q.shape[0]
    outs = []
    for kv in range(N_KV_HEADS):
        hs = slice(kv * HEAD_DIM, (kv + 1) * HEAD_DIM)
        qg = _stack_heads(q, kv, t)
        s_c = _dot_nt(cache_k[:, hs].astype(BF16), qg)
        s_n = _dot_nt(k_new[:, hs].astype(BF16), qg)
        sink = _sink_row(sinks_ref, kv, t)
        m = jnp.maximum(jnp.maximum(jnp.max(s_c, axis=0, keepdims=True), jnp.max(s_n, axis=0, keepdims=True)), sink)
        p_c = jnp.exp(s_c - m)
        p_n = jnp.exp(s_n - m)
        inv = 1.0 / (jnp.sum(p_c, axis=0, keepdims=True) + jnp.sum(p_n, axis=0, keepdims=True) + jnp.exp(sink - m))
        o = (_dot_tn((p_c * inv).astype(BF16), cache_v[:, hs].astype(BF16))
             + _dot_tn((p_n * inv).astype(BF16), v_new[:, hs].astype(BF16)))
        outs.append(_unstack_heads(o, t))
    return jnp.concatenate(outs, axis=1)


def _sample_ssd_setup(xact, dt, a_row, e, L):
    m = xact.shape[0]
    n_seq = m // L
    xs = xact[:, 0:SSM_INNER]
    ri = lax.broadcasted_iota(jnp.int32, (m, m), 0)
    lag = ri - lax.broadcasted_iota(jnp.int32, (m, m), 1)
    tri = jnp.where((lag >= 0) & (lag <= (ri & (L - 1))), 1.0, 0.0).astype(BF16)
    hi, mid, lo = _split3(dt * a_row)
    cum = _dot(tri, hi) + _dot(tri, mid) + _dot(tri, lo)
    hi, mid, lo = _split3(cum)
    cum_e = (_dot(hi, e) + _dot(mid, e) + _dot(lo, e)).reshape(n_seq, L, SSM_INNER)
    head_of_lane = jnp.right_shift(lax.broadcasted_iota(jnp.int32, (LANES, SSM_HEADS * L), 1), _log2(L))
    e_slots = jnp.where(head_of_lane == lax.broadcasted_iota(jnp.int32, (LANES, SSM_HEADS * L), 0), 1.0, 0.0)
    e_slots = e_slots.astype(BF16)
    cum_s = (_dot(hi, e_slots) + _dot(mid, e_slots) + _dot(lo, e_slots)).reshape(n_seq, L, SSM_HEADS * L)
    dt_hi = dt.astype(BF16)
    dt_lo = (dt - dt_hi.astype(F32)).astype(BF16)
    dt_e = _dot(dt_hi, e) + _dot(dt_lo, e)
    t3 = lax.broadcasted_iota(jnp.int32, cum_s.shape, 1)
    slot3 = lax.broadcasted_iota(jnp.int32, cum_s.shape, 2) & (L - 1)
    cum_src = jnp.sum(jnp.where(t3 == slot3, cum_s, 0.0), axis=1, keepdims=True)
    decay = jnp.exp(jnp.where(t3 >= slot3, cum_s - cum_src, -jnp.inf))
    last = cum_e[:, L - 1:L, :]
    xdt = xs * dt_e
    wx = (xdt.reshape(n_seq, L, SSM_INNER) * jnp.exp(last - cum_e)).astype(BF16)
    per_head = jnp.exp(cum.reshape(n_seq, L, LANES)[:, L - 1:L, :])
    rep = jnp.broadcast_to(per_head, (n_seq, SSM_HEADS, LANES)).reshape(n_seq * SSM_HEADS, LANES)
    own_lane = (lax.broadcasted_iota(jnp.int32, rep.shape, 1)
                == (lax.broadcasted_iota(jnp.int32, rep.shape, 0) & (SSM_HEADS - 1)))
    hi, mid, lo = _split3(jnp.where(own_lane, rep, 0.0))
    ones = jnp.ones((LANES, LANES), BF16)
    splat = (_dot(hi, ones) + _dot(mid, ones) + _dot(lo, ones)).reshape(n_seq, SSM_HEADS, LANES)
    return {
        "L": L, "xs": xs,
        "bm": xact[:, SSM_INNER:SSM_INNER + SSM_GROUPS * SSM_STATE].astype(BF16),
        "cm": xact[:, SSM_INNER + SSM_GROUPS * SSM_STATE:CONV_DIM].astype(BF16),
        "decay": decay, "exp_cum": jnp.exp(cum_e), "wx": wx, "xdt": xdt.astype(BF16), "state_decay": splat,
    }


def _sample_ssd_seq(i, s, h0, dskip_e):
    L = s["L"]
    rows = slice(i * L, (i + 1) * L)
    h0_b = h0.astype(BF16)
    ri = jnp.right_shift(lax.broadcasted_iota(jnp.int32, (HEADS_PER_GROUP * L, GROUP_INNER), 0), _log2(L))
    li = jnp.right_shift(lax.broadcasted_iota(jnp.int32, (HEADS_PER_GROUP * L, GROUP_INNER), 1), _log2(SSM_HEAD_DIM))
    y_parts, upd = [], []
    for g in range(SSM_GROUPS):
        gn = slice(g * SSM_STATE, (g + 1) * SSM_STATE)
        gp = slice(g * GROUP_INNER, (g + 1) * GROUP_INNER)
        gs = slice(g * HEADS_PER_GROUP * L, (g + 1) * HEADS_PER_GROUP * L)
        bm_g, cm_g = s["bm"][rows, gn], s["cm"][rows, gn]
        cb = _dot_nt(cm_g, jnp.concatenate([bm_g] * HEADS_PER_GROUP, axis=0))
        m_g = (cb * s["decay"][i][:, gs]).astype(BF16)
        tiled = jnp.concatenate([s["xdt"][rows, gp]] * HEADS_PER_GROUP, axis=0)
        block_diag = jnp.where(ri == li, tiled, jnp.zeros_like(tiled))
        y_g = _dot(m_g, block_diag) + _dot_nt(cm_g, h0_b[gp, :]) * s["exp_cum"][i][:, gp]
        y_parts.append(y_g)
        upd.append(_dot_tn(s["wx"][i][:, gp], bm_g))
    y = jnp.concatenate(y_parts, axis=1) + dskip_e * s["xs"][rows]
    decayed = h0.reshape(SSM_HEADS, SSM_HEAD_DIM, SSM_STATE) * s["state_decay"][i][:, None, :]
    return y, decayed.reshape(SSM_INNER, SSM_STATE) + jnp.concatenate(upd, axis=0)


def _sample_kernel(x_ref, mod_ref, gmix_ref, wa_ref, wb_ref, wdt_ref, cos_ref, sa_ref, sb_ref, sinks_ref,
                   ck_ref, cv_ref, h0_ref, cs_ref,
                   convw_ref, convb_ref, dtb_ref, alog_ref, dskip_ref, ssmnw_ref, glng_ref, glnb_ref,
                   gws_ref, gbst_ref, wao_ref, wso_ref, wgo_ref, wout_ref, e_ref,
                   xo_ref, ko_ref, vo_ref, sto_ref, cvo_ref, gvo_ref,
                   xp):
    nb = SAMPLE_NB
    t = x_ref.shape[1]
    m = nb * t
    win_ref = (wa_ref, wb_ref, wdt_ref)
    x3 = x_ref[...]
    mod3 = mod_ref[...]
    h = _modnorm(x3, gmix_ref[...], mod3)

    cos, sa, sb = cos_ref[...], sa_ref[...], sb_ref[...]
    q = _proj(h, win_ref, C_Q) * (HEAD_DIM ** -0.5)
    q = jnp.concatenate(
        [_rope(q[:, i * LANES:(i + 1) * LANES], cos, sa, sb) for i in range(Q_W // LANES)], axis=1).astype(BF16)
    k_new = _rope(_proj(h, win_ref, C_K), cos, sa, sb)
    v_new = _proj(h, win_ref, C_V)
    ko_ref[...] = k_new.reshape(nb, t, KV_W)
    vo_ref[...] = v_new.reshape(nb, t, KV_W)

    xp[:, 8 - (CONV_WIDTH - 1):8, :] = cs_ref[...]
    xp[:, 8:8 + t, :] = _proj(h, win_ref, C_XBC).reshape(nb, t, CONV_DIM)
    acc = convb_ref[...] + xp[:, 8:8 + t, :] * convw_ref[CONV_WIDTH - 1:CONV_WIDTH, :]
    for j in range(1, CONV_WIDTH):
        acc = acc + xp[:, 8 - j:8 - j + t, :] * convw_ref[CONV_WIDTH - 1 - j:CONV_WIDTH - j, :]
    xact = _silu_of_half(acc).reshape(m, CONV_DIM)
    cvo_ref[...] = xp[:, t + 8 - (CONV_WIDTH - 1):t + 8, :]
    dt = _softplus(_proj(h, win_ref, C_DT) + dtb_ref[...])
    dskip_e = dskip_ref[...]

    u = _gelu(_proj(h, win_ref, C_GU))
    vn = _layer_norm(_gelu(_proj(h, win_ref, C_GV)), glng_ref[...], glnb_ref[...])
    gvo_ref[...] = vn.reshape(nb, t, GM_WIDTH)

    ssd = _sample_ssd_setup(xact, dt, -jnp.exp(alog_ref[...]), e_ref[...], t)
    attn_rows, y_rows, gm_rows = [], [], []
    for i in range(nb):
        rows = slice(i * t, (i + 1) * t)
        attn_rows.append(_sample_attention(q[rows], k_new[rows], v_new[rows], ck_ref[i], cv_ref[i], sinks_ref))
        y_i, sto_ref[i] = _sample_ssd_seq(i, ssd, h0_ref[i], dskip_e)
        y_rows.append(y_i)
        gm_rows.append(_gmlp_chunk(vn[rows], u[rows], gws_ref, gbst_ref, t))

    a_out = _dot(jnp.concatenate(attn_rows, axis=0).astype(BF16), wao_ref[...])
    z = _proj(h, win_ref, C_Z)
    ssm = _rms(jnp.concatenate(y_rows, axis=0) * _silu_of_half(z), ssmnw_ref[...]).astype(BF16)
    b_out = _dot(ssm, wso_ref[...])
    c_out = _dot(jnp.concatenate(gm_rows, axis=0).astype(BF16), wgo_ref[...])
    gates = [_twice_sigmoid_of_half(_proj(h, win_ref, cols)) for cols in (C_G0, C_G1, C_G2)]
    xo_ref[...] = _merge_out(x3, mod3, gates, a_out, b_out, c_out, wout_ref)


def _ffn_kernel(x_ref, mod_ref, gff_ref, w1_ref, w2_ref, gfin_ref, o_ref, *, final_norm):
    x3 = x_ref[...]
    mod3 = mod_ref[...]
    nb, t, d = x3.shape
    h3 = _rms(x3, gff_ref[...]) * (1.0 + mod3[:, 4:5, :]) + mod3[:, 3:4, :]
    h = h3.reshape(nb * t, d).astype(BF16)
    a = jnp.maximum(_dot(h, w1_ref[...]), 0.0)
    y = _dot((a * a).astype(BF16), w2_ref[...])
    out = x3 + mod3[:, 5:6, :] * y.reshape(nb, t, d)
    if final_norm:
        out = _rms(out, gfin_ref[...])
    o_ref[...] = out


def _ada_kernel(c_ref, w_ref, b_ref, op_ref, os_ref):
    mod = _dot(_silu(c_ref[...]).astype(BF16), w_ref[...].astype(BF16)) + b_ref[...]
    n_prompt = op_ref.shape[0]
    op_ref[...] = mod[0:n_prompt]
    os_ref[...] = mod[n_prompt:]


def _w_in_split_kernel(wt_ref, a_ref, b_ref, dt_ref):
    j = pl.program_id(1)
    n_a = N_IN_A // W_SPLIT_COLS
    i = lax.broadcasted_iota(jnp.int32, (W_SPLIT_COLS, 1), 0)
    col_a = j * W_SPLIT_COLS + i
    col_b = (j - n_a - 1) * W_SPLIT_COLS + i
    scale_a = jnp.where((col_a >= C_Z[1]) & (col_a < C_Z[2]), 0.5, 1.0)
    scale_b = jnp.where(col_b >= C_G0[1], 0.5, 1.0)
    scale_dt = jnp.where(i < SSM_HEADS, 1.0, 0.0)
    scale = jnp.where(j < n_a, scale_a, jnp.where(j > n_a, scale_b, scale_dt))
    w = (wt_ref[...] * scale).astype(BF16).T

    @pl.when(j < n_a)
    def _():
        a_ref[...] = w

    @pl.when(j == n_a)
    def _():
        dt_ref[...] = w[:, 0:LANES]

    @pl.when(j > n_a)
    def _():
        b_ref[...] = w


def _w_in_split(w_in):
    w_t = jnp.swapaxes(w_in, 1, 2)
    n_a, n_b = N_IN_A // W_SPLIT_COLS, N_IN_B // W_SPLIT_COLS

    def first_row(l, j):
        row = jnp.where(j <= n_a, j * W_SPLIT_COLS, N_IN_A + SSM_HEADS + (j - n_a - 1) * W_SPLIT_COLS)
        return pl.multiple_of(row, SSM_HEADS)

    return pl.pallas_call(
        _w_in_split_kernel,
        grid=(DEPTH, n_a + 1 + n_b),
        in_specs=[pl.BlockSpec((pl.Squeezed(), pl.Element(W_SPLIT_COLS), pl.Element(D_MODEL)),
                               lambda l, j: (l, first_row(l, j), 0))],
        out_specs=(pl.BlockSpec((None, D_MODEL, W_SPLIT_COLS), lambda l, j: (l, 0, jnp.minimum(j, n_a - 1))),
                   pl.BlockSpec((None, D_MODEL, W_SPLIT_COLS), lambda l, j: (l, 0, jnp.maximum(j - n_a - 1, 0))),
                   pl.BlockSpec((None, D_MODEL, LANES), lambda l, j: (l, 0, 0))),
        out_shape=tuple(jax.ShapeDtypeStruct((DEPTH, D_MODEL, n), BF16) for n in (N_IN_A, N_IN_B, LANES)),
        compiler_params=_params(2),
        name="w_in_split",
    )(w_t)


def _with_ignored_inputs(body, n_in, n_ignored):
    if n_ignored == 0:
        return body
    return lambda *refs: body(*refs[:n_in], *refs[n_in + n_ignored:])


def _const_spec(shape):
    return pl.BlockSpec(shape, lambda *_: (0,) * len(shape), pipeline_mode=pl.Buffered(1))


def _layer_spec(l, shape):
    return pl.BlockSpec((None,) + shape, lambda *_: (l,) + (0,) * len(shape), pipeline_mode=pl.Buffered(1))


def _any_spec():
    return pl.BlockSpec(memory_space=pl.ANY)


def _smem_spec():
    return pl.BlockSpec(memory_space=pltpu.SMEM)


def _params(n_grid):
    return pltpu.CompilerParams(dimension_semantics=("arbitrary",) * n_grid, vmem_limit_bytes=VMEM_LIMIT)


def _ada_call(c_prompt, c_sample, w_ada, b_ada):
    n_p, n_s = c_prompt.shape[0], c_sample.shape[0]
    c_all = jnp.concatenate([c_prompt, c_sample], axis=0)
    tn = D_MODEL
    n_tiles = w_ada.shape[2] // tn
    mod_p, mod_s = pl.pallas_call(
        _ada_kernel,
        grid=(DEPTH, n_tiles),
        in_specs=[pl.BlockSpec((n_p + n_s, D_MODEL), lambda l, n: (0, 0)),
                  pl.BlockSpec((None, D_MODEL, tn), lambda l, n: (l, 0, n)),
                  pl.BlockSpec((None, 1, tn), lambda l, n: (l, 0, n))],
        out_specs=(pl.BlockSpec((None, n_p, tn), lambda l, n: (l, 0, n)),
                   pl.BlockSpec((None, n_s, tn), lambda l, n: (l, 0, n))),
        out_shape=(jax.ShapeDtypeStruct((DEPTH, n_p, w_ada.shape[2]), F32),
                   jax.ShapeDtypeStruct((DEPTH, n_s, w_ada.shape[2]), F32)),
        compiler_params=_params(2),
        name="ada_mod",
    )(c_all, w_ada, b_ada.reshape(DEPTH, 1, -1))
    return mod_p.reshape(DEPTH, n_p, 6, D_MODEL), mod_s.reshape(DEPTH, n_s, 6, D_MODEL)


def _w_in_specs(l):
    return [_layer_spec(l, (D_MODEL, N_IN_A)), _layer_spec(l, (D_MODEL, N_IN_B)), _layer_spec(l, (D_MODEL, LANES))]


def _layer_weight_specs(l):
    return [
        _layer_spec(l, (CONV_WIDTH, CONV_DIM)), _layer_spec(l, (1, CONV_DIM)), _layer_spec(l, (1, LANES)),
        _layer_spec(l, (1, LANES)), _layer_spec(l, (1, SSM_INNER)), _layer_spec(l, (1, SSM_INNER)),
        _layer_spec(l, (1, GM_WIDTH)), _layer_spec(l, (1, GM_WIDTH)),
        _layer_spec(l, (GM_GROUPS, GM_CHUNK, GM_CHUNK)), _layer_spec(l, (GM_CHUNK, GM_GROUPS)),
        _layer_spec(l, (Q_W, D_MODEL)), _layer_spec(l, (SSM_INNER, D_MODEL)), _layer_spec(l, (GM_WIDTH, D_MODEL)),
        _layer_spec(l, (D_MODEL, D_MODEL)), _const_spec((LANES, SSM_INNER)),
    ]


def _layer_weight_args(p):
    return (p["conv_w"], p["conv_b"], p["dt_bias"], p["a_log"], p["d_skip"], p["ssm_norm_w"], p["gm_ln_g"],
            p["gm_ln_b"], p["gm_w_s"], p["gm_b_st"], p["w_attn_o"], p["w_ssm_o"], p["w_gm_o"], p["w_out"], p["expand"])


def _prompt_mixer(l, x, mod, p, rope, prev):
    bsz, seq, d = x.shape
    ts = PROMPT_TS
    tab = pl.BlockSpec((ts, LANES), lambda b, s: (s, 0))
    in_specs = [
        pl.BlockSpec((1, ts, d), lambda b, s: (b, s, 0)),
        pl.BlockSpec((None, 1, 6, d), lambda b, s: (l, b, 0, 0)),
        _layer_spec(l, (1, d)), *_w_in_specs(l),
        tab, tab, tab, _smem_spec(),
    ] + _layer_weight_specs(l)
    args = (x, mod, p["g_mix"], *p["w_in"], *rope, p["sinks"][l], *_layer_weight_args(p))
    state_tails = ((WINDOW, KV_W), (WINDOW, KV_W), (SSM_INNER, SSM_STATE), (CONV_WIDTH - 1, CONV_DIM))
    out_shape = (jax.ShapeDtypeStruct((bsz, seq, d), F32),) + tuple(
        jax.ShapeDtypeStruct((DEPTH, bsz) + tail, F32) for tail in state_tails)
    out_specs = (pl.BlockSpec((1, ts, d), lambda b, s: (b, s, 0)),) + tuple(
        pl.BlockSpec((None, 1) + tail, lambda b, s: (l, b, 0, 0)) for tail in state_tails)
    scratch = [
        pltpu.VMEM((WINDOW + ts, KV_W), F32), pltpu.VMEM((WINDOW + ts, KV_W), F32),
        pltpu.VMEM((8, CONV_DIM), F32), pltpu.VMEM((SSM_STATE, SSM_INNER), F32),
    ]
    n_in = len(args)
    prev = () if prev is None else tuple(prev)
    return pl.pallas_call(
        _with_ignored_inputs(_prompt_kernel, n_in, len(prev)),
        grid=(bsz, seq // ts), in_specs=in_specs + [_any_spec()] * len(prev), out_specs=out_specs,
        out_shape=out_shape, scratch_shapes=scratch, compiler_params=_params(2), name="prompt_mixer",
        input_output_aliases={n_in + i: 1 + i for i in range(len(prev))},
    )(*args, *prev)


def _sample_mixer(l, x, mod, cache_k, cache_v, h0, conv_state, p, rope, prev):
    bsz, t, d = x.shape
    nb = SAMPLE_NB
    m = nb * t

    def blk(*tail):
        return pl.BlockSpec((nb,) + tail, lambda i: (i,) + (0,) * len(tail))

    def lblk(*tail):
        return pl.BlockSpec((None, nb) + tail, lambda i: (l, i) + (0,) * len(tail))

    in_specs = [
        blk(t, d), lblk(6, d), _layer_spec(l, (1, d)), *_w_in_specs(l),
        _const_spec((m, LANES)), _const_spec((m, LANES)), _const_spec((m, LANES)), _smem_spec(),
        lblk(WINDOW, KV_W), lblk(WINDOW, KV_W), lblk(SSM_INNER, SSM_STATE), lblk(CONV_WIDTH - 1, CONV_DIM),
    ] + _layer_weight_specs(l)
    args = (x, mod, p["g_mix"], *p["w_in"], *rope, p["sinks"][l], cache_k, cache_v, h0, conv_state,
            *_layer_weight_args(p))
    state_tails = ((t, KV_W), (t, KV_W), (SSM_INNER, SSM_STATE), (CONV_WIDTH - 1, CONV_DIM), (t, GM_WIDTH))
    out_shape = (jax.ShapeDtypeStruct((bsz, t, d), F32),) + tuple(
        jax.ShapeDtypeStruct((DEPTH, bsz) + tail, F32) for tail in state_tails)
    out_specs = (blk(t, d),) + tuple(lblk(*tail) for tail in state_tails)
    scratch = [pltpu.VMEM((nb, 8 + t, CONV_DIM), F32)]
    n_in = len(args)
    prev = () if prev is None else tuple(prev)
    return pl.pallas_call(
        _with_ignored_inputs(_sample_kernel, n_in, len(prev)),
        grid=(bsz // nb,), in_specs=in_specs + [_any_spec()] * len(prev), out_specs=out_specs,
        out_shape=out_shape, scratch_shapes=scratch, compiler_params=_params(1), name="sample_mixer",
        input_output_aliases={n_in + i: 1 + i for i in range(len(prev))},
    )(*args, *prev)


def _ffn(l, x, mod, p, g_final, final_norm, name):
    bsz, t, d = x.shape
    if t >= FFN_ROWS:
        nb, tt = 1, FFN_ROWS
    else:
        nb, tt = FFN_ROWS // t, t
    grid = (bsz // nb, t // tt)
    return pl.pallas_call(
        functools.partial(_ffn_kernel, final_norm=final_norm),
        grid=grid,
        in_specs=[pl.BlockSpec((nb, tt, d), lambda b, s: (b, s, 0)),
                  pl.BlockSpec((None, nb, 6, d), lambda b, s: (l, b, 0, 0)),
                  _layer_spec(l, (1, d)), _layer_spec(l, (d, D_FF)), _layer_spec(l, (D_FF, d)), _const_spec((1, d))],
        out_specs=pl.BlockSpec((nb, tt, d), lambda b, s: (b, s, 0)),
        out_shape=jax.ShapeDtypeStruct((bsz, t, d), F32),
        compiler_params=_params(2), name=name,
    )(x, mod, p["g_ff"], p["w_ff1"], p["w_ff2"], g_final)


def _rope_tables(pos):
    half = ROT_DIM // 2
    inv_freq = ROPE_THETA ** (-jnp.arange(half, dtype=F32) * (2.0 / ROT_DIM))
    ang = pos.astype(F32)[:, None] * inv_freq[None, :]
    cos, sin = jnp.cos(ang), jnp.sin(ang)
    n = pos.shape[0]
    ones = jnp.ones((n, HEAD_DIM - ROT_DIM), F32)
    zeros = jnp.zeros((n, HEAD_DIM - ROT_DIM), F32)
    zh = jnp.zeros((n, half), F32)
    cos_t = jnp.concatenate([cos, cos, ones], axis=1)
    sin_a = jnp.concatenate([-sin, zh, zeros], axis=1)
    sin_b = jnp.concatenate([zh, sin, zeros], axis=1)
    rep = LANES // HEAD_DIM
    return tuple(jnp.tile(a, (1, rep)) for a in (cos_t, sin_a, sin_b))


def _stacked_params(w_in, g_mix, sinks, conv_w, conv_b, dt_bias, a_log, d_skip, ssm_norm_w, gm_ln_g, gm_ln_b,
                    gm_w_s, gm_b_s, w_attn_o, w_ssm_o, w_gm_o, w_out, g_ff, w_ff1, w_ff2):
    w_r = _w_in_split(w_in)
    pad = jnp.zeros((DEPTH, LANES - SSM_HEADS), F32)
    expand = (jnp.arange(SSM_INNER)[None, :] // SSM_HEAD_DIM == jnp.arange(LANES)[:, None]).astype(BF16)

    def row(a):
        return a[:, None, :]

    return {
        "w_in": w_r, "g_mix": row(g_mix), "sinks": sinks,
        "conv_w": 0.5 * conv_w, "conv_b": row(0.5 * conv_b),
        "dt_bias": row(jnp.concatenate([dt_bias, pad], axis=1)), "a_log": row(jnp.concatenate([a_log, pad], axis=1)),
        "d_skip": row(jnp.repeat(d_skip, SSM_HEAD_DIM, axis=1)), "ssm_norm_w": row(ssm_norm_w),
        "gm_ln_g": row(gm_ln_g), "gm_ln_b": row(gm_ln_b),
        "gm_w_s": gm_w_s, "gm_b_st": jnp.swapaxes(gm_b_s, 1, 2),
        "w_attn_o": (0.5 * w_attn_o).astype(BF16), "w_ssm_o": (0.5 * w_ssm_o).astype(BF16),
        "w_gm_o": (0.5 * w_gm_o).astype(BF16), "w_out": w_out.astype(BF16), "expand": expand,
        "g_ff": row(g_ff), "w_ff1": w_ff1.astype(BF16), "w_ff2": w_ff2.astype(BF16),
    }


def kernel(x_prompt, x_sample, c_prompt, c_sample, cache_attn_k, cache_attn_v, state_ssm, state_conv, w_ada, b_ada, g_mix, w_in, sinks, conv_w, conv_b, dt_bias, a_log, d_skip, ssm_norm_w, gm_ln_g, gm_ln_b, gm_w_s, gm_b_s, w_attn_o, w_ssm_o, w_gm_o, w_out, g_ff, w_ff1, w_ff2, g_final):
    bp, sp, d = x_prompt.shape
    bs, ss, _ = x_sample.shape
    mod_p, mod_s = _ada_call(c_prompt, c_sample, w_ada, b_ada)
    rope_p = _rope_tables(jnp.arange(sp))
    rope_s = tuple(jnp.tile(a, (SAMPLE_NB, 1)) for a in _rope_tables(PAST_LEN + jnp.arange(ss)))
    g_fin = g_final[None]
    p = _stacked_params(w_in, g_mix, sinks, conv_w, conv_b, dt_bias, a_log, d_skip, ssm_norm_w, gm_ln_g,
                        gm_ln_b, gm_w_s, gm_b_s, w_attn_o, w_ssm_o, w_gm_o, w_out, g_ff, w_ff1, w_ff2)
    ck = cache_attn_k.reshape(DEPTH, bs, WINDOW, KV_W)
    cv = cache_attn_v.reshape(DEPTH, bs, WINDOW, KV_W)
    h0 = state_ssm.reshape(DEPTH, bs, SSM_INNER, SSM_STATE)
    xp, xs = x_prompt, x_sample
    state_p = state_s = None
    for l in range(DEPTH):
        final = l == DEPTH - 1
        xp, *state_p = _prompt_mixer(l, xp, mod_p, p, rope_p, state_p)
        xp = _ffn(l, xp, mod_p, p, g_fin, final, "prompt_ffn")
        xs, *state_s = _sample_mixer(l, xs, mod_s, ck, cv, h0, state_conv, p, rope_s, state_s)
        xs = _ffn(l, xs, mod_s, p, g_fin, final, "sample_ffn")
    kp, vp, ssm_p, conv_p = state_p
    ks, vs, ssm_s, conv_s, gv_s = state_s
    return (xp, xs,
            kp.reshape(DEPTH, bp, WINDOW, N_KV_HEADS, HEAD_DIM), vp.reshape(DEPTH, bp, WINDOW, N_KV_HEADS, HEAD_DIM),
            ssm_p.reshape(DEPTH, bp, SSM_HEADS, SSM_HEAD_DIM, SSM_STATE), conv_p,
            ks.reshape(DEPTH, bs, ss, N_KV_HEADS, HEAD_DIM), vs.reshape(DEPTH, bs, ss, N_KV_HEADS, HEAD_DIM),
            ssm_s.reshape(DEPTH, bs, SSM_HEADS, SSM_HEAD_DIM, SSM_STATE), conv_s, gv_s)
```

```python
import functools

import jax
import jax.numpy as jnp
from jax import lax
from jax.experimental import pallas as pl
from jax.experimental.pallas import tpu as pltpu

F32 = jnp.float32
BF16 = jnp.bfloat16

D_MODEL = 1024
DEPTH = 2
CHUNK = 64
N_HEADS = 8
N_KV_HEADS = 2
HEAD_DIM = 64
GQA_GROUP = N_HEADS // N_KV_HEADS
ROT_DIM = HEAD_DIM // 4
ROPE_THETA = 500000.0
WINDOW = 128
SSM_HEADS = 16
SSM_HEAD_DIM = 64
SSM_INNER = SSM_HEADS * SSM_HEAD_DIM
SSM_GROUPS = 2
SSM_STATE = 128
SSM_CHUNK = 64
CONV_WIDTH = 4
CONV_DIM = SSM_INNER + 2 * SSM_GROUPS * SSM_STATE
GM_WIDTH = 512
GM_GROUPS = 4
GM_GROUP_DIM = GM_WIDTH // GM_GROUPS
GM_CHUNK = 128
D_FF = 4 * D_MODEL
Q_W = N_HEADS * HEAD_DIM
KV_W = N_KV_HEADS * HEAD_DIM
PAST_LEN = 4096
EPS = 1e-6

LANES = 128
HEADS_PER_GROUP = SSM_HEADS // SSM_GROUPS
GROUP_INNER = HEADS_PER_GROUP * SSM_HEAD_DIM

C_Q = (0, 0, Q_W)
C_K = (0, C_Q[2], C_Q[2] + KV_W)
C_V = (0, C_K[2], C_K[2] + KV_W)
C_Z = (0, C_V[2], C_V[2] + SSM_INNER)
C_XBC = (0, C_Z[2], C_Z[2] + CONV_DIM)
N_IN_A = C_XBC[2]
C_GU = (1, 0, GM_WIDTH)
C_GV = (1, C_GU[2], C_GU[2] + GM_WIDTH)
C_G0 = (1, C_GV[2], C_GV[2] + D_MODEL)
C_G1 = (1, C_G0[2], C_G0[2] + D_MODEL)
C_G2 = (1, C_G1[2], C_G1[2] + D_MODEL)
N_IN_B = C_G2[2]
C_DT = (2, 0, LANES)

PROMPT_TS = 512
PROJ_BLOCK = 256
SAMPLE_NB = 8
FFN_ROWS = 512
W_SPLIT_BLOCKS = 2
VMEM_LIMIT = 56 * 1024 * 1024


def _log2(n):
    assert n & (n - 1) == 0, n
    return n.bit_length() - 1


def _dot(a, b):
    return jnp.dot(a, b, preferred_element_type=F32)


def _dot_nt(a, b):
    return lax.dot_general(a, b, (((1,), (1,)), ((), ())), preferred_element_type=F32)


def _dot_tn(a, b):
    return lax.dot_general(a, b, (((0,), (0,)), ((), ())), preferred_element_type=F32)


def _split3(x):
    hi = x.astype(BF16)
    r = x - hi.astype(F32)
    mid = r.astype(BF16)
    lo = (r - mid.astype(F32)).astype(BF16)
    return hi, mid, lo


def _sigmoid(x):
    return 0.5 * (1.0 + jnp.tanh(0.5 * x))


def _silu(x):
    return x * _sigmoid(x)


def _silu_of_half(xh):
    return xh * (1.0 + jnp.tanh(xh))


def _twice_sigmoid_of_half(xh):
    return 1.0 + jnp.tanh(xh)


def _gelu(x):
    return 0.5 * x * (1.0 + jnp.tanh(0.7978845608028654 * (x + 0.044715 * (x * x * x))))


def _softplus(x):
    return jnp.maximum(x, 0.0) + jnp.log1p(jnp.exp(-jnp.abs(x)))


def _rms(x, g):
    return x * lax.rsqrt(jnp.mean(x * x, axis=-1, keepdims=True) + EPS) * g


def _rope(x, cos, sin_a, sin_b):
    return x * cos + pltpu.roll(x, LANES - ROT_DIM // 2, 1) * sin_a + pltpu.roll(x, ROT_DIM // 2, 1) * sin_b


def _proj(h, win_refs, cols):
    return _dot(h, win_refs[cols[0]][:, cols[1]:cols[2]])


def _modnorm(x3, g, mod3):
    nb, t, d = x3.shape
    h3 = _rms(x3, g) * (1.0 + mod3[:, 1:2, :]) + mod3[:, 0:1, :]
    return h3.reshape(nb * t, d).astype(BF16)


def _sink_row(sinks_ref, kv, cols_per_head):
    c = lax.broadcasted_iota(jnp.int32, (1, GQA_GROUP * cols_per_head), 1)
    row = jnp.full((1, GQA_GROUP * cols_per_head), sinks_ref[kv * GQA_GROUP], F32)
    for i in range(1, GQA_GROUP):
        row = jnp.where(c >= i * cols_per_head, sinks_ref[kv * GQA_GROUP + i], row)
    return row


def _stack_heads(q, kv, rows):
    return jnp.concatenate(
        [q[:, (kv * GQA_GROUP + i) * HEAD_DIM:(kv * GQA_GROUP + i + 1) * HEAD_DIM] for i in range(GQA_GROUP)], axis=0)


def _unstack_heads(o, rows):
    return jnp.concatenate([o[i * rows:(i + 1) * rows, :] for i in range(GQA_GROUP)], axis=1)


def _gmlp_chunk(vn, u, gws_ref, gbst_ref, L):
    ri = lax.broadcasted_iota(jnp.int32, (L, L), 0)
    ci = lax.broadcasted_iota(jnp.int32, (L, L), 1)
    outs = []
    for g in range(GM_GROUPS):
        w = jnp.where(ri >= ci, gws_ref[g, 0:L, 0:L], 0.0).astype(BF16)
        v_g = vn[:, g * GM_GROUP_DIM:(g + 1) * GM_GROUP_DIM].astype(BF16)
        outs.append(_dot(w, v_g) + gbst_ref[0:L, g:g + 1])
    return u * jnp.concatenate(outs, axis=1)


def _layer_norm(x, g, b):
    xc = x - jnp.mean(x, axis=-1, keepdims=True)
    return xc * lax.rsqrt(jnp.mean(xc * xc, axis=-1, keepdims=True) + EPS) * g + b


def _merge_out(x3, mod3, gates, a, b, c, wout_ref):
    merged = gates[0] * a + gates[1] * b + gates[2] * c
    o = _dot(merged.astype(BF16), wout_ref[...])
    nb, t, d = x3.shape
    return x3 + mod3[:, 2:3, :] * o.reshape(nb, t, d)


def _attn_groups(ts):
    return [(c, kv) for c in range(ts // CHUNK) for kv in range(N_KV_HEADS)]


def _attn_scores(q, khist, pos0):
    ts = q.shape[0]
    n_keys = WINDOW + CHUNK
    key_i = lax.broadcasted_iota(jnp.int32, (n_keys, GQA_GROUP * CHUNK), 0)
    scores = []
    for c, kv in _attn_groups(ts):
        k_g = khist[c * CHUNK:c * CHUNK + n_keys, kv * HEAD_DIM:(kv + 1) * HEAD_DIM].astype(BF16)
        sc = _dot_nt(k_g, _stack_heads(q[c * CHUNK:(c + 1) * CHUNK, :], kv, CHUNK))
        if c * CHUNK < WINDOW:
            sc = jnp.where(key_i >= WINDOW - c * CHUNK - pos0, sc, -jnp.inf)
        scores.append(sc)
    return scores


def _attn_probs(scores, sinks_ref, ts):
    sink_rows = [_sink_row(sinks_ref, kv, CHUNK) for kv in range(N_KV_HEADS)]
    probs = []
    for (c, kv), sc in zip(_attn_groups(ts), scores):
        m = jnp.maximum(jnp.max(sc, axis=0, keepdims=True), sink_rows[kv])
        p = jnp.exp(sc - m)
        denom = jnp.sum(p, axis=0, keepdims=True) + jnp.exp(sink_rows[kv] - m)
        probs.append((p * (1.0 / denom)).astype(BF16))
    return probs


def _attn_values(probs, vhist, ts):
    n_keys = WINDOW + CHUNK
    outs = []
    for (c, kv), p in zip(_attn_groups(ts), probs):
        v_g = vhist[c * CHUNK:c * CHUNK + n_keys, kv * HEAD_DIM:(kv + 1) * HEAD_DIM].astype(BF16)
        outs.append(_unstack_heads(_dot_tn(p, v_g), CHUNK))
    rows = [jnp.concatenate(outs[c * N_KV_HEADS:(c + 1) * N_KV_HEADS], axis=1) for c in range(ts // CHUNK)]
    return jnp.concatenate(rows, axis=0).astype(BF16)


def _ssd_cumsum(dt, a_row, e):
    ts = dt.shape[0]
    L = SSM_CHUNK
    ri = lax.broadcasted_iota(jnp.int32, (ts, ts), 0)
    ci = lax.broadcasted_iota(jnp.int32, (ts, ts), 1)
    lag = ri - ci
    tri = jnp.where((lag >= 0) & (lag <= (ri & (L - 1))), 1.0, 0.0).astype(BF16)
    hi, mid, lo = _split3(dt * a_row)
    cum = _dot(tri, hi) + _dot(tri, mid) + _dot(tri, lo)
    hi, mid, lo = _split3(cum)
    cum_e = _dot(hi, e) + _dot(mid, e) + _dot(lo, e)
    dt_hi = dt.astype(BF16)
    dt_lo = (dt - dt_hi.astype(F32)).astype(BF16)
    dt_e = _dot(dt_hi, e) + _dot(dt_lo, e)
    return cum_e, dt_e


def _ssd_decays(xact, cum_e, dt_e):
    ts = xact.shape[0]
    L = SSM_CHUNK
    nc = ts // L
    xs = xact[:, 0:SSM_INNER]
    cum_e = cum_e.reshape(nc, L, SSM_INNER)
    t3 = lax.broadcasted_iota(jnp.int32, cum_e.shape, 1)
    slot3 = lax.broadcasted_iota(jnp.int32, cum_e.shape, 2) & (SSM_HEAD_DIM - 1)
    cum_src = jnp.sum(jnp.where(t3 == slot3, cum_e, 0.0), axis=1, keepdims=True)
    decay = jnp.exp(jnp.where(t3 >= slot3, cum_e - cum_src, -jnp.inf))
    last = cum_e[:, L - 1:L, :]
    exp_cum = jnp.exp(cum_e)
    chunk_decay = jnp.exp(last)
    xdt = xs * dt_e
    wx = (xdt.reshape(nc, L, SSM_INNER) * jnp.exp(last - cum_e)).astype(BF16)
    first_head = (lax.broadcasted_iota(jnp.int32, xdt.shape, 1) & (2 * SSM_HEAD_DIM - 1)) < SSM_HEAD_DIM
    return {
        "xs": xs,
        "bm": xact[:, SSM_INNER:SSM_INNER + SSM_GROUPS * SSM_STATE].astype(BF16),
        "cm": xact[:, SSM_INNER + SSM_GROUPS * SSM_STATE:CONV_DIM].astype(BF16),
        "decay": decay, "exp_cum": exp_cum, "chunk_decay": chunk_decay, "wx": wx,
        "xdt_a": jnp.where(first_head, xdt, 0.0).astype(BF16),
        "xdt_b": jnp.where(first_head, 0.0, xdt).astype(BF16),
    }


def _ssd_block(ctx, st, dskip_e, fillers):
    L = SSM_CHUNK
    nc = ctx["xs"].shape[0] // L
    bm, cm = ctx["bm"], ctx["cm"]
    fillers = iter(fillers)

    def grp(a, c, g, width):
        return a[c * L:(c + 1) * L, g * width:(g + 1) * width]

    chunk_state = [jnp.concatenate([_dot_tn(grp(bm, c, g, SSM_STATE), ctx["wx"][c][:, g * GROUP_INNER:(g + 1) * GROUP_INNER])
                                    for g in range(SSM_GROUPS)], axis=1) for c in range(nc)]
    next(fillers)()
    cb = [[_dot_nt(grp(cm, c, g, SSM_STATE), jnp.concatenate([grp(bm, c, g, SSM_STATE)] * 2, axis=0))
           for g in range(SSM_GROUPS)] for c in range(nc)]
    next(fillers)()
    st_in = []
    for c in range(nc):
        st_in.append(st.astype(BF16))
        st = st * ctx["chunk_decay"][c] + chunk_state[c]
    y_intra = []
    for c in range(nc):
        rows = slice(c * L, (c + 1) * L)
        parts = []
        for k in range(SSM_HEADS // 2):
            ls = slice(k * 2 * SSM_HEAD_DIM, (k + 1) * 2 * SSM_HEAD_DIM)
            m_pair = (cb[c][2 * k // HEADS_PER_GROUP] * ctx["decay"][c][:, ls]).astype(BF16)
            block_diag = jnp.concatenate([ctx["xdt_a"][rows, ls], ctx["xdt_b"][rows, ls]], axis=0)
            parts.append(_dot(m_pair, block_diag))
        y_intra.append(jnp.concatenate(parts, axis=1))
    next(fillers)()
    ys = []
    for c in range(nc):
        y_inter = jnp.concatenate(
            [_dot(grp(cm, c, g, SSM_STATE), st_in[c][:, g * GROUP_INNER:(g + 1) * GROUP_INNER])
             for g in range(SSM_GROUPS)], axis=1)
        ys.append(y_intra[c] + y_inter * ctx["exp_cum"][c] + dskip_e * ctx["xs"][c * L:(c + 1) * L])
    next(fillers)()
    return jnp.concatenate(ys, axis=0), st


class _Stagger:
    def __init__(self):
        self._pending = None

    def push(self, produce, consume):
        val = produce()
        self.flush()
        self._pending = (consume, val)

    def flush(self):
        if self._pending is not None:
            consume, val = self._pending
            self._pending = None
            consume(val)


def _prompt_kernel(x_ref, mod_ref, gmix_ref, wa_ref, wb_ref, wdt_ref, cos_ref, sa_ref, sb_ref, sinks_ref,
                   convw_ref, convb_ref, dtb_ref, alog_ref, dskip_ref, ssmnw_ref, glng_ref, glnb_ref,
                   gws_ref, gbst_ref, wao_ref, wso_ref, wgo_ref, wout_ref, e_ref,
                   xo_ref, ko_ref, vo_ref, sto_ref, cvo_ref,
                   khist, vhist, xp, st_s):
    ts = PROMPT_TS
    s = pl.program_id(1)
    last = pl.num_programs(1) - 1

    @pl.when(s == 0)
    def _():
        khist[0:WINDOW, :] = jnp.zeros((WINDOW, KV_W), F32)
        vhist[0:WINDOW, :] = jnp.zeros((WINDOW, KV_W), F32)
        xp[...] = jnp.zeros_like(xp)
        st_s[...] = jnp.zeros_like(st_s)

    x3 = x_ref[...]
    mod3 = mod_ref[...]
    h = _modnorm(x3, gmix_ref[...], mod3)

    cos, sa, sb = cos_ref[...], sa_ref[...], sb_ref[...]
    res = {}

    def rope_block(raw):
        return [_rope(raw[:, i * LANES:(i + 1) * LANES], cos, sa, sb) for i in range(raw.shape[1] // LANES)]

    def q_block(raw):
        res.setdefault("q", []).extend(p.astype(BF16) for p in rope_block(raw * (HEAD_DIM ** -0.5)))

    def kv_block(raw):
        khist[WINDOW:WINDOW + ts, :] = rope_block(raw[:, 0:KV_W])[0]
        vhist[WINDOW:WINDOW + ts, :] = raw[:, KV_W:2 * KV_W]

    def conv_block(lo, raw):
        cs = slice(lo, lo + PROJ_BLOCK)
        ext = jnp.concatenate([xp[:, cs], raw], axis=0)
        acc = convb_ref[:, cs] + raw * convw_ref[CONV_WIDTH - 1:CONV_WIDTH, cs]
        for j in range(1, CONV_WIDTH):
            acc = acc + pltpu.roll(ext, j, 0)[8:8 + ts] * convw_ref[CONV_WIDTH - 1 - j:CONV_WIDTH - j, cs]
        xp[:, cs] = raw[ts - 8:ts]
        res.setdefault("xact", []).append(_silu_of_half(acc))

    def collect(key, fn):
        return lambda raw: res.setdefault(key, []).append(fn(raw))

    wrefs = (wa_ref, wb_ref, wdt_ref)
    pipe = _Stagger()

    def push_proj(cols, off, width, consume):
        ref, lo = wrefs[cols[0]], cols[1] + off
        pipe.push(lambda: _dot(h, ref[:, lo:lo + width]), consume)

    def blocks(cols, consume):
        return [functools.partial(push_proj, cols, off, PROJ_BLOCK, consume)
                for off in range(0, cols[2] - cols[1], PROJ_BLOCK)]

    conv_blocks = [functools.partial(push_proj, C_XBC, off, PROJ_BLOCK, functools.partial(conv_block, off))
                   for off in range(0, CONV_DIM, PROJ_BLOCK)]
    gate_blocks = iter(blocks(C_G0, collect("g0", _twice_sigmoid_of_half))
                       + blocks(C_G1, collect("g1", _twice_sigmoid_of_half))
                       + blocks(C_G2, collect("g2", _twice_sigmoid_of_half)))
    z_blocks = blocks(C_Z, collect("sz", _silu_of_half))
    gm_blocks = blocks(C_GU, collect("u", _gelu)) + blocks(C_GV, collect("gv", _gelu))

    def cat(key):
        return jnp.concatenate(res[key], axis=1)

    for blk in blocks(C_Q, q_block):
        blk()
    push_proj(C_K, 0, 2 * KV_W, kv_block)
    conv_blocks[0]()
    next(gate_blocks)()
    conv_blocks[1]()
    scores = _attn_scores(cat("q"), khist, s * ts)
    conv_blocks[2]()
    probs = _attn_probs(scores, sinks_ref, ts)
    next(gate_blocks)()
    conv_blocks[3]()
    attn = _attn_values(probs, vhist, ts)
    conv_blocks[4]()
    next(gate_blocks)()
    conv_blocks[5]()
    push_proj(C_DT, 0, LANES, collect("dt", lambda raw: _softplus(raw + dtb_ref[...])))
    next(gate_blocks)()
    a_out = _dot(attn, wao_ref[...])
    z_blocks[0]()
    cum_e, dt_e = _ssd_cumsum(res["dt"][0], -jnp.exp(alog_ref[...]), e_ref[...])
    z_blocks[1]()
    merged = cat("g0") * a_out
    ctx = _ssd_decays(cat("xact"), cum_e, dt_e)
    z_blocks[2]()
    z_blocks[3]()
    y, st_s[...] = _ssd_block(ctx, st_s[...], dskip_ref[...], gm_blocks)
    next(gate_blocks)()
    vn = _layer_norm(cat("gv"), glng_ref[...], glnb_ref[...])
    u = cat("u")
    next(gate_blocks)()
    ssm = _rms(y * cat("sz"), ssmnw_ref[...]).astype(BF16)
    gm = jnp.concatenate(
        [_gmlp_chunk(vn[c * GM_CHUNK:(c + 1) * GM_CHUNK], u[c * GM_CHUNK:(c + 1) * GM_CHUNK], gws_ref, gbst_ref,
                     GM_CHUNK) for c in range(ts // GM_CHUNK)], axis=0).astype(BF16)
    next(gate_blocks)()
    b_out = _dot(ssm, wso_ref[...])
    next(gate_blocks)()
    pipe.flush()
    merged = merged + cat("g1") * b_out
    c_out = _dot(gm, wgo_ref[...])
    for blk in gate_blocks:
        blk()
    pipe.flush()
    merged = merged + cat("g2") * c_out
    o = _dot(merged.astype(BF16), wout_ref[...])
    xo_ref[...] = x3 + mod3[:, 2:3, :] * o.reshape(x3.shape)

    khist[0:WINDOW, :] = khist[ts:ts + WINDOW, :]
    vhist[0:WINDOW, :] = vhist[ts:ts + WINDOW, :]

    @pl.when(s == last)
    def _():
        ko_ref[0] = khist[ts:ts + WINDOW, :]
        vo_ref[0] = vhist[ts:ts + WINDOW, :]
        cvo_ref[0] = xp[8 - (CONV_WIDTH - 1):8, :]
        sto_ref[0] = st_s[...].T


def _sample_attention(q, k_new, v_new, cache_k, cache_v, sinks_ref):
    t = q.shape[0]
    outs = []
    for kv in range(N_KV_HEADS):
        hs = slice(kv * HEAD_DIM, (kv + 1) * HEAD_DIM)
        qg = _stack_heads(q, kv, t)
        s_c = _dot_nt(cache_k[:, hs].astype(BF16), qg)
        s_n = _dot_nt(k_new[:, hs].astype(BF16), qg)
        sink = _sink_row(sinks_ref, kv, t)
        m = jnp.maximum(jnp.maximum(jnp.max(s_c, axis=0, keepdims=True), jnp.max(s_n, axis=0, keepdims=True)), sink)
        p_c = jnp.exp(s_c - m)
        p_n = jnp.exp(s_n - m)
        inv = 1.0 / (jnp.sum(p_c, axis=0, keepdims=True) + jnp.sum(p_n, axis=0, keepdims=True) + jnp.exp(sink - m))
        o = (_dot_tn((p_c * inv).astype(BF16), cache_v[:, hs].astype(BF16))
             + _dot_tn((p_n * inv).astype(BF16), v_new[:, hs].astype(BF16)))
        outs.append(_unstack_heads(o, t))
    return jnp.concatenate(outs, axis=1)


def _sample_ssd_setup(xact, dt, a_row, e, L):
    m = xact.shape[0]
    n_seq = m // L
    xs = xact[:, 0:SSM_INNER]
    ri = lax.broadcasted_iota(jnp.int32, (m, m), 0)
    lag = ri - lax.broadcasted_iota(jnp.int32, (m, m), 1)
    tri = jnp.where((lag >= 0) & (lag <= (ri & (L - 1))), 1.0, 0.0).astype(BF16)
    hi, mid, lo = _split3(dt * a_row)
    cum = _dot(tri, hi) + _dot(tri, mid) + _dot(tri, lo)
    hi, mid, lo = _split3(cum)
    cum_e = (_dot(hi, e) + _dot(mid, e) + _dot(lo, e)).reshape(n_seq, L, SSM_INNER)
    head_of_lane = jnp.right_shift(lax.broadcasted_iota(jnp.int32, (LANES, SSM_HEADS * L), 1), _log2(L))
    e_slots = jnp.where(head_of_lane == lax.broadcasted_iota(jnp.int32, (LANES, SSM_HEADS * L), 0), 1.0, 0.0)
    e_slots = e_slots.astype(BF16)
    cum_s = (_dot(hi, e_slots) + _dot(mid, e_slots) + _dot(lo, e_slots)).reshape(n_seq, L, SSM_HEADS * L)
    dt_hi = dt.astype(BF16)
    dt_lo = (dt - dt_hi.astype(F32)).astype(BF16)
    dt_e = _dot(dt_hi, e) + _dot(dt_lo, e)
    t3 = lax.broadcasted_iota(jnp.int32, cum_s.shape, 1)
    slot3 = lax.broadcasted_iota(jnp.int32, cum_s.shape, 2) & (L - 1)
    cum_src = jnp.sum(jnp.where(t3 == slot3, cum_s, 0.0), axis=1, keepdims=True)
    decay = jnp.exp(jnp.where(t3 >= slot3, cum_s - cum_src, -jnp.inf))
    last = cum_e[:, L - 1:L, :]
    xdt = xs * dt_e
    wx = (xdt.reshape(n_seq, L, SSM_INNER) * jnp.exp(last - cum_e)).astype(BF16)
    per_head = jnp.exp(cum.reshape(n_seq, L, LANES)[:, L - 1:L, :])
    rep = jnp.broadcast_to(per_head, (n_seq, SSM_HEADS, LANES)).reshape(n_seq * SSM_HEADS, LANES)
    own_lane = (lax.broadcasted_iota(jnp.int32, rep.shape, 1)
                == (lax.broadcasted_iota(jnp.int32, rep.shape, 0) & (SSM_HEADS - 1)))
    hi, mid, lo = _split3(jnp.where(own_lane, rep, 0.0))
    ones = jnp.ones((LANES, LANES), BF16)
    splat = (_dot(hi, ones) + _dot(mid, ones) + _dot(lo, ones)).reshape(n_seq, SSM_HEADS, LANES)
    return {
        "L": L, "xs": xs,
        "bm": xact[:, SSM_INNER:SSM_INNER + SSM_GROUPS * SSM_STATE].astype(BF16),
        "cm": xact[:, SSM_INNER + SSM_GROUPS * SSM_STATE:CONV_DIM].astype(BF16),
        "decay": decay, "exp_cum": jnp.exp(cum_e), "wx": wx, "xdt": xdt.astype(BF16), "state_decay": splat,
    }


def _sample_ssd_seq(i, s, h0, dskip_e):
    L = s["L"]
    rows = slice(i * L, (i + 1) * L)
    h0_b = h0.astype(BF16)
    ri = jnp.right_shift(lax.broadcasted_iota(jnp.int32, (HEADS_PER_GROUP * L, GROUP_INNER), 0), _log2(L))
    li = jnp.right_shift(lax.broadcasted_iota(jnp.int32, (HEADS_PER_GROUP * L, GROUP_INNER), 1), _log2(SSM_HEAD_DIM))
    y_parts, upd = [], []
    for g in range(SSM_GROUPS):
        gn = slice(g * SSM_STATE, (g + 1) * SSM_STATE)
        gp = slice(g * GROUP_INNER, (g + 1) * GROUP_INNER)
        gs = slice(g * HEADS_PER_GROUP * L, (g + 1) * HEADS_PER_GROUP * L)
        bm_g, cm_g = s["bm"][rows, gn], s["cm"][rows, gn]
        cb = _dot_nt(cm_g, jnp.concatenate([bm_g] * HEADS_PER_GROUP, axis=0))
        m_g = (cb * s["decay"][i][:, gs]).astype(BF16)
        tiled = jnp.concatenate([s["xdt"][rows, gp]] * HEADS_PER_GROUP, axis=0)
        block_diag = jnp.where(ri == li, tiled, jnp.zeros_like(tiled))
        y_g = _dot(m_g, block_diag) + _dot_nt(cm_g, h0_b[gp, :]) * s["exp_cum"][i][:, gp]
        y_parts.append(y_g)
        upd.append(_dot_tn(s["wx"][i][:, gp], bm_g))
    y = jnp.concatenate(y_parts, axis=1) + dskip_e * s["xs"][rows]
    decayed = h0.reshape(SSM_HEADS, SSM_HEAD_DIM, SSM_STATE) * s["state_decay"][i][:, None, :]
    return y, decayed.reshape(SSM_INNER, SSM_STATE) + jnp.concatenate(upd, axis=0)


def _sample_kernel(x_ref, mod_ref, gmix_ref, wa_ref, wb_ref, wdt_ref, cos_ref, sa_ref, sb_ref, sinks_ref,
                   ck_ref, cv_ref, h0_ref, cs_ref,
                   convw_ref, convb_ref, dtb_ref, alog_ref, dskip_ref, ssmnw_ref, glng_ref, glnb_ref,
                   gws_ref, gbst_ref, wao_ref, wso_ref, wgo_ref, wout_ref, e_ref,
                   xo_ref, ko_ref, vo_ref, sto_ref, cvo_ref, gvo_ref,
                   xp):
    nb = SAMPLE_NB
    t = x_ref.shape[1]
    m = nb * t
    win_ref = (wa_ref, wb_ref, wdt_ref)
    x3 = x_ref[...]
    mod3 = mod_ref[...]
    h = _modnorm(x3, gmix_ref[...], mod3)

    cos, sa, sb = cos_ref[...], sa_ref[...], sb_ref[...]
    q = _proj(h, win_ref, C_Q) * (HEAD_DIM ** -0.5)
    q = jnp.concatenate(
        [_rope(q[:, i * LANES:(i + 1) * LANES], cos, sa, sb) for i in range(Q_W // LANES)], axis=1).astype(BF16)
    k_new = _rope(_proj(h, win_ref, C_K), cos, sa, sb)
    v_new = _proj(h, win_ref, C_V)
    ko_ref[...] = k_new.reshape(nb, t, KV_W)
    vo_ref[...] = v_new.reshape(nb, t, KV_W)

    xp[:, 8 - (CONV_WIDTH - 1):8, :] = cs_ref[...]
    xp[:, 8:8 + t, :] = _proj(h, win_ref, C_XBC).reshape(nb, t, CONV_DIM)
    acc = convb_ref[...] + xp[:, 8:8 + t, :] * convw_ref[CONV_WIDTH - 1:CONV_WIDTH, :]
    for j in range(1, CONV_WIDTH):
        acc = acc + xp[:, 8 - j:8 - j + t, :] * convw_ref[CONV_WIDTH - 1 - j:CONV_WIDTH - j, :]
    xact = _silu_of_half(acc).reshape(m, CONV_DIM)
    cvo_ref[...] = xp[:, t + 8 - (CONV_WIDTH - 1):t + 8, :]
    dt = _softplus(_proj(h, win_ref, C_DT) + dtb_ref[...])
    dskip_e = dskip_ref[...]

    u = _gelu(_proj(h, win_ref, C_GU))
    vn = _layer_norm(_gelu(_proj(h, win_ref, C_GV)), glng_ref[...], glnb_ref[...])
    gvo_ref[...] = vn.reshape(nb, t, GM_WIDTH)

    ssd = _sample_ssd_setup(xact, dt, -jnp.exp(alog_ref[...]), e_ref[...], t)
    attn_rows, y_rows, gm_rows = [], [], []
    for i in range(nb):
        rows = slice(i * t, (i + 1) * t)
        attn_rows.append(_sample_attention(q[rows], k_new[rows], v_new[rows], ck_ref[i], cv_ref[i], sinks_ref))
        y_i, sto_ref[i] = _sample_ssd_seq(i, ssd, h0_ref[i], dskip_e)
        y_rows.append(y_i)
        gm_rows.append(_gmlp_chunk(vn[rows], u[rows], gws_ref, gbst_ref, t))

    a_out = _dot(jnp.concatenate(attn_rows, axis=0).astype(BF16), wao_ref[...])
    z = _proj(h, win_ref, C_Z)
    ssm = _rms(jnp.concatenate(y_rows, axis=0) * _silu_of_half(z), ssmnw_ref[...]).astype(BF16)
    b_out = _dot(ssm, wso_ref[...])
    c_out = _dot(jnp.concatenate(gm_rows, axis=0).astype(BF16), wgo_ref[...])
    gates = [_twice_sigmoid_of_half(_proj(h, win_ref, cols)) for cols in (C_G0, C_G1, C_G2)]
    xo_ref[...] = _merge_out(x3, mod3, gates, a_out, b_out, c_out, wout_ref)


def _ffn_kernel(x_ref, mod_ref, gff_ref, w1_ref, w2_ref, gfin_ref, o_ref, *, final_norm):
    x3 = x_ref[...]
    mod3 = mod_ref[...]
    nb, t, d = x3.shape
    h3 = _rms(x3, gff_ref[...]) * (1.0 + mod3[:, 4:5, :]) + mod3[:, 3:4, :]
    h = h3.reshape(nb * t, d).astype(BF16)
    a = jnp.maximum(_dot(h, w1_ref[...]), 0.0)
    y = _dot((a * a).astype(BF16), w2_ref[...])
    out = x3 + mod3[:, 5:6, :] * y.reshape(nb, t, d)
    if final_norm:
        out = _rms(out, gfin_ref[...])
    o_ref[...] = out


def _ada_kernel(c_ref, w_ref, b_ref, op_ref, os_ref):
    mod = _dot(_silu(c_ref[...]).astype(BF16), w_ref[...].astype(BF16)) + b_ref[...]
    n_prompt = op_ref.shape[0]
    op_ref[...] = mod[0:n_prompt]
    os_ref[...] = mod[n_prompt:]


def _w_in_piece_kernel(wt_ref, o_ref, *, half_lo, half_hi):
    cols = wt_ref.shape[0]
    col = pl.program_id(1) * cols + lax.broadcasted_iota(jnp.int32, (cols, 1), 0)
    scale = jnp.where((col >= half_lo) & (col < half_hi), 0.5, 1.0)
    o_ref[...] = (wt_ref[...] * scale).astype(BF16).T


def _w_in_piece(w_t, first_col, n_cols, half_cols):
    cols = n_cols // W_SPLIT_BLOCKS
    return pl.pallas_call(
        functools.partial(_w_in_piece_kernel, half_lo=half_cols[0], half_hi=half_cols[1]),
        grid=(DEPTH, W_SPLIT_BLOCKS),
        in_specs=[pl.BlockSpec((pl.Squeezed(), pl.Element(cols), pl.Element(D_MODEL)),
                               lambda l, j: (l, pl.multiple_of(first_col + j * cols, SSM_HEADS), 0))],
        out_specs=pl.BlockSpec((None, D_MODEL, cols), lambda l, j: (l, 0, j)),
        out_shape=jax.ShapeDtypeStruct((DEPTH, D_MODEL, n_cols), BF16),
        compiler_params=_params(2),
        name="w_in_piece",
    )(w_t)


def _w_in_split(w_in):
    w_t = jnp.swapaxes(w_in, 1, 2)
    b_lo = N_IN_A + SSM_HEADS
    w_dt = jnp.pad(jnp.swapaxes(w_t[:, N_IN_A:b_lo, :], 1, 2), ((0, 0), (0, 0), (0, LANES - SSM_HEADS)))
    return (_w_in_piece(w_t, 0, N_IN_A, (C_Z[1], C_Z[2])), _w_in_piece(w_t, b_lo, N_IN_B, (C_G0[1], N_IN_B)),
            w_dt.astype(BF16))


def _with_ignored_inputs(body, n_in, n_ignored):
    if n_ignored == 0:
        return body
    return lambda *refs: body(*refs[:n_in], *refs[n_in + n_ignored:])


def _const_spec(shape):
    return pl.BlockSpec(shape, lambda *_: (0,) * len(shape), pipeline_mode=pl.Buffered(1))


def _layer_spec(l, shape):
    return pl.BlockSpec((None,) + shape, lambda *_: (l,) + (0,) * len(shape), pipeline_mode=pl.Buffered(1))


def _any_spec():
    return pl.BlockSpec(memory_space=pl.ANY)


def _smem_spec():
    return pl.BlockSpec(memory_space=pltpu.SMEM)


def _params(n_grid):
    return pltpu.CompilerParams(dimension_semantics=("arbitrary",) * n_grid, vmem_limit_bytes=VMEM_LIMIT)


def _ada_call(c_prompt, c_sample, w_ada, b_ada):
    n_p, n_s = c_prompt.shape[0], c_sample.shape[0]
    c_all = jnp.concatenate([c_prompt, c_sample], axis=0)
    tn = D_MODEL
    n_tiles = w_ada.shape[2] // tn
    mod_p, mod_s = pl.pallas_call(
        _ada_kernel,
        grid=(DEPTH, n_tiles),
        in_specs=[pl.BlockSpec((n_p + n_s, D_MODEL), lambda l, n: (0, 0)),
                  pl.BlockSpec((None, D_MODEL, tn), lambda l, n: (l, 0, n)),
                  pl.BlockSpec((None, 1, tn), lambda l, n: (l, 0, n))],
        out_specs=(pl.BlockSpec((None, n_p, tn), lambda l, n: (l, 0, n)),
                   pl.BlockSpec((None, n_s, tn), lambda l, n: (l, 0, n))),
        out_shape=(jax.ShapeDtypeStruct((DEPTH, n_p, w_ada.shape[2]), F32),
                   jax.ShapeDtypeStruct((DEPTH, n_s, w_ada.shape[2]), F32)),
        compiler_params=_params(2),
        name="ada_mod",
    )(c_all, w_ada, b_ada.reshape(DEPTH, 1, -1))
    return mod_p.reshape(DEPTH, n_p, 6, D_MODEL), mod_s.reshape(DEPTH, n_s, 6, D_MODEL)


def _w_in_specs(l):
    return [_layer_spec(l, (D_MODEL, N_IN_A)), _layer_spec(l, (D_MODEL, N_IN_B)), _layer_spec(l, (D_MODEL, LANES))]


def _layer_weight_specs(l):
    return [
        _layer_spec(l, (CONV_WIDTH, CONV_DIM)), _layer_spec(l, (1, CONV_DIM)), _layer_spec(l, (1, LANES)),
        _layer_spec(l, (1, LANES)), _layer_spec(l, (1, SSM_INNER)), _layer_spec(l, (1, SSM_INNER)),
        _layer_spec(l, (1, GM_WIDTH)), _layer_spec(l, (1, GM_WIDTH)),
        _layer_spec(l, (GM_GROUPS, GM_CHUNK, GM_CHUNK)), _layer_spec(l, (GM_CHUNK, GM_GROUPS)),
        _layer_spec(l, (Q_W, D_MODEL)), _layer_spec(l, (SSM_INNER, D_MODEL)), _layer_spec(l, (GM_WIDTH, D_MODEL)),
        _layer_spec(l, (D_MODEL, D_MODEL)), _const_spec((LANES, SSM_INNER)),
    ]


def _layer_weight_args(p):
    return (p["conv_w"], p["conv_b"], p["dt_bias"], p["a_log"], p["d_skip"], p["ssm_norm_w"], p["gm_ln_g"],
            p["gm_ln_b"], p["gm_w_s"], p["gm_b_st"], p["w_attn_o"], p["w_ssm_o"], p["w_gm_o"], p["w_out"], p["expand"])


def _prompt_mixer(l, x, mod, p, rope, prev):
    bsz, seq, d = x.shape
    ts = PROMPT_TS
    tab = pl.BlockSpec((ts, LANES), lambda b, s: (s, 0))
    in_specs = [
        pl.BlockSpec((1, ts, d), lambda b, s: (b, s, 0)),
        pl.BlockSpec((None, 1, 6, d), lambda b, s: (l, b, 0, 0)),
        _layer_spec(l, (1, d)), *_w_in_specs(l),
        tab, tab, tab, _smem_spec(),
    ] + _layer_weight_specs(l)
    args = (x, mod, p["g_mix"], *p["w_in"], *rope, p["sinks"][l], *_layer_weight_args(p))
    state_tails = ((WINDOW, KV_W), (WINDOW, KV_W), (SSM_INNER, SSM_STATE), (CONV_WIDTH - 1, CONV_DIM))
    out_shape = (jax.ShapeDtypeStruct((bsz, seq, d), F32),) + tuple(
        jax.ShapeDtypeStruct((DEPTH, bsz) + tail, F32) for tail in state_tails)
    out_specs = (pl.BlockSpec((1, ts, d), lambda b, s: (b, s, 0)),) + tuple(
        pl.BlockSpec((None, 1) + tail, lambda b, s: (l, b, 0, 0)) for tail in state_tails)
    scratch = [
        pltpu.VMEM((WINDOW + ts, KV_W), F32), pltpu.VMEM((WINDOW + ts, KV_W), F32),
        pltpu.VMEM((8, CONV_DIM), F32), pltpu.VMEM((SSM_STATE, SSM_INNER), F32),
    ]
    n_in = len(args)
    prev = () if prev is None else tuple(prev)
    return pl.pallas_call(
        _with_ignored_inputs(_prompt_kernel, n_in, len(prev)),
        grid=(bsz, seq // ts), in_specs=in_specs + [_any_spec()] * len(prev), out_specs=out_specs,
        out_shape=out_shape, scratch_shapes=scratch, compiler_params=_params(2), name="prompt_mixer",
        input_output_aliases={n_in + i: 1 + i for i in range(len(prev))},
    )(*args, *prev)


def _sample_mixer(l, x, mod, cache_k, cache_v, h0, conv_state, p, rope, prev):
    bsz, t, d = x.shape
    nb = SAMPLE_NB
    m = nb * t

    def blk(*tail):
        return pl.BlockSpec((nb,) + tail, lambda i: (i,) + (0,) * len(tail))

    def lblk(*tail):
        return pl.BlockSpec((None, nb) + tail, lambda i: (l, i) + (0,) * len(tail))

    in_specs = [
        blk(t, d), lblk(6, d), _layer_spec(l, (1, d)), *_w_in_specs(l),
        _const_spec((m, LANES)), _const_spec((m, LANES)), _const_spec((m, LANES)), _smem_spec(),
        lblk(WINDOW, KV_W), lblk(WINDOW, KV_W), lblk(SSM_INNER, SSM_STATE), lblk(CONV_WIDTH - 1, CONV_DIM),
    ] + _layer_weight_specs(l)
    args = (x, mod, p["g_mix"], *p["w_in"], *rope, p["sinks"][l], cache_k, cache_v, h0, conv_state,
            *_layer_weight_args(p))
    state_tails = ((t, KV_W), (t, KV_W), (SSM_INNER, SSM_STATE), (CONV_WIDTH - 1, CONV_DIM), (t, GM_WIDTH))
    out_shape = (jax.ShapeDtypeStruct((bsz, t, d), F32),) + tuple(
        jax.ShapeDtypeStruct((DEPTH, bsz) + tail, F32) for tail in state_tails)
    out_specs = (blk(t, d),) + tuple(lblk(*tail) for tail in state_tails)
    scratch = [pltpu.VMEM((nb, 8 + t, CONV_DIM), F32)]
    n_in = len(args)
    prev = () if prev is None else tuple(prev)
    return pl.pallas_call(
        _with_ignored_inputs(_sample_kernel, n_in, len(prev)),
        grid=(bsz // nb,), in_specs=in_specs + [_any_spec()] * len(prev), out_specs=out_specs,
        out_shape=out_shape, scratch_shapes=scratch, compiler_params=_params(1), name="sample_mixer",
        input_output_aliases={n_in + i: 1 + i for i in range(len(prev))},
    )(*args, *prev)


def _ffn(l, x, mod, p, g_final, final_norm, name):
    bsz, t, d = x.shape
    if t >= FFN_ROWS:
        nb, tt = 1, FFN_ROWS
    else:
        nb, tt = FFN_ROWS // t, t
    grid = (bsz // nb, t // tt)
    return pl.pallas_call(
        functools.partial(_ffn_kernel, final_norm=final_norm),
        grid=grid,
        in_specs=[pl.BlockSpec((nb, tt, d), lambda b, s: (b, s, 0)),
                  pl.BlockSpec((None, nb, 6, d), lambda b, s: (l, b, 0, 0)),
                  _layer_spec(l, (1, d)), _layer_spec(l, (d, D_FF)), _layer_spec(l, (D_FF, d)), _const_spec((1, d))],
        out_specs=pl.BlockSpec((nb, tt, d), lambda b, s: (b, s, 0)),
        out_shape=jax.ShapeDtypeStruct((bsz, t, d), F32),
        compiler_params=_params(2), name=name,
    )(x, mod, p["g_ff"], p["w_ff1"], p["w_ff2"], g_final)


def _rope_tables(pos):
    half = ROT_DIM // 2
    inv_freq = ROPE_THETA ** (-jnp.arange(half, dtype=F32) * (2.0 / ROT_DIM))
    ang = pos.astype(F32)[:, None] * inv_freq[None, :]
    cos, sin = jnp.cos(ang), jnp.sin(ang)
    n = pos.shape[0]
    ones = jnp.ones((n, HEAD_DIM - ROT_DIM), F32)
    zeros = jnp.zeros((n, HEAD_DIM - ROT_DIM), F32)
    zh = jnp.zeros((n, half), F32)
    cos_t = jnp.concatenate([cos, cos, ones], axis=1)
    sin_a = jnp.concatenate([-sin, zh, zeros], axis=1)
    sin_b = jnp.concatenate([zh, sin, zeros], axis=1)
    rep = LANES // HEAD_DIM
    return tuple(jnp.tile(a, (1, rep)) for a in (cos_t, sin_a, sin_b))


def _stacked_params(w_in, g_mix, sinks, conv_w, conv_b, dt_bias, a_log, d_skip, ssm_norm_w, gm_ln_g, gm_ln_b,
                    gm_w_s, gm_b_s, w_attn_o, w_ssm_o, w_gm_o, w_out, g_ff, w_ff1, w_ff2):
    w_r = _w_in_split(w_in)
    pad = jnp.zeros((DEPTH, LANES - SSM_HEADS), F32)
    expand = (jnp.arange(SSM_INNER)[None, :] // SSM_HEAD_DIM == jnp.arange(LANES)[:, None]).astype(BF16)

    def row(a):
        return a[:, None, :]

    return {
        "w_in": w_r, "g_mix": row(g_mix), "sinks": sinks,
        "conv_w": 0.5 * conv_w, "conv_b": row(0.5 * conv_b),
        "dt_bias": row(jnp.concatenate([dt_bias, pad], axis=1)), "a_log": row(jnp.concatenate([a_log, pad], axis=1)),
        "d_skip": row(jnp.repeat(d_skip, SSM_HEAD_DIM, axis=1)), "ssm_norm_w": row(ssm_norm_w),
        "gm_ln_g": row(gm_ln_g), "gm_ln_b": row(gm_ln_b),
        "gm_w_s": gm_w_s, "gm_b_st": jnp.swapaxes(gm_b_s, 1, 2),
        "w_attn_o": (0.5 * w_attn_o).astype(BF16), "w_ssm_o": (0.5 * w_ssm_o).astype(BF16),
        "w_gm_o": (0.5 * w_gm_o).astype(BF16), "w_out": w_out.astype(BF16), "expand": expand,
        "g_ff": row(g_ff), "w_ff1": w_ff1.astype(BF16), "w_ff2": w_ff2.astype(BF16),
    }


def kernel(x_prompt, x_sample, c_prompt, c_sample, cache_attn_k, cache_attn_v, state_ssm, state_conv, w_ada, b_ada, g_mix, w_in, sinks, conv_w, conv_b, dt_bias, a_log, d_skip, ssm_norm_w, gm_ln_g, gm_ln_b, gm_w_s, gm_b_s, w_attn_o, w_ssm_o, w_gm_o, w_out, g_ff, w_ff1, w_ff2, g_final):
    bp, sp, d = x_prompt.shape
    bs, ss, _ = x_sample.shape
    mod_p, mod_s = _ada_call(c_prompt, c_sample, w_ada, b_ada)
    rope_p = _rope_tables(jnp.arange(sp))
    rope_s = tuple(jnp.tile(a, (SAMPLE_NB, 1)) for a in _rope_tables(PAST_LEN + jnp.arange(ss)))
    g_fin = g_final[None]
    p = _stacked_params(w_in, g_mix, sinks, conv_w, conv_b, dt_bias, a_log, d_skip, ssm_norm_w, gm_ln_g,
                        gm_ln_b, gm_w_s, gm_b_s, w_attn_o, w_ssm_o, w_gm_o, w_out, g_ff, w_ff1, w_ff2)
    ck = cache_attn_k.reshape(DEPTH, bs, WINDOW, KV_W)
    cv = cache_attn_v.reshape(DEPTH, bs, WINDOW, KV_W)
    h0 = state_ssm.reshape(DEPTH, bs, SSM_INNER, SSM_STATE)
    xp, xs = x_prompt, x_sample
    state_p = state_s = None
    for l in range(DEPTH):
        final = l == DEPTH - 1
        xp, *state_p = _prompt_mixer(l, xp, mod_p, p, rope_p, state_p)
        xp = _ffn(l, xp, mod_p, p, g_fin, final, "prompt_ffn")
        xs, *state_s = _sample_mixer(l, xs, mod_s, ck, cv, h0, state_conv, p, rope_s, state_s)
        xs = _ffn(l, xs, mod_s, p, g_fin, final, "sample_ffn")
    kp, vp, ssm_p, conv_p = state_p
    ks, vs, ssm_s, conv_s, gv_s = state_s
    return (xp, xs,
            kp.reshape(DEPTH, bp, WINDOW, N_KV_HEADS, HEAD_DIM), vp.reshape(DEPTH, bp, WINDOW, N_KV_HEADS, HEAD_DIM),
            ssm_p.reshape(DEPTH, bp, SSM_HEADS, SSM_HEAD_DIM, SSM_STATE), conv_p,
            ks.reshape(DEPTH, bs, ss, N_KV_HEADS, HEAD_DIM), vs.reshape(DEPTH, bs, ss, N_KV_HEADS, HEAD_DIM),
            ssm_s.reshape(DEPTH, bs, SSM_HEADS, SSM_HEAD_DIM, SSM_STATE), conv_s, gv_s)
```

```python
import functools

import jax
import jax.numpy as jnp
from jax import lax
from jax.experimental import pallas as pl
from jax.experimental.pallas import tpu as pltpu

F32 = jnp.float32
BF16 = jnp.bfloat16

D_MODEL = 1024
DEPTH = 2
CHUNK = 64
N_HEADS = 8
N_KV_HEADS = 2
HEAD_DIM = 64
GQA_GROUP = N_HEADS // N_KV_HEADS
ROT_DIM = HEAD_DIM // 4
ROPE_THETA = 500000.0
WINDOW = 128
SSM_HEADS = 16
SSM_HEAD_DIM = 64
SSM_INNER = SSM_HEADS * SSM_HEAD_DIM
SSM_GROUPS = 2
SSM_STATE = 128
SSM_CHUNK = 64
CONV_WIDTH = 4
CONV_DIM = SSM_INNER + 2 * SSM_GROUPS * SSM_STATE
GM_WIDTH = 512
GM_GROUPS = 4
GM_GROUP_DIM = GM_WIDTH // GM_GROUPS
GM_CHUNK = 128
D_FF = 4 * D_MODEL
Q_W = N_HEADS * HEAD_DIM
KV_W = N_KV_HEADS * HEAD_DIM
PAST_LEN = 4096
EPS = 1e-6

LANES = 128
HEADS_PER_GROUP = SSM_HEADS // SSM_GROUPS
GROUP_INNER = HEADS_PER_GROUP * SSM_HEAD_DIM

C_Q = (0, 0, Q_W)
C_K = (0, C_Q[2], C_Q[2] + KV_W)
C_V = (0, C_K[2], C_K[2] + KV_W)
C_Z = (0, C_V[2], C_V[2] + SSM_INNER)
C_XBC = (0, C_Z[2], C_Z[2] + CONV_DIM)
N_IN_A = C_XBC[2]
C_GU = (1, 0, GM_WIDTH)
C_GV = (1, C_GU[2], C_GU[2] + GM_WIDTH)
C_G0 = (1, C_GV[2], C_GV[2] + D_MODEL)
C_G1 = (1, C_G0[2], C_G0[2] + D_MODEL)
C_G2 = (1, C_G1[2], C_G1[2] + D_MODEL)
N_IN_B = C_G2[2]
C_DT = (2, 0, LANES)

PROMPT_TS = 512
PROJ_BLOCK = 256
SAMPLE_NB = 8
FFN_ROWS = 512
W_SPLIT_BLOCKS = 2
VMEM_LIMIT = 56 * 1024 * 1024


def _log2(n):
    assert n & (n - 1) == 0, n
    return n.bit_length() - 1


def _dot(a, b):
    return jnp.dot(a, b, preferred_element_type=F32)


def _dot_nt(a, b):
    return lax.dot_general(a, b, (((1,), (1,)), ((), ())), preferred_element_type=F32)


def _dot_tn(a, b):
    return lax.dot_general(a, b, (((0,), (0,)), ((), ())), preferred_element_type=F32)


def _split3(x):
    hi = x.astype(BF16)
    r = x - hi.astype(F32)
    mid = r.astype(BF16)
    lo = (r - mid.astype(F32)).astype(BF16)
    return hi, mid, lo


def _sigmoid(x):
    return 0.5 * (1.0 + jnp.tanh(0.5 * x))


def _silu(x):
    return x * _sigmoid(x)


def _silu_of_half(xh):
    return xh * (1.0 + jnp.tanh(xh))


def _twice_sigmoid_of_half(xh):
    return 1.0 + jnp.tanh(xh)


def _gelu(x):
    return 0.5 * x * (1.0 + jnp.tanh(0.7978845608028654 * (x + 0.044715 * (x * x * x))))


def _softplus(x):
    return jnp.maximum(x, 0.0) + jnp.log1p(jnp.exp(-jnp.abs(x)))


def _rms(x, g):
    return x * lax.rsqrt(jnp.mean(x * x, axis=-1, keepdims=True) + EPS) * g


def _rope(x, cos, sin_a, sin_b):
    return x * cos + pltpu.roll(x, LANES - ROT_DIM // 2, 1) * sin_a + pltpu.roll(x, ROT_DIM // 2, 1) * sin_b


def _proj(h, win_refs, cols):
    return _dot(h, win_refs[cols[0]][:, cols[1]:cols[2]])


def _modnorm(x3, g, mod3):
    nb, t, d = x3.shape
    h3 = _rms(x3, g) * (1.0 + mod3[:, 1:2, :]) + mod3[:, 0:1, :]
    return h3.reshape(nb * t, d).astype(BF16)


def _sink_row(sinks_ref, kv, cols_per_head):
    c = lax.broadcasted_iota(jnp.int32, (1, GQA_GROUP * cols_per_head), 1)
    row = jnp.full((1, GQA_GROUP * cols_per_head), sinks_ref[kv * GQA_GROUP], F32)
    for i in range(1, GQA_GROUP):
        row = jnp.where(c >= i * cols_per_head, sinks_ref[kv * GQA_GROUP + i], row)
    return row


def _stack_heads(q, kv, rows):
    return jnp.concatenate(
        [q[:, (kv * GQA_GROUP + i) * HEAD_DIM:(kv * GQA_GROUP + i + 1) * HEAD_DIM] for i in range(GQA_GROUP)], axis=0)


def _unstack_heads(o, rows):
    return jnp.concatenate([o[i * rows:(i + 1) * rows, :] for i in range(GQA_GROUP)], axis=1)


def _gmlp_chunk(vn, u, gws_ref, gbst_ref, L):
    ri = lax.broadcasted_iota(jnp.int32, (L, L), 0)
    ci = lax.broadcasted_iota(jnp.int32, (L, L), 1)
    outs = []
    for g in range(GM_GROUPS):
        w = jnp.where(ri >= ci, gws_ref[g, 0:L, 0:L], 0.0).astype(BF16)
        v_g = vn[:, g * GM_GROUP_DIM:(g + 1) * GM_GROUP_DIM].astype(BF16)
        outs.append(_dot(w, v_g) + gbst_ref[0:L, g:g + 1])
    return u * jnp.concatenate(outs, axis=1)


def _layer_norm(x, g, b):
    xc = x - jnp.mean(x, axis=-1, keepdims=True)
    return xc * lax.rsqrt(jnp.mean(xc * xc, axis=-1, keepdims=True) + EPS) * g + b


def _merge_out(x3, mod3, gates, a, b, c, wout_ref):
    merged = gates[0] * a + gates[1] * b + gates[2] * c
    o = _dot(merged.astype(BF16), wout_ref[...])
    nb, t, d = x3.shape
    return x3 + mod3[:, 2:3, :] * o.reshape(nb, t, d)


def _attn_groups(ts):
    return [(c, kv) for c in range(ts // CHUNK) for kv in range(N_KV_HEADS)]


def _attn_scores(q, khist, pos0):
    ts = q.shape[0]
    n_keys = WINDOW + CHUNK
    key_i = lax.broadcasted_iota(jnp.int32, (n_keys, GQA_GROUP * CHUNK), 0)
    scores = []
    for c, kv in _attn_groups(ts):
        k_g = khist[c * CHUNK:c * CHUNK + n_keys, kv * HEAD_DIM:(kv + 1) * HEAD_DIM].astype(BF16)
        sc = _dot_nt(k_g, _stack_heads(q[c * CHUNK:(c + 1) * CHUNK, :], kv, CHUNK))
        if c * CHUNK < WINDOW:
            sc = jnp.where(key_i >= WINDOW - c * CHUNK - pos0, sc, -jnp.inf)
        scores.append(sc)
    return scores


def _attn_probs(scores, sinks_ref, ts):
    sink_rows = [_sink_row(sinks_ref, kv, CHUNK) for kv in range(N_KV_HEADS)]
    probs = []
    for (c, kv), sc in zip(_attn_groups(ts), scores):
        m = jnp.maximum(jnp.max(sc, axis=0, keepdims=True), sink_rows[kv])
        p = jnp.exp(sc - m)
        denom = jnp.sum(p, axis=0, keepdims=True) + jnp.exp(sink_rows[kv] - m)
        probs.append((p * (1.0 / denom)).astype(BF16))
    return probs


def _attn_values(probs, vhist, ts):
    n_keys = WINDOW + CHUNK
    outs = []
    for (c, kv), p in zip(_attn_groups(ts), probs):
        v_g = vhist[c * CHUNK:c * CHUNK + n_keys, kv * HEAD_DIM:(kv + 1) * HEAD_DIM].astype(BF16)
        outs.append(_unstack_heads(_dot_tn(p, v_g), CHUNK))
    rows = [jnp.concatenate(outs[c * N_KV_HEADS:(c + 1) * N_KV_HEADS], axis=1) for c in range(ts // CHUNK)]
    return jnp.concatenate(rows, axis=0).astype(BF16)


def _ssd_cumsum(dt, a_row, e):
    ts = dt.shape[0]
    L = SSM_CHUNK
    ri = lax.broadcasted_iota(jnp.int32, (ts, ts), 0)
    ci = lax.broadcasted_iota(jnp.int32, (ts, ts), 1)
    lag = ri - ci
    tri = jnp.where((lag >= 0) & (lag <= (ri & (L - 1))), 1.0, 0.0).astype(BF16)
    hi, mid, lo = _split3(dt * a_row)
    cum = _dot(tri, hi) + _dot(tri, mid) + _dot(tri, lo)
    hi, mid, lo = _split3(cum)
    cum_e = _dot(hi, e) + _dot(mid, e) + _dot(lo, e)
    dt_hi = dt.astype(BF16)
    dt_lo = (dt - dt_hi.astype(F32)).astype(BF16)
    dt_e = _dot(dt_hi, e) + _dot(dt_lo, e)
    return cum_e, dt_e


def _emit(fillers, n):
    for _ in range(n):
        block = next(fillers, None)
        if block is not None:
            block()


def _ssd_decays(xact, cum_e, dt_e, fillers, n_fillers):
    ts = xact.shape[0]
    L = SSM_CHUNK
    nc = ts // L
    xs = xact[:, 0:SSM_INNER]
    t2 = lax.broadcasted_iota(jnp.int32, (L, SSM_INNER), 0)
    lane = lax.broadcasted_iota(jnp.int32, (L, SSM_INNER), 1)
    slot2 = lane & (SSM_HEAD_DIM - 1)
    first_head = (lane & (2 * SSM_HEAD_DIM - 1)) < SSM_HEAD_DIM
    ctx = {
        "xs": xs,
        "bm": xact[:, SSM_INNER:SSM_INNER + SSM_GROUPS * SSM_STATE].astype(BF16),
        "cm": xact[:, SSM_INNER + SSM_GROUPS * SSM_STATE:CONV_DIM].astype(BF16),
        "decay": [], "exp_cum": [], "chunk_decay": [], "wx": [], "xdt_a": [], "xdt_b": [],
    }
    for c in range(nc):
        rows = slice(c * L, (c + 1) * L)
        cum_c = cum_e[rows]
        cum_src = jnp.sum(jnp.where(t2 == slot2, cum_c, 0.0), axis=0, keepdims=True)
        ctx["decay"].append(jnp.exp(jnp.where(t2 >= slot2, cum_c - cum_src, -jnp.inf)))
        last = cum_c[L - 1:L, :]
        ctx["exp_cum"].append(jnp.exp(cum_c))
        ctx["chunk_decay"].append(jnp.exp(last))
        xdt = xs[rows] * dt_e[rows]
        ctx["wx"].append((xdt * jnp.exp(last - cum_c)).astype(BF16))
        ctx["xdt_a"].append(jnp.where(first_head, xdt, 0.0).astype(BF16))
        ctx["xdt_b"].append(jnp.where(first_head, 0.0, xdt).astype(BF16))
        _emit(fillers, (c + 1) * n_fillers // nc - c * n_fillers // nc)
    return ctx


def _ssd_block(ctx, st, dskip_e, fillers, n_fillers):
    L = SSM_CHUNK
    nc = ctx["xs"].shape[0] // L
    bm, cm = ctx["bm"], ctx["cm"]

    def grp(a, c, g, width):
        return a[c * L:(c + 1) * L, g * width:(g + 1) * width]

    chunk_state = [jnp.concatenate([_dot_tn(grp(bm, c, g, SSM_STATE), ctx["wx"][c][:, g * GROUP_INNER:(g + 1) * GROUP_INNER])
                                    for g in range(SSM_GROUPS)], axis=1) for c in range(nc)]
    _emit(fillers, n_fillers)
    cb = [[_dot_nt(grp(cm, c, g, SSM_STATE), jnp.concatenate([grp(bm, c, g, SSM_STATE)] * 2, axis=0))
           for g in range(SSM_GROUPS)] for c in range(nc)]
    _emit(fillers, n_fillers)
    st_in = []
    for c in range(nc):
        st_in.append(st.astype(BF16))
        st = st * ctx["chunk_decay"][c] + chunk_state[c]
    y_intra = []
    for c in range(nc):
        parts = []
        for k in range(SSM_HEADS // 2):
            ls = slice(k * 2 * SSM_HEAD_DIM, (k + 1) * 2 * SSM_HEAD_DIM)
            m_pair = (cb[c][2 * k // HEADS_PER_GROUP] * ctx["decay"][c][:, ls]).astype(BF16)
            block_diag = jnp.concatenate([ctx["xdt_a"][c][:, ls], ctx["xdt_b"][c][:, ls]], axis=0)
            parts.append(_dot(m_pair, block_diag))
        y_intra.append(jnp.concatenate(parts, axis=1))
    _emit(fillers, n_fillers)
    ys = []
    for c in range(nc):
        y_inter = jnp.concatenate(
            [_dot(grp(cm, c, g, SSM_STATE), st_in[c][:, g * GROUP_INNER:(g + 1) * GROUP_INNER])
             for g in range(SSM_GROUPS)], axis=1)
        ys.append(y_intra[c] + y_inter * ctx["exp_cum"][c] + dskip_e * ctx["xs"][c * L:(c + 1) * L])
    _emit(fillers, n_fillers)
    return jnp.concatenate(ys, axis=0), st


class _Stagger:
    def __init__(self):
        self._pending = None

    def push(self, produce, consume):
        val = produce()
        self.flush()
        self._pending = (consume, val)

    def flush(self):
        if self._pending is not None:
            consume, val = self._pending
            self._pending = None
            consume(val)


def _prompt_kernel(x_ref, mod_ref, gmix_ref, wa_ref, wb_ref, wdt_ref, cos_ref, sa_ref, sb_ref, sinks_ref,
                   convw_ref, convb_ref, dtb_ref, alog_ref, dskip_ref, ssmnw_ref, glng_ref, glnb_ref,
                   gws_ref, gbst_ref, wao_ref, wso_ref, wgo_ref, wout_ref, e_ref,
                   xo_ref, ko_ref, vo_ref, sto_ref, cvo_ref,
                   khist, vhist, xp, st_s):
    ts = PROMPT_TS
    s = pl.program_id(1)
    last = pl.num_programs(1) - 1

    @pl.when(s == 0)
    def _():
        khist[0:WINDOW, :] = jnp.zeros((WINDOW, KV_W), F32)
        vhist[0:WINDOW, :] = jnp.zeros((WINDOW, KV_W), F32)
        xp[...] = jnp.zeros_like(xp)
        st_s[...] = jnp.zeros_like(st_s)

    x3 = x_ref[...]
    mod3 = mod_ref[...]
    h = _modnorm(x3, gmix_ref[...], mod3)

    cos, sa, sb = cos_ref[...], sa_ref[...], sb_ref[...]
    res = {}

    def rope_block(raw):
        return [_rope(raw[:, i * LANES:(i + 1) * LANES], cos, sa, sb) for i in range(raw.shape[1] // LANES)]

    def q_block(raw):
        res.setdefault("q", []).extend(p.astype(BF16) for p in rope_block(raw * (HEAD_DIM ** -0.5)))

    def kv_block(raw):
        khist[WINDOW:WINDOW + ts, :] = rope_block(raw[:, 0:KV_W])[0]
        vhist[WINDOW:WINDOW + ts, :] = raw[:, KV_W:2 * KV_W]

    def conv_block(lo, raw):
        cs = slice(lo, lo + PROJ_BLOCK)
        ext = jnp.concatenate([xp[:, cs], raw], axis=0)
        acc = convb_ref[:, cs] + raw * convw_ref[CONV_WIDTH - 1:CONV_WIDTH, cs]
        for j in range(1, CONV_WIDTH):
            acc = acc + pltpu.roll(ext, j, 0)[8:8 + ts] * convw_ref[CONV_WIDTH - 1 - j:CONV_WIDTH - j, cs]
        xp[:, cs] = raw[ts - 8:ts]
        res.setdefault("xact", []).append(_silu_of_half(acc))

    def collect(key, fn):
        return lambda raw: res.setdefault(key, []).append(fn(raw))

    wrefs = (wa_ref, wb_ref, wdt_ref)
    pipe = _Stagger()

    def push_proj(cols, off, width, consume):
        ref, lo = wrefs[cols[0]], cols[1] + off
        pipe.push(lambda: _dot(h, ref[:, lo:lo + width]), consume)

    def blocks(cols, consume):
        return [functools.partial(push_proj, cols, off, PROJ_BLOCK, consume)
                for off in range(0, cols[2] - cols[1], PROJ_BLOCK)]

    conv_blocks = [functools.partial(push_proj, C_XBC, off, PROJ_BLOCK, functools.partial(conv_block, off))
                   for off in range(0, CONV_DIM, PROJ_BLOCK)]
    gate_blocks = iter(blocks(C_G0, collect("g0", _twice_sigmoid_of_half))
                       + blocks(C_G1, collect("g1", _twice_sigmoid_of_half))
                       + blocks(C_G2, collect("g2", _twice_sigmoid_of_half)))
    z_blocks = blocks(C_Z, collect("sz", _silu_of_half))
    gm_blocks = blocks(C_GU, collect("u", _gelu)) + blocks(C_GV, collect("gv", _gelu))

    def cat(key):
        return jnp.concatenate(res[key], axis=1)

    for blk in blocks(C_Q, q_block):
        blk()
    push_proj(C_K, 0, 2 * KV_W, kv_block)
    conv_blocks[0]()
    next(gate_blocks)()
    conv_blocks[1]()
    scores = _attn_scores(cat("q"), khist, s * ts)
    conv_blocks[2]()
    probs = _attn_probs(scores, sinks_ref, ts)
    next(gate_blocks)()
    conv_blocks[3]()
    attn = _attn_values(probs, vhist, ts)
    conv_blocks[4]()
    next(gate_blocks)()
    conv_blocks[5]()
    push_proj(C_DT, 0, LANES, collect("dt", lambda raw: _softplus(raw + dtb_ref[...])))
    next(gate_blocks)()
    a_out = _dot(attn, wao_ref[...])
    z_blocks[0]()
    cum_e, dt_e = _ssd_cumsum(res["dt"][0], -jnp.exp(alog_ref[...]), e_ref[...])
    z_blocks[1]()
    merged = cat("g0") * a_out
    fillers = iter(z_blocks[2:] + gm_blocks + list(gate_blocks))
    ctx = _ssd_decays(cat("xact"), cum_e, dt_e, fillers, 4)
    y, st_s[...] = _ssd_block(ctx, st_s[...], dskip_ref[...], fillers, 2)
    _emit(fillers, 1)
    ssm = _rms(y * cat("sz"), ssmnw_ref[...]).astype(BF16)
    _emit(fillers, 1)
    b_out = _dot(ssm, wso_ref[...])
    pipe.flush()
    vn = _layer_norm(cat("gv"), glng_ref[...], glnb_ref[...])
    u = cat("u")
    gm = jnp.concatenate(
        [_gmlp_chunk(vn[c * GM_CHUNK:(c + 1) * GM_CHUNK], u[c * GM_CHUNK:(c + 1) * GM_CHUNK], gws_ref, gbst_ref,
                     GM_CHUNK) for c in range(ts // GM_CHUNK)], axis=0).astype(BF16)
    merged = merged + cat("g1") * b_out
    c_out = _dot(gm, wgo_ref[...])
    merged = merged + cat("g2") * c_out
    o = _dot(merged.astype(BF16), wout_ref[...])
    xo_ref[...] = x3 + mod3[:, 2:3, :] * o.reshape(x3.shape)

    khist[0:WINDOW, :] = khist[ts:ts + WINDOW, :]
    vhist[0:WINDOW, :] = vhist[ts:ts + WINDOW, :]

    @pl.when(s == last)
    def _():
        ko_ref[0] = khist[ts:ts + WINDOW, :]
        vo_ref[0] = vhist[ts:ts + WINDOW, :]
        cvo_ref[0] = xp[8 - (CONV_WIDTH - 1):8, :]
        sto_ref[0] = st_s[...].T


def _sample_attention(q, k_new, v_new, cache_k, cache_v, sinks_ref):
    t = q.shape[0]
    outs = []
    for kv in range(N_KV_HEADS):
        hs = slice(kv * HEAD_DIM, (kv + 1) * HEAD_DIM)
        qg = _stack_heads(q, kv, t)
        s_c = _dot_nt(cache_k[:, hs].astype(BF16), qg)
        s_n = _dot_nt(k_new[:, hs].astype(BF16), qg)
        sink = _sink_row(sinks_ref, kv, t)
        m = jnp.maximum(jnp.maximum(jnp.max(s_c, axis=0, keepdims=True), jnp.max(s_n, axis=0, keepdims=True)), sink)
        p_c = jnp.exp(s_c - m)
        p_n = jnp.exp(s_n - m)
        inv = 1.0 / (jnp.sum(p_c, axis=0, keepdims=True) + jnp.sum(p_n, axis=0, keepdims=True) + jnp.exp(sink - m))
        o = (_dot_tn((p_c * inv).astype(BF16), cache_v[:, hs].astype(BF16))
             + _dot_tn((p_n * inv).astype(BF16), v_new[:, hs].astype(BF16)))
        outs.append(_unstack_heads(o, t))
    return jnp.concatenate(outs, axis=1)


def _sample_ssd_setup(xact, dt, a_row, e, L):
    m = xact.shape[0]
    n_seq = m // L
    xs = xact[:, 0:SSM_INNER]
    ri = lax.broadcasted_iota(jnp.int32, (m, m), 0)
    lag = ri - lax.broadcasted_iota(jnp.int32, (m, m), 1)
    tri = jnp.where((lag >= 0) & (lag <= (ri & (L - 1))), 1.0, 0.0).astype(BF16)
    hi, mid, lo = _split3(dt * a_row)
    cum = _dot(tri, hi) + _dot(tri, mid) + _dot(tri, lo)
    hi, mid, lo = _split3(cum)
    cum_e = (_dot(hi, e) + _dot(mid, e) + _dot(lo, e)).reshape(n_seq, L, SSM_INNER)
    head_of_lane = jnp.right_shift(lax.broadcasted_iota(jnp.int32, (LANES, SSM_HEADS * L), 1), _log2(L))
    e_slots = jnp.where(head_of_lane == lax.broadcasted_iota(jnp.int32, (LANES, SSM_HEADS * L), 0), 1.0, 0.0)
    e_slots = e_slots.astype(BF16)
    cum_s = (_dot(hi, e_slots) + _dot(mid, e_slots) + _dot(lo, e_slots)).reshape(n_seq, L, SSM_HEADS * L)
    dt_hi = dt.astype(BF16)
    dt_lo = (dt - dt_hi.astype(F32)).astype(BF16)
    dt_e = _dot(dt_hi, e) + _dot(dt_lo, e)
    t3 = lax.broadcasted_iota(jnp.int32, cum_s.shape, 1)
    slot3 = lax.broadcasted_iota(jnp.int32, cum_s.shape, 2) & (L - 1)
    cum_src = jnp.sum(jnp.where(t3 == slot3, cum_s, 0.0), axis=1, keepdims=True)
    decay = jnp.exp(jnp.where(t3 >= slot3, cum_s - cum_src, -jnp.inf))
    last = cum_e[:, L - 1:L, :]
    xdt = xs * dt_e
    wx = (xdt.reshape(n_seq, L, SSM_INNER) * jnp.exp(last - cum_e)).astype(BF16)
    per_head = jnp.exp(cum.reshape(n_seq, L, LANES)[:, L - 1:L, :])
    rep = jnp.broadcast_to(per_head, (n_seq, SSM_HEADS, LANES)).reshape(n_seq * SSM_HEADS, LANES)
    own_lane = (lax.broadcasted_iota(jnp.int32, rep.shape, 1)
                == (lax.broadcasted_iota(jnp.int32, rep.shape, 0) & (SSM_HEADS - 1)))
    hi, mid, lo = _split3(jnp.where(own_lane, rep, 0.0))
    ones = jnp.ones((LANES, LANES), BF16)
    splat = (_dot(hi, ones) + _dot(mid, ones) + _dot(lo, ones)).reshape(n_seq, SSM_HEADS, LANES)
    return {
        "L": L, "xs": xs,
        "bm": xact[:, SSM_INNER:SSM_INNER + SSM_GROUPS * SSM_STATE].astype(BF16),
        "cm": xact[:, SSM_INNER + SSM_GROUPS * SSM_STATE:CONV_DIM].astype(BF16),
        "decay": decay, "exp_cum": jnp.exp(cum_e), "wx": wx, "xdt": xdt.astype(BF16), "state_decay": splat,
    }


def _sample_ssd_seq(i, s, h0, dskip_e):
    L = s["L"]
    rows = slice(i * L, (i + 1) * L)
    h0_b = h0.astype(BF16)
    ri = jnp.right_shift(lax.broadcasted_iota(jnp.int32, (HEADS_PER_GROUP * L, GROUP_INNER), 0), _log2(L))
    li = jnp.right_shift(lax.broadcasted_iota(jnp.int32, (HEADS_PER_GROUP * L, GROUP_INNER), 1), _log2(SSM_HEAD_DIM))
    y_parts, upd = [], []
    for g in range(SSM_GROUPS):
        gn = slice(g * SSM_STATE, (g + 1) * SSM_STATE)
        gp = slice(g * GROUP_INNER, (g + 1) * GROUP_INNER)
        gs = slice(g * HEADS_PER_GROUP * L, (g + 1) * HEADS_PER_GROUP * L)
        bm_g, cm_g = s["bm"][rows, gn], s["cm"][rows, gn]
        cb = _dot_nt(cm_g, jnp.concatenate([bm_g] * HEADS_PER_GROUP, axis=0))
        m_g = (cb * s["decay"][i][:, gs]).astype(BF16)
        tiled = jnp.concatenate([s["xdt"][rows, gp]] * HEADS_PER_GROUP, axis=0)
        block_diag = jnp.where(ri == li, tiled, jnp.zeros_like(tiled))
        y_g = _dot(m_g, block_diag) + _dot_nt(cm_g, h0_b[gp, :]) * s["exp_cum"][i][:, gp]
        y_parts.append(y_g)
        upd.append(_dot_tn(s["wx"][i][:, gp], bm_g))
    y = jnp.concatenate(y_parts, axis=1) + dskip_e * s["xs"][rows]
    decayed = h0.reshape(SSM_HEADS, SSM_HEAD_DIM, SSM_STATE) * s["state_decay"][i][:, None, :]
    return y, decayed.reshape(SSM_INNER, SSM_STATE) + jnp.concatenate(upd, axis=0)


def _sample_kernel(x_ref, mod_ref, gmix_ref, wa_ref, wb_ref, wdt_ref, cos_ref, sa_ref, sb_ref, sinks_ref,
                   ck_ref, cv_ref, h0_ref, cs_ref,
                   convw_ref, convb_ref, dtb_ref, alog_ref, dskip_ref, ssmnw_ref, glng_ref, glnb_ref,
                   gws_ref, gbst_ref, wao_ref, wso_ref, wgo_ref, wout_ref, e_ref,
                   xo_ref, ko_ref, vo_ref, sto_ref, cvo_ref, gvo_ref,
                   xp):
    nb = SAMPLE_NB
    t = x_ref.shape[1]
    m = nb * t
    win_ref = (wa_ref, wb_ref, wdt_ref)
    x3 = x_ref[...]
    mod3 = mod_ref[...]
    h = _modnorm(x3, gmix_ref[...], mod3)

    cos, sa, sb = cos_ref[...], sa_ref[...], sb_ref[...]
    q = _proj(h, win_ref, C_Q) * (HEAD_DIM ** -0.5)
    q = jnp.concatenate(
        [_rope(q[:, i * LANES:(i + 1) * LANES], cos, sa, sb) for i in range(Q_W // LANES)], axis=1).astype(BF16)
    k_new = _rope(_proj(h, win_ref, C_K), cos, sa, sb)
    v_new = _proj(h, win_ref, C_V)
    ko_ref[...] = k_new.reshape(nb, t, KV_W)
    vo_ref[...] = v_new.reshape(nb, t, KV_W)

    xp[:, 8 - (CONV_WIDTH - 1):8, :] = cs_ref[...]
    xp[:, 8:8 + t, :] = _proj(h, win_ref, C_XBC).reshape(nb, t, CONV_DIM)
    acc = convb_ref[...] + xp[:, 8:8 + t, :] * convw_ref[CONV_WIDTH - 1:CONV_WIDTH, :]
    for j in range(1, CONV_WIDTH):
        acc = acc + xp[:, 8 - j:8 - j + t, :] * convw_ref[CONV_WIDTH - 1 - j:CONV_WIDTH - j, :]
    xact = _silu_of_half(acc).reshape(m, CONV_DIM)
    cvo_ref[...] = xp[:, t + 8 - (CONV_WIDTH - 1):t + 8, :]
    dt = _softplus(_proj(h, win_ref, C_DT) + dtb_ref[...])
    dskip_e = dskip_ref[...]

    u = _gelu(_proj(h, win_ref, C_GU))
    vn = _layer_norm(_gelu(_proj(h, win_ref, C_GV)), glng_ref[...], glnb_ref[...])
    gvo_ref[...] = vn.reshape(nb, t, GM_WIDTH)

    ssd = _sample_ssd_setup(xact, dt, -jnp.exp(alog_ref[...]), e_ref[...], t)
    attn_rows, y_rows, gm_rows = [], [], []
    for i in range(nb):
        rows = slice(i * t, (i + 1) * t)
        attn_rows.append(_sample_attention(q[rows], k_new[rows], v_new[rows], ck_ref[i], cv_ref[i], sinks_ref))
        y_i, sto_ref[i] = _sample_ssd_seq(i, ssd, h0_ref[i], dskip_e)
        y_rows.append(y_i)
        gm_rows.append(_gmlp_chunk(vn[rows], u[rows], gws_ref, gbst_ref, t))

    a_out = _dot(jnp.concatenate(attn_rows, axis=0).astype(BF16), wao_ref[...])
    z = _proj(h, win_ref, C_Z)
    ssm = _rms(jnp.concatenate(y_rows, axis=0) * _silu_of_half(z), ssmnw_ref[...]).astype(BF16)
    b_out = _dot(ssm, wso_ref[...])
    c_out = _dot(jnp.concatenate(gm_rows, axis=0).astype(BF16), wgo_ref[...])
    gates = [_twice_sigmoid_of_half(_proj(h, win_ref, cols)) for cols in (C_G0, C_G1, C_G2)]
    xo_ref[...] = _merge_out(x3, mod3, gates, a_out, b_out, c_out, wout_ref)


def _ffn_block(x_ref, mod_ref, gff_ref, w1_ref, w2_ref, gfin_ref, o_ref, final_norm):
    x3 = x_ref[...]
    mod3 = mod_ref[...]
    nb, t, d = x3.shape
    h3 = _rms(x3, gff_ref[...]) * (1.0 + mod3[:, 4:5, :]) + mod3[:, 3:4, :]
    h = h3.reshape(nb * t, d).astype(BF16)
    a = jnp.maximum(_dot(h, w1_ref[...]), 0.0)
    y = _dot((a * a).astype(BF16), w2_ref[...])
    out = x3 + mod3[:, 5:6, :] * y.reshape(nb, t, d)
    if final_norm:
        out = _rms(out, gfin_ref[...])
    o_ref[...] = out


def _ffn_kernel(xp_ref, modp_ref, xs_ref, mods_ref, gff_ref, w1_ref, w2_ref, gfin_ref, op_ref, os_ref, *,
                final_norm, n_prompt_steps):
    i = pl.program_id(0)

    @pl.when(i < n_prompt_steps)
    def _():
        _ffn_block(xp_ref, modp_ref, gff_ref, w1_ref, w2_ref, gfin_ref, op_ref, final_norm)

    @pl.when(i == n_prompt_steps)
    def _():
        _ffn_block(xs_ref, mods_ref, gff_ref, w1_ref, w2_ref, gfin_ref, os_ref, final_norm)


def _ada_kernel(c_ref, w_ref, b_ref, op_ref, os_ref):
    mod = _dot(_silu(c_ref[...]).astype(BF16), w_ref[...].astype(BF16)) + b_ref[...]
    n_prompt = op_ref.shape[0]
    op_ref[...] = mod[0:n_prompt]
    os_ref[...] = mod[n_prompt:]


def _w_in_piece_kernel(wt_ref, o_ref, *, half_lo, half_hi, n_valid):
    cols = wt_ref.shape[0]
    col = pl.program_id(1) * cols + lax.broadcasted_iota(jnp.int32, (cols, 1), 0)
    scale = jnp.where(col < n_valid, jnp.where((col >= half_lo) & (col < half_hi), 0.5, 1.0), 0.0)
    o_ref[...] = (wt_ref[...] * scale).astype(BF16).T


def _w_in_piece(w_t, first_col, n_cols, half_cols, n_blocks, n_valid=None):
    cols = n_cols // n_blocks
    n_valid = n_cols if n_valid is None else n_valid
    return pl.pallas_call(
        functools.partial(_w_in_piece_kernel, half_lo=half_cols[0], half_hi=half_cols[1], n_valid=n_valid),
        grid=(DEPTH, n_blocks),
        in_specs=[pl.BlockSpec((pl.Squeezed(), pl.Element(cols), pl.Element(D_MODEL)),
                               lambda l, j: (l, pl.multiple_of(first_col + j * cols, SSM_HEADS), 0))],
        out_specs=pl.BlockSpec((None, D_MODEL, cols), lambda l, j: (l, 0, j)),
        out_shape=jax.ShapeDtypeStruct((DEPTH, D_MODEL, n_cols), BF16),
        compiler_params=_params(2),
        name="w_in_piece",
    )(w_t)


def _w_in_split(w_in):
    w_t = jnp.swapaxes(w_in, 1, 2)
    b_lo = N_IN_A + SSM_HEADS
    return (_w_in_piece(w_t, 0, N_IN_A, (C_Z[1], C_Z[2]), W_SPLIT_BLOCKS),
            _w_in_piece(w_t, b_lo, N_IN_B, (C_G0[1], N_IN_B), W_SPLIT_BLOCKS),
            _w_in_piece(w_t, N_IN_A, LANES, (0, 0), 1, n_valid=SSM_HEADS))


def _with_ignored_inputs(body, n_in, n_ignored):
    if n_ignored == 0:
        return body
    return lambda *refs: body(*refs[:n_in], *refs[n_in + n_ignored:])


def _const_spec(shape):
    return pl.BlockSpec(shape, lambda *_: (0,) * len(shape), pipeline_mode=pl.Buffered(1))


def _layer_spec(l, shape):
    return pl.BlockSpec((None,) + shape, lambda *_: (l,) + (0,) * len(shape), pipeline_mode=pl.Buffered(1))


def _any_spec():
    return pl.BlockSpec(memory_space=pl.ANY)


def _smem_spec():
    return pl.BlockSpec(memory_space=pltpu.SMEM)


def _params(n_grid):
    return pltpu.CompilerParams(dimension_semantics=("arbitrary",) * n_grid, vmem_limit_bytes=VMEM_LIMIT)


def _ada_call(c_prompt, c_sample, w_ada, b_ada):
    n_p, n_s = c_prompt.shape[0], c_sample.shape[0]
    c_all = jnp.concatenate([c_prompt, c_sample], axis=0)
    tn = D_MODEL
    n_tiles = w_ada.shape[2] // tn
    mod_p, mod_s = pl.pallas_call(
        _ada_kernel,
        grid=(DEPTH, n_tiles),
        in_specs=[pl.BlockSpec((n_p + n_s, D_MODEL), lambda l, n: (0, 0)),
                  pl.BlockSpec((None, D_MODEL, tn), lambda l, n: (l, 0, n)),
                  pl.BlockSpec((None, 1, tn), lambda l, n: (l, 0, n))],
        out_specs=(pl.BlockSpec((None, n_p, tn), lambda l, n: (l, 0, n)),
                   pl.BlockSpec((None, n_s, tn), lambda l, n: (l, 0, n))),
        out_shape=(jax.ShapeDtypeStruct((DEPTH, n_p, w_ada.shape[2]), F32),
                   jax.ShapeDtypeStruct((DEPTH, n_s, w_ada.shape[2]), F32)),
        compiler_params=_params(2),
        name="ada_mod",
    )(c_all, w_ada, b_ada.reshape(DEPTH, 1, -1))
    return mod_p.reshape(DEPTH, n_p, 6, D_MODEL), mod_s.reshape(DEPTH, n_s, 6, D_MODEL)


def _w_in_specs(l):
    return [_layer_spec(l, (D_MODEL, N_IN_A)), _layer_spec(l, (D_MODEL, N_IN_B)), _layer_spec(l, (D_MODEL, LANES))]


def _layer_weight_specs(l):
    return [
        _layer_spec(l, (CONV_WIDTH, CONV_DIM)), _layer_spec(l, (1, CONV_DIM)), _layer_spec(l, (1, LANES)),
        _layer_spec(l, (1, LANES)), _layer_spec(l, (1, SSM_INNER)), _layer_spec(l, (1, SSM_INNER)),
        _layer_spec(l, (1, GM_WIDTH)), _layer_spec(l, (1, GM_WIDTH)),
        _layer_spec(l, (GM_GROUPS, GM_CHUNK, GM_CHUNK)), _layer_spec(l, (GM_CHUNK, GM_GROUPS)),
        _layer_spec(l, (Q_W, D_MODEL)), _layer_spec(l, (SSM_INNER, D_MODEL)), _layer_spec(l, (GM_WIDTH, D_MODEL)),
        _layer_spec(l, (D_MODEL, D_MODEL)), _const_spec((LANES, SSM_INNER)),
    ]


def _layer_weight_args(p):
    return (p["conv_w"], p["conv_b"], p["dt_bias"], p["a_log"], p["d_skip"], p["ssm_norm_w"], p["gm_ln_g"],
            p["gm_ln_b"], p["gm_w_s"], p["gm_b_st"], p["w_attn_o"], p["w_ssm_o"], p["w_gm_o"], p["w_out"], p["expand"])


def _prompt_mixer(l, x, mod, p, rope, prev):
    bsz, seq, d = x.shape
    ts = PROMPT_TS
    tab = pl.BlockSpec((ts, LANES), lambda b, s: (s, 0))
    in_specs = [
        pl.BlockSpec((1, ts, d), lambda b, s: (b, s, 0)),
        pl.BlockSpec((None, 1, 6, d), lambda b, s: (l, b, 0, 0)),
        _layer_spec(l, (1, d)), *_w_in_specs(l),
        tab, tab, tab, _smem_spec(),
    ] + _layer_weight_specs(l)
    args = (x, mod, p["g_mix"], *p["w_in"], *rope, p["sinks"][l], *_layer_weight_args(p))
    state_tails = ((WINDOW, KV_W), (WINDOW, KV_W), (SSM_INNER, SSM_STATE), (CONV_WIDTH - 1, CONV_DIM))
    out_shape = (jax.ShapeDtypeStruct((bsz, seq, d), F32),) + tuple(
        jax.ShapeDtypeStruct((DEPTH, bsz) + tail, F32) for tail in state_tails)
    out_specs = (pl.BlockSpec((1, ts, d), lambda b, s: (b, s, 0)),) + tuple(
        pl.BlockSpec((None, 1) + tail, lambda b, s: (l, b, 0, 0)) for tail in state_tails)
    scratch = [
        pltpu.VMEM((WINDOW + ts, KV_W), F32), pltpu.VMEM((WINDOW + ts, KV_W), F32),
        pltpu.VMEM((8, CONV_DIM), F32), pltpu.VMEM((SSM_STATE, SSM_INNER), F32),
    ]
    n_in = len(args)
    prev = () if prev is None else tuple(prev)
    return pl.pallas_call(
        _with_ignored_inputs(_prompt_kernel, n_in, len(prev)),
        grid=(bsz, seq // ts), in_specs=in_specs + [_any_spec()] * len(prev), out_specs=out_specs,
        out_shape=out_shape, scratch_shapes=scratch, compiler_params=_params(2), name="prompt_mixer",
        input_output_aliases={n_in + i: 1 + i for i in range(len(prev))},
    )(*args, *prev)


def _sample_mixer(l, x, mod, cache_k, cache_v, h0, conv_state, p, rope, prev):
    bsz, t, d = x.shape
    nb = SAMPLE_NB
    m = nb * t

    def blk(*tail):
        return pl.BlockSpec((nb,) + tail, lambda i: (i,) + (0,) * len(tail))

    def lblk(*tail):
        return pl.BlockSpec((None, nb) + tail, lambda i: (l, i) + (0,) * len(tail))

    in_specs = [
        blk(t, d), lblk(6, d), _layer_spec(l, (1, d)), *_w_in_specs(l),
        _const_spec((m, LANES)), _const_spec((m, LANES)), _const_spec((m, LANES)), _smem_spec(),
        lblk(WINDOW, KV_W), lblk(WINDOW, KV_W), lblk(SSM_INNER, SSM_STATE), lblk(CONV_WIDTH - 1, CONV_DIM),
    ] + _layer_weight_specs(l)
    args = (x, mod, p["g_mix"], *p["w_in"], *rope, p["sinks"][l], cache_k, cache_v, h0, conv_state,
            *_layer_weight_args(p))
    state_tails = ((t, KV_W), (t, KV_W), (SSM_INNER, SSM_STATE), (CONV_WIDTH - 1, CONV_DIM), (t, GM_WIDTH))
    out_shape = (jax.ShapeDtypeStruct((bsz, t, d), F32),) + tuple(
        jax.ShapeDtypeStruct((DEPTH, bsz) + tail, F32) for tail in state_tails)
    out_specs = (blk(t, d),) + tuple(lblk(*tail) for tail in state_tails)
    scratch = [pltpu.VMEM((nb, 8 + t, CONV_DIM), F32)]
    n_in = len(args)
    prev = () if prev is None else tuple(prev)
    return pl.pallas_call(
        _with_ignored_inputs(_sample_kernel, n_in, len(prev)),
        grid=(bsz // nb,), in_specs=in_specs + [_any_spec()] * len(prev), out_specs=out_specs,
        out_shape=out_shape, scratch_shapes=scratch, compiler_params=_params(1), name="sample_mixer",
        input_output_aliases={n_in + i: 1 + i for i in range(len(prev))},
    )(*args, *prev)


def _ffn(l, x_p, mod_p, x_s, mod_s, p, g_final, final_norm):
    bp, sp, d = x_p.shape
    bs, ss, _ = x_s.shape
    assert bs * ss == FFN_ROWS and sp % FFN_ROWS == 0
    blocks_per_seq = sp // FFN_ROWS
    n_prompt_steps = bp * blocks_per_seq

    def prompt_block(i):
        j = jnp.minimum(i, n_prompt_steps - 1)
        return j // blocks_per_seq, j % blocks_per_seq

    x_spec = pl.BlockSpec((1, FFN_ROWS, d), lambda i: (*prompt_block(i), 0))
    s_spec = pl.BlockSpec((bs, ss, d), lambda i: (0, 0, 0))
    return pl.pallas_call(
        functools.partial(_ffn_kernel, final_norm=final_norm, n_prompt_steps=n_prompt_steps),
        grid=(n_prompt_steps + 1,),
        in_specs=[x_spec, pl.BlockSpec((None, 1, 6, d), lambda i: (l, prompt_block(i)[0], 0, 0)),
                  s_spec, pl.BlockSpec((None, bs, 6, d), lambda i: (l, 0, 0, 0)),
                  _layer_spec(l, (1, d)), _layer_spec(l, (d, D_FF)), _layer_spec(l, (D_FF, d)), _const_spec((1, d))],
        out_specs=(x_spec, s_spec),
        out_shape=(jax.ShapeDtypeStruct(x_p.shape, F32), jax.ShapeDtypeStruct(x_s.shape, F32)),
        compiler_params=_params(1), name="ffn",
    )(x_p, mod_p, x_s, mod_s, p["g_ff"], p["w_ff1"], p["w_ff2"], g_final)


def _rope_tables(pos):
    half = ROT_DIM // 2
    inv_freq = ROPE_THETA ** (-jnp.arange(half, dtype=F32) * (2.0 / ROT_DIM))
    ang = pos.astype(F32)[:, None] * inv_freq[None, :]
    cos, sin = jnp.cos(ang), jnp.sin(ang)
    n = pos.shape[0]
    ones = jnp.ones((n, HEAD_DIM - ROT_DIM), F32)
    zeros = jnp.zeros((n, HEAD_DIM - ROT_DIM), F32)
    zh = jnp.zeros((n, half), F32)
    cos_t = jnp.concatenate([cos, cos, ones], axis=1)
    sin_a = jnp.concatenate([-sin, zh, zeros], axis=1)
    sin_b = jnp.concatenate([zh, sin, zeros], axis=1)
    rep = LANES // HEAD_DIM
    return tuple(jnp.tile(a, (1, rep)) for a in (cos_t, sin_a, sin_b))


def _stacked_params(w_in, g_mix, sinks, conv_w, conv_b, dt_bias, a_log, d_skip, ssm_norm_w, gm_ln_g, gm_ln_b,
                    gm_w_s, gm_b_s, w_attn_o, w_ssm_o, w_gm_o, w_out, g_ff, w_ff1, w_ff2):
    w_r = _w_in_split(w_in)
    pad = jnp.zeros((DEPTH, LANES - SSM_HEADS), F32)
    expand = (jnp.arange(SSM_INNER)[None, :] // SSM_HEAD_DIM == jnp.arange(LANES)[:, None]).astype(BF16)

    def row(a):
        return a[:, None, :]

    return {
        "w_in": w_r, "g_mix": row(g_mix), "sinks": sinks,
        "conv_w": 0.5 * conv_w, "conv_b": row(0.5 * conv_b),
        "dt_bias": row(jnp.concatenate([dt_bias, pad], axis=1)), "a_log": row(jnp.concatenate([a_log, pad], axis=1)),
        "d_skip": row(jnp.repeat(d_skip, SSM_HEAD_DIM, axis=1)), "ssm_norm_w": row(ssm_norm_w),
        "gm_ln_g": row(gm_ln_g), "gm_ln_b": row(gm_ln_b),
        "gm_w_s": gm_w_s, "gm_b_st": jnp.swapaxes(gm_b_s, 1, 2),
        "w_attn_o": (0.5 * w_attn_o).astype(BF16), "w_ssm_o": (0.5 * w_ssm_o).astype(BF16),
        "w_gm_o": (0.5 * w_gm_o).astype(BF16), "w_out": w_out.astype(BF16), "expand": expand,
        "g_ff": row(g_ff), "w_ff1": w_ff1.astype(BF16), "w_ff2": w_ff2.astype(BF16),
    }


def kernel(x_prompt, x_sample, c_prompt, c_sample, cache_attn_k, cache_attn_v, state_ssm, state_conv, w_ada, b_ada, g_mix, w_in, sinks, conv_w, conv_b, dt_bias, a_log, d_skip, ssm_norm_w, gm_ln_g, gm_ln_b, gm_w_s, gm_b_s, w_attn_o, w_ssm_o, w_gm_o, w_out, g_ff, w_ff1, w_ff2, g_final):
    bp, sp, d = x_prompt.shape
    bs, ss, _ = x_sample.shape
    mod_p, mod_s = _ada_call(c_prompt, c_sample, w_ada, b_ada)
    rope_p = _rope_tables(jnp.arange(sp))
    rope_s = tuple(jnp.tile(a, (SAMPLE_NB, 1)) for a in _rope_tables(PAST_LEN + jnp.arange(ss)))
    g_fin = g_final[None]
    p = _stacked_params(w_in, g_mix, sinks, conv_w, conv_b, dt_bias, a_log, d_skip, ssm_norm_w, gm_ln_g,
                        gm_ln_b, gm_w_s, gm_b_s, w_attn_o, w_ssm_o, w_gm_o, w_out, g_ff, w_ff1, w_ff2)
    ck = cache_attn_k.reshape(DEPTH, bs, WINDOW, KV_W)
    cv = cache_attn_v.reshape(DEPTH, bs, WINDOW, KV_W)
    h0 = state_ssm.reshape(DEPTH, bs, SSM_INNER, SSM_STATE)
    xp, xs = x_prompt, x_sample
    state_p = state_s = None
    for l in range(DEPTH):
        final = l == DEPTH - 1
        xp, *state_p = _prompt_mixer(l, xp, mod_p, p, rope_p, state_p)
        xs, *state_s = _sample_mixer(l, xs, mod_s, ck, cv, h0, state_conv, p, rope_s, state_s)
        xp, xs = _ffn(l, xp, mod_p, xs, mod_s, p, g_fin, final)
    kp, vp, ssm_p, conv_p = state_p
    ks, vs, ssm_s, conv_s, gv_s = state_s
    return (xp, xs,
            kp.reshape(DEPTH, bp, WINDOW, N_KV_HEADS, HEAD_DIM), vp.reshape(DEPTH, bp, WINDOW, N_KV_HEADS, HEAD_DIM),
            ssm_p.reshape(DEPTH, bp, SSM_HEADS, SSM_HEAD_DIM, SSM_STATE), conv_p,
            ks.reshape(DEPTH, bs, ss, N_KV_HEADS, HEAD_DIM), vs.reshape(DEPTH, bs, ss, N_KV_HEADS, HEAD_DIM),
            ssm_s.reshape(DEPTH, bs, SSM_HEADS, SSM_HEAD_DIM, SSM_STATE), conv_s, gv_s)
```

```python
import functools

import jax
import jax.numpy as jnp
from jax import lax
from jax.experimental import pallas as pl
from jax.experimental.pallas import tpu as pltpu

F32 = jnp.float32
BF16 = jnp.bfloat16

D_MODEL = 1024
DEPTH = 2
CHUNK = 64
N_HEADS = 8
N_KV_HEADS = 2
HEAD_DIM = 64
GQA_GROUP = N_HEADS // N_KV_HEADS
ROT_DIM = HEAD_DIM // 4
ROPE_THETA = 500000.0
WINDOW = 128
SSM_HEADS = 16
SSM_HEAD_DIM = 64
SSM_INNER = SSM_HEADS * SSM_HEAD_DIM
SSM_GROUPS = 2
SSM_STATE = 128
SSM_CHUNK = 64
CONV_WIDTH = 4
CONV_DIM = SSM_INNER + 2 * SSM_GROUPS * SSM_STATE
GM_WIDTH = 512
GM_GROUPS = 4
GM_GROUP_DIM = GM_WIDTH // GM_GROUPS
GM_CHUNK = 128
D_FF = 4 * D_MODEL
Q_W = N_HEADS * HEAD_DIM
KV_W = N_KV_HEADS * HEAD_DIM
PAST_LEN = 4096
EPS = 1e-6

LANES = 128
HEADS_PER_GROUP = SSM_HEADS // SSM_GROUPS
GROUP_INNER = HEADS_PER_GROUP * SSM_HEAD_DIM

C_Q = (0, 0, Q_W)
C_K = (0, C_Q[2], C_Q[2] + KV_W)
C_V = (0, C_K[2], C_K[2] + KV_W)
C_Z = (0, C_V[2], C_V[2] + SSM_INNER)
C_XBC = (0, C_Z[2], C_Z[2] + CONV_DIM)
N_IN_A = C_XBC[2]
C_GU = (1, 0, GM_WIDTH)
C_GV = (1, C_GU[2], C_GU[2] + GM_WIDTH)
C_G0 = (1, C_GV[2], C_GV[2] + D_MODEL)
C_G1 = (1, C_G0[2], C_G0[2] + D_MODEL)
C_G2 = (1, C_G1[2], C_G1[2] + D_MODEL)
N_IN_B = C_G2[2]
C_DT = (2, 0, LANES)

PROMPT_TS = 512
PROJ_BLOCK = 256
SAMPLE_NB = 8
FFN_ROWS = 512
W_SPLIT_BLOCKS = 2
VMEM_LIMIT = 56 * 1024 * 1024


def _log2(n):
    assert n & (n - 1) == 0, n
    return n.bit_length() - 1


def _dot(a, b):
    return jnp.dot(a, b, preferred_element_type=F32)


def _dot_nt(a, b):
    return lax.dot_general(a, b, (((1,), (1,)), ((), ())), preferred_element_type=F32)


def _dot_tn(a, b):
    return lax.dot_general(a, b, (((0,), (0,)), ((), ())), preferred_element_type=F32)


def _split3(x):
    hi = x.astype(BF16)
    r = x - hi.astype(F32)
    mid = r.astype(BF16)
    lo = (r - mid.astype(F32)).astype(BF16)
    return hi, mid, lo


def _sigmoid(x):
    return 0.5 * (1.0 + jnp.tanh(0.5 * x))


def _silu(x):
    return x * _sigmoid(x)


def _silu_of_half(xh):
    return xh * (1.0 + jnp.tanh(xh))


def _twice_sigmoid_of_half(xh):
    return 1.0 + jnp.tanh(xh)


def _gelu(x):
    return 0.5 * x * (1.0 + jnp.tanh(0.7978845608028654 * (x + 0.044715 * (x * x * x))))


def _softplus(x):
    return jnp.maximum(x, 0.0) + jnp.log1p(jnp.exp(-jnp.abs(x)))


def _rms(x, g):
    return x * lax.rsqrt(jnp.mean(x * x, axis=-1, keepdims=True) + EPS) * g


def _rope(x, cos, sin_a, sin_b):
    return x * cos + pltpu.roll(x, LANES - ROT_DIM // 2, 1) * sin_a + pltpu.roll(x, ROT_DIM // 2, 1) * sin_b


def _proj(h, win_refs, cols):
    return _dot(h, win_refs[cols[0]][:, cols[1]:cols[2]])


def _modnorm(x3, g, mod3):
    nb, t, d = x3.shape
    h3 = _rms(x3, g) * (1.0 + mod3[:, 1:2, :]) + mod3[:, 0:1, :]
    return h3.reshape(nb * t, d).astype(BF16)


def _sink_row(sinks_ref, kv, cols_per_head):
    c = lax.broadcasted_iota(jnp.int32, (1, GQA_GROUP * cols_per_head), 1)
    row = jnp.full((1, GQA_GROUP * cols_per_head), sinks_ref[kv * GQA_GROUP], F32)
    for i in range(1, GQA_GROUP):
        row = jnp.where(c >= i * cols_per_head, sinks_ref[kv * GQA_GROUP + i], row)
    return row


def _stack_heads(q, kv, rows):
    return jnp.concatenate(
        [q[:, (kv * GQA_GROUP + i) * HEAD_DIM:(kv * GQA_GROUP + i + 1) * HEAD_DIM] for i in range(GQA_GROUP)], axis=0)


def _unstack_heads(o, rows):
    return jnp.concatenate([o[i * rows:(i + 1) * rows, :] for i in range(GQA_GROUP)], axis=1)


def _gmlp_chunk(vn, u, gws_ref, gbst_ref, L):
    ri = lax.broadcasted_iota(jnp.int32, (L, L), 0)
    ci = lax.broadcasted_iota(jnp.int32, (L, L), 1)
    outs = []
    for g in range(GM_GROUPS):
        w = jnp.where(ri >= ci, gws_ref[g, 0:L, 0:L], 0.0).astype(BF16)
        v_g = vn[:, g * GM_GROUP_DIM:(g + 1) * GM_GROUP_DIM].astype(BF16)
        outs.append(_dot(w, v_g) + gbst_ref[0:L, g:g + 1])
    return u * jnp.concatenate(outs, axis=1)


def _layer_norm(x, g, b):
    xc = x - jnp.mean(x, axis=-1, keepdims=True)
    return xc * lax.rsqrt(jnp.mean(xc * xc, axis=-1, keepdims=True) + EPS) * g + b


def _merge_out(x3, mod3, gates, a, b, c, wout_ref):
    merged = gates[0] * a + gates[1] * b + gates[2] * c
    o = _dot(merged.astype(BF16), wout_ref[...])
    nb, t, d = x3.shape
    return x3 + mod3[:, 2:3, :] * o.reshape(nb, t, d)


def _attn_groups(ts):
    return [(c, kv) for c in range(ts // CHUNK) for kv in range(N_KV_HEADS)]


def _attn_scores(q, khist, pos0):
    ts = q.shape[0]
    n_keys = WINDOW + CHUNK
    key_i = lax.broadcasted_iota(jnp.int32, (n_keys, GQA_GROUP * CHUNK), 0)
    scores = []
    for c, kv in _attn_groups(ts):
        k_g = khist[c * CHUNK:c * CHUNK + n_keys, kv * HEAD_DIM:(kv + 1) * HEAD_DIM].astype(BF16)
        sc = _dot_nt(k_g, _stack_heads(q[c * CHUNK:(c + 1) * CHUNK, :], kv, CHUNK))
        if c * CHUNK < WINDOW:
            sc = jnp.where(key_i >= WINDOW - c * CHUNK - pos0, sc, -jnp.inf)
        scores.append(sc)
    return scores


def _attn_probs(scores, sinks_ref, ts):
    sink_rows = [_sink_row(sinks_ref, kv, CHUNK) for kv in range(N_KV_HEADS)]
    probs = []
    for (c, kv), sc in zip(_attn_groups(ts), scores):
        m = jnp.maximum(jnp.max(sc, axis=0, keepdims=True), sink_rows[kv])
        p = jnp.exp(sc - m)
        denom = jnp.sum(p, axis=0, keepdims=True) + jnp.exp(sink_rows[kv] - m)
        probs.append((p * (1.0 / denom)).astype(BF16))
    return probs


def _attn_values(probs, vhist, ts):
    n_keys = WINDOW + CHUNK
    outs = []
    for (c, kv), p in zip(_attn_groups(ts), probs):
        v_g = vhist[c * CHUNK:c * CHUNK + n_keys, kv * HEAD_DIM:(kv + 1) * HEAD_DIM].astype(BF16)
        outs.append(_unstack_heads(_dot_tn(p, v_g), CHUNK))
    rows = [jnp.concatenate(outs[c * N_KV_HEADS:(c + 1) * N_KV_HEADS], axis=1) for c in range(ts // CHUNK)]
    return jnp.concatenate(rows, axis=0).astype(BF16)


def _ssd_cumsum(dt, a_row, e):
    ts = dt.shape[0]
    L = SSM_CHUNK
    ri = lax.broadcasted_iota(jnp.int32, (ts, ts), 0)
    ci = lax.broadcasted_iota(jnp.int32, (ts, ts), 1)
    lag = ri - ci
    tri = jnp.where((lag >= 0) & (lag <= (ri & (L - 1))), 1.0, 0.0).astype(BF16)
    hi, mid, lo = _split3(dt * a_row)
    cum = _dot(tri, hi) + _dot(tri, mid) + _dot(tri, lo)
    hi, mid, lo = _split3(cum)
    cum_e = _dot(hi, e) + _dot(mid, e) + _dot(lo, e)
    dt_hi = dt.astype(BF16)
    dt_lo = (dt - dt_hi.astype(F32)).astype(BF16)
    dt_e = _dot(dt_hi, e) + _dot(dt_lo, e)
    return cum_e, dt_e


def _ssd_decays(xact, cum_e, dt_e):
    ts = xact.shape[0]
    L = SSM_CHUNK
    nc = ts // L
    xs = xact[:, 0:SSM_INNER]
    cum_e = cum_e.reshape(nc, L, SSM_INNER)
    t3 = lax.broadcasted_iota(jnp.int32, cum_e.shape, 1)
    slot3 = lax.broadcasted_iota(jnp.int32, cum_e.shape, 2) & (SSM_HEAD_DIM - 1)
    cum_src = jnp.sum(jnp.where(t3 == slot3, cum_e, 0.0), axis=1, keepdims=True)
    decay = jnp.exp(jnp.where(t3 >= slot3, cum_e - cum_src, -jnp.inf))
    last = cum_e[:, L - 1:L, :]
    exp_cum = jnp.exp(cum_e)
    chunk_decay = jnp.exp(last)
    xdt = xs * dt_e
    wx = (xdt.reshape(nc, L, SSM_INNER) * jnp.exp(last - cum_e)).astype(BF16)
    first_head = (lax.broadcasted_iota(jnp.int32, xdt.shape, 1) & (2 * SSM_HEAD_DIM - 1)) < SSM_HEAD_DIM
    return {
        "xs": xs,
        "bm": xact[:, SSM_INNER:SSM_INNER + SSM_GROUPS * SSM_STATE].astype(BF16),
        "cm": xact[:, SSM_INNER + SSM_GROUPS * SSM_STATE:CONV_DIM].astype(BF16),
        "decay": decay, "exp_cum": exp_cum, "chunk_decay": chunk_decay, "wx": wx,
        "xdt_a": jnp.where(first_head, xdt, 0.0).astype(BF16),
        "xdt_b": jnp.where(first_head, 0.0, xdt).astype(BF16),
    }


def _ssd_block(ctx, st, dskip_e, fillers):
    L = SSM_CHUNK
    nc = ctx["xs"].shape[0] // L
    bm, cm = ctx["bm"], ctx["cm"]
    fillers = iter(fillers)

    def grp(a, c, g, width):
        return a[c * L:(c + 1) * L, g * width:(g + 1) * width]

    chunk_state = [jnp.concatenate([_dot_tn(grp(bm, c, g, SSM_STATE), ctx["wx"][c][:, g * GROUP_INNER:(g + 1) * GROUP_INNER])
                                    for g in range(SSM_GROUPS)], axis=1) for c in range(nc)]
    next(fillers)()
    cb = [[_dot_nt(grp(cm, c, g, SSM_STATE), jnp.concatenate([grp(bm, c, g, SSM_STATE)] * 2, axis=0))
           for g in range(SSM_GROUPS)] for c in range(nc)]
    next(fillers)()
    st_in = []
    for c in range(nc):
        st_in.append(st.astype(BF16))
        st = st * ctx["chunk_decay"][c] + chunk_state[c]
    y_intra = []
    for c in range(nc):
        rows = slice(c * L, (c + 1) * L)
        parts = []
        for k in range(SSM_HEADS // 2):
            ls = slice(k * 2 * SSM_HEAD_DIM, (k + 1) * 2 * SSM_HEAD_DIM)
            m_pair = (cb[c][2 * k // HEADS_PER_GROUP] * ctx["decay"][c][:, ls]).astype(BF16)
            block_diag = jnp.concatenate([ctx["xdt_a"][rows, ls], ctx["xdt_b"][rows, ls]], axis=0)
            parts.append(_dot(m_pair, block_diag))
        y_intra.append(jnp.concatenate(parts, axis=1))
    next(fillers)()
    ys = []
    for c in range(nc):
        y_inter = jnp.concatenate(
            [_dot(grp(cm, c, g, SSM_STATE), st_in[c][:, g * GROUP_INNER:(g + 1) * GROUP_INNER])
             for g in range(SSM_GROUPS)], axis=1)
        ys.append(y_intra[c] + y_inter * ctx["exp_cum"][c] + dskip_e * ctx["xs"][c * L:(c + 1) * L])
    next(fillers)()
    return jnp.concatenate(ys, axis=0), st


class _Stagger:
    def __init__(self):
        self._pending = None

    def push(self, produce, consume):
        val = produce()
        self.flush()
        self._pending = (consume, val)

    def flush(self):
        if self._pending is not None:
            consume, val = self._pending
            self._pending = None
            consume(val)


def _prompt_kernel(x_ref, mod_ref, gmix_ref, wa_ref, wb_ref, wdt_ref, cos_ref, sa_ref, sb_ref, sinks_ref,
                   convw_ref, convb_ref, dtb_ref, alog_ref, dskip_ref, ssmnw_ref, glng_ref, glnb_ref,
                   gws_ref, gbst_ref, wao_ref, wso_ref, wgo_ref, wout_ref, e_ref,
                   xo_ref, ko_ref, vo_ref, sto_ref, cvo_ref,
                   khist, vhist, xp, st_s):
    ts = PROMPT_TS
    s = pl.program_id(1)
    last = pl.num_programs(1) - 1

    @pl.when(s == 0)
    def _():
        khist[0:WINDOW, :] = jnp.zeros((WINDOW, KV_W), F32)
        vhist[0:WINDOW, :] = jnp.zeros((WINDOW, KV_W), F32)
        xp[...] = jnp.zeros_like(xp)
        st_s[...] = jnp.zeros_like(st_s)

    x3 = x_ref[...]
    mod3 = mod_ref[...]
    h = _modnorm(x3, gmix_ref[...], mod3)

    cos, sa, sb = cos_ref[...], sa_ref[...], sb_ref[...]
    res = {}

    def rope_block(raw):
        return [_rope(raw[:, i * LANES:(i + 1) * LANES], cos, sa, sb) for i in range(raw.shape[1] // LANES)]

    def q_block(raw):
        res.setdefault("q", []).extend(p.astype(BF16) for p in rope_block(raw * (HEAD_DIM ** -0.5)))

    def kv_block(raw):
        khist[WINDOW:WINDOW + ts, :] = rope_block(raw[:, 0:KV_W])[0]
        vhist[WINDOW:WINDOW + ts, :] = raw[:, KV_W:2 * KV_W]

    def conv_block(lo, raw):
        cs = slice(lo, lo + PROJ_BLOCK)
        ext = jnp.concatenate([xp[:, cs], raw], axis=0)
        acc = convb_ref[:, cs] + raw * convw_ref[CONV_WIDTH - 1:CONV_WIDTH, cs]
        for j in range(1, CONV_WIDTH):
            acc = acc + pltpu.roll(ext, j, 0)[8:8 + ts] * convw_ref[CONV_WIDTH - 1 - j:CONV_WIDTH - j, cs]
        xp[:, cs] = raw[ts - 8:ts]
        res.setdefault("xact", []).append(_silu_of_half(acc))

    def collect(key, fn):
        return lambda raw: res.setdefault(key, []).append(fn(raw))

    wrefs = (wa_ref, wb_ref, wdt_ref)
    pipe = _Stagger()

    def push_proj(cols, off, width, consume):
        ref, lo = wrefs[cols[0]], cols[1] + off
        pipe.push(lambda: _dot(h, ref[:, lo:lo + width]), consume)

    def blocks(cols, consume):
        return [functools.partial(push_proj, cols, off, PROJ_BLOCK, consume)
                for off in range(0, cols[2] - cols[1], PROJ_BLOCK)]

    conv_blocks = [functools.partial(push_proj, C_XBC, off, PROJ_BLOCK, functools.partial(conv_block, off))
                   for off in range(0, CONV_DIM, PROJ_BLOCK)]
    gate_blocks = iter(blocks(C_G0, collect("g0", _twice_sigmoid_of_half))
                       + blocks(C_G1, collect("g1", _twice_sigmoid_of_half))
                       + blocks(C_G2, collect("g2", _twice_sigmoid_of_half)))
    z_blocks = blocks(C_Z, collect("sz", _silu_of_half))
    gm_blocks = blocks(C_GU, collect("u", _gelu)) + blocks(C_GV, collect("gv", _gelu))

    def cat(key):
        return jnp.concatenate(res[key], axis=1)

    for blk in blocks(C_Q, q_block):
        blk()
    push_proj(C_K, 0, 2 * KV_W, kv_block)
    conv_blocks[0]()
    next(gate_blocks)()
    conv_blocks[1]()
    scores = _attn_scores(cat("q"), khist, s * ts)
    conv_blocks[2]()
    probs = _attn_probs(scores, sinks_ref, ts)
    next(gate_blocks)()
    conv_blocks[3]()
    attn = _attn_values(probs, vhist, ts)
    conv_blocks[4]()
    next(gate_blocks)()
    conv_blocks[5]()
    push_proj(C_DT, 0, LANES, collect("dt", lambda raw: _softplus(raw + dtb_ref[...])))
    next(gate_blocks)()
    a_out = _dot(attn, wao_ref[...])
    z_blocks[0]()
    cum_e, dt_e = _ssd_cumsum(res["dt"][0], -jnp.exp(alog_ref[...]), e_ref[...])
    z_blocks[1]()
    merged = cat("g0") * a_out
    ctx = _ssd_decays(cat("xact"), cum_e, dt_e)
    z_blocks[2]()
    z_blocks[3]()
    y, st_s[...] = _ssd_block(ctx, st_s[...], dskip_ref[...], gm_blocks)
    next(gate_blocks)()
    vn = _layer_norm(cat("gv"), glng_ref[...], glnb_ref[...])
    u = cat("u")
    next(gate_blocks)()
    ssm = _rms(y * cat("sz"), ssmnw_ref[...]).astype(BF16)
    gm = jnp.concatenate(
        [_gmlp_chunk(vn[c * GM_CHUNK:(c + 1) * GM_CHUNK], u[c * GM_CHUNK:(c + 1) * GM_CHUNK], gws_ref, gbst_ref,
                     GM_CHUNK) for c in range(ts // GM_CHUNK)], axis=0).astype(BF16)
    next(gate_blocks)()
    b_out = _dot(ssm, wso_ref[...])
    next(gate_blocks)()
    pipe.flush()
    merged = merged + cat("g1") * b_out
    c_out = _dot(gm, wgo_ref[...])
    for blk in gate_blocks:
        blk()
    pipe.flush()
    merged = merged + cat("g2") * c_out
    o = _dot(merged.astype(BF16), wout_ref[...])
    xo_ref[...] = x3 + mod3[:, 2:3, :] * o.reshape(x3.shape)

    khist[0:WINDOW, :] = khist[ts:ts + WINDOW, :]
    vhist[0:WINDOW, :] = vhist[ts:ts + WINDOW, :]

    @pl.when(s == last)
    def _():
        ko_ref[0] = khist[ts:ts + WINDOW, :]
        vo_ref[0] = vhist[ts:ts + WINDOW, :]
        cvo_ref[0] = xp[8 - (CONV_WIDTH - 1):8, :]
        sto_ref[0] = st_s[...].T


def _sample_attention(q, k_new, v_new, cache_k, cache_v, sinks_ref):
    t = q.shape[0]
    outs = []
    for kv in range(N_KV_HEADS):
        hs = slice(kv * HEAD_DIM, (kv + 1) * HEAD_DIM)
        qg = _stack_heads(q, kv, t)
        s_c = _dot_nt(cache_k[:, hs].astype(BF16), qg)
        s_n = _dot_nt(k_new[:, hs].astype(BF16), qg)
        sink = _sink_row(sinks_ref, kv, t)
        m = jnp.maximum(jnp.maximum(jnp.max(s_c, axis=0, keepdims=True), jnp.max(s_n, axis=0, keepdims=True)), sink)
        p_c = jnp.exp(s_c - m)
        p_n = jnp.exp(s_n - m)
        inv = 1.0 / (jnp.sum(p_c, axis=0, keepdims=True) + jnp.sum(p_n, axis=0, keepdims=True) + jnp.exp(sink - m))
        o = (_dot_tn((p_c * inv).astype(BF16), cache_v[:, hs].astype(BF16))
             + _dot_tn((p_n * inv).astype(BF16), v_new[:, hs].astype(BF16)))
        outs.append(_unstack_heads(o, t))
    return jnp.concatenate(outs, axis=1)


def _sample_ssd_setup(xact, dt, a_row, e, L):
    m = xact.shape[0]
    n_seq = m // L
    xs = xact[:, 0:SSM_INNER]
    ri = lax.broadcasted_iota(jnp.int32, (m, m), 0)
    lag = ri - lax.broadcasted_iota(jnp.int32, (m, m), 1)
    tri = jnp.where((lag >= 0) & (lag <= (ri & (L - 1))), 1.0, 0.0).astype(BF16)
    hi, mid, lo = _split3(dt * a_row)
    cum = _dot(tri, hi) + _dot(tri, mid) + _dot(tri, lo)
    hi, mid, lo = _split3(cum)
    cum_e = (_dot(hi, e) + _dot(mid, e) + _dot(lo, e)).reshape(n_seq, L, SSM_INNER)
    head_of_lane = jnp.right_shift(lax.broadcasted_iota(jnp.int32, (LANES, SSM_HEADS * L), 1), _log2(L))
    e_slots = jnp.where(head_of_lane == lax.broadcasted_iota(jnp.int32, (LANES, SSM_HEADS * L), 0), 1.0, 0.0)
    e_slots = e_slots.astype(BF16)
    cum_s = (_dot(hi, e_slots) + _dot(mid, e_slots) + _dot(lo, e_slots)).reshape(n_seq, L, SSM_HEADS * L)
    dt_hi = dt.astype(BF16)
    dt_lo = (dt - dt_hi.astype(F32)).astype(BF16)
    dt_e = _dot(dt_hi, e) + _dot(dt_lo, e)
    t3 = lax.broadcasted_iota(jnp.int32, cum_s.shape, 1)
    slot3 = lax.broadcasted_iota(jnp.int32, cum_s.shape, 2) & (L - 1)
    cum_src = jnp.sum(jnp.where(t3 == slot3, cum_s, 0.0), axis=1, keepdims=True)
    decay = jnp.exp(jnp.where(t3 >= slot3, cum_s - cum_src, -jnp.inf))
    last = cum_e[:, L - 1:L, :]
    xdt = xs * dt_e
    wx = (xdt.reshape(n_seq, L, SSM_INNER) * jnp.exp(last - cum_e)).astype(BF16)
    per_head = jnp.exp(cum.reshape(n_seq, L, LANES)[:, L - 1:L, :])
    rep = jnp.broadcast_to(per_head, (n_seq, SSM_HEADS, LANES)).reshape(n_seq * SSM_HEADS, LANES)
    own_lane = (lax.broadcasted_iota(jnp.int32, rep.shape, 1)
                == (lax.broadcasted_iota(jnp.int32, rep.shape, 0) & (SSM_HEADS - 1)))
    hi, mid, lo = _split3(jnp.where(own_lane, rep, 0.0))
    ones = jnp.ones((LANES, LANES), BF16)
    splat = (_dot(hi, ones) + _dot(mid, ones) + _dot(lo, ones)).reshape(n_seq, SSM_HEADS, LANES)
    return {
        "L": L, "xs": xs,
        "bm": xact[:, SSM_INNER:SSM_INNER + SSM_GROUPS * SSM_STATE].astype(BF16),
        "cm": xact[:, SSM_INNER + SSM_GROUPS * SSM_STATE:CONV_DIM].astype(BF16),
        "decay": decay, "exp_cum": jnp.exp(cum_e), "wx": wx, "xdt": xdt.astype(BF16), "state_decay": splat,
    }


def _sample_ssd_seq(i, s, h0, dskip_e):
    L = s["L"]
    rows = slice(i * L, (i + 1) * L)
    h0_b = h0.astype(BF16)
    ri = jnp.right_shift(lax.broadcasted_iota(jnp.int32, (HEADS_PER_GROUP * L, GROUP_INNER), 0), _log2(L))
    li = jnp.right_shift(lax.broadcasted_iota(jnp.int32, (HEADS_PER_GROUP * L, GROUP_INNER), 1), _log2(SSM_HEAD_DIM))
    y_parts, upd = [], []
    for g in range(SSM_GROUPS):
        gn = slice(g * SSM_STATE, (g + 1) * SSM_STATE)
        gp = slice(g * GROUP_INNER, (g + 1) * GROUP_INNER)
        gs = slice(g * HEADS_PER_GROUP * L, (g + 1) * HEADS_PER_GROUP * L)
        bm_g, cm_g = s["bm"][rows, gn], s["cm"][rows, gn]
        cb = _dot_nt(cm_g, jnp.concatenate([bm_g] * HEADS_PER_GROUP, axis=0))
        m_g = (cb * s["decay"][i][:, gs]).astype(BF16)
        tiled = jnp.concatenate([s["xdt"][rows, gp]] * HEADS_PER_GROUP, axis=0)
        block_diag = jnp.where(ri == li, tiled, jnp.zeros_like(tiled))
        y_g = _dot(m_g, block_diag) + _dot_nt(cm_g, h0_b[gp, :]) * s["exp_cum"][i][:, gp]
        y_parts.append(y_g)
        upd.append(_dot_tn(s["wx"][i][:, gp], bm_g))
    y = jnp.concatenate(y_parts, axis=1) + dskip_e * s["xs"][rows]
    decayed = h0.reshape(SSM_HEADS, SSM_HEAD_DIM, SSM_STATE) * s["state_decay"][i][:, None, :]
    return y, decayed.reshape(SSM_INNER, SSM_STATE) + jnp.concatenate(upd, axis=0)


def _sample_kernel(x_ref, mod_ref, gmix_ref, wa_ref, wb_ref, wdt_ref, cos_ref, sa_ref, sb_ref, sinks_ref,
                   ck_ref, cv_ref, h0_ref, cs_ref,
                   convw_ref, convb_ref, dtb_ref, alog_ref, dskip_ref, ssmnw_ref, glng_ref, glnb_ref,
                   gws_ref, gbst_ref, wao_ref, wso_ref, wgo_ref, wout_ref, e_ref,
                   xo_ref, ko_ref, vo_ref, sto_ref, cvo_ref, gvo_ref,
                   xp):
    nb = SAMPLE_NB
    t = x_ref.shape[1]
    m = nb * t
    win_ref = (wa_ref, wb_ref, wdt_ref)
    x3 = x_ref[...]
    mod3 = mod_ref[...]
    h = _modnorm(x3, gmix_ref[...], mod3)

    cos, sa, sb = cos_ref[...], sa_ref[...], sb_ref[...]
    q = _proj(h, win_ref, C_Q) * (HEAD_DIM ** -0.5)
    q = jnp.concatenate(
        [_rope(q[:, i * LANES:(i + 1) * LANES], cos, sa, sb) for i in range(Q_W // LANES)], axis=1).astype(BF16)
    k_new = _rope(_proj(h, win_ref, C_K), cos, sa, sb)
    v_new = _proj(h, win_ref, C_V)
    ko_ref[...] = k_new.reshape(nb, t, KV_W)
    vo_ref[...] = v_new.reshape(nb, t, KV_W)

    xp[:, 8 - (CONV_WIDTH - 1):8, :] = cs_ref[...]
    xp[:, 8:8 + t, :] = _proj(h, win_ref, C_XBC).reshape(nb, t, CONV_DIM)
    acc = convb_ref[...] + xp[:, 8:8 + t, :] * convw_ref[CONV_WIDTH - 1:CONV_WIDTH, :]
    for j in range(1, CONV_WIDTH):
        acc = acc + xp[:, 8 - j:8 - j + t, :] * convw_ref[CONV_WIDTH - 1 - j:CONV_WIDTH - j, :]
    xact = _silu_of_half(acc).reshape(m, CONV_DIM)
    cvo_ref[...] = xp[:, t + 8 - (CONV_WIDTH - 1):t + 8, :]
    dt = _softplus(_proj(h, win_ref, C_DT) + dtb_ref[...])
    dskip_e = dskip_ref[...]

    u = _gelu(_proj(h, win_ref, C_GU))
    vn = _layer_norm(_gelu(_proj(h, win_ref, C_GV)), glng_ref[...], glnb_ref[...])
    gvo_ref[...] = vn.reshape(nb, t, GM_WIDTH)

    ssd = _sample_ssd_setup(xact, dt, -jnp.exp(alog_ref[...]), e_ref[...], t)
    attn_rows, y_rows, gm_rows = [], [], []
    for i in range(nb):
        rows = slice(i * t, (i + 1) * t)
        attn_rows.append(_sample_attention(q[rows], k_new[rows], v_new[rows], ck_ref[i], cv_ref[i], sinks_ref))
        y_i, sto_ref[i] = _sample_ssd_seq(i, ssd, h0_ref[i], dskip_e)
        y_rows.append(y_i)
        gm_rows.append(_gmlp_chunk(vn[rows], u[rows], gws_ref, gbst_ref, t))

    a_out = _dot(jnp.concatenate(attn_rows, axis=0).astype(BF16), wao_ref[...])
    z = _proj(h, win_ref, C_Z)
    ssm = _rms(jnp.concatenate(y_rows, axis=0) * _silu_of_half(z), ssmnw_ref[...]).astype(BF16)
    b_out = _dot(ssm, wso_ref[...])
    c_out = _dot(jnp.concatenate(gm_rows, axis=0).astype(BF16), wgo_ref[...])
    gates = [_twice_sigmoid_of_half(_proj(h, win_ref, cols)) for cols in (C_G0, C_G1, C_G2)]
    xo_ref[...] = _merge_out(x3, mod3, gates, a_out, b_out, c_out, wout_ref)


def _ffn_block(x_ref, mod_ref, gff_ref, w1_ref, w2_ref, gfin_ref, o_ref, final_norm):
    x3 = x_ref[...]
    mod3 = mod_ref[...]
    nb, t, d = x3.shape
    h3 = _rms(x3, gff_ref[...]) * (1.0 + mod3[:, 4:5, :]) + mod3[:, 3:4, :]
    h = h3.reshape(nb * t, d).astype(BF16)
    a = jnp.maximum(_dot(h, w1_ref[...]), 0.0)
    y = _dot((a * a).astype(BF16), w2_ref[...])
    out = x3 + mod3[:, 5:6, :] * y.reshape(nb, t, d)
    if final_norm:
        out = _rms(out, gfin_ref[...])
    o_ref[...] = out


def _ffn_kernel(xp_ref, modp_ref, xs_ref, mods_ref, gff_ref, w1_ref, w2_ref, gfin_ref, op_ref, os_ref, *,
                final_norm, n_prompt_steps):
    i = pl.program_id(0)

    @pl.when(i < n_prompt_steps)
    def _():
        _ffn_block(xp_ref, modp_ref, gff_ref, w1_ref, w2_ref, gfin_ref, op_ref, final_norm)

    @pl.when(i == n_prompt_steps)
    def _():
        _ffn_block(xs_ref, mods_ref, gff_ref, w1_ref, w2_ref, gfin_ref, os_ref, final_norm)


def _ada_kernel(c_ref, w_ref, b_ref, op_ref, os_ref):
    mod = _dot(_silu(c_ref[...]).astype(BF16), w_ref[...].astype(BF16)) + b_ref[...]
    n_prompt = op_ref.shape[0]
    op_ref[...] = mod[0:n_prompt]
    os_ref[...] = mod[n_prompt:]


def _w_in_piece_kernel(wt_ref, o_ref, *, half_lo, half_hi, n_valid):
    cols = wt_ref.shape[0]
    col = pl.program_id(1) * cols + lax.broadcasted_iota(jnp.int32, (cols, 1), 0)
    scale = jnp.where(col < n_valid, jnp.where((col >= half_lo) & (col < half_hi), 0.5, 1.0), 0.0)
    o_ref[...] = (wt_ref[...] * scale).astype(BF16).T


def _w_in_piece(w_t, first_col, n_cols, half_cols, n_blocks, n_valid=None):
    cols = n_cols // n_blocks
    n_valid = n_cols if n_valid is None else n_valid
    return pl.pallas_call(
        functools.partial(_w_in_piece_kernel, half_lo=half_cols[0], half_hi=half_cols[1], n_valid=n_valid),
        grid=(DEPTH, n_blocks),
        in_specs=[pl.BlockSpec((pl.Squeezed(), pl.Element(cols), pl.Element(D_MODEL)),
                               lambda l, j: (l, pl.multiple_of(first_col + j * cols, SSM_HEADS), 0))],
        out_specs=pl.BlockSpec((None, D_MODEL, cols), lambda l, j: (l, 0, j)),
        out_shape=jax.ShapeDtypeStruct((DEPTH, D_MODEL, n_cols), BF16),
        compiler_params=_params(2),
        name="w_in_piece",
    )(w_t)


def _w_in_split(w_in):
    w_t = jnp.swapaxes(w_in, 1, 2)
    b_lo = N_IN_A + SSM_HEADS
    return (_w_in_piece(w_t, 0, N_IN_A, (C_Z[1], C_Z[2]), W_SPLIT_BLOCKS),
            _w_in_piece(w_t, b_lo, N_IN_B, (C_G0[1], N_IN_B), W_SPLIT_BLOCKS),
            _w_in_piece(w_t, N_IN_A, LANES, (0, 0), 1, n_valid=SSM_HEADS))


def _with_ignored_inputs(body, n_in, n_ignored):
    if n_ignored == 0:
        return body
    return lambda *refs: body(*refs[:n_in], *refs[n_in + n_ignored:])


def _const_spec(shape):
    return pl.BlockSpec(shape, lambda *_: (0,) * len(shape), pipeline_mode=pl.Buffered(1))


def _layer_spec(l, shape):
    return pl.BlockSpec((None,) + shape, lambda *_: (l,) + (0,) * len(shape), pipeline_mode=pl.Buffered(1))


def _any_spec():
    return pl.BlockSpec(memory_space=pl.ANY)


def _smem_spec():
    return pl.BlockSpec(memory_space=pltpu.SMEM)


def _params(n_grid):
    return pltpu.CompilerParams(dimension_semantics=("arbitrary",) * n_grid, vmem_limit_bytes=VMEM_LIMIT)


def _ada_call(c_prompt, c_sample, w_ada, b_ada):
    n_p, n_s = c_prompt.shape[0], c_sample.shape[0]
    c_all = jnp.concatenate([c_prompt, c_sample], axis=0)
    tn = D_MODEL
    n_tiles = w_ada.shape[2] // tn
    mod_p, mod_s = pl.pallas_call(
        _ada_kernel,
        grid=(DEPTH, n_tiles),
        in_specs=[pl.BlockSpec((n_p + n_s, D_MODEL), lambda l, n: (0, 0)),
                  pl.BlockSpec((None, D_MODEL, tn), lambda l, n: (l, 0, n)),
                  pl.BlockSpec((None, 1, tn), lambda l, n: (l, 0, n))],
        out_specs=(pl.BlockSpec((None, n_p, tn), lambda l, n: (l, 0, n)),
                   pl.BlockSpec((None, n_s, tn), lambda l, n: (l, 0, n))),
        out_shape=(jax.ShapeDtypeStruct((DEPTH, n_p, w_ada.shape[2]), F32),
                   jax.ShapeDtypeStruct((DEPTH, n_s, w_ada.shape[2]), F32)),
        compiler_params=_params(2),
        name="ada_mod",
    )(c_all, w_ada, b_ada.reshape(DEPTH, 1, -1))
    return mod_p.reshape(DEPTH, n_p, 6, D_MODEL), mod_s.reshape(DEPTH, n_s, 6, D_MODEL)


def _w_in_specs(l):
    return [_layer_spec(l, (D_MODEL, N_IN_A)), _layer_spec(l, (D_MODEL, N_IN_B)), _layer_spec(l, (D_MODEL, LANES))]


def _layer_weight_specs(l):
    return [
        _layer_spec(l, (CONV_WIDTH, CONV_DIM)), _layer_spec(l, (1, CONV_DIM)), _layer_spec(l, (1, LANES)),
        _layer_spec(l, (1, LANES)), _layer_spec(l, (1, SSM_INNER)), _layer_spec(l, (1, SSM_INNER)),
        _layer_spec(l, (1, GM_WIDTH)), _layer_spec(l, (1, GM_WIDTH)),
        _layer_spec(l, (GM_GROUPS, GM_CHUNK, GM_CHUNK)), _layer_spec(l, (GM_CHUNK, GM_GROUPS)),
        _layer_spec(l, (Q_W, D_MODEL)), _layer_spec(l, (SSM_INNER, D_MODEL)), _layer_spec(l, (GM_WIDTH, D_MODEL)),
        _layer_spec(l, (D_MODEL, D_MODEL)), _const_spec((LANES, SSM_INNER)),
    ]


def _layer_weight_args(p):
    return (p["conv_w"], p["conv_b"], p["dt_bias"], p["a_log"], p["d_skip"], p["ssm_norm_w"], p["gm_ln_g"],
            p["gm_ln_b"], p["gm_w_s"], p["gm_b_st"], p["w_attn_o"], p["w_ssm_o"], p["w_gm_o"], p["w_out"], p["expand"])


def _prompt_mixer(l, x, mod, p, rope, prev):
    bsz, seq, d = x.shape
    ts = PROMPT_TS
    tab = pl.BlockSpec((ts, LANES), lambda b, s: (s, 0))
    in_specs = [
        pl.BlockSpec((1, ts, d), lambda b, s: (b, s, 0)),
        pl.BlockSpec((None, 1, 6, d), lambda b, s: (l, b, 0, 0)),
        _layer_spec(l, (1, d)), *_w_in_specs(l),
        tab, tab, tab, _smem_spec(),
    ] + _layer_weight_specs(l)
    args = (x, mod, p["g_mix"], *p["w_in"], *rope, p["sinks"][l], *_layer_weight_args(p))
    state_tails = ((WINDOW, KV_W), (WINDOW, KV_W), (SSM_INNER, SSM_STATE), (CONV_WIDTH - 1, CONV_DIM))
    out_shape = (jax.ShapeDtypeStruct((bsz, seq, d), F32),) + tuple(
        jax.ShapeDtypeStruct((DEPTH, bsz) + tail, F32) for tail in state_tails)
    out_specs = (pl.BlockSpec((1, ts, d), lambda b, s: (b, s, 0)),) + tuple(
        pl.BlockSpec((None, 1) + tail, lambda b, s: (l, b, 0, 0)) for tail in state_tails)
    scratch = [
        pltpu.VMEM((WINDOW + ts, KV_W), F32), pltpu.VMEM((WINDOW + ts, KV_W), F32),
        pltpu.VMEM((8, CONV_DIM), F32), pltpu.VMEM((SSM_STATE, SSM_INNER), F32),
    ]
    n_in = len(args)
    prev = () if prev is None else tuple(prev)
    return pl.pallas_call(
        _with_ignored_inputs(_prompt_kernel, n_in, len(prev)),
        grid=(bsz, seq // ts), in_specs=in_specs + [_any_spec()] * len(prev), out_specs=out_specs,
        out_shape=out_shape, scratch_shapes=scratch, compiler_params=_params(2), name="prompt_mixer",
        input_output_aliases={n_in + i: 1 + i for i in range(len(prev))},
    )(*args, *prev)


def _sample_mixer(l, x, mod, cache_k, cache_v, h0, conv_state, p, rope, prev):
    bsz, t, d = x.shape
    nb = SAMPLE_NB
    m = nb * t

    def blk(*tail):
        return pl.BlockSpec((nb,) + tail, lambda i: (i,) + (0,) * len(tail))

    def lblk(*tail):
        return pl.BlockSpec((None, nb) + tail, lambda i: (l, i) + (0,) * len(tail))

    in_specs = [
        blk(t, d), lblk(6, d), _layer_spec(l, (1, d)), *_w_in_specs(l),
        _const_spec((m, LANES)), _const_spec((m, LANES)), _const_spec((m, LANES)), _smem_spec(),
        lblk(WINDOW, KV_W), lblk(WINDOW, KV_W), lblk(SSM_INNER, SSM_STATE), lblk(CONV_WIDTH - 1, CONV_DIM),
    ] + _layer_weight_specs(l)
    args = (x, mod, p["g_mix"], *p["w_in"], *rope, p["sinks"][l], cache_k, cache_v, h0, conv_state,
            *_layer_weight_args(p))
    state_tails = ((t, KV_W), (t, KV_W), (SSM_INNER, SSM_STATE), (CONV_WIDTH - 1, CONV_DIM), (t, GM_WIDTH))
    out_shape = (jax.ShapeDtypeStruct((bsz, t, d), F32),) + tuple(
        jax.ShapeDtypeStruct((DEPTH, bsz) + tail, F32) for tail in state_tails)
    out_specs = (blk(t, d),) + tuple(lblk(*tail) for tail in state_tails)
    scratch = [pltpu.VMEM((nb, 8 + t, CONV_DIM), F32)]
    n_in = len(args)
    prev = () if prev is None else tuple(prev)
    return pl.pallas_call(
        _with_ignored_inputs(_sample_kernel, n_in, len(prev)),
        grid=(bsz // nb,), in_specs=in_specs + [_any_spec()] * len(prev), out_specs=out_specs,
        out_shape=out_shape, scratch_shapes=scratch, compiler_params=_params(1), name="sample_mixer",
        input_output_aliases={n_in + i: 1 + i for i in range(len(prev))},
    )(*args, *prev)


def _ffn(l, x_p, mod_p, x_s, mod_s, p, g_final, final_norm):
    bp, sp, d = x_p.shape
    bs, ss, _ = x_s.shape
    assert bs * ss == FFN_ROWS and sp % FFN_ROWS == 0
    blocks_per_seq = sp // FFN_ROWS
    n_prompt_steps = bp * blocks_per_seq

    def prompt_block(i):
        j = jnp.minimum(i, n_prompt_steps - 1)
        return j // blocks_per_seq, j % blocks_per_seq

    x_spec = pl.BlockSpec((1, FFN_ROWS, d), lambda i: (*prompt_block(i), 0))
    s_spec = pl.BlockSpec((bs, ss, d), lambda i: (0, 0, 0))
    return pl.pallas_call(
        functools.partial(_ffn_kernel, final_norm=final_norm, n_prompt_steps=n_prompt_steps),
        grid=(n_prompt_steps + 1,),
        in_specs=[x_spec, pl.BlockSpec((None, 1, 6, d), lambda i: (l, prompt_block(i)[0], 0, 0)),
                  s_spec, pl.BlockSpec((None, bs, 6, d), lambda i: (l, 0, 0, 0)),
                  _layer_spec(l, (1, d)), _layer_spec(l, (d, D_FF)), _layer_spec(l, (D_FF, d)), _const_spec((1, d))],
        out_specs=(x_spec, s_spec),
        out_shape=(jax.ShapeDtypeStruct(x_p.shape, F32), jax.ShapeDtypeStruct(x_s.shape, F32)),
        compiler_params=_params(1), name="ffn",
    )(x_p, mod_p, x_s, mod_s, p["g_ff"], p["w_ff1"], p["w_ff2"], g_final)


def _rope_tables(pos):
    half = ROT_DIM // 2
    inv_freq = ROPE_THETA ** (-jnp.arange(half, dtype=F32) * (2.0 / ROT_DIM))
    ang = pos.astype(F32)[:, None] * inv_freq[None, :]
    cos, sin = jnp.cos(ang), jnp.sin(ang)
    n = pos.shape[0]
    ones = jnp.ones((n, HEAD_DIM - ROT_DIM), F32)
    zeros = jnp.zeros((n, HEAD_DIM - ROT_DIM), F32)
    zh = jnp.zeros((n, half), F32)
    cos_t = jnp.concatenate([cos, cos, ones], axis=1)
    sin_a = jnp.concatenate([-sin, zh, zeros], axis=1)
    sin_b = jnp.concatenate([zh, sin, zeros], axis=1)
    rep = LANES // HEAD_DIM
    return tuple(jnp.tile(a, (1, rep)) for a in (cos_t, sin_a, sin_b))


def _stacked_params(w_in, g_mix, sinks, conv_w, conv_b, dt_bias, a_log, d_skip, ssm_norm_w, gm_ln_g, gm_ln_b,
                    gm_w_s, gm_b_s, w_attn_o, w_ssm_o, w_gm_o, w_out, g_ff, w_ff1, w_ff2):
    w_r = _w_in_split(w_in)
    pad = jnp.zeros((DEPTH, LANES - SSM_HEADS), F32)
    expand = (jnp.arange(SSM_INNER)[None, :] // SSM_HEAD_DIM == jnp.arange(LANES)[:, None]).astype(BF16)

    def row(a):
        return a[:, None, :]

    return {
        "w_in": w_r, "g_mix": row(g_mix), "sinks": sinks,
        "conv_w": 0.5 * conv_w, "conv_b": row(0.5 * conv_b),
        "dt_bias": row(jnp.concatenate([dt_bias, pad], axis=1)), "a_log": row(jnp.concatenate([a_log, pad], axis=1)),
        "d_skip": row(jnp.repeat(d_skip, SSM_HEAD_DIM, axis=1)), "ssm_norm_w": row(ssm_norm_w),
        "gm_ln_g": row(gm_ln_g), "gm_ln_b": row(gm_ln_b),
        "gm_w_s": gm_w_s, "gm_b_st": jnp.swapaxes(gm_b_s, 1, 2),
        "w_attn_o": (0.5 * w_attn_o).astype(BF16), "w_ssm_o": (0.5 * w_ssm_o).astype(BF16),
        "w_gm_o": (0.5 * w_gm_o).astype(BF16), "w_out": w_out.astype(BF16), "expand": expand,
        "g_ff": row(g_ff), "w_ff1": w_ff1.astype(BF16), "w_ff2": w_ff2.astype(BF16),
    }


def kernel(x_prompt, x_sample, c_prompt, c_sample, cache_attn_k, cache_attn_v, state_ssm, state_conv, w_ada, b_ada, g_mix, w_in, sinks, conv_w, conv_b, dt_bias, a_log, d_skip, ssm_norm_w, gm_ln_g, gm_ln_b, gm_w_s, gm_b_s, w_attn_o, w_ssm_o, w_gm_o, w_out, g_ff, w_ff1, w_ff2, g_final):
    bp, sp, d = x_prompt.shape
    bs, ss, _ = x_sample.shape
    mod_p, mod_s = _ada_call(c_prompt, c_sample, w_ada, b_ada)
    rope_p = _rope_tables(jnp.arange(sp))
    rope_s = tuple(jnp.tile(a, (SAMPLE_NB, 1)) for a in _rope_tables(PAST_LEN + jnp.arange(ss)))
    g_fin = g_final[None]
    p = _stacked_params(w_in, g_mix, sinks, conv_w, conv_b, dt_bias, a_log, d_skip, ssm_norm_w, gm_ln_g,
                        gm_ln_b, gm_w_s, gm_b_s, w_attn_o, w_ssm_o, w_gm_o, w_out, g_ff, w_ff1, w_ff2)
    ck = cache_attn_k.reshape(DEPTH, bs, WINDOW, KV_W)
    cv = cache_attn_v.reshape(DEPTH, bs, WINDOW, KV_W)
    h0 = state_ssm.reshape(DEPTH, bs, SSM_INNER, SSM_STATE)
    xp, xs = x_prompt, x_sample
    state_p = state_s = None
    for l in range(DEPTH):
        final = l == DEPTH - 1
        xp, *state_p = _prompt_mixer(l, xp, mod_p, p, rope_p, state_p)
        xs, *state_s = _sample_mixer(l, xs, mod_s, ck, cv, h0, state_conv, p, rope_s, state_s)
        xp, xs = _ffn(l, xp, mod_p, xs, mod_s, p, g_fin, final)
    kp, vp, ssm_p, conv_p = state_p
    ks, vs, ssm_s, conv_s, gv_s = state_s
    return (xp, xs,
            kp.reshape(DEPTH, bp, WINDOW, N_KV_HEADS, HEAD_DIM), vp.reshape(DEPTH, bp, WINDOW, N_KV_HEADS, HEAD_DIM),
            ssm_p.reshape(DEPTH, bp, SSM_HEADS, SSM_HEAD_DIM, SSM_STATE), conv_p,
            ks.reshape(DEPTH, bs, ss, N_KV_HEADS, HEAD_DIM), vs.reshape(DEPTH, bs, ss, N_KV_HEADS, HEAD_DIM),
            ssm_s.reshape(DEPTH, bs, SSM_HEADS, SSM_HEAD_DIM, SSM_STATE), conv_s, gv_s)
```

```python
import functools

import jax
import jax.numpy as jnp
from jax import lax
from jax.experimental import pallas as pl
from jax.experimental.pallas import tpu as pltpu

F32 = jnp.float32
BF16 = jnp.bfloat16

D_MODEL = 1024
DEPTH = 2
CHUNK = 64
N_HEADS = 8
N_KV_HEADS = 2
HEAD_DIM = 64
GQA_GROUP = N_HEADS // N_KV_HEADS
ROT_DIM = HEAD_DIM // 4
ROPE_THETA = 500000.0
WINDOW = 128
SSM_HEADS = 16
SSM_HEAD_DIM = 64
SSM_INNER = SSM_HEADS * SSM_HEAD_DIM
SSM_GROUPS = 2
SSM_STATE = 128
SSM_CHUNK = 64
CONV_WIDTH = 4
CONV_DIM = SSM_INNER + 2 * SSM_GROUPS * SSM_STATE
GM_WIDTH = 512
GM_GROUPS = 4
GM_GROUP_DIM = GM_WIDTH // GM_GROUPS
GM_CHUNK = 128
D_FF = 4 * D_MODEL
Q_W = N_HEADS * HEAD_DIM
KV_W = N_KV_HEADS * HEAD_DIM
PAST_LEN = 4096
EPS = 1e-6

LANES = 128
MXU_TILE = 256
HEADS_PER_GROUP = SSM_HEADS // SSM_GROUPS
GROUP_INNER = HEADS_PER_GROUP * SSM_HEAD_DIM

C_Q = (0, 0, Q_W)
C_K = (0, C_Q[2], C_Q[2] + KV_W)
C_V = (0, C_K[2], C_K[2] + KV_W)
C_Z = (0, C_V[2], C_V[2] + SSM_INNER)
C_XBC = (0, C_Z[2], C_Z[2] + CONV_DIM)
N_IN_A = C_XBC[2]
C_GU = (1, 0, GM_WIDTH)
C_GV = (1, C_GU[2], C_GU[2] + GM_WIDTH)
C_G0 = (1, C_GV[2], C_GV[2] + D_MODEL)
C_G1 = (1, C_G0[2], C_G0[2] + D_MODEL)
C_G2 = (1, C_G1[2], C_G1[2] + D_MODEL)
N_IN_B = C_G2[2]
C_DT = (2, 0, LANES)

PROMPT_TS = 512
PROJ_BLOCK = 256
SAMPLE_NB = 8
FFN_ROWS = 512
W_SPLIT_BLOCKS = 2
VMEM_LIMIT = 56 * 1024 * 1024


def _log2(n):
    assert n & (n - 1) == 0, n
    return n.bit_length() - 1


def _dot(a, b):
    return jnp.dot(a, b, preferred_element_type=F32)


def _dot_nt(a, b):
    return lax.dot_general(a, b, (((1,), (1,)), ((), ())), preferred_element_type=F32)


def _dot_tn(a, b):
    return lax.dot_general(a, b, (((0,), (0,)), ((), ())), preferred_element_type=F32)


def _split3(x):
    hi = x.astype(BF16)
    r = x - hi.astype(F32)
    mid = r.astype(BF16)
    lo = (r - mid.astype(F32)).astype(BF16)
    return hi, mid, lo


def _sigmoid(x):
    return 0.5 * (1.0 + jnp.tanh(0.5 * x))


def _silu(x):
    return x * _sigmoid(x)


def _silu_of_half(xh):
    return xh * (1.0 + jnp.tanh(xh))


def _twice_sigmoid_of_half(xh):
    return 1.0 + jnp.tanh(xh)


def _gelu(x):
    return 0.5 * x * (1.0 + jnp.tanh(0.7978845608028654 * (x + 0.044715 * (x * x * x))))


def _softplus(x):
    return jnp.maximum(x, 0.0) + jnp.log1p(jnp.exp(-jnp.abs(x)))


def _rms(x, g):
    return x * lax.rsqrt(jnp.mean(x * x, axis=-1, keepdims=True) + EPS) * g


def _rope(x, cos, sin_a, sin_b):
    return x * cos + pltpu.roll(x, LANES - ROT_DIM // 2, 1) * sin_a + pltpu.roll(x, ROT_DIM // 2, 1) * sin_b


def _proj(h, win_refs, cols):
    return _dot(h, win_refs[cols[0]][:, cols[1]:cols[2]])


def _modnorm(x3, g, mod3):
    nb, t, d = x3.shape
    h3 = _rms(x3, g) * (1.0 + mod3[:, 1:2, :]) + mod3[:, 0:1, :]
    return h3.reshape(nb * t, d).astype(BF16)


def _sink_row(sinks_ref, kv, cols_per_head):
    c = lax.broadcasted_iota(jnp.int32, (1, GQA_GROUP * cols_per_head), 1)
    row = jnp.full((1, GQA_GROUP * cols_per_head), sinks_ref[kv * GQA_GROUP], F32)
    for i in range(1, GQA_GROUP):
        row = jnp.where(c >= i * cols_per_head, sinks_ref[kv * GQA_GROUP + i], row)
    return row


def _stack_heads(q, kv, rows):
    return jnp.concatenate(
        [q[:, (kv * GQA_GROUP + i) * HEAD_DIM:(kv * GQA_GROUP + i + 1) * HEAD_DIM] for i in range(GQA_GROUP)], axis=0)


def _unstack_heads(o, rows):
    return jnp.concatenate([o[i * rows:(i + 1) * rows, :] for i in range(GQA_GROUP)], axis=1)


def _gmlp_chunk(vn, u, gws_ref, gbst_ref, L):
    ri = lax.broadcasted_iota(jnp.int32, (L, L), 0)
    ci = lax.broadcasted_iota(jnp.int32, (L, L), 1)
    outs = []
    for g in range(GM_GROUPS):
        w = jnp.where(ri >= ci, gws_ref[g, 0:L, 0:L], 0.0).astype(BF16)
        v_g = vn[:, g * GM_GROUP_DIM:(g + 1) * GM_GROUP_DIM].astype(BF16)
        outs.append(_dot(w, v_g) + gbst_ref[0:L, g:g + 1])
    return u * jnp.concatenate(outs, axis=1)


def _layer_norm(x, g, b):
    xc = x - jnp.mean(x, axis=-1, keepdims=True)
    return xc * lax.rsqrt(jnp.mean(xc * xc, axis=-1, keepdims=True) + EPS) * g + b


def _merge_out(x3, mod3, gates, a, b, c, wout_ref):
    merged = gates[0] * a + gates[1] * b + gates[2] * c
    o = _dot(merged.astype(BF16), wout_ref[...])
    nb, t, d = x3.shape
    return x3 + mod3[:, 2:3, :] * o.reshape(nb, t, d)


def _attn_groups(ts):
    return [(c, kv) for c in range(ts // CHUNK) for kv in range(N_KV_HEADS)]


def _attn_scores(q, khist, pos0):
    ts = q.shape[0]
    n_keys = WINDOW + CHUNK
    key_i = lax.broadcasted_iota(jnp.int32, (n_keys, GQA_GROUP * CHUNK), 0)
    scores = []
    for c, kv in _attn_groups(ts):
        k_g = khist[c * CHUNK:c * CHUNK + n_keys, kv * HEAD_DIM:(kv + 1) * HEAD_DIM].astype(BF16)
        sc = _dot_nt(k_g, _stack_heads(q[c * CHUNK:(c + 1) * CHUNK, :], kv, CHUNK))
        if c * CHUNK < WINDOW:
            sc = jnp.where(key_i >= WINDOW - c * CHUNK - pos0, sc, -jnp.inf)
        scores.append(sc)
    return scores


def _attn_probs(scores, sinks_ref, ts):
    sink_rows = [_sink_row(sinks_ref, kv, CHUNK) for kv in range(N_KV_HEADS)]
    probs = []
    for (c, kv), sc in zip(_attn_groups(ts), scores):
        m = jnp.maximum(jnp.max(sc, axis=0, keepdims=True), sink_rows[kv])
        p = jnp.exp(sc - m)
        denom = jnp.sum(p, axis=0, keepdims=True) + jnp.exp(sink_rows[kv] - m)
        probs.append((p * (1.0 / denom)).astype(BF16))
    return probs


def _attn_values(probs, vhist, ts):
    n_keys = WINDOW + CHUNK
    outs = []
    for (c, kv), p in zip(_attn_groups(ts), probs):
        v_g = vhist[c * CHUNK:c * CHUNK + n_keys, kv * HEAD_DIM:(kv + 1) * HEAD_DIM].astype(BF16)
        outs.append(_unstack_heads(_dot_tn(p, v_g), CHUNK))
    rows = [jnp.concatenate(outs[c * N_KV_HEADS:(c + 1) * N_KV_HEADS], axis=1) for c in range(ts // CHUNK)]
    return jnp.concatenate(rows, axis=0).astype(BF16)


def _ssd_cumsum(dt, a_row, e):
    ts = dt.shape[0]
    L = SSM_CHUNK
    span = min(ts, MXU_TILE)
    ri = lax.broadcasted_iota(jnp.int32, (span, span), 0)
    lag = ri - lax.broadcasted_iota(jnp.int32, (span, span), 1)
    tri = jnp.where((lag >= 0) & (lag <= (ri & (L - 1))), 1.0, 0.0).astype(BF16)
    hi, mid, lo = _split3(dt * a_row)
    cum = jnp.concatenate(
        [_dot(tri, hi[r:r + span]) + _dot(tri, mid[r:r + span]) + _dot(tri, lo[r:r + span])
         for r in range(0, ts, span)], axis=0)
    hi = cum.astype(BF16)
    mid = (cum - hi.astype(F32)).astype(BF16)
    cum_e = _dot(hi, e) + _dot(mid, e)
    dt_hi = dt.astype(BF16)
    dt_lo = (dt - dt_hi.astype(F32)).astype(BF16)
    dt_e = _dot(dt_hi, e) + _dot(dt_lo, e)
    return cum_e, dt_e


def _ssd_decays(xact, cum_e, dt_e):
    ts = xact.shape[0]
    L = SSM_CHUNK
    nc = ts // L
    xs = xact[:, 0:SSM_INNER]
    cum_e = cum_e.reshape(nc, L, SSM_INNER)
    t3 = lax.broadcasted_iota(jnp.int32, cum_e.shape, 1)
    slot3 = lax.broadcasted_iota(jnp.int32, cum_e.shape, 2) & (SSM_HEAD_DIM - 1)
    cum_src = jnp.sum(jnp.where(t3 == slot3, cum_e, 0.0), axis=1, keepdims=True)
    decay = jnp.exp(jnp.where(t3 >= slot3, cum_e - cum_src, -jnp.inf))
    last = cum_e[:, L - 1:L, :]
    exp_cum = jnp.exp(cum_e)
    chunk_decay = jnp.exp(last)
    xdt = xs * dt_e
    wx = (xdt.reshape(nc, L, SSM_INNER) * jnp.exp(last - cum_e)).astype(BF16)
    first_head = (lax.broadcasted_iota(jnp.int32, xdt.shape, 1) & (2 * SSM_HEAD_DIM - 1)) < SSM_HEAD_DIM
    return {
        "xs": xs,
        "bm": xact[:, SSM_INNER:SSM_INNER + SSM_GROUPS * SSM_STATE].astype(BF16),
        "cm": xact[:, SSM_INNER + SSM_GROUPS * SSM_STATE:CONV_DIM].astype(BF16),
        "decay": decay, "exp_cum": exp_cum, "chunk_decay": chunk_decay, "wx": wx,
        "xdt_a": jnp.where(first_head, xdt, 0.0).astype(BF16),
        "xdt_b": jnp.where(first_head, 0.0, xdt).astype(BF16),
    }


def _ssd_block(ctx, st, dskip_e, fillers):
    L = SSM_CHUNK
    nc = ctx["xs"].shape[0] // L
    bm, cm = ctx["bm"], ctx["cm"]
    fillers = iter(fillers)

    def grp(a, c, g, width):
        return a[c * L:(c + 1) * L, g * width:(g + 1) * width]

    chunk_state = [jnp.concatenate([_dot_tn(grp(bm, c, g, SSM_STATE), ctx["wx"][c][:, g * GROUP_INNER:(g + 1) * GROUP_INNER])
                                    for g in range(SSM_GROUPS)], axis=1) for c in range(nc)]
    next(fillers)()
    cb = [[_dot_nt(grp(cm, c, g, SSM_STATE), jnp.concatenate([grp(bm, c, g, SSM_STATE)] * 2, axis=0))
           for g in range(SSM_GROUPS)] for c in range(nc)]
    next(fillers)()
    st_in = []
    for c in range(nc):
        st_in.append(st.astype(BF16))
        st = st * ctx["chunk_decay"][c] + chunk_state[c]
    y_intra = []
    for c in range(nc):
        rows = slice(c * L, (c + 1) * L)
        parts = []
        for k in range(SSM_HEADS // 2):
            ls = slice(k * 2 * SSM_HEAD_DIM, (k + 1) * 2 * SSM_HEAD_DIM)
            m_pair = (cb[c][2 * k // HEADS_PER_GROUP] * ctx["decay"][c][:, ls]).astype(BF16)
            block_diag = jnp.concatenate([ctx["xdt_a"][rows, ls], ctx["xdt_b"][rows, ls]], axis=0)
            parts.append(_dot(m_pair, block_diag))
        y_intra.append(jnp.concatenate(parts, axis=1))
    next(fillers)()
    ys = []
    for c in range(nc):
        y_inter = jnp.concatenate(
            [_dot(grp(cm, c, g, SSM_STATE), st_in[c][:, g * GROUP_INNER:(g + 1) * GROUP_INNER])
             for g in range(SSM_GROUPS)], axis=1)
        ys.append(y_intra[c] + y_inter * ctx["exp_cum"][c] + dskip_e * ctx["xs"][c * L:(c + 1) * L])
    next(fillers)()
    return jnp.concatenate(ys, axis=0), st


class _Stagger:
    def __init__(self):
        self._pending = None

    def push(self, produce, consume):
        val = produce()
        self.flush()
        self._pending = (consume, val)

    def flush(self):
        if self._pending is not None:
            consume, val = self._pending
            self._pending = None
            consume(val)


def _prompt_kernel(x_ref, mod_ref, gmix_ref, wa_ref, wb_ref, wdt_ref, cos_ref, sa_ref, sb_ref, sinks_ref,
                   convw_ref, convb_ref, dtb_ref, alog_ref, dskip_ref, ssmnw_ref, glng_ref, glnb_ref,
                   gws_ref, gbst_ref, wao_ref, wso_ref, wgo_ref, wout_ref, e_ref,
                   xo_ref, ko_ref, vo_ref, sto_ref, cvo_ref,
                   khist, vhist, xp, st_s):
    ts = PROMPT_TS
    s = pl.program_id(1)
    last = pl.num_programs(1) - 1

    @pl.when(s == 0)
    def _():
        khist[0:WINDOW, :] = jnp.zeros((WINDOW, KV_W), F32)
        vhist[0:WINDOW, :] = jnp.zeros((WINDOW, KV_W), F32)
        xp[...] = jnp.zeros_like(xp)
        st_s[...] = jnp.zeros_like(st_s)

    x3 = x_ref[...]
    mod3 = mod_ref[...]
    h = _modnorm(x3, gmix_ref[...], mod3)

    cos, sa, sb = cos_ref[...], sa_ref[...], sb_ref[...]
    res = {}

    def rope_block(raw):
        return [_rope(raw[:, i * LANES:(i + 1) * LANES], cos, sa, sb) for i in range(raw.shape[1] // LANES)]

    def q_block(raw):
        res.setdefault("q", []).extend(p.astype(BF16) for p in rope_block(raw * (HEAD_DIM ** -0.5)))

    def kv_block(raw):
        khist[WINDOW:WINDOW + ts, :] = rope_block(raw[:, 0:KV_W])[0]
        vhist[WINDOW:WINDOW + ts, :] = raw[:, KV_W:2 * KV_W]

    def conv_block(lo, raw):
        cs = slice(lo, lo + PROJ_BLOCK)
        ext = jnp.concatenate([xp[:, cs], raw], axis=0)
        acc = convb_ref[:, cs] + raw * convw_ref[CONV_WIDTH - 1:CONV_WIDTH, cs]
        for j in range(1, CONV_WIDTH):
            acc = acc + pltpu.roll(ext, j, 0)[8:8 + ts] * convw_ref[CONV_WIDTH - 1 - j:CONV_WIDTH - j, cs]
        xp[:, cs] = raw[ts - 8:ts]
        res.setdefault("xact", []).append(_silu_of_half(acc))

    def collect(key, fn):
        return lambda raw: res.setdefault(key, []).append(fn(raw))

    wrefs = (wa_ref, wb_ref, wdt_ref)
    pipe = _Stagger()

    def push_proj(cols, off, width, consume):
        ref, lo = wrefs[cols[0]], cols[1] + off
        pipe.push(lambda: _dot(h, ref[:, lo:lo + width]), consume)

    def blocks(cols, consume):
        return [functools.partial(push_proj, cols, off, PROJ_BLOCK, consume)
                for off in range(0, cols[2] - cols[1], PROJ_BLOCK)]

    conv_blocks = [functools.partial(push_proj, C_XBC, off, PROJ_BLOCK, functools.partial(conv_block, off))
                   for off in range(0, CONV_DIM, PROJ_BLOCK)]
    gate_blocks = iter(blocks(C_G0, collect("g0", _twice_sigmoid_of_half))
                       + blocks(C_G1, collect("g1", _twice_sigmoid_of_half))
                       + blocks(C_G2, collect("g2", _twice_sigmoid_of_half)))
    z_blocks = blocks(C_Z, collect("sz", _silu_of_half))
    gm_blocks = blocks(C_GU, collect("u", _gelu)) + blocks(C_GV, collect("gv", _gelu))

    def cat(key):
        return jnp.concatenate(res[key], axis=1)

    for blk in blocks(C_Q, q_block):
        blk()
    push_proj(C_K, 0, 2 * KV_W, kv_block)
    conv_blocks[0]()
    next(gate_blocks)()
    conv_blocks[1]()
    scores = _attn_scores(cat("q"), khist, s * ts)
    conv_blocks[2]()
    probs = _attn_probs(scores, sinks_ref, ts)
    next(gate_blocks)()
    conv_blocks[3]()
    attn = _attn_values(probs, vhist, ts)
    conv_blocks[4]()
    next(gate_blocks)()
    conv_blocks[5]()
    push_proj(C_DT, 0, LANES, collect("dt", lambda raw: _softplus(raw + dtb_ref[...])))
    next(gate_blocks)()
    a_out = _dot(attn, wao_ref[...])
    z_blocks[0]()
    cum_e, dt_e = _ssd_cumsum(res["dt"][0], -jnp.exp(alog_ref[...]), e_ref[...])
    z_blocks[1]()
    merged = cat("g0") * a_out
    ctx = _ssd_decays(cat("xact"), cum_e, dt_e)
    z_blocks[2]()
    z_blocks[3]()
    y, st_s[...] = _ssd_block(ctx, st_s[...], dskip_ref[...], gm_blocks)
    next(gate_blocks)()
    vn = _layer_norm(cat("gv"), glng_ref[...], glnb_ref[...])
    u = cat("u")
    next(gate_blocks)()
    ssm = _rms(y * cat("sz"), ssmnw_ref[...]).astype(BF16)
    gm = jnp.concatenate(
        [_gmlp_chunk(vn[c * GM_CHUNK:(c + 1) * GM_CHUNK], u[c * GM_CHUNK:(c + 1) * GM_CHUNK], gws_ref, gbst_ref,
                     GM_CHUNK) for c in range(ts // GM_CHUNK)], axis=0).astype(BF16)
    next(gate_blocks)()
    b_out = _dot(ssm, wso_ref[...])
    next(gate_blocks)()
    pipe.flush()
    merged = merged + cat("g1") * b_out
    c_out = _dot(gm, wgo_ref[...])
    for blk in gate_blocks:
        blk()
    pipe.flush()
    merged = merged + cat("g2") * c_out
    o = _dot(merged.astype(BF16), wout_ref[...])
    xo_ref[...] = x3 + mod3[:, 2:3, :] * o.reshape(x3.shape)

    khist[0:WINDOW, :] = khist[ts:ts + WINDOW, :]
    vhist[0:WINDOW, :] = vhist[ts:ts + WINDOW, :]

    @pl.when(s == last)
    def _():
        ko_ref[0] = khist[ts:ts + WINDOW, :]
        vo_ref[0] = vhist[ts:ts + WINDOW, :]
        cvo_ref[0] = xp[8 - (CONV_WIDTH - 1):8, :]
        sto_ref[0] = st_s[...].T


def _sample_attention(q, k_new, v_new, cache_k, cache_v, sinks_ref):
    t = q.shape[0]
    outs = []
    for kv in range(N_KV_HEADS):
        hs = slice(kv * HEAD_DIM, (kv + 1) * HEAD_DIM)
        qg = _stack_heads(q, kv, t)
        s_c = _dot_nt(cache_k[:, hs].astype(BF16), qg)
        s_n = _dot_nt(k_new[:, hs].astype(BF16), qg)
        sink = _sink_row(sinks_ref, kv, t)
        m = jnp.maximum(jnp.maximum(jnp.max(s_c, axis=0, keepdims=True), jnp.max(s_n, axis=0, keepdims=True)), sink)
        p_c = jnp.exp(s_c - m)
        p_n = jnp.exp(s_n - m)
        inv = 1.0 / (jnp.sum(p_c, axis=0, keepdims=True) + jnp.sum(p_n, axis=0, keepdims=True) + jnp.exp(sink - m))
        o = (_dot_tn((p_c * inv).astype(BF16), cache_v[:, hs].astype(BF16))
             + _dot_tn((p_n * inv).astype(BF16), v_new[:, hs].astype(BF16)))
        outs.append(_unstack_heads(o, t))
    return jnp.concatenate(outs, axis=1)


def _sample_ssd_setup(xact, dt, a_row, e, L):
    m = xact.shape[0]
    n_seq = m // L
    xs = xact[:, 0:SSM_INNER]
    ri = lax.broadcasted_iota(jnp.int32, (m, m), 0)
    lag = ri - lax.broadcasted_iota(jnp.int32, (m, m), 1)
    tri = jnp.where((lag >= 0) & (lag <= (ri & (L - 1))), 1.0, 0.0).astype(BF16)
    hi, mid, lo = _split3(dt * a_row)
    cum = _dot(tri, hi) + _dot(tri, mid) + _dot(tri, lo)
    hi, mid, lo = _split3(cum)
    cum_e = (_dot(hi, e) + _dot(mid, e) + _dot(lo, e)).reshape(n_seq, L, SSM_INNER)
    head_of_lane = jnp.right_shift(lax.broadcasted_iota(jnp.int32, (LANES, SSM_HEADS * L), 1), _log2(L))
    e_slots = jnp.where(head_of_lane == lax.broadcasted_iota(jnp.int32, (LANES, SSM_HEADS * L), 0), 1.0, 0.0)
    e_slots = e_slots.astype(BF16)
    cum_s = (_dot(hi, e_slots) + _dot(mid, e_slots) + _dot(lo, e_slots)).reshape(n_seq, L, SSM_HEADS * L)
    dt_hi = dt.astype(BF16)
    dt_lo = (dt - dt_hi.astype(F32)).astype(BF16)
    dt_e = _dot(dt_hi, e) + _dot(dt_lo, e)
    t3 = lax.broadcasted_iota(jnp.int32, cum_s.shape, 1)
    slot3 = lax.broadcasted_iota(jnp.int32, cum_s.shape, 2) & (L - 1)
    cum_src = jnp.sum(jnp.where(t3 == slot3, cum_s, 0.0), axis=1, keepdims=True)
    decay = jnp.exp(jnp.where(t3 >= slot3, cum_s - cum_src, -jnp.inf))
    last = cum_e[:, L - 1:L, :]
    xdt = xs * dt_e
    wx = (xdt.reshape(n_seq, L, SSM_INNER) * jnp.exp(last - cum_e)).astype(BF16)
    per_head = jnp.exp(cum.reshape(n_seq, L, LANES)[:, L - 1:L, :])
    rep = jnp.broadcast_to(per_head, (n_seq, SSM_HEADS, LANES)).reshape(n_seq * SSM_HEADS, LANES)
    own_lane = (lax.broadcasted_iota(jnp.int32, rep.shape, 1)
                == (lax.broadcasted_iota(jnp.int32, rep.shape, 0) & (SSM_HEADS - 1)))
    hi, mid, lo = _split3(jnp.where(own_lane, rep, 0.0))
    ones = jnp.ones((LANES, LANES), BF16)
    splat = (_dot(hi, ones) + _dot(mid, ones) + _dot(lo, ones)).reshape(n_seq, SSM_HEADS, LANES)
    return {
        "L": L, "xs": xs,
        "bm": xact[:, SSM_INNER:SSM_INNER + SSM_GROUPS * SSM_STATE].astype(BF16),
        "cm": xact[:, SSM_INNER + SSM_GROUPS * SSM_STATE:CONV_DIM].astype(BF16),
        "decay": decay, "exp_cum": jnp.exp(cum_e), "wx": wx, "xdt": xdt.astype(BF16), "state_decay": splat,
    }


def _sample_ssd_seq(i, s, h0, dskip_e):
    L = s["L"]
    rows = slice(i * L, (i + 1) * L)
    h0_b = h0.astype(BF16)
    ri = jnp.right_shift(lax.broadcasted_iota(jnp.int32, (HEADS_PER_GROUP * L, GROUP_INNER), 0), _log2(L))
    li = jnp.right_shift(lax.broadcasted_iota(jnp.int32, (HEADS_PER_GROUP * L, GROUP_INNER), 1), _log2(SSM_HEAD_DIM))
    y_parts, upd = [], []
    for g in range(SSM_GROUPS):
        gn = slice(g * SSM_STATE, (g + 1) * SSM_STATE)
        gp = slice(g * GROUP_INNER, (g + 1) * GROUP_INNER)
        gs = slice(g * HEADS_PER_GROUP * L, (g + 1) * HEADS_PER_GROUP * L)
        bm_g, cm_g = s["bm"][rows, gn], s["cm"][rows, gn]
        cb = _dot_nt(cm_g, jnp.concatenate([bm_g] * HEADS_PER_GROUP, axis=0))
        m_g = (cb * s["decay"][i][:, gs]).astype(BF16)
        tiled = jnp.concatenate([s["xdt"][rows, gp]] * HEADS_PER_GROUP, axis=0)
        block_diag = jnp.where(ri == li, tiled, jnp.zeros_like(tiled))
        y_g = _dot(m_g, block_diag) + _dot_nt(cm_g, h0_b[gp, :]) * s["exp_cum"][i][:, gp]
        y_parts.append(y_g)
        upd.append(_dot_tn(s["wx"][i][:, gp], bm_g))
    y = jnp.concatenate(y_parts, axis=1) + dskip_e * s["xs"][rows]
    decayed = h0.reshape(SSM_HEADS, SSM_HEAD_DIM, SSM_STATE) * s["state_decay"][i][:, None, :]
    return y, decayed.reshape(SSM_INNER, SSM_STATE) + jnp.concatenate(upd, axis=0)


def _sample_kernel(x_ref, mod_ref, gmix_ref, wa_ref, wb_ref, wdt_ref, cos_ref, sa_ref, sb_ref, sinks_ref,
                   ck_ref, cv_ref, h0_ref, cs_ref,
                   convw_ref, convb_ref, dtb_ref, alog_ref, dskip_ref, ssmnw_ref, glng_ref, glnb_ref,
                   gws_ref, gbst_ref, wao_ref, wso_ref, wgo_ref, wout_ref, e_ref,
                   xo_ref, ko_ref, vo_ref, sto_ref, cvo_ref, gvo_ref,
                   xp):
    nb = SAMPLE_NB
    t = x_ref.shape[1]
    m = nb * t
    win_ref = (wa_ref, wb_ref, wdt_ref)
    x3 = x_ref[...]
    mod3 = mod_ref[...]
    h = _modnorm(x3, gmix_ref[...], mod3)

    cos, sa, sb = cos_ref[...], sa_ref[...], sb_ref[...]
    q = _proj(h, win_ref, C_Q) * (HEAD_DIM ** -0.5)
    q = jnp.concatenate(
        [_rope(q[:, i * LANES:(i + 1) * LANES], cos, sa, sb) for i in range(Q_W // LANES)], axis=1).astype(BF16)
    k_new = _rope(_proj(h, win_ref, C_K), cos, sa, sb)
    v_new = _proj(h, win_ref, C_V)
    ko_ref[...] = k_new.reshape(nb, t, KV_W)
    vo_ref[...] = v_new.reshape(nb, t, KV_W)

    xp[:, 8 - (CONV_WIDTH - 1):8, :] = cs_ref[...]
    xp[:, 8:8 + t, :] = _proj(h, win_ref, C_XBC).reshape(nb, t, CONV_DIM)
    acc = convb_ref[...] + xp[:, 8:8 + t, :] * convw_ref[CONV_WIDTH - 1:CONV_WIDTH, :]
    for j in range(1, CONV_WIDTH):
        acc = acc + xp[:, 8 - j:8 - j + t, :] * convw_ref[CONV_WIDTH - 1 - j:CONV_WIDTH - j, :]
    xact = _silu_of_half(acc).reshape(m, CONV_DIM)
    cvo_ref[...] = xp[:, t + 8 - (CONV_WIDTH - 1):t + 8, :]
    dt = _softplus(_proj(h, win_ref, C_DT) + dtb_ref[...])
    dskip_e = dskip_ref[...]

    u = _gelu(_proj(h, win_ref, C_GU))
    vn = _layer_norm(_gelu(_proj(h, win_ref, C_GV)), glng_ref[...], glnb_ref[...])
    gvo_ref[...] = vn.reshape(nb, t, GM_WIDTH)

    ssd = _sample_ssd_setup(xact, dt, -jnp.exp(alog_ref[...]), e_ref[...], t)
    attn_rows, y_rows, gm_rows = [], [], []
    for i in range(nb):
        rows = slice(i * t, (i + 1) * t)
        attn_rows.append(_sample_attention(q[rows], k_new[rows], v_new[rows], ck_ref[i], cv_ref[i], sinks_ref))
        y_i, sto_ref[i] = _sample_ssd_seq(i, ssd, h0_ref[i], dskip_e)
        y_rows.append(y_i)
        gm_rows.append(_gmlp_chunk(vn[rows], u[rows], gws_ref, gbst_ref, t))

    a_out = _dot(jnp.concatenate(attn_rows, axis=0).astype(BF16), wao_ref[...])
    z = _proj(h, win_ref, C_Z)
    ssm = _rms(jnp.concatenate(y_rows, axis=0) * _silu_of_half(z), ssmnw_ref[...]).astype(BF16)
    b_out = _dot(ssm, wso_ref[...])
    c_out = _dot(jnp.concatenate(gm_rows, axis=0).astype(BF16), wgo_ref[...])
    gates = [_twice_sigmoid_of_half(_proj(h, win_ref, cols)) for cols in (C_G0, C_G1, C_G2)]
    xo_ref[...] = _merge_out(x3, mod3, gates, a_out, b_out, c_out, wout_ref)


def _ffn_block(x_ref, mod_ref, gff_ref, w1_ref, w2_ref, gfin_ref, o_ref, final_norm):
    x3 = x_ref[...]
    mod3 = mod_ref[...]
    nb, t, d = x3.shape
    h3 = _rms(x3, gff_ref[...]) * (1.0 + mod3[:, 4:5, :]) + mod3[:, 3:4, :]
    h = h3.reshape(nb * t, d).astype(BF16)
    a = jnp.maximum(_dot(h, w1_ref[...]), 0.0)
    y = _dot((a * a).astype(BF16), w2_ref[...])
    out = x3 + mod3[:, 5:6, :] * y.reshape(nb, t, d)
    if final_norm:
        out = _rms(out, gfin_ref[...])
    o_ref[...] = out


def _ffn_kernel(xp_ref, modp_ref, xs_ref, mods_ref, gff_ref, w1_ref, w2_ref, gfin_ref, op_ref, os_ref, *,
                final_norm, n_prompt_steps):
    i = pl.program_id(0)

    @pl.when(i < n_prompt_steps)
    def _():
        _ffn_block(xp_ref, modp_ref, gff_ref, w1_ref, w2_ref, gfin_ref, op_ref, final_norm)

    @pl.when(i == n_prompt_steps)
    def _():
        _ffn_block(xs_ref, mods_ref, gff_ref, w1_ref, w2_ref, gfin_ref, os_ref, final_norm)


def _ada_kernel(c_ref, w_ref, b_ref, op_ref, os_ref):
    mod = _dot(_silu(c_ref[...]).astype(BF16), w_ref[...].astype(BF16)) + b_ref[...]
    n_prompt = op_ref.shape[0]
    op_ref[...] = mod[0:n_prompt]
    os_ref[...] = mod[n_prompt:]


def _w_in_piece_kernel(wt_ref, o_ref, *, half_lo, half_hi, n_valid):
    cols = wt_ref.shape[0]
    col = pl.program_id(1) * cols + lax.broadcasted_iota(jnp.int32, (cols, 1), 0)
    scale = jnp.where(col < n_valid, jnp.where((col >= half_lo) & (col < half_hi), 0.5, 1.0), 0.0)
    o_ref[...] = (wt_ref[...] * scale).astype(BF16).T


def _w_in_piece(w_t, first_col, n_cols, half_cols, n_blocks, n_valid=None):
    cols = n_cols // n_blocks
    n_valid = n_cols if n_valid is None else n_valid
    return pl.pallas_call(
        functools.partial(_w_in_piece_kernel, half_lo=half_cols[0], half_hi=half_cols[1], n_valid=n_valid),
        grid=(DEPTH, n_blocks),
        in_specs=[pl.BlockSpec((pl.Squeezed(), pl.Element(cols), pl.Element(D_MODEL)),
                               lambda l, j: (l, pl.multiple_of(first_col + j * cols, SSM_HEADS), 0))],
        out_specs=pl.BlockSpec((None, D_MODEL, cols), lambda l, j: (l, 0, j)),
        out_shape=jax.ShapeDtypeStruct((DEPTH, D_MODEL, n_cols), BF16),
        compiler_params=_params(2),
        name="w_in_piece",
    )(w_t)


def _w_in_split(w_in):
    w_t = jnp.swapaxes(w_in, 1, 2)
    b_lo = N_IN_A + SSM_HEADS
    return (_w_in_piece(w_t, 0, N_IN_A, (C_Z[1], C_Z[2]), W_SPLIT_BLOCKS),
            _w_in_piece(w_t, b_lo, N_IN_B, (C_G0[1], N_IN_B), W_SPLIT_BLOCKS),
            _w_in_piece(w_t, N_IN_A, LANES, (0, 0), 1, n_valid=SSM_HEADS))


def _with_ignored_inputs(body, n_in, n_ignored):
    if n_ignored == 0:
        return body
    return lambda *refs: body(*refs[:n_in], *refs[n_in + n_ignored:])


def _const_spec(shape):
    return pl.BlockSpec(shape, lambda *_: (0,) * len(shape), pipeline_mode=pl.Buffered(1))


def _layer_spec(l, shape):
    return pl.BlockSpec((None,) + shape, lambda *_: (l,) + (0,) * len(shape), pipeline_mode=pl.Buffered(1))


def _any_spec():
    return pl.BlockSpec(memory_space=pl.ANY)


def _smem_spec():
    return pl.BlockSpec(memory_space=pltpu.SMEM)


def _params(n_grid):
    return pltpu.CompilerParams(dimension_semantics=("arbitrary",) * n_grid, vmem_limit_bytes=VMEM_LIMIT)


def _ada_call(c_prompt, c_sample, w_ada, b_ada):
    n_p, n_s = c_prompt.shape[0], c_sample.shape[0]
    c_all = jnp.concatenate([c_prompt, c_sample], axis=0)
    tn = D_MODEL
    n_tiles = w_ada.shape[2] // tn
    mod_p, mod_s = pl.pallas_call(
        _ada_kernel,
        grid=(DEPTH, n_tiles),
        in_specs=[pl.BlockSpec((n_p + n_s, D_MODEL), lambda l, n: (0, 0)),
                  pl.BlockSpec((None, D_MODEL, tn), lambda l, n: (l, 0, n)),
                  pl.BlockSpec((None, 1, tn), lambda l, n: (l, 0, n))],
        out_specs=(pl.BlockSpec((None, n_p, tn), lambda l, n: (l, 0, n)),
                   pl.BlockSpec((None, n_s, tn), lambda l, n: (l, 0, n))),
        out_shape=(jax.ShapeDtypeStruct((DEPTH, n_p, w_ada.shape[2]), F32),
                   jax.ShapeDtypeStruct((DEPTH, n_s, w_ada.shape[2]), F32)),
        compiler_params=_params(2),
        name="ada_mod",
    )(c_all, w_ada, b_ada.reshape(DEPTH, 1, -1))
    return mod_p.reshape(DEPTH, n_p, 6, D_MODEL), mod_s.reshape(DEPTH, n_s, 6, D_MODEL)


def _w_in_specs(l):
    return [_layer_spec(l, (D_MODEL, N_IN_A)), _layer_spec(l, (D_MODEL, N_IN_B)), _layer_spec(l, (D_MODEL, LANES))]


def _layer_weight_specs(l):
    return [
        _layer_spec(l, (CONV_WIDTH, CONV_DIM)), _layer_spec(l, (1, CONV_DIM)), _layer_spec(l, (1, LANES)),
        _layer_spec(l, (1, LANES)), _layer_spec(l, (1, SSM_INNER)), _layer_spec(l, (1, SSM_INNER)),
        _layer_spec(l, (1, GM_WIDTH)), _layer_spec(l, (1, GM_WIDTH)),
        _layer_spec(l, (GM_GROUPS, GM_CHUNK, GM_CHUNK)), _layer_spec(l, (GM_CHUNK, GM_GROUPS)),
        _layer_spec(l, (Q_W, D_MODEL)), _layer_spec(l, (SSM_INNER, D_MODEL)), _layer_spec(l, (GM_WIDTH, D_MODEL)),
        _layer_spec(l, (D_MODEL, D_MODEL)), _const_spec((LANES, SSM_INNER)),
    ]


def _layer_weight_args(p):
    return (p["conv_w"], p["conv_b"], p["dt_bias"], p["a_log"], p["d_skip"], p["ssm_norm_w"], p["gm_ln_g"],
            p["gm_ln_b"], p["gm_w_s"], p["gm_b_st"], p["w_attn_o"], p["w_ssm_o"], p["w_gm_o"], p["w_out"], p["expand"])


def _prompt_mixer(l, x, mod, p, rope, prev):
    bsz, seq, d = x.shape
    ts = PROMPT_TS
    tab = pl.BlockSpec((ts, LANES), lambda b, s: (s, 0))
    in_specs = [
        pl.BlockSpec((1, ts, d), lambda b, s: (b, s, 0)),
        pl.BlockSpec((None, 1, 6, d), lambda b, s: (l, b, 0, 0)),
        _layer_spec(l, (1, d)), *_w_in_specs(l),
        tab, tab, tab, _smem_spec(),
    ] + _layer_weight_specs(l)
    args = (x, mod, p["g_mix"], *p["w_in"], *rope, p["sinks"][l], *_layer_weight_args(p))
    state_tails = ((WINDOW, KV_W), (WINDOW, KV_W), (SSM_INNER, SSM_STATE), (CONV_WIDTH - 1, CONV_DIM))
    out_shape = (jax.ShapeDtypeStruct((bsz, seq, d), F32),) + tuple(
        jax.ShapeDtypeStruct((DEPTH, bsz) + tail, F32) for tail in state_tails)
    out_specs = (pl.BlockSpec((1, ts, d), lambda b, s: (b, s, 0)),) + tuple(
        pl.BlockSpec((None, 1) + tail, lambda b, s: (l, b, 0, 0)) for tail in state_tails)
    scratch = [
        pltpu.VMEM((WINDOW + ts, KV_W), F32), pltpu.VMEM((WINDOW + ts, KV_W), F32),
        pltpu.VMEM((8, CONV_DIM), F32), pltpu.VMEM((SSM_STATE, SSM_INNER), F32),
    ]
    n_in = len(args)
    prev = () if prev is None else tuple(prev)
    return pl.pallas_call(
        _with_ignored_inputs(_prompt_kernel, n_in, len(prev)),
        grid=(bsz, seq // ts), in_specs=in_specs + [_any_spec()] * len(prev), out_specs=out_specs,
        out_shape=out_shape, scratch_shapes=scratch, compiler_params=_params(2), name="prompt_mixer",
        input_output_aliases={n_in + i: 1 + i for i in range(len(prev))},
    )(*args, *prev)


def _sample_mixer(l, x, mod, cache_k, cache_v, h0, conv_state, p, rope, prev):
    bsz, t, d = x.shape
    nb = SAMPLE_NB
    m = nb * t

    def blk(*tail):
        return pl.BlockSpec((nb,) + tail, lambda i: (i,) + (0,) * len(tail))

    def lblk(*tail):
        return pl.BlockSpec((None, nb) + tail, lambda i: (l, i) + (0,) * len(tail))

    in_specs = [
        blk(t, d), lblk(6, d), _layer_spec(l, (1, d)), *_w_in_specs(l),
        _const_spec((m, LANES)), _const_spec((m, LANES)), _const_spec((m, LANES)), _smem_spec(),
        lblk(WINDOW, KV_W), lblk(WINDOW, KV_W), lblk(SSM_INNER, SSM_STATE), lblk(CONV_WIDTH - 1, CONV_DIM),
    ] + _layer_weight_specs(l)
    args = (x, mod, p["g_mix"], *p["w_in"], *rope, p["sinks"][l], cache_k, cache_v, h0, conv_state,
            *_layer_weight_args(p))
    state_tails = ((t, KV_W), (t, KV_W), (SSM_INNER, SSM_STATE), (CONV_WIDTH - 1, CONV_DIM), (t, GM_WIDTH))
    out_shape = (jax.ShapeDtypeStruct((bsz, t, d), F32),) + tuple(
        jax.ShapeDtypeStruct((DEPTH, bsz) + tail, F32) for tail in state_tails)
    out_specs = (blk(t, d),) + tuple(lblk(*tail) for tail in state_tails)
    scratch = [pltpu.VMEM((nb, 8 + t, CONV_DIM), F32)]
    n_in = len(args)
    prev = () if prev is None else tuple(prev)
    return pl.pallas_call(
        _with_ignored_inputs(_sample_kernel, n_in, len(prev)),
        grid=(bsz // nb,), in_specs=in_specs + [_any_spec()] * len(prev), out_specs=out_specs,
        out_shape=out_shape, scratch_shapes=scratch, compiler_params=_params(1), name="sample_mixer",
        input_output_aliases={n_in + i: 1 + i for i in range(len(prev))},
    )(*args, *prev)


def _ffn(l, x_p, mod_p, x_s, mod_s, p, g_final, final_norm):
    bp, sp, d = x_p.shape
    bs, ss, _ = x_s.shape
    assert bs * ss == FFN_ROWS and sp % FFN_ROWS == 0
    blocks_per_seq = sp // FFN_ROWS
    n_prompt_steps = bp * blocks_per_seq

    def prompt_block(i):
        j = jnp.minimum(i, n_prompt_steps - 1)
        return j // blocks_per_seq, j % blocks_per_seq

    x_spec = pl.BlockSpec((1, FFN_ROWS, d), lambda i: (*prompt_block(i), 0))
    s_spec = pl.BlockSpec((bs, ss, d), lambda i: (0, 0, 0))
    return pl.pallas_call(
        functools.partial(_ffn_kernel, final_norm=final_norm, n_prompt_steps=n_prompt_steps),
        grid=(n_prompt_steps + 1,),
        in_specs=[x_spec, pl.BlockSpec((None, 1, 6, d), lambda i: (l, prompt_block(i)[0], 0, 0)),
                  s_spec, pl.BlockSpec((None, bs, 6, d), lambda i: (l, 0, 0, 0)),
                  _layer_spec(l, (1, d)), _layer_spec(l, (d, D_FF)), _layer_spec(l, (D_FF, d)), _const_spec((1, d))],
        out_specs=(x_spec, s_spec),
        out_shape=(jax.ShapeDtypeStruct(x_p.shape, F32), jax.ShapeDtypeStruct(x_s.shape, F32)),
        compiler_params=_params(1), name="ffn",
    )(x_p, mod_p, x_s, mod_s, p["g_ff"], p["w_ff1"], p["w_ff2"], g_final)


def _rope_tables(pos):
    half = ROT_DIM // 2
    inv_freq = ROPE_THETA ** (-jnp.arange(half, dtype=F32) * (2.0 / ROT_DIM))
    ang = pos.astype(F32)[:, None] * inv_freq[None, :]
    cos, sin = jnp.cos(ang), jnp.sin(ang)
    n = pos.shape[0]
    ones = jnp.ones((n, HEAD_DIM - ROT_DIM), F32)
    zeros = jnp.zeros((n, HEAD_DIM - ROT_DIM), F32)
    zh = jnp.zeros((n, half), F32)
    cos_t = jnp.concatenate([cos, cos, ones], axis=1)
    sin_a = jnp.concatenate([-sin, zh, zeros], axis=1)
    sin_b = jnp.concatenate([zh, sin, zeros], axis=1)
    rep = LANES // HEAD_DIM
    return tuple(jnp.tile(a, (1, rep)) for a in (cos_t, sin_a, sin_b))


def _stacked_params(w_in, g_mix, sinks, conv_w, conv_b, dt_bias, a_log, d_skip, ssm_norm_w, gm_ln_g, gm_ln_b,
                    gm_w_s, gm_b_s, w_attn_o, w_ssm_o, w_gm_o, w_out, g_ff, w_ff1, w_ff2):
    w_r = _w_in_split(w_in)
    pad = jnp.zeros((DEPTH, LANES - SSM_HEADS), F32)
    expand = (jnp.arange(SSM_INNER)[None, :] // SSM_HEAD_DIM == jnp.arange(LANES)[:, None]).astype(BF16)

    def row(a):
        return a[:, None, :]

    return {
        "w_in": w_r, "g_mix": row(g_mix), "sinks": sinks,
        "conv_w": 0.5 * conv_w, "conv_b": row(0.5 * conv_b),
        "dt_bias": row(jnp.concatenate([dt_bias, pad], axis=1)), "a_log": row(jnp.concatenate([a_log, pad], axis=1)),
        "d_skip": row(jnp.repeat(d_skip, SSM_HEAD_DIM, axis=1)), "ssm_norm_w": row(ssm_norm_w),
        "gm_ln_g": row(gm_ln_g), "gm_ln_b": row(gm_ln_b),
        "gm_w_s": gm_w_s, "gm_b_st": jnp.swapaxes(gm_b_s, 1, 2),
        "w_attn_o": (0.5 * w_attn_o).astype(BF16), "w_ssm_o": (0.5 * w_ssm_o).astype(BF16),
        "w_gm_o": (0.5 * w_gm_o).astype(BF16), "w_out": w_out.astype(BF16), "expand": expand,
        "g_ff": row(g_ff), "w_ff1": w_ff1.astype(BF16), "w_ff2": w_ff2.astype(BF16),
    }


def kernel(x_prompt, x_sample, c_prompt, c_sample, cache_attn_k, cache_attn_v, state_ssm, state_conv, w_ada, b_ada, g_mix, w_in, sinks, conv_w, conv_b, dt_bias, a_log, d_skip, ssm_norm_w, gm_ln_g, gm_ln_b, gm_w_s, gm_b_s, w_attn_o, w_ssm_o, w_gm_o, w_out, g_ff, w_ff1, w_ff2, g_final):
    bp, sp, d = x_prompt.shape
    bs, ss, _ = x_sample.shape
    mod_p, mod_s = _ada_call(c_prompt, c_sample, w_ada, b_ada)
    rope_p = _rope_tables(jnp.arange(sp))
    rope_s = tuple(jnp.tile(a, (SAMPLE_NB, 1)) for a in _rope_tables(PAST_LEN + jnp.arange(ss)))
    g_fin = g_final[None]
    p = _stacked_params(w_in, g_mix, sinks, conv_w, conv_b, dt_bias, a_log, d_skip, ssm_norm_w, gm_ln_g,
                        gm_ln_b, gm_w_s, gm_b_s, w_attn_o, w_ssm_o, w_gm_o, w_out, g_ff, w_ff1, w_ff2)
    ck = cache_attn_k.reshape(DEPTH, bs, WINDOW, KV_W)
    cv = cache_attn_v.reshape(DEPTH, bs, WINDOW, KV_W)
    h0 = state_ssm.reshape(DEPTH, bs, SSM_INNER, SSM_STATE)
    xp, xs = x_prompt, x_sample
    state_p = state_s = None
    for l in range(DEPTH):
        final = l == DEPTH - 1
        xp, *state_p = _prompt_mixer(l, xp, mod_p, p, rope_p, state_p)
        xs, *state_s = _sample_mixer(l, xs, mod_s, ck, cv, h0, state_conv, p, rope_s, state_s)
        xp, xs = _ffn(l, xp, mod_p, xs, mod_s, p, g_fin, final)
    kp, vp, ssm_p, conv_p = state_p
    ks, vs, ssm_s, conv_s, gv_s = state_s
    return (xp, xs,
            kp.reshape(DEPTH, bp, WINDOW, N_KV_HEADS, HEAD_DIM), vp.reshape(DEPTH, bp, WINDOW, N_KV_HEADS, HEAD_DIM),
            ssm_p.reshape(DEPTH, bp, SSM_HEADS, SSM_HEAD_DIM, SSM_STATE), conv_p,
            ks.reshape(DEPTH, bs, ss, N_KV_HEADS, HEAD_DIM), vs.reshape(DEPTH, bs, ss, N_KV_HEADS, HEAD_DIM),
            ssm_s.reshape(DEPTH, bs, SSM_HEADS, SSM_HEAD_DIM, SSM_STATE), conv_s, gv_s)
```

```python
import functools

import jax
import jax.numpy as jnp
from jax import lax
from jax.experimental import pallas as pl
from jax.experimental.pallas import tpu as pltpu

F32 = jnp.float32
BF16 = jnp.bfloat16

D_MODEL = 1024
DEPTH = 2
CHUNK = 64
N_HEADS = 8
N_KV_HEADS = 2
HEAD_DIM = 64
GQA_GROUP = N_HEADS // N_KV_HEADS
ROT_DIM = HEAD_DIM // 4
ROPE_THETA = 500000.0
WINDOW = 128
SSM_HEADS = 16
SSM_HEAD_DIM = 64
SSM_INNER = SSM_HEADS * SSM_HEAD_DIM
SSM_GROUPS = 2
SSM_STATE = 128
SSM_CHUNK = 64
CONV_WIDTH = 4
CONV_DIM = SSM_INNER + 2 * SSM_GROUPS * SSM_STATE
GM_WIDTH = 512
GM_GROUPS = 4
GM_GROUP_DIM = GM_WIDTH // GM_GROUPS
GM_CHUNK = 128
D_FF = 4 * D_MODEL
Q_W = N_HEADS * HEAD_DIM
KV_W = N_KV_HEADS * HEAD_DIM
PAST_LEN = 4096
EPS = 1e-6

LANES = 128
MXU_TILE = 256
HEADS_PER_GROUP = SSM_HEADS // SSM_GROUPS
GROUP_INNER = HEADS_PER_GROUP * SSM_HEAD_DIM

C_Q = (0, 0, Q_W)
C_K = (0, C_Q[2], C_Q[2] + KV_W)
C_V = (0, C_K[2], C_K[2] + KV_W)
C_Z = (0, C_V[2], C_V[2] + SSM_INNER)
C_XBC = (0, C_Z[2], C_Z[2] + CONV_DIM)
N_IN_A = C_XBC[2]
C_GU = (1, 0, GM_WIDTH)
C_GV = (1, C_GU[2], C_GU[2] + GM_WIDTH)
C_G0 = (1, C_GV[2], C_GV[2] + D_MODEL)
C_G1 = (1, C_G0[2], C_G0[2] + D_MODEL)
C_G2 = (1, C_G1[2], C_G1[2] + D_MODEL)
N_IN_B = C_G2[2]
C_DT = (2, 0, LANES)

PROMPT_TS = 512
PROJ_BLOCK = 256
SAMPLE_NB = 8
FFN_ROWS = 512
W_SPLIT_BLOCKS = 2
VMEM_LIMIT = 56 * 1024 * 1024


def _log2(n):
    assert n & (n - 1) == 0, n
    return n.bit_length() - 1


def _dot(a, b):
    return jnp.dot(a, b, preferred_element_type=F32)


def _dot_nt(a, b):
    return lax.dot_general(a, b, (((1,), (1,)), ((), ())), preferred_element_type=F32)


def _dot_tn(a, b):
    return lax.dot_general(a, b, (((0,), (0,)), ((), ())), preferred_element_type=F32)


def _split3(x):
    hi = x.astype(BF16)
    r = x - hi.astype(F32)
    mid = r.astype(BF16)
    lo = (r - mid.astype(F32)).astype(BF16)
    return hi, mid, lo


def _sigmoid(x):
    return 0.5 * (1.0 + jnp.tanh(0.5 * x))


def _silu(x):
    return x * _sigmoid(x)


def _silu_of_half(xh):
    return xh * (1.0 + jnp.tanh(xh))


def _twice_sigmoid_of_half(xh):
    return 1.0 + jnp.tanh(xh)


def _gelu(x):
    return 0.5 * x * (1.0 + jnp.tanh(0.7978845608028654 * (x + 0.044715 * (x * x * x))))


def _softplus(x):
    return jnp.maximum(x, 0.0) + jnp.log1p(jnp.exp(-jnp.abs(x)))


def _rms(x, g):
    return x * lax.rsqrt(jnp.mean(x * x, axis=-1, keepdims=True) + EPS) * g


def _rope(x, cos, sin_a, sin_b):
    return x * cos + pltpu.roll(x, LANES - ROT_DIM // 2, 1) * sin_a + pltpu.roll(x, ROT_DIM // 2, 1) * sin_b


def _proj(h, win_refs, cols):
    return _dot(h, win_refs[cols[0]][:, cols[1]:cols[2]])


def _modnorm(x3, g, mod3):
    nb, t, d = x3.shape
    h3 = _rms(x3, g) * (1.0 + mod3[:, 1:2, :]) + mod3[:, 0:1, :]
    return h3.reshape(nb * t, d).astype(BF16)


def _sink_row(sinks_ref, kv, cols_per_head):
    c = lax.broadcasted_iota(jnp.int32, (1, GQA_GROUP * cols_per_head), 1)
    row = jnp.full((1, GQA_GROUP * cols_per_head), sinks_ref[kv * GQA_GROUP], F32)
    for i in range(1, GQA_GROUP):
        row = jnp.where(c >= i * cols_per_head, sinks_ref[kv * GQA_GROUP + i], row)
    return row


def _stack_heads(q, kv, rows):
    return jnp.concatenate(
        [q[:, (kv * GQA_GROUP + i) * HEAD_DIM:(kv * GQA_GROUP + i + 1) * HEAD_DIM] for i in range(GQA_GROUP)], axis=0)


def _unstack_heads(o, rows):
    return jnp.concatenate([o[i * rows:(i + 1) * rows, :] for i in range(GQA_GROUP)], axis=1)


def _gmlp_chunk(vn, u, gws_ref, gbst_ref, L):
    ri = lax.broadcasted_iota(jnp.int32, (L, L), 0)
    ci = lax.broadcasted_iota(jnp.int32, (L, L), 1)
    outs = []
    for g in range(GM_GROUPS):
        w = jnp.where(ri >= ci, gws_ref[g, 0:L, 0:L], 0.0).astype(BF16)
        v_g = vn[:, g * GM_GROUP_DIM:(g + 1) * GM_GROUP_DIM].astype(BF16)
        outs.append(_dot(w, v_g) + gbst_ref[0:L, g:g + 1])
    return u * jnp.concatenate(outs, axis=1)


def _layer_norm(x, g, b):
    xc = x - jnp.mean(x, axis=-1, keepdims=True)
    return xc * lax.rsqrt(jnp.mean(xc * xc, axis=-1, keepdims=True) + EPS) * g + b


def _merge_out(x3, mod3, gates, a, b, c, wout_ref):
    merged = gates[0] * a + gates[1] * b + gates[2] * c
    o = _dot(merged.astype(BF16), wout_ref[...])
    nb, t, d = x3.shape
    return x3 + mod3[:, 2:3, :] * o.reshape(nb, t, d)


def _attn_groups(ts):
    return [(c, kv) for c in range(ts // CHUNK) for kv in range(N_KV_HEADS)]


def _attn_scores(q, khist, pos0):
    ts = q.shape[0]
    n_keys = WINDOW + CHUNK
    key_i = lax.broadcasted_iota(jnp.int32, (n_keys, GQA_GROUP * CHUNK), 0)
    scores = []
    for c, kv in _attn_groups(ts):
        k_g = khist[c * CHUNK:c * CHUNK + n_keys, kv * HEAD_DIM:(kv + 1) * HEAD_DIM].astype(BF16)
        sc = _dot_nt(k_g, _stack_heads(q[c * CHUNK:(c + 1) * CHUNK, :], kv, CHUNK))
        if c * CHUNK < WINDOW:
            sc = jnp.where(key_i >= WINDOW - c * CHUNK - pos0, sc, -jnp.inf)
        scores.append(sc)
    return scores


def _attn_probs(scores, sinks_ref, ts):
    sink_rows = [_sink_row(sinks_ref, kv, CHUNK) for kv in range(N_KV_HEADS)]
    probs = []
    for (c, kv), sc in zip(_attn_groups(ts), scores):
        m = jnp.maximum(jnp.max(sc, axis=0, keepdims=True), sink_rows[kv])
        p = jnp.exp(sc - m)
        denom = jnp.sum(p, axis=0, keepdims=True) + jnp.exp(sink_rows[kv] - m)
        probs.append((p * (1.0 / denom)).astype(BF16))
    return probs


def _attn_values(probs, vhist, ts):
    n_keys = WINDOW + CHUNK
    outs = []
    for (c, kv), p in zip(_attn_groups(ts), probs):
        v_g = vhist[c * CHUNK:c * CHUNK + n_keys, kv * HEAD_DIM:(kv + 1) * HEAD_DIM].astype(BF16)
        outs.append(_unstack_heads(_dot_tn(p, v_g), CHUNK))
    rows = [jnp.concatenate(outs[c * N_KV_HEADS:(c + 1) * N_KV_HEADS], axis=1) for c in range(ts // CHUNK)]
    return jnp.concatenate(rows, axis=0).astype(BF16)


def _ssd_cumsum(dt, a_row, e):
    ts = dt.shape[0]
    L = SSM_CHUNK
    span = min(ts, MXU_TILE)
    ri = lax.broadcasted_iota(jnp.int32, (span, span), 0)
    lag = ri - lax.broadcasted_iota(jnp.int32, (span, span), 1)
    tri = jnp.where((lag >= 0) & (lag <= (ri & (L - 1))), 1.0, 0.0).astype(BF16)
    hi, mid, lo = _split3(dt * a_row)
    cum = jnp.concatenate(
        [_dot(tri, hi[r:r + span]) + _dot(tri, mid[r:r + span]) + _dot(tri, lo[r:r + span])
         for r in range(0, ts, span)], axis=0)
    hi = cum.astype(BF16)
    mid = (cum - hi.astype(F32)).astype(BF16)
    cum_e = _dot(hi, e) + _dot(mid, e)
    dt_e = _dot(dt.astype(BF16), e)
    return cum_e, dt_e


def _ssd_decays(xact, cum_e, dt_e):
    ts = xact.shape[0]
    L = SSM_CHUNK
    nc = ts // L
    xs = xact[:, 0:SSM_INNER]
    cum_e = cum_e.reshape(nc, L, SSM_INNER)
    t3 = lax.broadcasted_iota(jnp.int32, cum_e.shape, 1)
    slot3 = lax.broadcasted_iota(jnp.int32, cum_e.shape, 2) & (SSM_HEAD_DIM - 1)
    cum_src = jnp.sum(jnp.where(t3 == slot3, cum_e, 0.0), axis=1, keepdims=True)
    decay = jnp.exp(jnp.where(t3 >= slot3, cum_e - cum_src, -jnp.inf))
    last = cum_e[:, L - 1:L, :]
    exp_cum = jnp.exp(cum_e)
    chunk_decay = jnp.exp(last)
    xdt = xs * dt_e
    wx = (xdt.reshape(nc, L, SSM_INNER) * jnp.exp(last - cum_e)).astype(BF16)
    first_head = (lax.broadcasted_iota(jnp.int32, xdt.shape, 1) & (2 * SSM_HEAD_DIM - 1)) < SSM_HEAD_DIM
    return {
        "xs": xs,
        "bm": xact[:, SSM_INNER:SSM_INNER + SSM_GROUPS * SSM_STATE].astype(BF16),
        "cm": xact[:, SSM_INNER + SSM_GROUPS * SSM_STATE:CONV_DIM].astype(BF16),
        "decay": decay, "exp_cum": exp_cum, "chunk_decay": chunk_decay, "wx": wx,
        "xdt_a": jnp.where(first_head, xdt, 0.0).astype(BF16),
        "xdt_b": jnp.where(first_head, 0.0, xdt).astype(BF16),
    }


def _ssd_block(ctx, st, dskip_e, fillers):
    L = SSM_CHUNK
    nc = ctx["xs"].shape[0] // L
    bm, cm = ctx["bm"], ctx["cm"]
    fillers = iter(fillers)

    def grp(a, c, g, width):
        return a[c * L:(c + 1) * L, g * width:(g + 1) * width]

    chunk_state = [jnp.concatenate([_dot_tn(grp(bm, c, g, SSM_STATE), ctx["wx"][c][:, g * GROUP_INNER:(g + 1) * GROUP_INNER])
                                    for g in range(SSM_GROUPS)], axis=1) for c in range(nc)]
    next(fillers)()
    cb = [[_dot_nt(grp(cm, c, g, SSM_STATE), jnp.concatenate([grp(bm, c, g, SSM_STATE)] * 2, axis=0))
           for g in range(SSM_GROUPS)] for c in range(nc)]
    next(fillers)()
    st_in = []
    for c in range(nc):
        st_in.append(st.astype(BF16))
        st = st * ctx["chunk_decay"][c] + chunk_state[c]
    y_intra = []
    for c in range(nc):
        rows = slice(c * L, (c + 1) * L)
        parts = []
        for k in range(SSM_HEADS // 2):
            ls = slice(k * 2 * SSM_HEAD_DIM, (k + 1) * 2 * SSM_HEAD_DIM)
            m_pair = (cb[c][2 * k // HEADS_PER_GROUP] * ctx["decay"][c][:, ls]).astype(BF16)
            block_diag = jnp.concatenate([ctx["xdt_a"][rows, ls], ctx["xdt_b"][rows, ls]], axis=0)
            parts.append(_dot(m_pair, block_diag))
        y_intra.append(jnp.concatenate(parts, axis=1))
    next(fillers)()
    ys = []
    for c in range(nc):
        y_inter = jnp.concatenate(
            [_dot(grp(cm, c, g, SSM_STATE), st_in[c][:, g * GROUP_INNER:(g + 1) * GROUP_INNER])
             for g in range(SSM_GROUPS)], axis=1)
        ys.append(y_intra[c] + y_inter * ctx["exp_cum"][c] + dskip_e * ctx["xs"][c * L:(c + 1) * L])
    next(fillers)()
    return jnp.concatenate(ys, axis=0), st


class _Stagger:
    def __init__(self):
        self._pending = None

    def push(self, produce, consume):
        val = produce()
        self.flush()
        self._pending = (consume, val)

    def flush(self):
        if self._pending is not None:
            consume, val = self._pending
            self._pending = None
            consume(val)


def _prompt_kernel(x_ref, mod_ref, gmix_ref, wa_ref, wb_ref, wdt_ref, cos_ref, sa_ref, sb_ref, sinks_ref,
                   convw_ref, convb_ref, dtb_ref, alog_ref, dskip_ref, ssmnw_ref, glng_ref, glnb_ref,
                   gws_ref, gbst_ref, wao_ref, wso_ref, wgo_ref, wout_ref, e_ref,
                   xo_ref, ko_ref, vo_ref, sto_ref, cvo_ref,
                   khist, vhist, xp, st_s):
    ts = PROMPT_TS
    s = pl.program_id(1)
    last = pl.num_programs(1) - 1

    @pl.when(s == 0)
    def _():
        khist[0:WINDOW, :] = jnp.zeros((WINDOW, KV_W), F32)
        vhist[0:WINDOW, :] = jnp.zeros((WINDOW, KV_W), F32)
        xp[...] = jnp.zeros_like(xp)
        st_s[...] = jnp.zeros_like(st_s)

    x3 = x_ref[...]
    mod3 = mod_ref[...]
    half_rows = ts // 2
    h_top = _modnorm(x3[:, 0:half_rows], gmix_ref[...], mod3)
    q0_top = _dot(h_top, wa_ref[:, C_Q[1]:C_Q[1] + PROJ_BLOCK])
    h_bot = _modnorm(x3[:, half_rows:ts], gmix_ref[...], mod3)
    q0 = jnp.concatenate([q0_top, _dot(h_bot, wa_ref[:, C_Q[1]:C_Q[1] + PROJ_BLOCK])], axis=0)
    h = jnp.concatenate([h_top, h_bot], axis=0)

    cos, sa, sb = cos_ref[...], sa_ref[...], sb_ref[...]
    res = {}

    def rope_block(raw):
        return [_rope(raw[:, i * LANES:(i + 1) * LANES], cos, sa, sb) for i in range(raw.shape[1] // LANES)]

    def q_block(raw):
        res.setdefault("q", []).extend(p.astype(BF16) for p in rope_block(raw * (HEAD_DIM ** -0.5)))

    def kv_block(raw):
        khist[WINDOW:WINDOW + ts, :] = rope_block(raw[:, 0:KV_W])[0]
        vhist[WINDOW:WINDOW + ts, :] = raw[:, KV_W:2 * KV_W]

    def conv_block(lo, raw):
        cs = slice(lo, lo + PROJ_BLOCK)
        ext = jnp.concatenate([xp[:, cs], raw], axis=0)
        acc = convb_ref[:, cs] + raw * convw_ref[CONV_WIDTH - 1:CONV_WIDTH, cs]
        for j in range(1, CONV_WIDTH):
            acc = acc + pltpu.roll(ext, j, 0)[8:8 + ts] * convw_ref[CONV_WIDTH - 1 - j:CONV_WIDTH - j, cs]
        xp[:, cs] = raw[ts - 8:ts]
        res.setdefault("xact", []).append(_silu_of_half(acc))

    def collect(key, fn):
        return lambda raw: res.setdefault(key, []).append(fn(raw))

    wrefs = (wa_ref, wb_ref, wdt_ref)
    pipe = _Stagger()

    def push_proj(cols, off, width, consume):
        ref, lo = wrefs[cols[0]], cols[1] + off
        pipe.push(lambda: _dot(h, ref[:, lo:lo + width]), consume)

    def blocks(cols, consume):
        return [functools.partial(push_proj, cols, off, PROJ_BLOCK, consume)
                for off in range(0, cols[2] - cols[1], PROJ_BLOCK)]

    conv_blocks = [functools.partial(push_proj, C_XBC, off, PROJ_BLOCK, functools.partial(conv_block, off))
                   for off in range(0, CONV_DIM, PROJ_BLOCK)]
    gate_blocks = iter(blocks(C_G0, collect("g0", _twice_sigmoid_of_half))
                       + blocks(C_G1, collect("g1", _twice_sigmoid_of_half))
                       + blocks(C_G2, collect("g2", _twice_sigmoid_of_half)))
    z_blocks = blocks(C_Z, collect("sz", _silu_of_half))
    gm_blocks = blocks(C_GU, collect("u", _gelu)) + blocks(C_GV, collect("gv", _gelu))

    def cat(key):
        return jnp.concatenate(res[key], axis=1)

    pipe.push(lambda: q0, q_block)
    for blk in blocks(C_Q, q_block)[1:]:
        blk()
    push_proj(C_K, 0, 2 * KV_W, kv_block)
    conv_blocks[0]()
    next(gate_blocks)()
    conv_blocks[1]()
    scores = _attn_scores(cat("q"), khist, s * ts)
    conv_blocks[2]()
    probs = _attn_probs(scores, sinks_ref, ts)
    next(gate_blocks)()
    conv_blocks[3]()
    attn = _attn_values(probs, vhist, ts)
    conv_blocks[4]()
    next(gate_blocks)()
    conv_blocks[5]()
    push_proj(C_DT, 0, LANES, collect("dt", lambda raw: _softplus(raw + dtb_ref[...])))
    next(gate_blocks)()
    a_out = _dot(attn, wao_ref[...])
    z_blocks[0]()
    cum_e, dt_e = _ssd_cumsum(res["dt"][0], -jnp.exp(alog_ref[...]), e_ref[...])
    z_blocks[1]()
    merged = cat("g0") * a_out
    ctx = _ssd_decays(cat("xact"), cum_e, dt_e)
    z_blocks[2]()
    z_blocks[3]()
    y, st_s[...] = _ssd_block(ctx, st_s[...], dskip_ref[...], gm_blocks)
    next(gate_blocks)()
    vn = _layer_norm(cat("gv"), glng_ref[...], glnb_ref[...])
    u = cat("u")
    next(gate_blocks)()
    ssm = _rms(y * cat("sz"), ssmnw_ref[...]).astype(BF16)
    gm = jnp.concatenate(
        [_gmlp_chunk(vn[c * GM_CHUNK:(c + 1) * GM_CHUNK], u[c * GM_CHUNK:(c + 1) * GM_CHUNK], gws_ref, gbst_ref,
                     GM_CHUNK) for c in range(ts // GM_CHUNK)], axis=0).astype(BF16)
    next(gate_blocks)()
    b_out = _dot(ssm, wso_ref[...])
    next(gate_blocks)()
    pipe.flush()
    merged = merged + cat("g1") * b_out
    c_out = _dot(gm, wgo_ref[...])
    for blk in gate_blocks:
        blk()
    pipe.flush()
    merged = merged + cat("g2") * c_out
    o = _dot(merged.astype(BF16), wout_ref[...])
    xo_ref[...] = x3 + mod3[:, 2:3, :] * o.reshape(x3.shape)

    khist[0:WINDOW, :] = khist[ts:ts + WINDOW, :]
    vhist[0:WINDOW, :] = vhist[ts:ts + WINDOW, :]

    @pl.when(s == last)
    def _():
        ko_ref[0] = khist[ts:ts + WINDOW, :]
        vo_ref[0] = vhist[ts:ts + WINDOW, :]
        cvo_ref[0] = xp[8 - (CONV_WIDTH - 1):8, :]
        sto_ref[0] = st_s[...].T


def _sample_attention(q, k_new, v_new, ck_ref, cv_ref, sinks_ref, nb):
    t = q.shape[0] // nb
    groups = [(i, kv) for i in range(nb) for kv in range(N_KV_HEADS)]

    def new_rows(a, i, kv):
        return a[i * t:(i + 1) * t, kv * HEAD_DIM:(kv + 1) * HEAD_DIM].astype(BF16)

    def cached(ref, i, kv):
        return ref[i][:, kv * HEAD_DIM:(kv + 1) * HEAD_DIM].astype(BF16)

    qg = [_stack_heads(q[i * t:(i + 1) * t], kv, t) for i, kv in groups]
    s_c = [_dot_nt(cached(ck_ref, i, kv), qg[n]) for n, (i, kv) in enumerate(groups)]
    s_n = [_dot_nt(new_rows(k_new, i, kv), qg[n]) for n, (i, kv) in enumerate(groups)]
    sinks = [_sink_row(sinks_ref, kv, t) for kv in range(N_KV_HEADS)]
    probs = []
    for n, (i, kv) in enumerate(groups):
        m = jnp.maximum(jnp.maximum(jnp.max(s_c[n], axis=0, keepdims=True), jnp.max(s_n[n], axis=0, keepdims=True)),
                        sinks[kv])
        p_c = jnp.exp(s_c[n] - m)
        p_n = jnp.exp(s_n[n] - m)
        inv = 1.0 / (jnp.sum(p_c, axis=0, keepdims=True) + jnp.sum(p_n, axis=0, keepdims=True)
                     + jnp.exp(sinks[kv] - m))
        probs.append(((p_c * inv).astype(BF16), (p_n * inv).astype(BF16)))
    outs = [_dot_tn(probs[n][0], cached(cv_ref, i, kv)) + _dot_tn(probs[n][1], new_rows(v_new, i, kv))
            for n, (i, kv) in enumerate(groups)]
    rows = [jnp.concatenate([_unstack_heads(outs[i * N_KV_HEADS + kv], t) for kv in range(N_KV_HEADS)], axis=1)
            for i in range(nb)]
    return jnp.concatenate(rows, axis=0)


def _sample_ssd_setup(xact, dt, a_row, e, L):
    m = xact.shape[0]
    n_seq = m // L
    xs = xact[:, 0:SSM_INNER]
    ri = lax.broadcasted_iota(jnp.int32, (m, m), 0)
    lag = ri - lax.broadcasted_iota(jnp.int32, (m, m), 1)
    tri = jnp.where((lag >= 0) & (lag <= (ri & (L - 1))), 1.0, 0.0).astype(BF16)
    hi, mid, lo = _split3(dt * a_row)
    cum = _dot(tri, hi) + _dot(tri, mid) + _dot(tri, lo)
    hi, mid, lo = _split3(cum)
    cum_e = (_dot(hi, e) + _dot(mid, e) + _dot(lo, e)).reshape(n_seq, L, SSM_INNER)
    head_of_lane = jnp.right_shift(lax.broadcasted_iota(jnp.int32, (LANES, SSM_HEADS * L), 1), _log2(L))
    e_slots = jnp.where(head_of_lane == lax.broadcasted_iota(jnp.int32, (LANES, SSM_HEADS * L), 0), 1.0, 0.0)
    e_slots = e_slots.astype(BF16)
    cum_s = (_dot(hi, e_slots) + _dot(mid, e_slots) + _dot(lo, e_slots)).reshape(n_seq, L, SSM_HEADS * L)
    dt_hi = dt.astype(BF16)
    dt_lo = (dt - dt_hi.astype(F32)).astype(BF16)
    dt_e = _dot(dt_hi, e) + _dot(dt_lo, e)
    t3 = lax.broadcasted_iota(jnp.int32, cum_s.shape, 1)
    slot3 = lax.broadcasted_iota(jnp.int32, cum_s.shape, 2) & (L - 1)
    cum_src = jnp.sum(jnp.where(t3 == slot3, cum_s, 0.0), axis=1, keepdims=True)
    decay = jnp.exp(jnp.where(t3 >= slot3, cum_s - cum_src, -jnp.inf))
    last = cum_e[:, L - 1:L, :]
    xdt = xs * dt_e
    wx = (xdt.reshape(n_seq, L, SSM_INNER) * jnp.exp(last - cum_e)).astype(BF16)
    per_head = jnp.exp(cum.reshape(n_seq, L, LANES)[:, L - 1:L, :])
    rep = jnp.broadcast_to(per_head, (n_seq, SSM_HEADS, LANES)).reshape(n_seq * SSM_HEADS, LANES)
    own_lane = (lax.broadcasted_iota(jnp.int32, rep.shape, 1)
                == (lax.broadcasted_iota(jnp.int32, rep.shape, 0) & (SSM_HEADS - 1)))
    hi, mid, lo = _split3(jnp.where(own_lane, rep, 0.0))
    ones = jnp.ones((LANES, LANES), BF16)
    splat = (_dot(hi, ones) + _dot(mid, ones) + _dot(lo, ones)).reshape(n_seq, SSM_HEADS, LANES)
    return {
        "L": L, "xs": xs,
        "bm": xact[:, SSM_INNER:SSM_INNER + SSM_GROUPS * SSM_STATE].astype(BF16),
        "cm": xact[:, SSM_INNER + SSM_GROUPS * SSM_STATE:CONV_DIM].astype(BF16),
        "decay": decay, "exp_cum": jnp.exp(cum_e), "wx": wx, "xdt": xdt.astype(BF16), "state_decay": splat,
    }


def _sample_ssd_block(s, h0_ref, sto_ref, dskip_e, nb):
    L = s["L"]
    pairs = [(i, g) for i in range(nb) for g in range(SSM_GROUPS)]
    ri = jnp.right_shift(lax.broadcasted_iota(jnp.int32, (HEADS_PER_GROUP * L, GROUP_INNER), 0), _log2(L))
    li = jnp.right_shift(lax.broadcasted_iota(jnp.int32, (HEADS_PER_GROUP * L, GROUP_INNER), 1), _log2(SSM_HEAD_DIM))

    def rows(a, i, g, width):
        return a[i * L:(i + 1) * L, g * width:(g + 1) * width]

    cb = [_dot_nt(rows(s["cm"], i, g, SSM_STATE), jnp.concatenate([rows(s["bm"], i, g, SSM_STATE)] * HEADS_PER_GROUP,
                                                                  axis=0)) for i, g in pairs]
    upd = [_dot_tn(s["wx"][i][:, g * GROUP_INNER:(g + 1) * GROUP_INNER], rows(s["bm"], i, g, SSM_STATE))
           for i, g in pairs]
    y_inter = [_dot_nt(rows(s["cm"], i, g, SSM_STATE), h0_ref[i][g * GROUP_INNER:(g + 1) * GROUP_INNER, :].astype(BF16))
               for i, g in pairs]
    y_intra = []
    for n, (i, g) in enumerate(pairs):
        gs = slice(g * HEADS_PER_GROUP * L, (g + 1) * HEADS_PER_GROUP * L)
        m_g = (cb[n] * s["decay"][i][:, gs]).astype(BF16)
        tiled = jnp.concatenate([rows(s["xdt"], i, g, GROUP_INNER)] * HEADS_PER_GROUP, axis=0)
        y_intra.append(_dot(m_g, jnp.where(ri == li, tiled, jnp.zeros_like(tiled))))
    ys = []
    for i in range(nb):
        parts = [y_intra[i * SSM_GROUPS + g] + y_inter[i * SSM_GROUPS + g]
                 * s["exp_cum"][i][:, g * GROUP_INNER:(g + 1) * GROUP_INNER] for g in range(SSM_GROUPS)]
        ys.append(jnp.concatenate(parts, axis=1) + dskip_e * s["xs"][i * L:(i + 1) * L])
        decayed = h0_ref[i].reshape(SSM_HEADS, SSM_HEAD_DIM, SSM_STATE) * s["state_decay"][i][:, None, :]
        sto_ref[i] = decayed.reshape(SSM_INNER, SSM_STATE) + jnp.concatenate(
            upd[i * SSM_GROUPS:(i + 1) * SSM_GROUPS], axis=0)
    return jnp.concatenate(ys, axis=0)


def _sample_kernel(x_ref, mod_ref, gmix_ref, wa_ref, wb_ref, wdt_ref, cos_ref, sa_ref, sb_ref, sinks_ref,
                   ck_ref, cv_ref, h0_ref, cs_ref,
                   convw_ref, convb_ref, dtb_ref, alog_ref, dskip_ref, ssmnw_ref, glng_ref, glnb_ref,
                   gws_ref, gbst_ref, wao_ref, wso_ref, wgo_ref, wout_ref, e_ref,
                   xo_ref, ko_ref, vo_ref, sto_ref, cvo_ref, gvo_ref,
                   xp):
    nb = SAMPLE_NB
    t = x_ref.shape[1]
    m = nb * t
    win_ref = (wa_ref, wb_ref, wdt_ref)
    x3 = x_ref[...]
    mod3 = mod_ref[...]
    h = _modnorm(x3, gmix_ref[...], mod3)

    cos, sa, sb = cos_ref[...], sa_ref[...], sb_ref[...]
    q = _proj(h, win_ref, C_Q) * (HEAD_DIM ** -0.5)
    q = jnp.concatenate(
        [_rope(q[:, i * LANES:(i + 1) * LANES], cos, sa, sb) for i in range(Q_W // LANES)], axis=1).astype(BF16)
    k_new = _rope(_proj(h, win_ref, C_K), cos, sa, sb)
    v_new = _proj(h, win_ref, C_V)
    ko_ref[...] = k_new.reshape(nb, t, KV_W)
    vo_ref[...] = v_new.reshape(nb, t, KV_W)

    xp[:, 8 - (CONV_WIDTH - 1):8, :] = cs_ref[...]
    xp[:, 8:8 + t, :] = _proj(h, win_ref, C_XBC).reshape(nb, t, CONV_DIM)
    acc = convb_ref[...] + xp[:, 8:8 + t, :] * convw_ref[CONV_WIDTH - 1:CONV_WIDTH, :]
    for j in range(1, CONV_WIDTH):
        acc = acc + xp[:, 8 - j:8 - j + t, :] * convw_ref[CONV_WIDTH - 1 - j:CONV_WIDTH - j, :]
    xact = _silu_of_half(acc).reshape(m, CONV_DIM)
    cvo_ref[...] = xp[:, t + 8 - (CONV_WIDTH - 1):t + 8, :]
    dt = _softplus(_proj(h, win_ref, C_DT) + dtb_ref[...])
    dskip_e = dskip_ref[...]

    u = _gelu(_proj(h, win_ref, C_GU))
    vn = _layer_norm(_gelu(_proj(h, win_ref, C_GV)), glng_ref[...], glnb_ref[...])
    gvo_ref[...] = vn.reshape(nb, t, GM_WIDTH)

    ssd = _sample_ssd_setup(xact, dt, -jnp.exp(alog_ref[...]), e_ref[...], t)
    attn = _sample_attention(q, k_new, v_new, ck_ref, cv_ref, sinks_ref, nb)
    y = _sample_ssd_block(ssd, h0_ref, sto_ref, dskip_e, nb)
    gm_rows = [_gmlp_chunk(vn[i * t:(i + 1) * t], u[i * t:(i + 1) * t], gws_ref, gbst_ref, t) for i in range(nb)]

    a_out = _dot(attn.astype(BF16), wao_ref[...])
    z = _proj(h, win_ref, C_Z)
    ssm = _rms(y * _silu_of_half(z), ssmnw_ref[...]).astype(BF16)
    b_out = _dot(ssm, wso_ref[...])
    c_out = _dot(jnp.concatenate(gm_rows, axis=0).astype(BF16), wgo_ref[...])
    gates = [_twice_sigmoid_of_half(_proj(h, win_ref, cols)) for cols in (C_G0, C_G1, C_G2)]
    xo_ref[...] = _merge_out(x3, mod3, gates, a_out, b_out, c_out, wout_ref)


def _ffn_block(x_ref, mod_ref, gff_ref, w1_ref, w2_ref, gfin_ref, o_ref, final_norm):
    x3 = x_ref[...]
    mod3 = mod_ref[...]
    nb, t, d = x3.shape
    h3 = _rms(x3, gff_ref[...]) * (1.0 + mod3[:, 4:5, :]) + mod3[:, 3:4, :]
    h = h3.reshape(nb * t, d).astype(BF16)
    a = jnp.maximum(_dot(h, w1_ref[...]), 0.0)
    y = _dot((a * a).astype(BF16), w2_ref[...])
    out = x3 + mod3[:, 5:6, :] * y.reshape(nb, t, d)
    if final_norm:
        out = _rms(out, gfin_ref[...])
    o_ref[...] = out


def _ffn_kernel(xp_ref, modp_ref, xs_ref, mods_ref, gff_ref, w1_ref, w2_ref, gfin_ref, op_ref, os_ref, *,
                final_norm, n_prompt_steps):
    i = pl.program_id(0)

    @pl.when(i < n_prompt_steps)
    def _():
        _ffn_block(xp_ref, modp_ref, gff_ref, w1_ref, w2_ref, gfin_ref, op_ref, final_norm)

    @pl.when(i == n_prompt_steps)
    def _():
        _ffn_block(xs_ref, mods_ref, gff_ref, w1_ref, w2_ref, gfin_ref, os_ref, final_norm)


def _ada_kernel(c_ref, w_ref, b_ref, op_ref, os_ref):
    mod = _dot(_silu(c_ref[...]).astype(BF16), w_ref[...].astype(BF16)) + b_ref[...]
    n_prompt = op_ref.shape[0]
    op_ref[...] = mod[0:n_prompt]
    os_ref[...] = mod[n_prompt:]


def _w_in_piece_kernel(wt_ref, o_ref, *, half_lo, half_hi, n_valid):
    cols = wt_ref.shape[0]
    col = pl.program_id(1) * cols + lax.broadcasted_iota(jnp.int32, (cols, 1), 0)
    scale = jnp.where(col < n_valid, jnp.where((col >= half_lo) & (col < half_hi), 0.5, 1.0), 0.0)
    o_ref[...] = (wt_ref[...] * scale).astype(BF16).T


def _w_in_piece(w_t, first_col, n_cols, half_cols, n_blocks, n_valid=None):
    cols = n_cols // n_blocks
    n_valid = n_cols if n_valid is None else n_valid
    return pl.pallas_call(
        functools.partial(_w_in_piece_kernel, half_lo=half_cols[0], half_hi=half_cols[1], n_valid=n_valid),
        grid=(DEPTH, n_blocks),
        in_specs=[pl.BlockSpec((pl.Squeezed(), pl.Element(cols), pl.Element(D_MODEL)),
                               lambda l, j: (l, pl.multiple_of(first_col + j * cols, SSM_HEADS), 0))],
        out_specs=pl.BlockSpec((None, D_MODEL, cols), lambda l, j: (l, 0, j)),
        out_shape=jax.ShapeDtypeStruct((DEPTH, D_MODEL, n_cols), BF16),
        compiler_params=_params(2),
        name="w_in_piece",
    )(w_t)


def _w_in_split(w_in):
    w_t = jnp.swapaxes(w_in, 1, 2)
    b_lo = N_IN_A + SSM_HEADS
    return (_w_in_piece(w_t, 0, N_IN_A, (C_Z[1], C_Z[2]), W_SPLIT_BLOCKS),
            _w_in_piece(w_t, b_lo, N_IN_B, (C_G0[1], N_IN_B), W_SPLIT_BLOCKS),
            _w_in_piece(w_t, N_IN_A, LANES, (0, 0), 1, n_valid=SSM_HEADS))


def _with_ignored_inputs(body, n_in, n_ignored):
    if n_ignored == 0:
        return body
    return lambda *refs: body(*refs[:n_in], *refs[n_in + n_ignored:])


def _const_spec(shape):
    return pl.BlockSpec(shape, lambda *_: (0,) * len(shape), pipeline_mode=pl.Buffered(1))


def _layer_spec(l, shape):
    return pl.BlockSpec((None,) + shape, lambda *_: (l,) + (0,) * len(shape), pipeline_mode=pl.Buffered(1))


def _any_spec():
    return pl.BlockSpec(memory_space=pl.ANY)


def _smem_spec():
    return pl.BlockSpec(memory_space=pltpu.SMEM)


def _params(n_grid):
    return pltpu.CompilerParams(dimension_semantics=("arbitrary",) * n_grid, vmem_limit_bytes=VMEM_LIMIT)


def _ada_call(c_prompt, c_sample, w_ada, b_ada):
    n_p, n_s = c_prompt.shape[0], c_sample.shape[0]
    c_all = jnp.concatenate([c_prompt, c_sample], axis=0)
    tn = D_MODEL
    n_tiles = w_ada.shape[2] // tn
    mod_p, mod_s = pl.pallas_call(
        _ada_kernel,
        grid=(DEPTH, n_tiles),
        in_specs=[pl.BlockSpec((n_p + n_s, D_MODEL), lambda l, n: (0, 0)),
                  pl.BlockSpec((None, D_MODEL, tn), lambda l, n: (l, 0, n)),
                  pl.BlockSpec((None, 1, tn), lambda l, n: (l, 0, n))],
        out_specs=(pl.BlockSpec((None, n_p, tn), lambda l, n: (l, 0, n)),
                   pl.BlockSpec((None, n_s, tn), lambda l, n: (l, 0, n))),
        out_shape=(jax.ShapeDtypeStruct((DEPTH, n_p, w_ada.shape[2]), F32),
                   jax.ShapeDtypeStruct((DEPTH, n_s, w_ada.shape[2]), F32)),
        compiler_params=_params(2),
        name="ada_mod",
    )(c_all, w_ada, b_ada.reshape(DEPTH, 1, -1))
    return mod_p.reshape(DEPTH, n_p, 6, D_MODEL), mod_s.reshape(DEPTH, n_s, 6, D_MODEL)


def _w_in_specs(l):
    return [_layer_spec(l, (D_MODEL, N_IN_A)), _layer_spec(l, (D_MODEL, N_IN_B)), _layer_spec(l, (D_MODEL, LANES))]


def _layer_weight_specs(l):
    return [
        _layer_spec(l, (CONV_WIDTH, CONV_DIM)), _layer_spec(l, (1, CONV_DIM)), _layer_spec(l, (1, LANES)),
        _layer_spec(l, (1, LANES)), _layer_spec(l, (1, SSM_INNER)), _layer_spec(l, (1, SSM_INNER)),
        _layer_spec(l, (1, GM_WIDTH)), _layer_spec(l, (1, GM_WIDTH)),
        _layer_spec(l, (GM_GROUPS, GM_CHUNK, GM_CHUNK)), _layer_spec(l, (GM_CHUNK, GM_GROUPS)),
        _layer_spec(l, (Q_W, D_MODEL)), _layer_spec(l, (SSM_INNER, D_MODEL)), _layer_spec(l, (GM_WIDTH, D_MODEL)),
        _layer_spec(l, (D_MODEL, D_MODEL)), _const_spec((LANES, SSM_INNER)),
    ]


def _layer_weight_args(p):
    return (p["conv_w"], p["conv_b"], p["dt_bias"], p["a_log"], p["d_skip"], p["ssm_norm_w"], p["gm_ln_g"],
            p["gm_ln_b"], p["gm_w_s"], p["gm_b_st"], p["w_attn_o"], p["w_ssm_o"], p["w_gm_o"], p["w_out"], p["expand"])


def _prompt_mixer(l, x, mod, p, rope, prev):
    bsz, seq, d = x.shape
    ts = PROMPT_TS
    tab = pl.BlockSpec((ts, LANES), lambda b, s: (s, 0))
    in_specs = [
        pl.BlockSpec((1, ts, d), lambda b, s: (b, s, 0)),
        pl.BlockSpec((None, 1, 6, d), lambda b, s: (l, b, 0, 0)),
        _layer_spec(l, (1, d)), *_w_in_specs(l),
        tab, tab, tab, _smem_spec(),
    ] + _layer_weight_specs(l)
    args = (x, mod, p["g_mix"], *p["w_in"], *rope, p["sinks"][l], *_layer_weight_args(p))
    state_tails = ((WINDOW, KV_W), (WINDOW, KV_W), (SSM_INNER, SSM_STATE), (CONV_WIDTH - 1, CONV_DIM))
    out_shape = (jax.ShapeDtypeStruct((bsz, seq, d), F32),) + tuple(
        jax.ShapeDtypeStruct((DEPTH, bsz) + tail, F32) for tail in state_tails)
    out_specs = (pl.BlockSpec((1, ts, d), lambda b, s: (b, s, 0)),) + tuple(
        pl.BlockSpec((None, 1) + tail, lambda b, s: (l, b, 0, 0)) for tail in state_tails)
    scratch = [
        pltpu.VMEM((WINDOW + ts, KV_W), F32), pltpu.VMEM((WINDOW + ts, KV_W), F32),
        pltpu.VMEM((8, CONV_DIM), F32), pltpu.VMEM((SSM_STATE, SSM_INNER), F32),
    ]
    n_in = len(args)
    prev = () if prev is None else tuple(prev)
    return pl.pallas_call(
        _with_ignored_inputs(_prompt_kernel, n_in, len(prev)),
        grid=(bsz, seq // ts), in_specs=in_specs + [_any_spec()] * len(prev), out_specs=out_specs,
        out_shape=out_shape, scratch_shapes=scratch, compiler_params=_params(2), name="prompt_mixer",
        input_output_aliases={n_in + i: 1 + i for i in range(len(prev))},
    )(*args, *prev)


def _sample_mixer(l, x, mod, cache_k, cache_v, h0, conv_state, p, rope, prev):
    bsz, t, d = x.shape
    nb = SAMPLE_NB
    m = nb * t

    def blk(*tail):
        return pl.BlockSpec((nb,) + tail, lambda i: (i,) + (0,) * len(tail))

    def lblk(*tail):
        return pl.BlockSpec((None, nb) + tail, lambda i: (l, i) + (0,) * len(tail))

    in_specs = [
        blk(t, d), lblk(6, d), _layer_spec(l, (1, d)), *_w_in_specs(l),
        _const_spec((m, LANES)), _const_spec((m, LANES)), _const_spec((m, LANES)), _smem_spec(),
        lblk(WINDOW, KV_W), lblk(WINDOW, KV_W), lblk(SSM_INNER, SSM_STATE), lblk(CONV_WIDTH - 1, CONV_DIM),
    ] + _layer_weight_specs(l)
    args = (x, mod, p["g_mix"], *p["w_in"], *rope, p["sinks"][l], cache_k, cache_v, h0, conv_state,
            *_layer_weight_args(p))
    state_tails = ((t, KV_W), (t, KV_W), (SSM_INNER, SSM_STATE), (CONV_WIDTH - 1, CONV_DIM), (t, GM_WIDTH))
    out_shape = (jax.ShapeDtypeStruct((bsz, t, d), F32),) + tuple(
        jax.ShapeDtypeStruct((DEPTH, bsz) + tail, F32) for tail in state_tails)
    out_specs = (blk(t, d),) + tuple(lblk(*tail) for tail in state_tails)
    scratch = [pltpu.VMEM((nb, 8 + t, CONV_DIM), F32)]
    n_in = len(args)
    prev = () if prev is None else tuple(prev)
    return pl.pallas_call(
        _with_ignored_inputs(_sample_kernel, n_in, len(prev)),
        grid=(bsz // nb,), in_specs=in_specs + [_any_spec()] * len(prev), out_specs=out_specs,
        out_shape=out_shape, scratch_shapes=scratch, compiler_params=_params(1), name="sample_mixer",
        input_output_aliases={n_in + i: 1 + i for i in range(len(prev))},
    )(*args, *prev)


def _ffn(l, x_p, mod_p, x_s, mod_s, p, g_final, final_norm):
    bp, sp, d = x_p.shape
    bs, ss, _ = x_s.shape
    assert bs * ss == FFN_ROWS and sp % FFN_ROWS == 0
    blocks_per_seq = sp // FFN_ROWS
    n_prompt_steps = bp * blocks_per_seq

    def prompt_block(i):
        j = jnp.minimum(i, n_prompt_steps - 1)
        return j // blocks_per_seq, j % blocks_per_seq

    x_spec = pl.BlockSpec((1, FFN_ROWS, d), lambda i: (*prompt_block(i), 0))
    s_spec = pl.BlockSpec((bs, ss, d), lambda i: (0, 0, 0))
    return pl.pallas_call(
        functools.partial(_ffn_kernel, final_norm=final_norm, n_prompt_steps=n_prompt_steps),
        grid=(n_prompt_steps + 1,),
        in_specs=[x_spec, pl.BlockSpec((None, 1, 6, d), lambda i: (l, prompt_block(i)[0], 0, 0)),
                  s_spec, pl.BlockSpec((None, bs, 6, d), lambda i: (l, 0, 0, 0)),
                  _layer_spec(l, (1, d)), _layer_spec(l, (d, D_FF)), _layer_spec(l, (D_FF, d)), _const_spec((1, d))],
        out_specs=(x_spec, s_spec),
        out_shape=(jax.ShapeDtypeStruct(x_p.shape, F32), jax.ShapeDtypeStruct(x_s.shape, F32)),
        compiler_params=_params(1), name="ffn",
    )(x_p, mod_p, x_s, mod_s, p["g_ff"], p["w_ff1"], p["w_ff2"], g_final)


def _rope_tables(pos):
    half = ROT_DIM // 2
    inv_freq = ROPE_THETA ** (-jnp.arange(half, dtype=F32) * (2.0 / ROT_DIM))
    ang = pos.astype(F32)[:, None] * inv_freq[None, :]
    cos, sin = jnp.cos(ang), jnp.sin(ang)
    n = pos.shape[0]
    ones = jnp.ones((n, HEAD_DIM - ROT_DIM), F32)
    zeros = jnp.zeros((n, HEAD_DIM - ROT_DIM), F32)
    zh = jnp.zeros((n, half), F32)
    cos_t = jnp.concatenate([cos, cos, ones], axis=1)
    sin_a = jnp.concatenate([-sin, zh, zeros], axis=1)
    sin_b = jnp.concatenate([zh, sin, zeros], axis=1)
    rep = LANES // HEAD_DIM
    return tuple(jnp.tile(a, (1, rep)) for a in (cos_t, sin_a, sin_b))


def _stacked_params(w_in, g_mix, sinks, conv_w, conv_b, dt_bias, a_log, d_skip, ssm_norm_w, gm_ln_g, gm_ln_b,
                    gm_w_s, gm_b_s, w_attn_o, w_ssm_o, w_gm_o, w_out, g_ff, w_ff1, w_ff2):
    w_r = _w_in_split(w_in)
    pad = jnp.zeros((DEPTH, LANES - SSM_HEADS), F32)
    expand = (jnp.arange(SSM_INNER)[None, :] // SSM_HEAD_DIM == jnp.arange(LANES)[:, None]).astype(BF16)

    def row(a):
        return a[:, None, :]

    return {
        "w_in": w_r, "g_mix": row(g_mix), "sinks": sinks,
        "conv_w": 0.5 * conv_w, "conv_b": row(0.5 * conv_b),
        "dt_bias": row(jnp.concatenate([dt_bias, pad], axis=1)), "a_log": row(jnp.concatenate([a_log, pad], axis=1)),
        "d_skip": row(jnp.repeat(d_skip, SSM_HEAD_DIM, axis=1)), "ssm_norm_w": row(ssm_norm_w),
        "gm_ln_g": row(gm_ln_g), "gm_ln_b": row(gm_ln_b),
        "gm_w_s": gm_w_s, "gm_b_st": jnp.swapaxes(gm_b_s, 1, 2),
        "w_attn_o": (0.5 * w_attn_o).astype(BF16), "w_ssm_o": (0.5 * w_ssm_o).astype(BF16),
        "w_gm_o": (0.5 * w_gm_o).astype(BF16), "w_out": w_out.astype(BF16), "expand": expand,
        "g_ff": row(g_ff), "w_ff1": w_ff1.astype(BF16), "w_ff2": w_ff2.astype(BF16),
    }


def kernel(x_prompt, x_sample, c_prompt, c_sample, cache_attn_k, cache_attn_v, state_ssm, state_conv, w_ada, b_ada, g_mix, w_in, sinks, conv_w, conv_b, dt_bias, a_log, d_skip, ssm_norm_w, gm_ln_g, gm_ln_b, gm_w_s, gm_b_s, w_attn_o, w_ssm_o, w_gm_o, w_out, g_ff, w_ff1, w_ff2, g_final):
    bp, sp, d = x_prompt.shape
    bs, ss, _ = x_sample.shape
    mod_p, mod_s = _ada_call(c_prompt, c_sample, w_ada, b_ada)
    rope_p = _rope_tables(jnp.arange(sp))
    rope_s = tuple(jnp.tile(a, (SAMPLE_NB, 1)) for a in _rope_tables(PAST_LEN + jnp.arange(ss)))
    g_fin = g_final[None]
    p = _stacked_params(w_in, g_mix, sinks, conv_w, conv_b, dt_bias, a_log, d_skip, ssm_norm_w, gm_ln_g,
                        gm_ln_b, gm_w_s, gm_b_s, w_attn_o, w_ssm_o, w_gm_o, w_out, g_ff, w_ff1, w_ff2)
    ck = cache_attn_k.reshape(DEPTH, bs, WINDOW, KV_W)
    cv = cache_attn_v.reshape(DEPTH, bs, WINDOW, KV_W)
    h0 = state_ssm.reshape(DEPTH, bs, SSM_INNER, SSM_STATE)
    xp, xs = x_prompt, x_sample
    state_p = state_s = None
    for l in range(DEPTH):
        final = l == DEPTH - 1
        xp, *state_p = _prompt_mixer(l, xp, mod_p, p, rope_p, state_p)
        xs, *state_s = _sample_mixer(l, xs, mod_s, ck, cv, h0, state_conv, p, rope_s, state_s)
        xp, xs = _ffn(l, xp, mod_p, xs, mod_s, p, g_fin, final)
    kp, vp, ssm_p, conv_p = state_p
    ks, vs, ssm_s, conv_s, gv_s = state_s
    return (xp, xs,
            kp.reshape(DEPTH, bp, WINDOW, N_KV_HEADS, HEAD_DIM), vp.reshape(DEPTH, bp, WINDOW, N_KV_HEADS, HEAD_DIM),
            ssm_p.reshape(DEPTH, bp, SSM_HEADS, SSM_HEAD_DIM, SSM_STATE), conv_p,
            ks.reshape(DEPTH, bs, ss, N_KV_HEADS, HEAD_DIM), vs.reshape(DEPTH, bs, ss, N_KV_HEADS, HEAD_DIM),
            ssm_s.reshape(DEPTH, bs, SSM_HEADS, SSM_HEAD_DIM, SSM_STATE), conv_s, gv_s)
```

```python
import functools

import jax
import jax.numpy as jnp
from jax import lax
from jax.experimental import pallas as pl
from jax.experimental.pallas import tpu as pltpu

F32 = jnp.float32
BF16 = jnp.bfloat16

D_MODEL = 1024
DEPTH = 2
CHUNK = 64
N_HEADS = 8
N_KV_HEADS = 2
HEAD_DIM = 64
GQA_GROUP = N_HEADS // N_KV_HEADS
ROT_DIM = HEAD_DIM // 4
ROPE_THETA = 500000.0
WINDOW = 128
SSM_HEADS = 16
SSM_HEAD_DIM = 64
SSM_INNER = SSM_HEADS * SSM_HEAD_DIM
SSM_GROUPS = 2
SSM_STATE = 128
SSM_CHUNK = 64
CONV_WIDTH = 4
CONV_DIM = SSM_INNER + 2 * SSM_GROUPS * SSM_STATE
GM_WIDTH = 512
GM_GROUPS = 4
GM_GROUP_DIM = GM_WIDTH // GM_GROUPS
GM_CHUNK = 128
D_FF = 4 * D_MODEL
Q_W = N_HEADS * HEAD_DIM
KV_W = N_KV_HEADS * HEAD_DIM
PAST_LEN = 4096
EPS = 1e-6

LANES = 128
MXU_TILE = 256
HEADS_PER_GROUP = SSM_HEADS // SSM_GROUPS
GROUP_INNER = HEADS_PER_GROUP * SSM_HEAD_DIM

C_Q = (0, 0, Q_W)
C_K = (0, C_Q[2], C_Q[2] + KV_W)
C_V = (0, C_K[2], C_K[2] + KV_W)
C_Z = (0, C_V[2], C_V[2] + SSM_INNER)
C_XBC = (0, C_Z[2], C_Z[2] + CONV_DIM)
N_IN_A = C_XBC[2]
C_GU = (1, 0, GM_WIDTH)
C_GV = (1, C_GU[2], C_GU[2] + GM_WIDTH)
C_G0 = (1, C_GV[2], C_GV[2] + D_MODEL)
C_G1 = (1, C_G0[2], C_G0[2] + D_MODEL)
C_G2 = (1, C_G1[2], C_G1[2] + D_MODEL)
N_IN_B = C_G2[2]
C_DT = (2, 0, LANES)

PROMPT_TS = 512
PROJ_BLOCK = 256
SAMPLE_NB = 8
FFN_ROWS = 512
W_SPLIT_BLOCKS = 2
VMEM_LIMIT = 56 * 1024 * 1024


def _log2(n):
    assert n & (n - 1) == 0, n
    return n.bit_length() - 1


def _dot(a, b):
    return jnp.dot(a, b, preferred_element_type=F32)


def _dot_nt(a, b):
    return lax.dot_general(a, b, (((1,), (1,)), ((), ())), preferred_element_type=F32)


def _dot_tn(a, b):
    return lax.dot_general(a, b, (((0,), (0,)), ((), ())), preferred_element_type=F32)


def _split3(x):
    hi = x.astype(BF16)
    r = x - hi.astype(F32)
    mid = r.astype(BF16)
    lo = (r - mid.astype(F32)).astype(BF16)
    return hi, mid, lo


def _sigmoid(x):
    return 0.5 * (1.0 + jnp.tanh(0.5 * x))


def _silu(x):
    return x * _sigmoid(x)


def _silu_of_half(xh):
    return xh * (1.0 + jnp.tanh(xh))


def _twice_sigmoid_of_half(xh):
    return 1.0 + jnp.tanh(xh)


def _gelu(x):
    return 0.5 * x * (1.0 + jnp.tanh(0.7978845608028654 * (x + 0.044715 * (x * x * x))))


def _softplus(x):
    return jnp.maximum(x, 0.0) + jnp.log1p(jnp.exp(-jnp.abs(x)))


def _rms(x, g):
    return x * lax.rsqrt(jnp.mean(x * x, axis=-1, keepdims=True) + EPS) * g


def _rope(x, cos, sin_a, sin_b):
    return x * cos + pltpu.roll(x, LANES - ROT_DIM // 2, 1) * sin_a + pltpu.roll(x, ROT_DIM // 2, 1) * sin_b


def _proj(h, win_refs, cols):
    return _dot(h, win_refs[cols[0]][:, cols[1]:cols[2]])


def _modnorm(x3, g, mod3):
    nb, t, d = x3.shape
    h3 = _rms(x3, g) * (1.0 + mod3[:, 1:2, :]) + mod3[:, 0:1, :]
    return h3.reshape(nb * t, d).astype(BF16)


def _sink_row(sinks_ref, kv, cols_per_head):
    c = lax.broadcasted_iota(jnp.int32, (1, GQA_GROUP * cols_per_head), 1)
    row = jnp.full((1, GQA_GROUP * cols_per_head), sinks_ref[kv * GQA_GROUP], F32)
    for i in range(1, GQA_GROUP):
        row = jnp.where(c >= i * cols_per_head, sinks_ref[kv * GQA_GROUP + i], row)
    return row


def _stack_heads(q, kv, rows):
    return jnp.concatenate(
        [q[:, (kv * GQA_GROUP + i) * HEAD_DIM:(kv * GQA_GROUP + i + 1) * HEAD_DIM] for i in range(GQA_GROUP)], axis=0)


def _unstack_heads(o, rows):
    return jnp.concatenate([o[i * rows:(i + 1) * rows, :] for i in range(GQA_GROUP)], axis=1)


def _gmlp_chunk(vn, u, gws_ref, gbst_ref, L):
    ri = lax.broadcasted_iota(jnp.int32, (L, L), 0)
    ci = lax.broadcasted_iota(jnp.int32, (L, L), 1)
    outs = []
    for g in range(GM_GROUPS):
        w = jnp.where(ri >= ci, gws_ref[g, 0:L, 0:L], 0.0).astype(BF16)
        v_g = vn[:, g * GM_GROUP_DIM:(g + 1) * GM_GROUP_DIM].astype(BF16)
        outs.append(_dot(w, v_g) + gbst_ref[0:L, g:g + 1])
    return u * jnp.concatenate(outs, axis=1)


def _layer_norm(x, g, b):
    xc = x - jnp.mean(x, axis=-1, keepdims=True)
    return xc * lax.rsqrt(jnp.mean(xc * xc, axis=-1, keepdims=True) + EPS) * g + b


def _merge_out(x3, mod3, gates, a, b, c, wout_ref):
    merged = gates[0] * a + gates[1] * b + gates[2] * c
    o = _dot(merged.astype(BF16), wout_ref[...])
    nb, t, d = x3.shape
    return x3 + mod3[:, 2:3, :] * o.reshape(nb, t, d)


def _attn_groups(ts):
    return [(c, kv) for c in range(ts // CHUNK) for kv in range(N_KV_HEADS)]


def _attn_scores(q, khist, pos0):
    ts = q.shape[0]
    n_keys = WINDOW + CHUNK
    key_i = lax.broadcasted_iota(jnp.int32, (n_keys, GQA_GROUP * CHUNK), 0)
    scores = []
    for c, kv in _attn_groups(ts):
        k_g = khist[c * CHUNK:c * CHUNK + n_keys, kv * HEAD_DIM:(kv + 1) * HEAD_DIM].astype(BF16)
        sc = _dot_nt(k_g, _stack_heads(q[c * CHUNK:(c + 1) * CHUNK, :], kv, CHUNK))
        if c * CHUNK < WINDOW:
            sc = jnp.where(key_i >= WINDOW - c * CHUNK - pos0, sc, -jnp.inf)
        scores.append(sc)
    return scores


def _attn_probs(scores, sinks_ref, ts):
    sink_rows = [_sink_row(sinks_ref, kv, CHUNK) for kv in range(N_KV_HEADS)]
    probs = []
    for (c, kv), sc in zip(_attn_groups(ts), scores):
        m = jnp.maximum(jnp.max(sc, axis=0, keepdims=True), sink_rows[kv])
        p = jnp.exp(sc - m)
        denom = jnp.sum(p, axis=0, keepdims=True) + jnp.exp(sink_rows[kv] - m)
        probs.append((p * (1.0 / denom)).astype(BF16))
    return probs


def _attn_values(probs, vhist, ts):
    n_keys = WINDOW + CHUNK
    outs = []
    for (c, kv), p in zip(_attn_groups(ts), probs):
        v_g = vhist[c * CHUNK:c * CHUNK + n_keys, kv * HEAD_DIM:(kv + 1) * HEAD_DIM].astype(BF16)
        outs.append(_unstack_heads(_dot_tn(p, v_g), CHUNK))
    rows = [jnp.concatenate(outs[c * N_KV_HEADS:(c + 1) * N_KV_HEADS], axis=1) for c in range(ts // CHUNK)]
    return jnp.concatenate(rows, axis=0).astype(BF16)


def _ssd_cumsum(dt, a_row, e):
    ts = dt.shape[0]
    L = SSM_CHUNK
    span = min(ts, MXU_TILE)
    ri = lax.broadcasted_iota(jnp.int32, (span, span), 0)
    lag = ri - lax.broadcasted_iota(jnp.int32, (span, span), 1)
    tri = jnp.where((lag >= 0) & (lag <= (ri & (L - 1))), 1.0, 0.0).astype(BF16)
    hi, mid, lo = _split3(dt * a_row)
    cum = jnp.concatenate(
        [_dot(tri, hi[r:r + span]) + _dot(tri, mid[r:r + span]) + _dot(tri, lo[r:r + span])
         for r in range(0, ts, span)], axis=0)
    hi = cum.astype(BF16)
    mid = (cum - hi.astype(F32)).astype(BF16)
    cum_e = _dot(hi, e) + _dot(mid, e)
    dt_e = _dot(dt.astype(BF16), e)
    return cum_e, dt_e


def _ssd_decays(xact, cum_e, dt_e):
    ts = xact.shape[0]
    L = SSM_CHUNK
    nc = ts // L
    xs = xact[:, 0:SSM_INNER]
    cum_e = cum_e.reshape(nc, L, SSM_INNER)
    t3 = lax.broadcasted_iota(jnp.int32, cum_e.shape, 1)
    slot3 = lax.broadcasted_iota(jnp.int32, cum_e.shape, 2) & (SSM_HEAD_DIM - 1)
    cum_src = jnp.sum(jnp.where(t3 == slot3, cum_e, 0.0), axis=1, keepdims=True)
    decay = jnp.exp(jnp.where(t3 >= slot3, cum_e - cum_src, -jnp.inf))
    last = cum_e[:, L - 1:L, :]
    exp_cum = jnp.exp(cum_e)
    chunk_decay = jnp.exp(last)
    xdt = xs * dt_e
    wx = (xdt.reshape(nc, L, SSM_INNER) * jnp.exp(last - cum_e)).astype(BF16)
    first_head = (lax.broadcasted_iota(jnp.int32, xdt.shape, 1) & (2 * SSM_HEAD_DIM - 1)) < SSM_HEAD_DIM
    return {
        "xs": xs,
        "bm": xact[:, SSM_INNER:SSM_INNER + SSM_GROUPS * SSM_STATE].astype(BF16),
        "cm": xact[:, SSM_INNER + SSM_GROUPS * SSM_STATE:CONV_DIM].astype(BF16),
        "decay": decay, "exp_cum": exp_cum, "chunk_decay": chunk_decay, "wx": wx,
        "xdt_a": jnp.where(first_head, xdt, 0.0).astype(BF16),
        "xdt_b": jnp.where(first_head, 0.0, xdt).astype(BF16),
    }


def _ssd_block(ctx, st, dskip_e, fillers):
    L = SSM_CHUNK
    nc = ctx["xs"].shape[0] // L
    bm, cm = ctx["bm"], ctx["cm"]
    fillers = iter(fillers)

    def grp(a, c, g, width):
        return a[c * L:(c + 1) * L, g * width:(g + 1) * width]

    chunk_state = [jnp.concatenate([_dot_tn(grp(bm, c, g, SSM_STATE), ctx["wx"][c][:, g * GROUP_INNER:(g + 1) * GROUP_INNER])
                                    for g in range(SSM_GROUPS)], axis=1) for c in range(nc)]
    next(fillers)()
    cb = [[_dot_nt(grp(cm, c, g, SSM_STATE), jnp.concatenate([grp(bm, c, g, SSM_STATE)] * 2, axis=0))
           for g in range(SSM_GROUPS)] for c in range(nc)]
    next(fillers)()
    st_in = []
    for c in range(nc):
        st_in.append(st.astype(BF16))
        st = st * ctx["chunk_decay"][c] + chunk_state[c]
    y_intra = []
    for c in range(nc):
        rows = slice(c * L, (c + 1) * L)
        parts = []
        for k in range(SSM_HEADS // 2):
            ls = slice(k * 2 * SSM_HEAD_DIM, (k + 1) * 2 * SSM_HEAD_DIM)
            m_pair = (cb[c][2 * k // HEADS_PER_GROUP] * ctx["decay"][c][:, ls]).astype(BF16)
            block_diag = jnp.concatenate([ctx["xdt_a"][rows, ls], ctx["xdt_b"][rows, ls]], axis=0)
            parts.append(_dot(m_pair, block_diag))
        y_intra.append(jnp.concatenate(parts, axis=1))
    next(fillers)()
    ys = []
    for c in range(nc):
        y_inter = jnp.concatenate(
            [_dot(grp(cm, c, g, SSM_STATE), st_in[c][:, g * GROUP_INNER:(g + 1) * GROUP_INNER])
             for g in range(SSM_GROUPS)], axis=1)
        ys.append(y_intra[c] + y_inter * ctx["exp_cum"][c] + dskip_e * ctx["xs"][c * L:(c + 1) * L])
    next(fillers)()
    return jnp.concatenate(ys, axis=0), st


class _Stagger:
    def __init__(self):
        self._pending = None

    def push(self, produce, consume):
        val = produce()
        self.flush()
        self._pending = (consume, val)

    def flush(self):
        if self._pending is not None:
            consume, val = self._pending
            self._pending = None
            consume(val)


def _prompt_kernel(x_ref, mod_ref, gmix_ref, wa_ref, wb_ref, wdt_ref, cos_ref, sa_ref, sb_ref, sinks_ref,
                   convw_ref, convb_ref, dtb_ref, alog_ref, dskip_ref, ssmnw_ref, glng_ref, glnb_ref,
                   gws_ref, gbst_ref, wao_ref, wso_ref, wgo_ref, wout_ref, e_ref,
                   xo_ref, ko_ref, vo_ref, sto_ref, cvo_ref,
                   khist, vhist, xp, st_s):
    ts = PROMPT_TS
    s = pl.program_id(1)
    last = pl.num_programs(1) - 1

    @pl.when(s == 0)
    def _():
        khist[0:WINDOW, :] = jnp.zeros((WINDOW, KV_W), F32)
        vhist[0:WINDOW, :] = jnp.zeros((WINDOW, KV_W), F32)
        xp[...] = jnp.zeros_like(xp)
        st_s[...] = jnp.zeros_like(st_s)

    x3 = x_ref[...]
    mod3 = mod_ref[...]
    half_rows = ts // 2
    h_top = _modnorm(x3[:, 0:half_rows], gmix_ref[...], mod3)
    q0_top = _dot(h_top, wa_ref[:, C_Q[1]:C_Q[1] + PROJ_BLOCK])
    h_bot = _modnorm(x3[:, half_rows:ts], gmix_ref[...], mod3)
    q0 = jnp.concatenate([q0_top, _dot(h_bot, wa_ref[:, C_Q[1]:C_Q[1] + PROJ_BLOCK])], axis=0)
    h = jnp.concatenate([h_top, h_bot], axis=0)

    cos, sa, sb = cos_ref[...], sa_ref[...], sb_ref[...]
    res = {}

    def rope_block(raw):
        return [_rope(raw[:, i * LANES:(i + 1) * LANES], cos, sa, sb) for i in range(raw.shape[1] // LANES)]

    def q_block(raw):
        res.setdefault("q", []).extend(p.astype(BF16) for p in rope_block(raw * (HEAD_DIM ** -0.5)))

    def kv_block(raw):
        khist[WINDOW:WINDOW + ts, :] = rope_block(raw[:, 0:KV_W])[0]
        vhist[WINDOW:WINDOW + ts, :] = raw[:, KV_W:2 * KV_W]

    def conv_block(lo, raw):
        cs = slice(lo, lo + PROJ_BLOCK)
        ext = jnp.concatenate([xp[:, cs], raw], axis=0)
        acc = convb_ref[:, cs] + raw * convw_ref[CONV_WIDTH - 1:CONV_WIDTH, cs]
        for j in range(1, CONV_WIDTH):
            acc = acc + pltpu.roll(ext, j, 0)[8:8 + ts] * convw_ref[CONV_WIDTH - 1 - j:CONV_WIDTH - j, cs]
        xp[:, cs] = raw[ts - 8:ts]
        res.setdefault("xact", []).append(_silu_of_half(acc))

    def collect(key, fn):
        return lambda raw: res.setdefault(key, []).append(fn(raw))

    wrefs = (wa_ref, wb_ref, wdt_ref)
    pipe = _Stagger()

    def push_proj(cols, off, width, consume):
        ref, lo = wrefs[cols[0]], cols[1] + off
        pipe.push(lambda: _dot(h, ref[:, lo:lo + width]), consume)

    def blocks(cols, consume):
        return [functools.partial(push_proj, cols, off, PROJ_BLOCK, consume)
                for off in range(0, cols[2] - cols[1], PROJ_BLOCK)]

    conv_blocks = [functools.partial(push_proj, C_XBC, off, PROJ_BLOCK, functools.partial(conv_block, off))
                   for off in range(0, CONV_DIM, PROJ_BLOCK)]
    gate_blocks = iter(blocks(C_G0, collect("g0", _twice_sigmoid_of_half))
                       + blocks(C_G1, collect("g1", _twice_sigmoid_of_half))
                       + blocks(C_G2, collect("g2", _twice_sigmoid_of_half)))
    z_blocks = blocks(C_Z, collect("sz", _silu_of_half))
    gm_blocks = blocks(C_GU, collect("u", _gelu)) + blocks(C_GV, collect("gv", _gelu))

    def cat(key):
        return jnp.concatenate(res[key], axis=1)

    pipe.push(lambda: q0, q_block)
    for blk in blocks(C_Q, q_block)[1:]:
        blk()
    push_proj(C_K, 0, 2 * KV_W, kv_block)
    conv_blocks[0]()
    next(gate_blocks)()
    conv_blocks[1]()
    scores = _attn_scores(cat("q"), khist, s * ts)
    conv_blocks[2]()
    probs = _attn_probs(scores, sinks_ref, ts)
    next(gate_blocks)()
    conv_blocks[3]()
    attn = _attn_values(probs, vhist, ts)
    conv_blocks[4]()
    next(gate_blocks)()
    conv_blocks[5]()
    push_proj(C_DT, 0, LANES, collect("dt", lambda raw: _softplus(raw + dtb_ref[...])))
    next(gate_blocks)()
    a_out = _dot(attn, wao_ref[...])
    z_blocks[0]()
    cum_e, dt_e = _ssd_cumsum(res["dt"][0], -jnp.exp(alog_ref[...]), e_ref[...])
    z_blocks[1]()
    merged = cat("g0") * a_out
    ctx = _ssd_decays(cat("xact"), cum_e, dt_e)
    z_blocks[2]()
    z_blocks[3]()
    y, st_s[...] = _ssd_block(ctx, st_s[...], dskip_ref[...], gm_blocks)
    next(gate_blocks)()
    vn = _layer_norm(cat("gv"), glng_ref[...], glnb_ref[...])
    u = cat("u")
    next(gate_blocks)()
    ssm = _rms(y * cat("sz"), ssmnw_ref[...]).astype(BF16)
    gm = jnp.concatenate(
        [_gmlp_chunk(vn[c * GM_CHUNK:(c + 1) * GM_CHUNK], u[c * GM_CHUNK:(c + 1) * GM_CHUNK], gws_ref, gbst_ref,
                     GM_CHUNK) for c in range(ts // GM_CHUNK)], axis=0).astype(BF16)
    next(gate_blocks)()
    b_out = _dot(ssm, wso_ref[...])
    next(gate_blocks)()
    pipe.flush()
    merged = merged + cat("g1") * b_out
    c_out = _dot(gm, wgo_ref[...])
    for blk in gate_blocks:
        blk()
    pipe.flush()
    merged = merged + cat("g2") * c_out
    o = _dot(merged.astype(BF16), wout_ref[...])
    xo_ref[...] = x3 + mod3[:, 2:3, :] * o.reshape(x3.shape)

    khist[0:WINDOW, :] = khist[ts:ts + WINDOW, :]
    vhist[0:WINDOW, :] = vhist[ts:ts + WINDOW, :]

    @pl.when(s == last)
    def _():
        ko_ref[0] = khist[ts:ts + WINDOW, :]
        vo_ref[0] = vhist[ts:ts + WINDOW, :]
        cvo_ref[0] = xp[8 - (CONV_WIDTH - 1):8, :]
        sto_ref[0] = st_s[...].T


def _sample_attention(q, k_new, v_new, ck_ref, cv_ref, sinks_ref, nb):
    t = q.shape[0] // nb
    groups = [(i, kv) for i in range(nb) for kv in range(N_KV_HEADS)]

    def new_rows(a, i, kv):
        return a[i * t:(i + 1) * t, kv * HEAD_DIM:(kv + 1) * HEAD_DIM].astype(BF16)

    def cached(ref, i, kv):
        return ref[i, kv].T.astype(BF16)

    qg = [_stack_heads(q[i * t:(i + 1) * t], kv, t) for i, kv in groups]
    s_c = [_dot_nt(cached(ck_ref, i, kv), qg[n]) for n, (i, kv) in enumerate(groups)]
    s_n = [_dot_nt(new_rows(k_new, i, kv), qg[n]) for n, (i, kv) in enumerate(groups)]
    sinks = [_sink_row(sinks_ref, kv, t) for kv in range(N_KV_HEADS)]
    probs = []
    for n, (i, kv) in enumerate(groups):
        m = jnp.maximum(jnp.maximum(jnp.max(s_c[n], axis=0, keepdims=True), jnp.max(s_n[n], axis=0, keepdims=True)),
                        sinks[kv])
        p_c = jnp.exp(s_c[n] - m)
        p_n = jnp.exp(s_n[n] - m)
        inv = 1.0 / (jnp.sum(p_c, axis=0, keepdims=True) + jnp.sum(p_n, axis=0, keepdims=True)
                     + jnp.exp(sinks[kv] - m))
        probs.append(((p_c * inv).astype(BF16), (p_n * inv).astype(BF16)))
    outs = [_dot_tn(probs[n][0], cached(cv_ref, i, kv)) + _dot_tn(probs[n][1], new_rows(v_new, i, kv))
            for n, (i, kv) in enumerate(groups)]
    rows = [jnp.concatenate([_unstack_heads(outs[i * N_KV_HEADS + kv], t) for kv in range(N_KV_HEADS)], axis=1)
            for i in range(nb)]
    return jnp.concatenate(rows, axis=0)


def _sample_ssd_setup(xact, dt, a_row, e, L):
    m = xact.shape[0]
    n_seq = m // L
    xs = xact[:, 0:SSM_INNER]
    ri = lax.broadcasted_iota(jnp.int32, (m, m), 0)
    lag = ri - lax.broadcasted_iota(jnp.int32, (m, m), 1)
    tri = jnp.where((lag >= 0) & (lag <= (ri & (L - 1))), 1.0, 0.0).astype(BF16)
    hi, mid, lo = _split3(dt * a_row)
    cum = _dot(tri, hi) + _dot(tri, mid) + _dot(tri, lo)
    hi, mid, lo = _split3(cum)
    cum_e = (_dot(hi, e) + _dot(mid, e) + _dot(lo, e)).reshape(n_seq, L, SSM_INNER)
    head_of_lane = jnp.right_shift(lax.broadcasted_iota(jnp.int32, (LANES, SSM_HEADS * L), 1), _log2(L))
    e_slots = jnp.where(head_of_lane == lax.broadcasted_iota(jnp.int32, (LANES, SSM_HEADS * L), 0), 1.0, 0.0)
    e_slots = e_slots.astype(BF16)
    cum_s = (_dot(hi, e_slots) + _dot(mid, e_slots) + _dot(lo, e_slots)).reshape(n_seq, L, SSM_HEADS * L)
    dt_hi = dt.astype(BF16)
    dt_lo = (dt - dt_hi.astype(F32)).astype(BF16)
    dt_e = _dot(dt_hi, e) + _dot(dt_lo, e)
    t3 = lax.broadcasted_iota(jnp.int32, cum_s.shape, 1)
    slot3 = lax.broadcasted_iota(jnp.int32, cum_s.shape, 2) & (L - 1)
    cum_src = jnp.sum(jnp.where(t3 == slot3, cum_s, 0.0), axis=1, keepdims=True)
    decay = jnp.exp(jnp.where(t3 >= slot3, cum_s - cum_src, -jnp.inf))
    last = cum_e[:, L - 1:L, :]
    xdt = xs * dt_e
    wx = (xdt.reshape(n_seq, L, SSM_INNER) * jnp.exp(last - cum_e)).astype(BF16)
    per_head = jnp.exp(cum.reshape(n_seq, L, LANES)[:, L - 1:L, :])
    rep = jnp.broadcast_to(per_head, (n_seq, SSM_HEADS, LANES)).reshape(n_seq * SSM_HEADS, LANES)
    own_lane = (lax.broadcasted_iota(jnp.int32, rep.shape, 1)
                == (lax.broadcasted_iota(jnp.int32, rep.shape, 0) & (SSM_HEADS - 1)))
    hi, mid, lo = _split3(jnp.where(own_lane, rep, 0.0))
    ones = jnp.ones((LANES, LANES), BF16)
    splat = (_dot(hi, ones) + _dot(mid, ones) + _dot(lo, ones)).reshape(n_seq, SSM_HEADS, LANES)
    return {
        "L": L, "xs": xs,
        "bm": xact[:, SSM_INNER:SSM_INNER + SSM_GROUPS * SSM_STATE].astype(BF16),
        "cm": xact[:, SSM_INNER + SSM_GROUPS * SSM_STATE:CONV_DIM].astype(BF16),
        "decay": decay, "exp_cum": jnp.exp(cum_e), "wx": wx, "xdt": xdt.astype(BF16), "state_decay": splat,
    }


def _sample_ssd_block(s, h0_ref, sto_ref, dskip_e, nb):
    L = s["L"]
    pairs = [(i, g) for i in range(nb) for g in range(SSM_GROUPS)]
    ri = jnp.right_shift(lax.broadcasted_iota(jnp.int32, (HEADS_PER_GROUP * L, GROUP_INNER), 0), _log2(L))
    li = jnp.right_shift(lax.broadcasted_iota(jnp.int32, (HEADS_PER_GROUP * L, GROUP_INNER), 1), _log2(SSM_HEAD_DIM))

    def rows(a, i, g, width):
        return a[i * L:(i + 1) * L, g * width:(g + 1) * width]

    cb = [_dot_nt(rows(s["cm"], i, g, SSM_STATE), jnp.concatenate([rows(s["bm"], i, g, SSM_STATE)] * HEADS_PER_GROUP,
                                                                  axis=0)) for i, g in pairs]
    upd = [_dot_tn(s["wx"][i][:, g * GROUP_INNER:(g + 1) * GROUP_INNER], rows(s["bm"], i, g, SSM_STATE))
           for i, g in pairs]
    y_inter = [_dot_nt(rows(s["cm"], i, g, SSM_STATE), h0_ref[i][g * GROUP_INNER:(g + 1) * GROUP_INNER, :].astype(BF16))
               for i, g in pairs]
    y_intra = []
    for n, (i, g) in enumerate(pairs):
        gs = slice(g * HEADS_PER_GROUP * L, (g + 1) * HEADS_PER_GROUP * L)
        m_g = (cb[n] * s["decay"][i][:, gs]).astype(BF16)
        tiled = jnp.concatenate([rows(s["xdt"], i, g, GROUP_INNER)] * HEADS_PER_GROUP, axis=0)
        y_intra.append(_dot(m_g, jnp.where(ri == li, tiled, jnp.zeros_like(tiled))))
    ys = []
    for i in range(nb):
        parts = [y_intra[i * SSM_GROUPS + g] + y_inter[i * SSM_GROUPS + g]
                 * s["exp_cum"][i][:, g * GROUP_INNER:(g + 1) * GROUP_INNER] for g in range(SSM_GROUPS)]
        ys.append(jnp.concatenate(parts, axis=1) + dskip_e * s["xs"][i * L:(i + 1) * L])
        decayed = h0_ref[i].reshape(SSM_HEADS, SSM_HEAD_DIM, SSM_STATE) * s["state_decay"][i][:, None, :]
        sto_ref[i] = decayed.reshape(SSM_INNER, SSM_STATE) + jnp.concatenate(
            upd[i * SSM_GROUPS:(i + 1) * SSM_GROUPS], axis=0)
    return jnp.concatenate(ys, axis=0)


def _sample_kernel(x_ref, mod_ref, gmix_ref, wa_ref, wb_ref, wdt_ref, cos_ref, sa_ref, sb_ref, sinks_ref,
                   ck_ref, cv_ref, h0_ref, cs_ref,
                   convw_ref, convb_ref, dtb_ref, alog_ref, dskip_ref, ssmnw_ref, glng_ref, glnb_ref,
                   gws_ref, gbst_ref, wao_ref, wso_ref, wgo_ref, wout_ref, e_ref,
                   xo_ref, ko_ref, vo_ref, sto_ref, cvo_ref, gvo_ref,
                   xp):
    nb = SAMPLE_NB
    t = x_ref.shape[1]
    m = nb * t
    win_ref = (wa_ref, wb_ref, wdt_ref)
    x3 = x_ref[...]
    mod3 = mod_ref[...]
    h = _modnorm(x3, gmix_ref[...], mod3)

    cos, sa, sb = cos_ref[...], sa_ref[...], sb_ref[...]
    q = _proj(h, win_ref, C_Q) * (HEAD_DIM ** -0.5)
    q = jnp.concatenate(
        [_rope(q[:, i * LANES:(i + 1) * LANES], cos, sa, sb) for i in range(Q_W // LANES)], axis=1).astype(BF16)
    k_new = _rope(_proj(h, win_ref, C_K), cos, sa, sb)
    v_new = _proj(h, win_ref, C_V)
    ko_ref[...] = k_new.reshape(nb, t, KV_W)
    vo_ref[...] = v_new.reshape(nb, t, KV_W)

    xp[:, 8 - (CONV_WIDTH - 1):8, :] = cs_ref[...]
    xp[:, 8:8 + t, :] = _proj(h, win_ref, C_XBC).reshape(nb, t, CONV_DIM)
    acc = convb_ref[...] + xp[:, 8:8 + t, :] * convw_ref[CONV_WIDTH - 1:CONV_WIDTH, :]
    for j in range(1, CONV_WIDTH):
        acc = acc + xp[:, 8 - j:8 - j + t, :] * convw_ref[CONV_WIDTH - 1 - j:CONV_WIDTH - j, :]
    xact = _silu_of_half(acc).reshape(m, CONV_DIM)
    cvo_ref[...] = xp[:, t + 8 - (CONV_WIDTH - 1):t + 8, :]
    dt = _softplus(_proj(h, win_ref, C_DT) + dtb_ref[...])
    dskip_e = dskip_ref[...]

    u = _gelu(_proj(h, win_ref, C_GU))
    vn = _layer_norm(_gelu(_proj(h, win_ref, C_GV)), glng_ref[...], glnb_ref[...])
    gvo_ref[...] = vn.reshape(nb, t, GM_WIDTH)

    ssd = _sample_ssd_setup(xact, dt, -jnp.exp(alog_ref[...]), e_ref[...], t)
    attn = _sample_attention(q, k_new, v_new, ck_ref, cv_ref, sinks_ref, nb)
    y = _sample_ssd_block(ssd, h0_ref, sto_ref, dskip_e, nb)
    gm_rows = [_gmlp_chunk(vn[i * t:(i + 1) * t], u[i * t:(i + 1) * t], gws_ref, gbst_ref, t) for i in range(nb)]

    a_out = _dot(attn.astype(BF16), wao_ref[...])
    z = _proj(h, win_ref, C_Z)
    ssm = _rms(y * _silu_of_half(z), ssmnw_ref[...]).astype(BF16)
    b_out = _dot(ssm, wso_ref[...])
    c_out = _dot(jnp.concatenate(gm_rows, axis=0).astype(BF16), wgo_ref[...])
    gates = [_twice_sigmoid_of_half(_proj(h, win_ref, cols)) for cols in (C_G0, C_G1, C_G2)]
    xo_ref[...] = _merge_out(x3, mod3, gates, a_out, b_out, c_out, wout_ref)


def _ffn_block(x_ref, mod_ref, gff_ref, w1_ref, w2_ref, gfin_ref, o_ref, final_norm):
    x3 = x_ref[...]
    mod3 = mod_ref[...]
    nb, t, d = x3.shape
    h3 = _rms(x3, gff_ref[...]) * (1.0 + mod3[:, 4:5, :]) + mod3[:, 3:4, :]
    h = h3.reshape(nb * t, d).astype(BF16)
    a = jnp.maximum(_dot(h, w1_ref[...]), 0.0)
    y = _dot((a * a).astype(BF16), w2_ref[...])
    out = x3 + mod3[:, 5:6, :] * y.reshape(nb, t, d)
    if final_norm:
        out = _rms(out, gfin_ref[...])
    o_ref[...] = out


def _ffn_kernel(xp_ref, modp_ref, xs_ref, mods_ref, gff_ref, w1_ref, w2_ref, gfin_ref, op_ref, os_ref, *,
                final_norm, n_prompt_steps):
    i = pl.program_id(0)

    @pl.when(i < n_prompt_steps)
    def _():
        _ffn_block(xp_ref, modp_ref, gff_ref, w1_ref, w2_ref, gfin_ref, op_ref, final_norm)

    @pl.when(i == n_prompt_steps)
    def _():
        _ffn_block(xs_ref, mods_ref, gff_ref, w1_ref, w2_ref, gfin_ref, os_ref, final_norm)


def _ada_kernel(c_ref, w_ref, b_ref, op_ref, os_ref):
    mod = _dot(_silu(c_ref[...]).astype(BF16), w_ref[...].astype(BF16)) + b_ref[...]
    n_prompt = op_ref.shape[0]
    op_ref[...] = mod[0:n_prompt]
    os_ref[...] = mod[n_prompt:]


def _w_in_piece_kernel(wt_ref, o_ref, *, half_lo, half_hi, n_valid):
    cols = wt_ref.shape[0]
    col = pl.program_id(1) * cols + lax.broadcasted_iota(jnp.int32, (cols, 1), 0)
    scale = jnp.where(col < n_valid, jnp.where((col >= half_lo) & (col < half_hi), 0.5, 1.0), 0.0)
    o_ref[...] = (wt_ref[...] * scale).astype(BF16).T


def _w_in_piece(w_t, first_col, n_cols, half_cols, n_blocks, n_valid=None):
    cols = n_cols // n_blocks
    n_valid = n_cols if n_valid is None else n_valid
    return pl.pallas_call(
        functools.partial(_w_in_piece_kernel, half_lo=half_cols[0], half_hi=half_cols[1], n_valid=n_valid),
        grid=(DEPTH, n_blocks),
        in_specs=[pl.BlockSpec((pl.Squeezed(), pl.Element(cols), pl.Element(D_MODEL)),
                               lambda l, j: (l, pl.multiple_of(first_col + j * cols, SSM_HEADS), 0))],
        out_specs=pl.BlockSpec((None, D_MODEL, cols), lambda l, j: (l, 0, j)),
        out_shape=jax.ShapeDtypeStruct((DEPTH, D_MODEL, n_cols), BF16),
        compiler_params=_params(2),
        name="w_in_piece",
    )(w_t)


def _w_in_split(w_in):
    w_t = jnp.swapaxes(w_in, 1, 2)
    b_lo = N_IN_A + SSM_HEADS
    return (_w_in_piece(w_t, 0, N_IN_A, (C_Z[1], C_Z[2]), W_SPLIT_BLOCKS),
            _w_in_piece(w_t, b_lo, N_IN_B, (C_G0[1], N_IN_B), W_SPLIT_BLOCKS),
            _w_in_piece(w_t, N_IN_A, LANES, (0, 0), 1, n_valid=SSM_HEADS))


def _with_ignored_inputs(body, n_in, n_ignored):
    if n_ignored == 0:
        return body
    return lambda *refs: body(*refs[:n_in], *refs[n_in + n_ignored:])


def _const_spec(shape):
    return pl.BlockSpec(shape, lambda *_: (0,) * len(shape), pipeline_mode=pl.Buffered(1))


def _layer_spec(l, shape):
    return pl.BlockSpec((None,) + shape, lambda *_: (l,) + (0,) * len(shape), pipeline_mode=pl.Buffered(1))


def _any_spec():
    return pl.BlockSpec(memory_space=pl.ANY)


def _smem_spec():
    return pl.BlockSpec(memory_space=pltpu.SMEM)


def _params(n_grid):
    return pltpu.CompilerParams(dimension_semantics=("arbitrary",) * n_grid, vmem_limit_bytes=VMEM_LIMIT)


def _ada_call(c_prompt, c_sample, w_ada, b_ada):
    n_p, n_s = c_prompt.shape[0], c_sample.shape[0]
    c_all = jnp.concatenate([c_prompt, c_sample], axis=0)
    tn = 2 * D_MODEL
    n_tiles = w_ada.shape[2] // tn
    mod_p, mod_s = pl.pallas_call(
        _ada_kernel,
        grid=(DEPTH, n_tiles),
        in_specs=[pl.BlockSpec((n_p + n_s, D_MODEL), lambda l, n: (0, 0)),
                  pl.BlockSpec((None, D_MODEL, tn), lambda l, n: (l, 0, n)),
                  pl.BlockSpec((None, 1, tn), lambda l, n: (l, 0, n))],
        out_specs=(pl.BlockSpec((None, n_p, tn), lambda l, n: (l, 0, n)),
                   pl.BlockSpec((None, n_s, tn), lambda l, n: (l, 0, n))),
        out_shape=(jax.ShapeDtypeStruct((DEPTH, n_p, w_ada.shape[2]), F32),
                   jax.ShapeDtypeStruct((DEPTH, n_s, w_ada.shape[2]), F32)),
        compiler_params=_params(2),
        name="ada_mod",
    )(c_all, w_ada, b_ada.reshape(DEPTH, 1, -1))
    return mod_p.reshape(DEPTH, n_p, 6, D_MODEL), mod_s.reshape(DEPTH, n_s, 6, D_MODEL)


def _w_in_specs(l):
    return [_layer_spec(l, (D_MODEL, N_IN_A)), _layer_spec(l, (D_MODEL, N_IN_B)), _layer_spec(l, (D_MODEL, LANES))]


def _layer_weight_specs(l):
    return [
        _layer_spec(l, (CONV_WIDTH, CONV_DIM)), _layer_spec(l, (1, CONV_DIM)), _layer_spec(l, (1, LANES)),
        _layer_spec(l, (1, LANES)), _layer_spec(l, (1, SSM_INNER)), _layer_spec(l, (1, SSM_INNER)),
        _layer_spec(l, (1, GM_WIDTH)), _layer_spec(l, (1, GM_WIDTH)),
        _layer_spec(l, (GM_GROUPS, GM_CHUNK, GM_CHUNK)), _layer_spec(l, (GM_CHUNK, GM_GROUPS)),
        _layer_spec(l, (Q_W, D_MODEL)), _layer_spec(l, (SSM_INNER, D_MODEL)), _layer_spec(l, (GM_WIDTH, D_MODEL)),
        _layer_spec(l, (D_MODEL, D_MODEL)), _const_spec((LANES, SSM_INNER)),
    ]


def _layer_weight_args(p):
    return (p["conv_w"], p["conv_b"], p["dt_bias"], p["a_log"], p["d_skip"], p["ssm_norm_w"], p["gm_ln_g"],
            p["gm_ln_b"], p["gm_w_s"], p["gm_b_st"], p["w_attn_o"], p["w_ssm_o"], p["w_gm_o"], p["w_out"], p["expand"])


def _prompt_mixer(l, x, mod, p, rope, prev):
    bsz, seq, d = x.shape
    ts = PROMPT_TS
    tab = pl.BlockSpec((ts, LANES), lambda b, s: (s, 0))
    in_specs = [
        pl.BlockSpec((1, ts, d), lambda b, s: (b, s, 0)),
        pl.BlockSpec((None, 1, 6, d), lambda b, s: (l, b, 0, 0)),
        _layer_spec(l, (1, d)), *_w_in_specs(l),
        tab, tab, tab, _smem_spec(),
    ] + _layer_weight_specs(l)
    args = (x, mod, p["g_mix"], *p["w_in"], *rope, p["sinks"][l], *_layer_weight_args(p))
    state_tails = ((WINDOW, KV_W), (WINDOW, KV_W), (SSM_INNER, SSM_STATE), (CONV_WIDTH - 1, CONV_DIM))
    out_shape = (jax.ShapeDtypeStruct((bsz, seq, d), F32),) + tuple(
        jax.ShapeDtypeStruct((DEPTH, bsz) + tail, F32) for tail in state_tails)
    out_specs = (pl.BlockSpec((1, ts, d), lambda b, s: (b, s, 0)),) + tuple(
        pl.BlockSpec((None, 1) + tail, lambda b, s: (l, b, 0, 0)) for tail in state_tails)
    scratch = [
        pltpu.VMEM((WINDOW + ts, KV_W), F32), pltpu.VMEM((WINDOW + ts, KV_W), F32),
        pltpu.VMEM((8, CONV_DIM), F32), pltpu.VMEM((SSM_STATE, SSM_INNER), F32),
    ]
    n_in = len(args)
    prev = () if prev is None else tuple(prev)
    return pl.pallas_call(
        _with_ignored_inputs(_prompt_kernel, n_in, len(prev)),
        grid=(bsz, seq // ts), in_specs=in_specs + [_any_spec()] * len(prev), out_specs=out_specs,
        out_shape=out_shape, scratch_shapes=scratch, compiler_params=_params(2), name="prompt_mixer",
        input_output_aliases={n_in + i: 1 + i for i in range(len(prev))},
    )(*args, *prev)


def _sample_mixer(l, x, mod, cache_k, cache_v, h0, conv_state, p, rope, prev):
    bsz, t, d = x.shape
    nb = SAMPLE_NB
    m = nb * t

    def blk(*tail):
        return pl.BlockSpec((nb,) + tail, lambda i: (i,) + (0,) * len(tail))

    def lblk(*tail):
        return pl.BlockSpec((None, nb) + tail, lambda i: (l, i) + (0,) * len(tail))

    in_specs = [
        blk(t, d), lblk(6, d), _layer_spec(l, (1, d)), *_w_in_specs(l),
        _const_spec((m, LANES)), _const_spec((m, LANES)), _const_spec((m, LANES)), _smem_spec(),
        lblk(N_KV_HEADS, HEAD_DIM, WINDOW), lblk(N_KV_HEADS, HEAD_DIM, WINDOW), lblk(SSM_INNER, SSM_STATE),
        lblk(CONV_WIDTH - 1, CONV_DIM),
    ] + _layer_weight_specs(l)
    args = (x, mod, p["g_mix"], *p["w_in"], *rope, p["sinks"][l], cache_k, cache_v, h0, conv_state,
            *_layer_weight_args(p))
    state_tails = ((t, KV_W), (t, KV_W), (SSM_INNER, SSM_STATE), (CONV_WIDTH - 1, CONV_DIM), (t, GM_WIDTH))
    out_shape = (jax.ShapeDtypeStruct((bsz, t, d), F32),) + tuple(
        jax.ShapeDtypeStruct((DEPTH, bsz) + tail, F32) for tail in state_tails)
    out_specs = (blk(t, d),) + tuple(lblk(*tail) for tail in state_tails)
    scratch = [pltpu.VMEM((nb, 8 + t, CONV_DIM), F32)]
    n_in = len(args)
    prev = () if prev is None else tuple(prev)
    return pl.pallas_call(
        _with_ignored_inputs(_sample_kernel, n_in, len(prev)),
        grid=(bsz // nb,), in_specs=in_specs + [_any_spec()] * len(prev), out_specs=out_specs,
        out_shape=out_shape, scratch_shapes=scratch, compiler_params=_params(1), name="sample_mixer",
        input_output_aliases={n_in + i: 1 + i for i in range(len(prev))},
    )(*args, *prev)


def _ffn(l, x_p, mod_p, x_s, mod_s, p, g_final, final_norm):
    bp, sp, d = x_p.shape
    bs, ss, _ = x_s.shape
    assert bs * ss == FFN_ROWS and sp % FFN_ROWS == 0
    blocks_per_seq = sp // FFN_ROWS
    n_prompt_steps = bp * blocks_per_seq

    def prompt_block(i):
        j = jnp.minimum(i, n_prompt_steps - 1)
        return j // blocks_per_seq, j % blocks_per_seq

    x_spec = pl.BlockSpec((1, FFN_ROWS, d), lambda i: (*prompt_block(i), 0))
    s_spec = pl.BlockSpec((bs, ss, d), lambda i: (0, 0, 0))
    return pl.pallas_call(
        functools.partial(_ffn_kernel, final_norm=final_norm, n_prompt_steps=n_prompt_steps),
        grid=(n_prompt_steps + 1,),
        in_specs=[x_spec, pl.BlockSpec((None, 1, 6, d), lambda i: (l, prompt_block(i)[0], 0, 0)),
                  s_spec, pl.BlockSpec((None, bs, 6, d), lambda i: (l, 0, 0, 0)),
                  _layer_spec(l, (1, d)), _layer_spec(l, (d, D_FF)), _layer_spec(l, (D_FF, d)), _const_spec((1, d))],
        out_specs=(x_spec, s_spec),
        out_shape=(jax.ShapeDtypeStruct(x_p.shape, F32), jax.ShapeDtypeStruct(x_s.shape, F32)),
        compiler_params=_params(1), name="ffn",
    )(x_p, mod_p, x_s, mod_s, p["g_ff"], p["w_ff1"], p["w_ff2"], g_final)


def _rope_tables(pos):
    half = ROT_DIM // 2
    inv_freq = ROPE_THETA ** (-jnp.arange(half, dtype=F32) * (2.0 / ROT_DIM))
    ang = pos.astype(F32)[:, None] * inv_freq[None, :]
    cos, sin = jnp.cos(ang), jnp.sin(ang)
    n = pos.shape[0]
    ones = jnp.ones((n, HEAD_DIM - ROT_DIM), F32)
    zeros = jnp.zeros((n, HEAD_DIM - ROT_DIM), F32)
    zh = jnp.zeros((n, half), F32)
    cos_t = jnp.concatenate([cos, cos, ones], axis=1)
    sin_a = jnp.concatenate([-sin, zh, zeros], axis=1)
    sin_b = jnp.concatenate([zh, sin, zeros], axis=1)
    rep = LANES // HEAD_DIM
    return tuple(jnp.tile(a, (1, rep)) for a in (cos_t, sin_a, sin_b))


def _stacked_params(w_in, g_mix, sinks, conv_w, conv_b, dt_bias, a_log, d_skip, ssm_norm_w, gm_ln_g, gm_ln_b,
                    gm_w_s, gm_b_s, w_attn_o, w_ssm_o, w_gm_o, w_out, g_ff, w_ff1, w_ff2):
    w_r = _w_in_split(w_in)
    pad = jnp.zeros((DEPTH, LANES - SSM_HEADS), F32)
    expand = (jnp.arange(SSM_INNER)[None, :] // SSM_HEAD_DIM == jnp.arange(LANES)[:, None]).astype(BF16)

    def row(a):
        return a[:, None, :]

    return {
        "w_in": w_r, "g_mix": row(g_mix), "sinks": sinks,
        "conv_w": 0.5 * conv_w, "conv_b": row(0.5 * conv_b),
        "dt_bias": row(jnp.concatenate([dt_bias, pad], axis=1)), "a_log": row(jnp.concatenate([a_log, pad], axis=1)),
        "d_skip": row(jnp.repeat(d_skip, SSM_HEAD_DIM, axis=1)), "ssm_norm_w": row(ssm_norm_w),
        "gm_ln_g": row(gm_ln_g), "gm_ln_b": row(gm_ln_b),
        "gm_w_s": gm_w_s, "gm_b_st": jnp.swapaxes(gm_b_s, 1, 2),
        "w_attn_o": (0.5 * w_attn_o).astype(BF16), "w_ssm_o": (0.5 * w_ssm_o).astype(BF16),
        "w_gm_o": (0.5 * w_gm_o).astype(BF16), "w_out": w_out.astype(BF16), "expand": expand,
        "g_ff": row(g_ff), "w_ff1": w_ff1.astype(BF16), "w_ff2": w_ff2.astype(BF16),
    }


def kernel(x_prompt, x_sample, c_prompt, c_sample, cache_attn_k, cache_attn_v, state_ssm, state_conv, w_ada, b_ada, g_mix, w_in, sinks, conv_w, conv_b, dt_bias, a_log, d_skip, ssm_norm_w, gm_ln_g, gm_ln_b, gm_w_s, gm_b_s, w_attn_o, w_ssm_o, w_gm_o, w_out, g_ff, w_ff1, w_ff2, g_final):
    bp, sp, d = x_prompt.shape
    bs, ss, _ = x_sample.shape
    mod_p, mod_s = _ada_call(c_prompt, c_sample, w_ada, b_ada)
    rope_p = _rope_tables(jnp.arange(sp))
    rope_s = tuple(jnp.tile(a, (SAMPLE_NB, 1)) for a in _rope_tables(PAST_LEN + jnp.arange(ss)))
    g_fin = g_final[None]
    p = _stacked_params(w_in, g_mix, sinks, conv_w, conv_b, dt_bias, a_log, d_skip, ssm_norm_w, gm_ln_g,
                        gm_ln_b, gm_w_s, gm_b_s, w_attn_o, w_ssm_o, w_gm_o, w_out, g_ff, w_ff1, w_ff2)
    ck = jnp.transpose(cache_attn_k, (0, 1, 3, 4, 2))
    cv = jnp.transpose(cache_attn_v, (0, 1, 3, 4, 2))
    h0 = state_ssm.reshape(DEPTH, bs, SSM_INNER, SSM_STATE)
    xp, xs = x_prompt, x_sample
    state_p = state_s = None
    for l in range(DEPTH):
        final = l == DEPTH - 1
        xp, *state_p = _prompt_mixer(l, xp, mod_p, p, rope_p, state_p)
        xs, *state_s = _sample_mixer(l, xs, mod_s, ck, cv, h0, state_conv, p, rope_s, state_s)
        xp, xs = _ffn(l, xp, mod_p, xs, mod_s, p, g_fin, final)
    kp, vp, ssm_p, conv_p = state_p
    ks, vs, ssm_s, conv_s, gv_s = state_s
    return (xp, xs,
            kp.reshape(DEPTH, bp, WINDOW, N_KV_HEADS, HEAD_DIM), vp.reshape(DEPTH, bp, WINDOW, N_KV_HEADS, HEAD_DIM),
            ssm_p.reshape(DEPTH, bp, SSM_HEADS, SSM_HEAD_DIM, SSM_STATE), conv_p,
            ks.reshape(DEPTH, bs, ss, N_KV_HEADS, HEAD_DIM), vs.reshape(DEPTH, bs, ss, N_KV_HEADS, HEAD_DIM),
            ssm_s.reshape(DEPTH, bs, SSM_HEADS, SSM_HEAD_DIM, SSM_STATE), conv_s, gv_s)
```

```python
import functools

import jax
import jax.numpy as jnp
from jax import lax
from jax.experimental import pallas as pl
from jax.experimental.pallas import tpu as pltpu

F32 = jnp.float32
BF16 = jnp.bfloat16

D_MODEL = 1024
DEPTH = 2
CHUNK = 64
N_HEADS = 8
N_KV_HEADS = 2
HEAD_DIM = 64
GQA_GROUP = N_HEADS // N_KV_HEADS
ROT_DIM = HEAD_DIM // 4
ROPE_THETA = 500000.0
WINDOW = 128
SSM_HEADS = 16
SSM_HEAD_DIM = 64
SSM_INNER = SSM_HEADS * SSM_HEAD_DIM
SSM_GROUPS = 2
SSM_STATE = 128
SSM_CHUNK = 64
CONV_WIDTH = 4
CONV_DIM = SSM_INNER + 2 * SSM_GROUPS * SSM_STATE
GM_WIDTH = 512
GM_GROUPS = 4
GM_GROUP_DIM = GM_WIDTH // GM_GROUPS
GM_CHUNK = 128
D_FF = 4 * D_MODEL
Q_W = N_HEADS * HEAD_DIM
KV_W = N_KV_HEADS * HEAD_DIM
PAST_LEN = 4096
EPS = 1e-6

LANES = 128
MXU_TILE = 256
HEADS_PER_GROUP = SSM_HEADS // SSM_GROUPS
GROUP_INNER = HEADS_PER_GROUP * SSM_HEAD_DIM

C_Q = (0, 0, Q_W)
C_K = (0, C_Q[2], C_Q[2] + KV_W)
C_V = (0, C_K[2], C_K[2] + KV_W)
C_Z = (0, C_V[2], C_V[2] + SSM_INNER)
C_XBC = (0, C_Z[2], C_Z[2] + CONV_DIM)
N_IN_A = C_XBC[2]
C_GU = (1, 0, GM_WIDTH)
C_GV = (1, C_GU[2], C_GU[2] + GM_WIDTH)
C_G0 = (1, C_GV[2], C_GV[2] + D_MODEL)
C_G1 = (1, C_G0[2], C_G0[2] + D_MODEL)
C_G2 = (1, C_G1[2], C_G1[2] + D_MODEL)
N_IN_B = C_G2[2]
C_DT = (2, 0, LANES)

PROMPT_TS = 512
PROJ_BLOCK = MXU_TILE
SAMPLE_NB = 8
FFN_ROWS = 512
W_SPLIT_BLOCKS = 2
V7X_VMEM_BYTES = 64 * 1024 * 1024
VMEM_LIMIT = V7X_VMEM_BYTES * 7 // 8


def _log2(n):
    assert n & (n - 1) == 0, n
    return n.bit_length() - 1


def _dot(a, b):
    return jnp.dot(a, b, preferred_element_type=F32)


def _dot_nt(a, b):
    return lax.dot_general(a, b, (((1,), (1,)), ((), ())), preferred_element_type=F32)


def _dot_tn(a, b):
    return lax.dot_general(a, b, (((0,), (0,)), ((), ())), preferred_element_type=F32)


def _split3(x):
    hi = x.astype(BF16)
    r = x - hi.astype(F32)
    mid = r.astype(BF16)
    lo = (r - mid.astype(F32)).astype(BF16)
    return hi, mid, lo


def _sigmoid(x):
    return 0.5 * (1.0 + jnp.tanh(0.5 * x))


def _silu(x):
    return x * _sigmoid(x)


def _silu_of_half(xh):
    return xh * (1.0 + jnp.tanh(xh))


def _twice_sigmoid_of_half(xh):
    return 1.0 + jnp.tanh(xh)


def _gelu(x):
    return 0.5 * x * (1.0 + jnp.tanh(0.7978845608028654 * (x + 0.044715 * (x * x * x))))


def _softplus(x):
    return jnp.maximum(x, 0.0) + jnp.log1p(jnp.exp(-jnp.abs(x)))


def _rms(x, g):
    return x * lax.rsqrt(jnp.mean(x * x, axis=-1, keepdims=True) + EPS) * g


def _rope(x, cos, sin_a, sin_b):
    return x * cos + pltpu.roll(x, LANES - ROT_DIM // 2, 1) * sin_a + pltpu.roll(x, ROT_DIM // 2, 1) * sin_b


def _proj(h, win_refs, cols):
    return _dot(h, win_refs[cols[0]][:, cols[1]:cols[2]])


def _modnorm(x3, g, mod3):
    nb, t, d = x3.shape
    h3 = _rms(x3, g) * (1.0 + mod3[:, 1:2, :]) + mod3[:, 0:1, :]
    return h3.reshape(nb * t, d).astype(BF16)


def _sink_row(sinks_ref, kv, cols_per_head):
    c = lax.broadcasted_iota(jnp.int32, (1, GQA_GROUP * cols_per_head), 1)
    row = jnp.full((1, GQA_GROUP * cols_per_head), sinks_ref[kv * GQA_GROUP], F32)
    for i in range(1, GQA_GROUP):
        row = jnp.where(c >= i * cols_per_head, sinks_ref[kv * GQA_GROUP + i], row)
    return row


def _stack_heads(q, kv, rows):
    return jnp.concatenate(
        [q[:, (kv * GQA_GROUP + i) * HEAD_DIM:(kv * GQA_GROUP + i + 1) * HEAD_DIM] for i in range(GQA_GROUP)], axis=0)


def _unstack_heads(o, rows):
    return jnp.concatenate([o[i * rows:(i + 1) * rows, :] for i in range(GQA_GROUP)], axis=1)


def _gmlp_chunk(vn, u, gws_ref, gbst_ref, L):
    ri = lax.broadcasted_iota(jnp.int32, (L, L), 0)
    ci = lax.broadcasted_iota(jnp.int32, (L, L), 1)
    outs = []
    for g in range(GM_GROUPS):
        w = jnp.where(ri >= ci, gws_ref[g, 0:L, 0:L], 0.0).astype(BF16)
        v_g = vn[:, g * GM_GROUP_DIM:(g + 1) * GM_GROUP_DIM].astype(BF16)
        outs.append(_dot(w, v_g) + gbst_ref[0:L, g:g + 1])
    return u * jnp.concatenate(outs, axis=1)


def _layer_norm(x, g, b):
    xc = x - jnp.mean(x, axis=-1, keepdims=True)
    return xc * lax.rsqrt(jnp.mean(xc * xc, axis=-1, keepdims=True) + EPS) * g + b


def _merge_out(x3, mod3, gates, a, b, c, wout_ref):
    merged = gates[0] * a + gates[1] * b + gates[2] * c
    o = _dot(merged.astype(BF16), wout_ref[...])
    nb, t, d = x3.shape
    return x3 + mod3[:, 2:3, :] * o.reshape(nb, t, d)


def _attn_groups(ts):
    return [(c, kv) for c in range(ts // CHUNK) for kv in range(N_KV_HEADS)]


def _attn_scores(q, khist, pos0):
    ts = q.shape[0]
    n_keys = WINDOW + CHUNK
    key_i = lax.broadcasted_iota(jnp.int32, (n_keys, GQA_GROUP * CHUNK), 0)
    scores = []
    for c, kv in _attn_groups(ts):
        k_g = khist[c * CHUNK:c * CHUNK + n_keys, kv * HEAD_DIM:(kv + 1) * HEAD_DIM].astype(BF16)
        sc = _dot_nt(k_g, _stack_heads(q[c * CHUNK:(c + 1) * CHUNK, :], kv, CHUNK))
        if c * CHUNK < WINDOW:
            sc = jnp.where(key_i >= WINDOW - c * CHUNK - pos0, sc, -jnp.inf)
        scores.append(sc)
    return scores


def _attn_probs(scores, sinks_ref, ts):
    sink_rows = [_sink_row(sinks_ref, kv, CHUNK) for kv in range(N_KV_HEADS)]
    probs = []
    for (c, kv), sc in zip(_attn_groups(ts), scores):
        m = jnp.maximum(jnp.max(sc, axis=0, keepdims=True), sink_rows[kv])
        p = jnp.exp(sc - m)
        denom = jnp.sum(p, axis=0, keepdims=True) + jnp.exp(sink_rows[kv] - m)
        probs.append((p * (1.0 / denom)).astype(BF16))
    return probs


def _attn_values(probs, vhist, ts):
    n_keys = WINDOW + CHUNK
    outs = []
    for (c, kv), p in zip(_attn_groups(ts), probs):
        v_g = vhist[c * CHUNK:c * CHUNK + n_keys, kv * HEAD_DIM:(kv + 1) * HEAD_DIM].astype(BF16)
        outs.append(_unstack_heads(_dot_tn(p, v_g), CHUNK))
    rows = [jnp.concatenate(outs[c * N_KV_HEADS:(c + 1) * N_KV_HEADS], axis=1) for c in range(ts // CHUNK)]
    return jnp.concatenate(rows, axis=0).astype(BF16)


def _ssd_cumsum(dt, a_row, e):
    ts = dt.shape[0]
    L = SSM_CHUNK
    span = min(ts, MXU_TILE)
    ri = lax.broadcasted_iota(jnp.int32, (span, span), 0)
    lag = ri - lax.broadcasted_iota(jnp.int32, (span, span), 1)
    tri = jnp.where((lag >= 0) & (lag <= (ri & (L - 1))), 1.0, 0.0).astype(BF16)
    hi, mid, lo = _split3(dt * a_row)
    cum = jnp.concatenate(
        [_dot(tri, hi[r:r + span]) + _dot(tri, mid[r:r + span]) + _dot(tri, lo[r:r + span])
         for r in range(0, ts, span)], axis=0)
    hi = cum.astype(BF16)
    mid = (cum - hi.astype(F32)).astype(BF16)
    cum_e = _dot(hi, e) + _dot(mid, e)
    dt_e = _dot(dt.astype(BF16), e)
    return cum_e, dt_e


def _ssd_decays(xact, cum_e, dt_e):
    ts = xact.shape[0]
    L = SSM_CHUNK
    nc = ts // L
    xs = xact[:, 0:SSM_INNER]
    cum_e = cum_e.reshape(nc, L, SSM_INNER)
    t3 = lax.broadcasted_iota(jnp.int32, cum_e.shape, 1)
    slot3 = lax.broadcasted_iota(jnp.int32, cum_e.shape, 2) & (SSM_HEAD_DIM - 1)
    cum_src = jnp.sum(jnp.where(t3 == slot3, cum_e, 0.0), axis=1, keepdims=True)
    decay = jnp.exp(jnp.where(t3 >= slot3, cum_e - cum_src, -jnp.inf))
    last = cum_e[:, L - 1:L, :]
    exp_cum = jnp.exp(cum_e)
    chunk_decay = jnp.exp(last)
    xdt = xs * dt_e
    wx = (xdt.reshape(nc, L, SSM_INNER) * jnp.exp(last - cum_e)).astype(BF16)
    first_head = (lax.broadcasted_iota(jnp.int32, xdt.shape, 1) & (2 * SSM_HEAD_DIM - 1)) < SSM_HEAD_DIM
    return {
        "xs": xs,
        "bm": xact[:, SSM_INNER:SSM_INNER + SSM_GROUPS * SSM_STATE].astype(BF16),
        "cm": xact[:, SSM_INNER + SSM_GROUPS * SSM_STATE:CONV_DIM].astype(BF16),
        "decay": decay, "exp_cum": exp_cum, "chunk_decay": chunk_decay, "wx": wx,
        "xdt_a": jnp.where(first_head, xdt, 0.0).astype(BF16),
        "xdt_b": jnp.where(first_head, 0.0, xdt).astype(BF16),
    }


def _ssd_block(ctx, st, dskip_e, fillers):
    L = SSM_CHUNK
    nc = ctx["xs"].shape[0] // L
    bm, cm = ctx["bm"], ctx["cm"]
    fillers = iter(fillers)

    def grp(a, c, g, width):
        return a[c * L:(c + 1) * L, g * width:(g + 1) * width]

    chunk_state = [jnp.concatenate([_dot_tn(grp(bm, c, g, SSM_STATE), ctx["wx"][c][:, g * GROUP_INNER:(g + 1) * GROUP_INNER])
                                    for g in range(SSM_GROUPS)], axis=1) for c in range(nc)]
    next(fillers)()
    cb = [[_dot_nt(grp(cm, c, g, SSM_STATE), jnp.concatenate([grp(bm, c, g, SSM_STATE)] * 2, axis=0))
           for g in range(SSM_GROUPS)] for c in range(nc)]
    next(fillers)()
    st_in = []
    for c in range(nc):
        st_in.append(st.astype(BF16))
        st = st * ctx["chunk_decay"][c] + chunk_state[c]
    y_intra = []
    for c in range(nc):
        rows = slice(c * L, (c + 1) * L)
        parts = []
        for k in range(SSM_HEADS // 2):
            ls = slice(k * 2 * SSM_HEAD_DIM, (k + 1) * 2 * SSM_HEAD_DIM)
            m_pair = (cb[c][2 * k // HEADS_PER_GROUP] * ctx["decay"][c][:, ls]).astype(BF16)
            block_diag = jnp.concatenate([ctx["xdt_a"][rows, ls], ctx["xdt_b"][rows, ls]], axis=0)
            parts.append(_dot(m_pair, block_diag))
        y_intra.append(jnp.concatenate(parts, axis=1))
    next(fillers)()
    ys = []
    for c in range(nc):
        y_inter = jnp.concatenate(
            [_dot(grp(cm, c, g, SSM_STATE), st_in[c][:, g * GROUP_INNER:(g + 1) * GROUP_INNER])
             for g in range(SSM_GROUPS)], axis=1)
        ys.append(y_intra[c] + y_inter * ctx["exp_cum"][c] + dskip_e * ctx["xs"][c * L:(c + 1) * L])
    next(fillers)()
    return jnp.concatenate(ys, axis=0), st


class _Stagger:
    def __init__(self):
        self._pending = None

    def push(self, produce, consume):
        val = produce()
        self.flush()
        self._pending = (consume, val)

    def flush(self):
        if self._pending is not None:
            consume, val = self._pending
            self._pending = None
            consume(val)


def _prompt_kernel(x_ref, mod_ref, gmix_ref, wa_ref, wb_ref, wdt_ref, cos_ref, sa_ref, sb_ref, sinks_ref,
                   convw_ref, convb_ref, dtb_ref, alog_ref, dskip_ref, ssmnw_ref, glng_ref, glnb_ref,
                   gws_ref, gbst_ref, wao_ref, wso_ref, wgo_ref, wout_ref, e_ref,
                   xo_ref, ko_ref, vo_ref, sto_ref, cvo_ref,
                   khist, vhist, xp, st_s):
    ts = PROMPT_TS
    s = pl.program_id(1)
    last = pl.num_programs(1) - 1

    @pl.when(s == 0)
    def _():
        khist[0:WINDOW, :] = jnp.zeros((WINDOW, KV_W), F32)
        vhist[0:WINDOW, :] = jnp.zeros((WINDOW, KV_W), F32)
        xp[...] = jnp.zeros_like(xp)
        st_s[...] = jnp.zeros_like(st_s)

    x3 = x_ref[...]
    mod3 = mod_ref[...]
    half_rows = ts // 2
    h_top = _modnorm(x3[:, 0:half_rows], gmix_ref[...], mod3)
    q0_top = _dot(h_top, wa_ref[:, C_Q[1]:C_Q[1] + PROJ_BLOCK])
    h_bot = _modnorm(x3[:, half_rows:ts], gmix_ref[...], mod3)
    q0 = jnp.concatenate([q0_top, _dot(h_bot, wa_ref[:, C_Q[1]:C_Q[1] + PROJ_BLOCK])], axis=0)
    h = jnp.concatenate([h_top, h_bot], axis=0)

    cos, sa, sb = cos_ref[...], sa_ref[...], sb_ref[...]
    res = {}

    def rope_block(raw):
        return [_rope(raw[:, i * LANES:(i + 1) * LANES], cos, sa, sb) for i in range(raw.shape[1] // LANES)]

    def q_block(raw):
        res.setdefault("q", []).extend(p.astype(BF16) for p in rope_block(raw * (HEAD_DIM ** -0.5)))

    def kv_block(raw):
        khist[WINDOW:WINDOW + ts, :] = rope_block(raw[:, 0:KV_W])[0]
        vhist[WINDOW:WINDOW + ts, :] = raw[:, KV_W:2 * KV_W]

    def conv_block(lo, raw):
        cs = slice(lo, lo + PROJ_BLOCK)
        ext = jnp.concatenate([xp[:, cs], raw], axis=0)
        acc = convb_ref[:, cs] + raw * convw_ref[CONV_WIDTH - 1:CONV_WIDTH, cs]
        for j in range(1, CONV_WIDTH):
            acc = acc + pltpu.roll(ext, j, 0)[8:8 + ts] * convw_ref[CONV_WIDTH - 1 - j:CONV_WIDTH - j, cs]
        xp[:, cs] = raw[ts - 8:ts]
        res.setdefault("xact", []).append(_silu_of_half(acc))

    def collect(key, fn):
        return lambda raw: res.setdefault(key, []).append(fn(raw))

    wrefs = (wa_ref, wb_ref, wdt_ref)
    pipe = _Stagger()

    def push_proj(cols, off, width, consume):
        ref, lo = wrefs[cols[0]], cols[1] + off
        pipe.push(lambda: _dot(h, ref[:, lo:lo + width]), consume)

    def blocks(cols, consume):
        return [functools.partial(push_proj, cols, off, PROJ_BLOCK, consume)
                for off in range(0, cols[2] - cols[1], PROJ_BLOCK)]

    conv_blocks = [functools.partial(push_proj, C_XBC, off, PROJ_BLOCK, functools.partial(conv_block, off))
                   for off in range(0, CONV_DIM, PROJ_BLOCK)]
    gate_blocks = iter(blocks(C_G0, collect("g0", _twice_sigmoid_of_half))
                       + blocks(C_G1, collect("g1", _twice_sigmoid_of_half))
                       + blocks(C_G2, collect("g2", _twice_sigmoid_of_half)))
    z_blocks = blocks(C_Z, collect("sz", _silu_of_half))
    gm_blocks = blocks(C_GU, collect("u", _gelu)) + blocks(C_GV, collect("gv", _gelu))

    def cat(key):
        return jnp.concatenate(res[key], axis=1)

    pipe.push(lambda: q0, q_block)
    for blk in blocks(C_Q, q_block)[1:]:
        blk()
    push_proj(C_K, 0, 2 * KV_W, kv_block)
    conv_blocks[0]()
    next(gate_blocks)()
    conv_blocks[1]()
    scores = _attn_scores(cat("q"), khist, s * ts)
    conv_blocks[2]()
    probs = _attn_probs(scores, sinks_ref, ts)
    next(gate_blocks)()
    conv_blocks[3]()
    attn = _attn_values(probs, vhist, ts)
    conv_blocks[4]()
    next(gate_blocks)()
    conv_blocks[5]()
    push_proj(C_DT, 0, LANES, collect("dt", lambda raw: _softplus(raw + dtb_ref[...])))
    next(gate_blocks)()
    a_out = _dot(attn, wao_ref[...])
    z_blocks[0]()
    cum_e, dt_e = _ssd_cumsum(res["dt"][0], -jnp.exp(alog_ref[...]), e_ref[...])
    z_blocks[1]()
    merged = cat("g0") * a_out
    ctx = _ssd_decays(cat("xact"), cum_e, dt_e)
    z_blocks[2]()
    z_blocks[3]()
    y, st_s[...] = _ssd_block(ctx, st_s[...], dskip_ref[...], gm_blocks)
    next(gate_blocks)()
    vn = _layer_norm(cat("gv"), glng_ref[...], glnb_ref[...])
    u = cat("u")
    next(gate_blocks)()
    ssm = _rms(y * cat("sz"), ssmnw_ref[...]).astype(BF16)
    gm = jnp.concatenate(
        [_gmlp_chunk(vn[c * GM_CHUNK:(c + 1) * GM_CHUNK], u[c * GM_CHUNK:(c + 1) * GM_CHUNK], gws_ref, gbst_ref,
                     GM_CHUNK) for c in range(ts // GM_CHUNK)], axis=0).astype(BF16)
    next(gate_blocks)()
    b_out = _dot(ssm, wso_ref[...])
    next(gate_blocks)()
    pipe.flush()
    merged = merged + cat("g1") * b_out
    c_out = _dot(gm, wgo_ref[...])
    for blk in gate_blocks:
        blk()
    pipe.flush()
    merged = merged + cat("g2") * c_out
    o = _dot(merged.astype(BF16), wout_ref[...])
    xo_ref[...] = x3 + mod3[:, 2:3, :] * o.reshape(x3.shape)

    khist[0:WINDOW, :] = khist[ts:ts + WINDOW, :]
    vhist[0:WINDOW, :] = vhist[ts:ts + WINDOW, :]

    @pl.when(s == last)
    def _():
        ko_ref[0] = khist[ts:ts + WINDOW, :]
        vo_ref[0] = vhist[ts:ts + WINDOW, :]
        cvo_ref[0] = xp[8 - (CONV_WIDTH - 1):8, :]
        sto_ref[0] = st_s[...].T


def _sample_attention(q, k_new, v_new, ck_ref, cv_ref, sinks_ref, nb):
    t = q.shape[0] // nb
    groups = [(i, kv) for i in range(nb) for kv in range(N_KV_HEADS)]

    def new_rows(a, i, kv):
        return a[i * t:(i + 1) * t, kv * HEAD_DIM:(kv + 1) * HEAD_DIM].astype(BF16)

    def cached(ref, i, kv):
        return ref[i, kv].T.astype(BF16)

    qg = [_stack_heads(q[i * t:(i + 1) * t], kv, t) for i, kv in groups]
    s_c = [_dot_nt(cached(ck_ref, i, kv), qg[n]) for n, (i, kv) in enumerate(groups)]
    s_n = [_dot_nt(new_rows(k_new, i, kv), qg[n]) for n, (i, kv) in enumerate(groups)]
    sinks = [_sink_row(sinks_ref, kv, t) for kv in range(N_KV_HEADS)]
    probs = []
    for n, (i, kv) in enumerate(groups):
        m = jnp.maximum(jnp.maximum(jnp.max(s_c[n], axis=0, keepdims=True), jnp.max(s_n[n], axis=0, keepdims=True)),
                        sinks[kv])
        p_c = jnp.exp(s_c[n] - m)
        p_n = jnp.exp(s_n[n] - m)
        inv = 1.0 / (jnp.sum(p_c, axis=0, keepdims=True) + jnp.sum(p_n, axis=0, keepdims=True)
                     + jnp.exp(sinks[kv] - m))
        probs.append(((p_c * inv).astype(BF16), (p_n * inv).astype(BF16)))
    outs = [_dot_tn(probs[n][0], cached(cv_ref, i, kv)) + _dot_tn(probs[n][1], new_rows(v_new, i, kv))
            for n, (i, kv) in enumerate(groups)]
    rows = [jnp.concatenate([_unstack_heads(outs[i * N_KV_HEADS + kv], t) for kv in range(N_KV_HEADS)], axis=1)
            for i in range(nb)]
    return jnp.concatenate(rows, axis=0)


def _sample_ssd_setup(xact, dt, a_row, e, L):
    m = xact.shape[0]
    n_seq = m // L
    xs = xact[:, 0:SSM_INNER]
    ri = lax.broadcasted_iota(jnp.int32, (m, m), 0)
    lag = ri - lax.broadcasted_iota(jnp.int32, (m, m), 1)
    tri = jnp.where((lag >= 0) & (lag <= (ri & (L - 1))), 1.0, 0.0).astype(BF16)
    hi, mid, lo = _split3(dt * a_row)
    cum = _dot(tri, hi) + _dot(tri, mid) + _dot(tri, lo)
    hi = cum.astype(BF16)
    mid = (cum - hi.astype(F32)).astype(BF16)
    cum_e = (_dot(hi, e) + _dot(mid, e)).reshape(n_seq, L, SSM_INNER)
    head_of_lane = jnp.right_shift(lax.broadcasted_iota(jnp.int32, (LANES, SSM_HEADS * L), 1), _log2(L))
    e_slots = jnp.where(head_of_lane == lax.broadcasted_iota(jnp.int32, (LANES, SSM_HEADS * L), 0), 1.0, 0.0)
    e_slots = e_slots.astype(BF16)
    cum_s = (_dot(hi, e_slots) + _dot(mid, e_slots)).reshape(n_seq, L, SSM_HEADS * L)
    dt_e = _dot(dt.astype(BF16), e)
    t3 = lax.broadcasted_iota(jnp.int32, cum_s.shape, 1)
    slot3 = lax.broadcasted_iota(jnp.int32, cum_s.shape, 2) & (L - 1)
    cum_src = jnp.sum(jnp.where(t3 == slot3, cum_s, 0.0), axis=1, keepdims=True)
    decay = jnp.exp(jnp.where(t3 >= slot3, cum_s - cum_src, -jnp.inf))
    last = cum_e[:, L - 1:L, :]
    xdt = xs * dt_e
    wx = (xdt.reshape(n_seq, L, SSM_INNER) * jnp.exp(last - cum_e)).astype(BF16)
    per_head = jnp.exp(cum.reshape(n_seq, L, LANES)[:, L - 1:L, :])
    rep = jnp.broadcast_to(per_head, (n_seq, SSM_HEADS, LANES)).reshape(n_seq * SSM_HEADS, LANES)
    own_lane = (lax.broadcasted_iota(jnp.int32, rep.shape, 1)
                == (lax.broadcasted_iota(jnp.int32, rep.shape, 0) & (SSM_HEADS - 1)))
    hi, mid, lo = _split3(jnp.where(own_lane, rep, 0.0))
    ones = jnp.ones((LANES, LANES), BF16)
    splat = (_dot(hi, ones) + _dot(mid, ones) + _dot(lo, ones)).reshape(n_seq, SSM_HEADS, LANES)
    return {
        "L": L, "xs": xs,
        "bm": xact[:, SSM_INNER:SSM_INNER + SSM_GROUPS * SSM_STATE].astype(BF16),
        "cm": xact[:, SSM_INNER + SSM_GROUPS * SSM_STATE:CONV_DIM].astype(BF16),
        "decay": decay, "exp_cum": jnp.exp(cum_e), "wx": wx, "xdt": xdt.astype(BF16), "state_decay": splat,
    }


def _sample_ssd_block(s, h0_ref, sto_ref, dskip_e, nb):
    L = s["L"]
    pairs = [(i, g) for i in range(nb) for g in range(SSM_GROUPS)]
    ri = jnp.right_shift(lax.broadcasted_iota(jnp.int32, (HEADS_PER_GROUP * L, GROUP_INNER), 0), _log2(L))
    li = jnp.right_shift(lax.broadcasted_iota(jnp.int32, (HEADS_PER_GROUP * L, GROUP_INNER), 1), _log2(SSM_HEAD_DIM))

    def rows(a, i, g, width):
        return a[i * L:(i + 1) * L, g * width:(g + 1) * width]

    cb = [_dot_nt(rows(s["cm"], i, g, SSM_STATE), jnp.concatenate([rows(s["bm"], i, g, SSM_STATE)] * HEADS_PER_GROUP,
                                                                  axis=0)) for i, g in pairs]
    upd = [_dot_tn(s["wx"][i][:, g * GROUP_INNER:(g + 1) * GROUP_INNER], rows(s["bm"], i, g, SSM_STATE))
           for i, g in pairs]
    y_inter = [_dot_nt(rows(s["cm"], i, g, SSM_STATE), h0_ref[i][g * GROUP_INNER:(g + 1) * GROUP_INNER, :].astype(BF16))
               for i, g in pairs]
    y_intra = []
    for n, (i, g) in enumerate(pairs):
        gs = slice(g * HEADS_PER_GROUP * L, (g + 1) * HEADS_PER_GROUP * L)
        m_g = (cb[n] * s["decay"][i][:, gs]).astype(BF16)
        tiled = jnp.concatenate([rows(s["xdt"], i, g, GROUP_INNER)] * HEADS_PER_GROUP, axis=0)
        y_intra.append(_dot(m_g, jnp.where(ri == li, tiled, jnp.zeros_like(tiled))))
    ys = []
    for i in range(nb):
        parts = [y_intra[i * SSM_GROUPS + g] + y_inter[i * SSM_GROUPS + g]
                 * s["exp_cum"][i][:, g * GROUP_INNER:(g + 1) * GROUP_INNER] for g in range(SSM_GROUPS)]
        ys.append(jnp.concatenate(parts, axis=1) + dskip_e * s["xs"][i * L:(i + 1) * L])
        decayed = h0_ref[i].reshape(SSM_HEADS, SSM_HEAD_DIM, SSM_STATE) * s["state_decay"][i][:, None, :]
        sto_ref[i] = decayed.reshape(SSM_INNER, SSM_STATE) + jnp.concatenate(
            upd[i * SSM_GROUPS:(i + 1) * SSM_GROUPS], axis=0)
    return jnp.concatenate(ys, axis=0)


def _sample_kernel(x_ref, mod_ref, gmix_ref, wa_ref, wb_ref, wdt_ref, cos_ref, sa_ref, sb_ref, sinks_ref,
                   ck_ref, cv_ref, h0_ref, cs_ref,
                   convw_ref, convb_ref, dtb_ref, alog_ref, dskip_ref, ssmnw_ref, glng_ref, glnb_ref,
                   gws_ref, gbst_ref, wao_ref, wso_ref, wgo_ref, wout_ref, e_ref,
                   xo_ref, ko_ref, vo_ref, sto_ref, cvo_ref, gvo_ref,
                   xp):
    nb = SAMPLE_NB
    t = x_ref.shape[1]
    m = nb * t
    win_ref = (wa_ref, wb_ref, wdt_ref)
    x3 = x_ref[...]
    mod3 = mod_ref[...]
    h = _modnorm(x3, gmix_ref[...], mod3)

    cos, sa, sb = cos_ref[...], sa_ref[...], sb_ref[...]
    q = _proj(h, win_ref, C_Q) * (HEAD_DIM ** -0.5)
    q = jnp.concatenate(
        [_rope(q[:, i * LANES:(i + 1) * LANES], cos, sa, sb) for i in range(Q_W // LANES)], axis=1).astype(BF16)
    k_new = _rope(_proj(h, win_ref, C_K), cos, sa, sb)
    v_new = _proj(h, win_ref, C_V)
    ko_ref[...] = k_new.reshape(nb, t, KV_W)
    vo_ref[...] = v_new.reshape(nb, t, KV_W)

    xp[:, 8 - (CONV_WIDTH - 1):8, :] = cs_ref[...]
    xp[:, 8:8 + t, :] = _proj(h, win_ref, C_XBC).reshape(nb, t, CONV_DIM)
    acc = convb_ref[...] + xp[:, 8:8 + t, :] * convw_ref[CONV_WIDTH - 1:CONV_WIDTH, :]
    for j in range(1, CONV_WIDTH):
        acc = acc + xp[:, 8 - j:8 - j + t, :] * convw_ref[CONV_WIDTH - 1 - j:CONV_WIDTH - j, :]
    xact = _silu_of_half(acc).reshape(m, CONV_DIM)
    cvo_ref[...] = xp[:, t + 8 - (CONV_WIDTH - 1):t + 8, :]
    dt = _softplus(_proj(h, win_ref, C_DT) + dtb_ref[...])
    dskip_e = dskip_ref[...]

    u = _gelu(_proj(h, win_ref, C_GU))
    vn = _layer_norm(_gelu(_proj(h, win_ref, C_GV)), glng_ref[...], glnb_ref[...])
    gvo_ref[...] = vn.reshape(nb, t, GM_WIDTH)

    ssd = _sample_ssd_setup(xact, dt, -jnp.exp(alog_ref[...]), e_ref[...], t)
    attn = _sample_attention(q, k_new, v_new, ck_ref, cv_ref, sinks_ref, nb)
    y = _sample_ssd_block(ssd, h0_ref, sto_ref, dskip_e, nb)
    gm_rows = [_gmlp_chunk(vn[i * t:(i + 1) * t], u[i * t:(i + 1) * t], gws_ref, gbst_ref, t) for i in range(nb)]

    a_out = _dot(attn.astype(BF16), wao_ref[...])
    z = _proj(h, win_ref, C_Z)
    ssm = _rms(y * _silu_of_half(z), ssmnw_ref[...]).astype(BF16)
    b_out = _dot(ssm, wso_ref[...])
    c_out = _dot(jnp.concatenate(gm_rows, axis=0).astype(BF16), wgo_ref[...])
    gates = [_twice_sigmoid_of_half(_proj(h, win_ref, cols)) for cols in (C_G0, C_G1, C_G2)]
    xo_ref[...] = _merge_out(x3, mod3, gates, a_out, b_out, c_out, wout_ref)


def _ffn_block(x_ref, mod_ref, gff_ref, w1_ref, w2_ref, gfin_ref, o_ref, final_norm):
    x3 = x_ref[...]
    mod3 = mod_ref[...]
    nb, t, d = x3.shape
    h3 = _rms(x3, gff_ref[...]) * (1.0 + mod3[:, 4:5, :]) + mod3[:, 3:4, :]
    h = h3.reshape(nb * t, d).astype(BF16)
    a = jnp.maximum(_dot(h, w1_ref[...]), 0.0)
    y = _dot((a * a).astype(BF16), w2_ref[...])
    out = x3 + mod3[:, 5:6, :] * y.reshape(nb, t, d)
    if final_norm:
        out = _rms(out, gfin_ref[...])
    o_ref[...] = out


def _ffn_kernel(xp_ref, modp_ref, xs_ref, mods_ref, gff_ref, w1_ref, w2_ref, gfin_ref, op_ref, os_ref, *,
                final_norm, n_prompt_steps):
    i = pl.program_id(0)

    @pl.when(i < n_prompt_steps)
    def _():
        _ffn_block(xp_ref, modp_ref, gff_ref, w1_ref, w2_ref, gfin_ref, op_ref, final_norm)

    @pl.when(i == n_prompt_steps)
    def _():
        _ffn_block(xs_ref, mods_ref, gff_ref, w1_ref, w2_ref, gfin_ref, os_ref, final_norm)


def _ada_kernel(c_ref, w_ref, b_ref, op_ref, os_ref):
    mod = _dot(_silu(c_ref[...]).astype(BF16), w_ref[...].astype(BF16)) + b_ref[...]
    n_prompt = op_ref.shape[0]
    op_ref[...] = mod[0:n_prompt]
    os_ref[...] = mod[n_prompt:]


def _w_in_piece_kernel(wt_ref, o_ref, *, half_lo, half_hi, n_valid):
    cols = wt_ref.shape[0]
    col = pl.program_id(1) * cols + lax.broadcasted_iota(jnp.int32, (cols, 1), 0)
    scale = jnp.where(col < n_valid, jnp.where((col >= half_lo) & (col < half_hi), 0.5, 1.0), 0.0)
    o_ref[...] = (wt_ref[...] * scale).astype(BF16).T


def _w_in_piece(w_t, first_col, n_cols, half_cols, n_blocks, n_valid=None):
    cols = n_cols // n_blocks
    n_valid = n_cols if n_valid is None else n_valid
    return pl.pallas_call(
        functools.partial(_w_in_piece_kernel, half_lo=half_cols[0], half_hi=half_cols[1], n_valid=n_valid),
        grid=(DEPTH, n_blocks),
        in_specs=[pl.BlockSpec((pl.Squeezed(), pl.Element(cols), pl.Element(D_MODEL)),
                               lambda l, j: (l, pl.multiple_of(first_col + j * cols, SSM_HEADS), 0))],
        out_specs=pl.BlockSpec((None, D_MODEL, cols), lambda l, j: (l, 0, j)),
        out_shape=jax.ShapeDtypeStruct((DEPTH, D_MODEL, n_cols), BF16),
        compiler_params=_params(2),
        name="w_in_piece",
    )(w_t)


def _w_in_split(w_in):
    w_t = jnp.swapaxes(w_in, 1, 2)
    b_lo = N_IN_A + SSM_HEADS
    return (_w_in_piece(w_t, 0, N_IN_A, (C_Z[1], C_Z[2]), W_SPLIT_BLOCKS),
            _w_in_piece(w_t, b_lo, N_IN_B, (C_G0[1], N_IN_B), W_SPLIT_BLOCKS),
            _w_in_piece(w_t, N_IN_A, LANES, (0, 0), 1, n_valid=SSM_HEADS))


def _with_ignored_inputs(body, n_in, n_ignored):
    if n_ignored == 0:
        return body
    return lambda *refs: body(*refs[:n_in], *refs[n_in + n_ignored:])


def _const_spec(shape):
    return pl.BlockSpec(shape, lambda *_: (0,) * len(shape), pipeline_mode=pl.Buffered(1))


def _layer_spec(l, shape):
    return pl.BlockSpec((None,) + shape, lambda *_: (l,) + (0,) * len(shape), pipeline_mode=pl.Buffered(1))


def _any_spec():
    return pl.BlockSpec(memory_space=pl.ANY)


def _smem_spec():
    return pl.BlockSpec(memory_space=pltpu.SMEM)


def _params(n_grid):
    return pltpu.CompilerParams(dimension_semantics=("arbitrary",) * n_grid, vmem_limit_bytes=VMEM_LIMIT)


def _ada_call(c_prompt, c_sample, w_ada, b_ada):
    n_p, n_s = c_prompt.shape[0], c_sample.shape[0]
    c_all = jnp.concatenate([c_prompt, c_sample], axis=0)
    tn = 2 * D_MODEL
    n_tiles = w_ada.shape[2] // tn
    mod_p, mod_s = pl.pallas_call(
        _ada_kernel,
        grid=(DEPTH, n_tiles),
        in_specs=[pl.BlockSpec((n_p + n_s, D_MODEL), lambda l, n: (0, 0)),
                  pl.BlockSpec((None, D_MODEL, tn), lambda l, n: (l, 0, n)),
                  pl.BlockSpec((None, 1, tn), lambda l, n: (l, 0, n))],
        out_specs=(pl.BlockSpec((None, n_p, tn), lambda l, n: (l, 0, n)),
                   pl.BlockSpec((None, n_s, tn), lambda l, n: (l, 0, n))),
        out_shape=(jax.ShapeDtypeStruct((DEPTH, n_p, w_ada.shape[2]), F32),
                   jax.ShapeDtypeStruct((DEPTH, n_s, w_ada.shape[2]), F32)),
        compiler_params=_params(2),
        name="ada_mod",
    )(c_all, w_ada, b_ada.reshape(DEPTH, 1, -1))
    return mod_p.reshape(DEPTH, n_p, 6, D_MODEL), mod_s.reshape(DEPTH, n_s, 6, D_MODEL)


def _w_in_specs(l):
    return [_layer_spec(l, (D_MODEL, N_IN_A)), _layer_spec(l, (D_MODEL, N_IN_B)), _layer_spec(l, (D_MODEL, LANES))]


def _layer_weight_specs(l):
    return [
        _layer_spec(l, (CONV_WIDTH, CONV_DIM)), _layer_spec(l, (1, CONV_DIM)), _layer_spec(l, (1, LANES)),
        _layer_spec(l, (1, LANES)), _layer_spec(l, (1, SSM_INNER)), _layer_spec(l, (1, SSM_INNER)),
        _layer_spec(l, (1, GM_WIDTH)), _layer_spec(l, (1, GM_WIDTH)),
        _layer_spec(l, (GM_GROUPS, GM_CHUNK, GM_CHUNK)), _layer_spec(l, (GM_CHUNK, GM_GROUPS)),
        _layer_spec(l, (Q_W, D_MODEL)), _layer_spec(l, (SSM_INNER, D_MODEL)), _layer_spec(l, (GM_WIDTH, D_MODEL)),
        _layer_spec(l, (D_MODEL, D_MODEL)), _const_spec((LANES, SSM_INNER)),
    ]


def _layer_weight_args(p):
    return (p["conv_w"], p["conv_b"], p["dt_bias"], p["a_log"], p["d_skip"], p["ssm_norm_w"], p["gm_ln_g"],
            p["gm_ln_b"], p["gm_w_s"], p["gm_b_st"], p["w_attn_o"], p["w_ssm_o"], p["w_gm_o"], p["w_out"], p["expand"])


def _prompt_mixer(l, x, mod, p, rope, prev):
    bsz, seq, d = x.shape
    ts = PROMPT_TS
    tab = pl.BlockSpec((ts, LANES), lambda b, s: (s, 0))
    in_specs = [
        pl.BlockSpec((1, ts, d), lambda b, s: (b, s, 0)),
        pl.BlockSpec((None, 1, 6, d), lambda b, s: (l, b, 0, 0)),
        _layer_spec(l, (1, d)), *_w_in_specs(l),
        tab, tab, tab, _smem_spec(),
    ] + _layer_weight_specs(l)
    args = (x, mod, p["g_mix"], *p["w_in"], *rope, p["sinks"][l], *_layer_weight_args(p))
    state_tails = ((WINDOW, KV_W), (WINDOW, KV_W), (SSM_INNER, SSM_STATE), (CONV_WIDTH - 1, CONV_DIM))
    out_shape = (jax.ShapeDtypeStruct((bsz, seq, d), F32),) + tuple(
        jax.ShapeDtypeStruct((DEPTH, bsz) + tail, F32) for tail in state_tails)
    out_specs = (pl.BlockSpec((1, ts, d), lambda b, s: (b, s, 0)),) + tuple(
        pl.BlockSpec((None, 1) + tail, lambda b, s: (l, b, 0, 0)) for tail in state_tails)
    scratch = [
        pltpu.VMEM((WINDOW + ts, KV_W), F32), pltpu.VMEM((WINDOW + ts, KV_W), F32),
        pltpu.VMEM((8, CONV_DIM), F32), pltpu.VMEM((SSM_STATE, SSM_INNER), F32),
    ]
    n_in = len(args)
    prev = () if prev is None else tuple(prev)
    return pl.pallas_call(
        _with_ignored_inputs(_prompt_kernel, n_in, len(prev)),
        grid=(bsz, seq // ts), in_specs=in_specs + [_any_spec()] * len(prev), out_specs=out_specs,
        out_shape=out_shape, scratch_shapes=scratch, compiler_params=_params(2), name="prompt_mixer",
        input_output_aliases={n_in + i: 1 + i for i in range(len(prev))},
    )(*args, *prev)


def _sample_mixer(l, x, mod, cache_k, cache_v, h0, conv_state, p, rope, prev):
    bsz, t, d = x.shape
    nb = SAMPLE_NB
    m = nb * t

    def blk(*tail):
        return pl.BlockSpec((nb,) + tail, lambda i: (i,) + (0,) * len(tail))

    def lblk(*tail):
        return pl.BlockSpec((None, nb) + tail, lambda i: (l, i) + (0,) * len(tail))

    in_specs = [
        blk(t, d), lblk(6, d), _layer_spec(l, (1, d)), *_w_in_specs(l),
        _const_spec((m, LANES)), _const_spec((m, LANES)), _const_spec((m, LANES)), _smem_spec(),
        lblk(N_KV_HEADS, HEAD_DIM, WINDOW), lblk(N_KV_HEADS, HEAD_DIM, WINDOW), lblk(SSM_INNER, SSM_STATE),
        lblk(CONV_WIDTH - 1, CONV_DIM),
    ] + _layer_weight_specs(l)
    args = (x, mod, p["g_mix"], *p["w_in"], *rope, p["sinks"][l], cache_k, cache_v, h0, conv_state,
            *_layer_weight_args(p))
    state_tails = ((t, KV_W), (t, KV_W), (SSM_INNER, SSM_STATE), (CONV_WIDTH - 1, CONV_DIM), (t, GM_WIDTH))
    out_shape = (jax.ShapeDtypeStruct((bsz, t, d), F32),) + tuple(
        jax.ShapeDtypeStruct((DEPTH, bsz) + tail, F32) for tail in state_tails)
    out_specs = (blk(t, d),) + tuple(lblk(*tail) for tail in state_tails)
    scratch = [pltpu.VMEM((nb, 8 + t, CONV_DIM), F32)]
    n_in = len(args)
    prev = () if prev is None else tuple(prev)
    return pl.pallas_call(
        _with_ignored_inputs(_sample_kernel, n_in, len(prev)),
        grid=(bsz // nb,), in_specs=in_specs + [_any_spec()] * len(prev), out_specs=out_specs,
        out_shape=out_shape, scratch_shapes=scratch, compiler_params=_params(1), name="sample_mixer",
        input_output_aliases={n_in + i: 1 + i for i in range(len(prev))},
    )(*args, *prev)


def _ffn(l, x_p, mod_p, x_s, mod_s, p, g_final, final_norm):
    bp, sp, d = x_p.shape
    bs, ss, _ = x_s.shape
    assert bs * ss == FFN_ROWS and sp % FFN_ROWS == 0
    blocks_per_seq = sp // FFN_ROWS
    n_prompt_steps = bp * blocks_per_seq

    def prompt_block(i):
        j = jnp.minimum(i, n_prompt_steps - 1)
        return j // blocks_per_seq, j % blocks_per_seq

    x_spec = pl.BlockSpec((1, FFN_ROWS, d), lambda i: (*prompt_block(i), 0))
    s_spec = pl.BlockSpec((bs, ss, d), lambda i: (0, 0, 0))
    return pl.pallas_call(
        functools.partial(_ffn_kernel, final_norm=final_norm, n_prompt_steps=n_prompt_steps),
        grid=(n_prompt_steps + 1,),
        in_specs=[x_spec, pl.BlockSpec((None, 1, 6, d), lambda i: (l, prompt_block(i)[0], 0, 0)),
                  s_spec, pl.BlockSpec((None, bs, 6, d), lambda i: (l, 0, 0, 0)),
                  _layer_spec(l, (1, d)), _layer_spec(l, (d, D_FF)), _layer_spec(l, (D_FF, d)), _const_spec((1, d))],
        out_specs=(x_spec, s_spec),
        out_shape=(jax.ShapeDtypeStruct(x_p.shape, F32), jax.ShapeDtypeStruct(x_s.shape, F32)),
        compiler_params=_params(1), name="ffn",
    )(x_p, mod_p, x_s, mod_s, p["g_ff"], p["w_ff1"], p["w_ff2"], g_final)


def _rope_tables(pos):
    half = ROT_DIM // 2
    inv_freq = ROPE_THETA ** (-jnp.arange(half, dtype=F32) * (2.0 / ROT_DIM))
    ang = pos.astype(F32)[:, None] * inv_freq[None, :]
    cos, sin = jnp.cos(ang), jnp.sin(ang)
    n = pos.shape[0]
    ones = jnp.ones((n, HEAD_DIM - ROT_DIM), F32)
    zeros = jnp.zeros((n, HEAD_DIM - ROT_DIM), F32)
    zh = jnp.zeros((n, half), F32)
    cos_t = jnp.concatenate([cos, cos, ones], axis=1)
    sin_a = jnp.concatenate([-sin, zh, zeros], axis=1)
    sin_b = jnp.concatenate([zh, sin, zeros], axis=1)
    rep = LANES // HEAD_DIM
    return tuple(jnp.tile(a, (1, rep)) for a in (cos_t, sin_a, sin_b))


def _stacked_params(w_in, g_mix, sinks, conv_w, conv_b, dt_bias, a_log, d_skip, ssm_norm_w, gm_ln_g, gm_ln_b,
                    gm_w_s, gm_b_s, w_attn_o, w_ssm_o, w_gm_o, w_out, g_ff, w_ff1, w_ff2):
    w_r = _w_in_split(w_in)
    pad = jnp.zeros((DEPTH, LANES - SSM_HEADS), F32)
    expand = (jnp.arange(SSM_INNER)[None, :] // SSM_HEAD_DIM == jnp.arange(LANES)[:, None]).astype(BF16)

    def row(a):
        return a[:, None, :]

    return {
        "w_in": w_r, "g_mix": row(g_mix), "sinks": sinks,
        "conv_w": 0.5 * conv_w, "conv_b": row(0.5 * conv_b),
        "dt_bias": row(jnp.concatenate([dt_bias, pad], axis=1)), "a_log": row(jnp.concatenate([a_log, pad], axis=1)),
        "d_skip": row(jnp.repeat(d_skip, SSM_HEAD_DIM, axis=1)), "ssm_norm_w": row(ssm_norm_w),
        "gm_ln_g": row(gm_ln_g), "gm_ln_b": row(gm_ln_b),
        "gm_w_s": gm_w_s, "gm_b_st": jnp.swapaxes(gm_b_s, 1, 2),
        "w_attn_o": (0.5 * w_attn_o).astype(BF16), "w_ssm_o": (0.5 * w_ssm_o).astype(BF16),
        "w_gm_o": (0.5 * w_gm_o).astype(BF16), "w_out": w_out.astype(BF16), "expand": expand,
        "g_ff": row(g_ff), "w_ff1": w_ff1.astype(BF16), "w_ff2": w_ff2.astype(BF16),
    }


def kernel(x_prompt, x_sample, c_prompt, c_sample, cache_attn_k, cache_attn_v, state_ssm, state_conv, w_ada, b_ada, g_mix, w_in, sinks, conv_w, conv_b, dt_bias, a_log, d_skip, ssm_norm_w, gm_ln_g, gm_ln_b, gm_w_s, gm_b_s, w_attn_o, w_ssm_o, w_gm_o, w_out, g_ff, w_ff1, w_ff2, g_final):
    bp, sp, d = x_prompt.shape
    bs, ss, _ = x_sample.shape
    mod_p, mod_s = _ada_call(c_prompt, c_sample, w_ada, b_ada)
    rope_p = _rope_tables(jnp.arange(sp))
    rope_s = tuple(jnp.tile(a, (SAMPLE_NB, 1)) for a in _rope_tables(PAST_LEN + jnp.arange(ss)))
    g_fin = g_final[None]
    p = _stacked_params(w_in, g_mix, sinks, conv_w, conv_b, dt_bias, a_log, d_skip, ssm_norm_w, gm_ln_g,
                        gm_ln_b, gm_w_s, gm_b_s, w_attn_o, w_ssm_o, w_gm_o, w_out, g_ff, w_ff1, w_ff2)
    ck = jnp.transpose(cache_attn_k, (0, 1, 3, 4, 2))
    cv = jnp.transpose(cache_attn_v, (0, 1, 3, 4, 2))
    h0 = state_ssm.reshape(DEPTH, bs, SSM_INNER, SSM_STATE)
    xp, xs = x_prompt, x_sample
    state_p = state_s = None
    for l in range(DEPTH):
        final = l == DEPTH - 1
        xp, *state_p = _prompt_mixer(l, xp, mod_p, p, rope_p, state_p)
        xs, *state_s = _sample_mixer(l, xs, mod_s, ck, cv, h0, state_conv, p, rope_s, state_s)
        xp, xs = _ffn(l, xp, mod_p, xs, mod_s, p, g_fin, final)
    kp, vp, ssm_p, conv_p = state_p
    ks, vs, ssm_s, conv_s, gv_s = state_s
    return (xp, xs,
            kp.reshape(DEPTH, bp, WINDOW, N_KV_HEADS, HEAD_DIM), vp.reshape(DEPTH, bp, WINDOW, N_KV_HEADS, HEAD_DIM),
            ssm_p.reshape(DEPTH, bp, SSM_HEADS, SSM_HEAD_DIM, SSM_STATE), conv_p,
            ks.reshape(DEPTH, bs, ss, N_KV_HEADS, HEAD_DIM), vs.reshape(DEPTH, bs, ss, N_KV_HEADS, HEAD_DIM),
            ssm_s.reshape(DEPTH, bs, SSM_HEADS, SSM_HEAD_DIM, SSM_STATE), conv_s, gv_s)
```

```python
import functools

import jax
import jax.numpy as jnp
from jax import lax
from jax.experimental import pallas as pl
from jax.experimental.pallas import tpu as pltpu

F32 = jnp.float32
BF16 = jnp.bfloat16

D_MODEL = 1024
DEPTH = 2
CHUNK = 64
N_HEADS = 8
N_KV_HEADS = 2
HEAD_DIM = 64
GQA_GROUP = N_HEADS // N_KV_HEADS
ROT_DIM = HEAD_DIM // 4
ROPE_THETA = 500000.0
WINDOW = 128
SSM_HEADS = 16
SSM_HEAD_DIM = 64
SSM_INNER = SSM_HEADS * SSM_HEAD_DIM
SSM_GROUPS = 2
SSM_STATE = 128
SSM_CHUNK = 64
CONV_WIDTH = 4
CONV_DIM = SSM_INNER + 2 * SSM_GROUPS * SSM_STATE
GM_WIDTH = 512
GM_GROUPS = 4
GM_GROUP_DIM = GM_WIDTH // GM_GROUPS
GM_CHUNK = 128
D_FF = 4 * D_MODEL
Q_W = N_HEADS * HEAD_DIM
KV_W = N_KV_HEADS * HEAD_DIM
PAST_LEN = 4096
EPS = 1e-6

LANES = 128
MXU_TILE = 256
HEADS_PER_GROUP = SSM_HEADS // SSM_GROUPS
GROUP_INNER = HEADS_PER_GROUP * SSM_HEAD_DIM

C_Q = (0, 0, Q_W)
C_K = (0, C_Q[2], C_Q[2] + KV_W)
C_V = (0, C_K[2], C_K[2] + KV_W)
C_Z = (0, C_V[2], C_V[2] + SSM_INNER)
C_XBC = (0, C_Z[2], C_Z[2] + CONV_DIM)
N_IN_A = C_XBC[2]
C_GU = (1, 0, GM_WIDTH)
C_GV = (1, C_GU[2], C_GU[2] + GM_WIDTH)
C_G0 = (1, C_GV[2], C_GV[2] + D_MODEL)
C_G1 = (1, C_G0[2], C_G0[2] + D_MODEL)
C_G2 = (1, C_G1[2], C_G1[2] + D_MODEL)
N_IN_B = C_G2[2]
C_DT = (2, 0, LANES)

PROMPT_TS = 512
PROJ_BLOCK = MXU_TILE
SAMPLE_NB = 8
FFN_ROWS = 512
W_SPLIT_BLOCKS = 2
V7X_VMEM_BYTES = 64 * 1024 * 1024
VMEM_LIMIT = V7X_VMEM_BYTES * 7 // 8


def _log2(n):
    assert n & (n - 1) == 0, n
    return n.bit_length() - 1


def _dot(a, b):
    return jnp.dot(a, b, preferred_element_type=F32)


def _dot_nt(a, b):
    return lax.dot_general(a, b, (((1,), (1,)), ((), ())), preferred_element_type=F32)


def _dot_tn(a, b):
    return lax.dot_general(a, b, (((0,), (0,)), ((), ())), preferred_element_type=F32)


def _split3(x):
    hi = x.astype(BF16)
    r = x - hi.astype(F32)
    mid = r.astype(BF16)
    lo = (r - mid.astype(F32)).astype(BF16)
    return hi, mid, lo


def _sigmoid(x):
    return 0.5 * (1.0 + jnp.tanh(0.5 * x))


def _silu(x):
    return x * _sigmoid(x)


def _silu_of_half(xh):
    return xh * (1.0 + jnp.tanh(xh))


def _twice_sigmoid_of_half(xh):
    return 1.0 + jnp.tanh(xh)


def _gelu(x):
    return 0.5 * x * (1.0 + jnp.tanh(0.7978845608028654 * (x + 0.044715 * (x * x * x))))


def _softplus(x):
    return jnp.maximum(x, 0.0) + jnp.log1p(jnp.exp(-jnp.abs(x)))


def _rms(x, g):
    return x * lax.rsqrt(jnp.mean(x * x, axis=-1, keepdims=True) + EPS) * g


def _rope(x, cos, sin_a, sin_b):
    return x * cos + pltpu.roll(x, LANES - ROT_DIM // 2, 1) * sin_a + pltpu.roll(x, ROT_DIM // 2, 1) * sin_b


def _proj(h, win_refs, cols):
    return _dot(h, win_refs[cols[0]][:, cols[1]:cols[2]])


def _modnorm(x3, g, mod3):
    nb, t, d = x3.shape
    gain = g * (1.0 + mod3[:, 1:2, :])
    h3 = x3 * lax.rsqrt(jnp.mean(x3 * x3, axis=-1, keepdims=True) + EPS) * gain + mod3[:, 0:1, :]
    return h3.reshape(nb * t, d).astype(BF16)


def _sink_row(sinks_ref, kv, cols_per_head):
    c = lax.broadcasted_iota(jnp.int32, (1, GQA_GROUP * cols_per_head), 1)
    row = jnp.full((1, GQA_GROUP * cols_per_head), sinks_ref[kv * GQA_GROUP], F32)
    for i in range(1, GQA_GROUP):
        row = jnp.where(c >= i * cols_per_head, sinks_ref[kv * GQA_GROUP + i], row)
    return row


def _stack_heads(q, kv, rows):
    return jnp.concatenate(
        [q[:, (kv * GQA_GROUP + i) * HEAD_DIM:(kv * GQA_GROUP + i + 1) * HEAD_DIM] for i in range(GQA_GROUP)], axis=0)


def _unstack_heads(o, rows):
    return jnp.concatenate([o[i * rows:(i + 1) * rows, :] for i in range(GQA_GROUP)], axis=1)


def _gmlp_chunk(vn, u, gws_ref, gbst_ref, L):
    ri = lax.broadcasted_iota(jnp.int32, (L, L), 0)
    ci = lax.broadcasted_iota(jnp.int32, (L, L), 1)
    outs = []
    for g in range(GM_GROUPS):
        w = jnp.where(ri >= ci, gws_ref[g, 0:L, 0:L], 0.0).astype(BF16)
        v_g = vn[:, g * GM_GROUP_DIM:(g + 1) * GM_GROUP_DIM].astype(BF16)
        outs.append(_dot(w, v_g) + gbst_ref[0:L, g:g + 1])
    return u * jnp.concatenate(outs, axis=1)


def _layer_norm(x, g, b):
    xc = x - jnp.mean(x, axis=-1, keepdims=True)
    return xc * lax.rsqrt(jnp.mean(xc * xc, axis=-1, keepdims=True) + EPS) * g + b


def _merge_out(x3, mod3, gates, a, b, c, wout_ref):
    merged = gates[0] * a + gates[1] * b + gates[2] * c
    o = _dot(merged.astype(BF16), wout_ref[...])
    nb, t, d = x3.shape
    return x3 + mod3[:, 2:3, :] * o.reshape(nb, t, d)


def _attn_groups(ts):
    return [(c, kv) for c in range(ts // CHUNK) for kv in range(N_KV_HEADS)]


def _attn_scores(q, khist, pos0):
    ts = q.shape[0]
    n_keys = WINDOW + CHUNK
    key_i = lax.broadcasted_iota(jnp.int32, (n_keys, GQA_GROUP * CHUNK), 0)
    scores = []
    for c, kv in _attn_groups(ts):
        k_g = khist[c * CHUNK:c * CHUNK + n_keys, kv * HEAD_DIM:(kv + 1) * HEAD_DIM].astype(BF16)
        sc = _dot_nt(k_g, _stack_heads(q[c * CHUNK:(c + 1) * CHUNK, :], kv, CHUNK))
        if c * CHUNK < WINDOW:
            sc = jnp.where(key_i >= WINDOW - c * CHUNK - pos0, sc, -jnp.inf)
        scores.append(sc)
    return scores


def _attn_probs(scores, sinks_ref, ts):
    sink_rows = [_sink_row(sinks_ref, kv, CHUNK) for kv in range(N_KV_HEADS)]
    probs = []
    for (c, kv), sc in zip(_attn_groups(ts), scores):
        m = jnp.maximum(jnp.max(sc, axis=0, keepdims=True), sink_rows[kv])
        p = jnp.exp(sc - m)
        denom = jnp.sum(p, axis=0, keepdims=True) + jnp.exp(sink_rows[kv] - m)
        probs.append((p * (1.0 / denom)).astype(BF16))
    return probs


def _attn_values(probs, vhist, ts):
    n_keys = WINDOW + CHUNK
    outs = []
    for (c, kv), p in zip(_attn_groups(ts), probs):
        v_g = vhist[c * CHUNK:c * CHUNK + n_keys, kv * HEAD_DIM:(kv + 1) * HEAD_DIM].astype(BF16)
        outs.append(_unstack_heads(_dot_tn(p, v_g), CHUNK))
    rows = [jnp.concatenate(outs[c * N_KV_HEADS:(c + 1) * N_KV_HEADS], axis=1) for c in range(ts // CHUNK)]
    return jnp.concatenate(rows, axis=0).astype(BF16)


def _ssd_cumsum(dt, a_row, e):
    ts = dt.shape[0]
    L = SSM_CHUNK
    span = min(ts, MXU_TILE)
    ri = lax.broadcasted_iota(jnp.int32, (span, span), 0)
    lag = ri - lax.broadcasted_iota(jnp.int32, (span, span), 1)
    tri = jnp.where((lag >= 0) & (lag <= (ri & (L - 1))), 1.0, 0.0).astype(BF16)
    hi, mid, lo = _split3(dt * a_row)
    cum = jnp.concatenate(
        [_dot(tri, hi[r:r + span]) + _dot(tri, mid[r:r + span]) + _dot(tri, lo[r:r + span])
         for r in range(0, ts, span)], axis=0)
    hi = cum.astype(BF16)
    mid = (cum - hi.astype(F32)).astype(BF16)
    cum_e = _dot(hi, e) + _dot(mid, e)
    dt_e = _dot(dt.astype(BF16), e)
    return cum_e, dt_e


def _ssd_decays(xact, cum_e, dt_e):
    ts = xact.shape[0]
    L = SSM_CHUNK
    nc = ts // L
    xs = xact[:, 0:SSM_INNER]
    cum_e = cum_e.reshape(nc, L, SSM_INNER)
    t3 = lax.broadcasted_iota(jnp.int32, cum_e.shape, 1)
    slot3 = lax.broadcasted_iota(jnp.int32, cum_e.shape, 2) & (SSM_HEAD_DIM - 1)
    cum_src = jnp.sum(jnp.where(t3 == slot3, cum_e, 0.0), axis=1, keepdims=True)
    decay = jnp.exp(jnp.where(t3 >= slot3, cum_e - cum_src, -jnp.inf))
    last = cum_e[:, L - 1:L, :]
    exp_cum = jnp.exp(cum_e)
    chunk_decay = jnp.exp(last)
    xdt = xs * dt_e
    wx = (xdt.reshape(nc, L, SSM_INNER) * jnp.exp(last - cum_e)).astype(BF16)
    first_head = (lax.broadcasted_iota(jnp.int32, xdt.shape, 1) & (2 * SSM_HEAD_DIM - 1)) < SSM_HEAD_DIM
    xdt_b16 = xdt.astype(BF16)
    return {
        "xs": xs,
        "bm": xact[:, SSM_INNER:SSM_INNER + SSM_GROUPS * SSM_STATE].astype(BF16),
        "cm": xact[:, SSM_INNER + SSM_GROUPS * SSM_STATE:CONV_DIM].astype(BF16),
        "decay": decay, "exp_cum": exp_cum, "chunk_decay": chunk_decay, "wx": wx,
        "xdt_a": jnp.where(first_head, xdt_b16, jnp.zeros_like(xdt_b16)),
        "xdt_b": jnp.where(first_head, jnp.zeros_like(xdt_b16), xdt_b16),
    }


def _ssd_block(ctx, st, dskip_e, fillers):
    L = SSM_CHUNK
    nc = ctx["xs"].shape[0] // L
    bm, cm = ctx["bm"], ctx["cm"]
    fillers = iter(fillers)

    def grp(a, c, g, width):
        return a[c * L:(c + 1) * L, g * width:(g + 1) * width]

    chunk_state = [jnp.concatenate([_dot_tn(grp(bm, c, g, SSM_STATE), ctx["wx"][c][:, g * GROUP_INNER:(g + 1) * GROUP_INNER])
                                    for g in range(SSM_GROUPS)], axis=1) for c in range(nc)]
    next(fillers)()
    cb = [[_dot_nt(grp(cm, c, g, SSM_STATE), jnp.concatenate([grp(bm, c, g, SSM_STATE)] * 2, axis=0))
           for g in range(SSM_GROUPS)] for c in range(nc)]
    next(fillers)()
    st_in = []
    for c in range(nc):
        st_in.append(st.astype(BF16))
        st = st * ctx["chunk_decay"][c] + chunk_state[c]
    y_intra = []
    for c in range(nc):
        rows = slice(c * L, (c + 1) * L)
        parts = []
        for k in range(SSM_HEADS // 2):
            ls = slice(k * 2 * SSM_HEAD_DIM, (k + 1) * 2 * SSM_HEAD_DIM)
            m_pair = (cb[c][2 * k // HEADS_PER_GROUP] * ctx["decay"][c][:, ls]).astype(BF16)
            block_diag = jnp.concatenate([ctx["xdt_a"][rows, ls], ctx["xdt_b"][rows, ls]], axis=0)
            parts.append(_dot(m_pair, block_diag))
        y_intra.append(jnp.concatenate(parts, axis=1))
    next(fillers)()
    ys = []
    for c in range(nc):
        y_inter = jnp.concatenate(
            [_dot(grp(cm, c, g, SSM_STATE), st_in[c][:, g * GROUP_INNER:(g + 1) * GROUP_INNER])
             for g in range(SSM_GROUPS)], axis=1)
        ys.append(y_intra[c] + y_inter * ctx["exp_cum"][c] + dskip_e * ctx["xs"][c * L:(c + 1) * L])
    next(fillers)()
    return jnp.concatenate(ys, axis=0), st


class _Stagger:
    def __init__(self):
        self._pending = None

    def push(self, produce, consume):
        val = produce()
        self.flush()
        self._pending = (consume, val)

    def flush(self):
        if self._pending is not None:
            consume, val = self._pending
            self._pending = None
            consume(val)


def _prompt_kernel(x_ref, mod_ref, gmix_ref, wa_ref, wb_ref, wdt_ref, cos_ref, sa_ref, sb_ref, sinks_ref,
                   convw_ref, convb_ref, dtb_ref, alog_ref, dskip_ref, ssmnw_ref, glng_ref, glnb_ref,
                   gws_ref, gbst_ref, wao_ref, wso_ref, wgo_ref, wout_ref, e_ref,
                   xo_ref, ko_ref, vo_ref, sto_ref, cvo_ref,
                   khist, vhist, xp, st_s):
    ts = PROMPT_TS
    s = pl.program_id(1)
    last = pl.num_programs(1) - 1

    @pl.when(s == 0)
    def _():
        khist[0:WINDOW, :] = jnp.zeros((WINDOW, KV_W), F32)
        vhist[0:WINDOW, :] = jnp.zeros((WINDOW, KV_W), F32)
        xp[...] = jnp.zeros_like(xp)
        st_s[...] = jnp.zeros_like(st_s)

    x3 = x_ref[...]
    mod3 = mod_ref[...]
    half_rows = ts // 2
    h_top = _modnorm(x3[:, 0:half_rows], gmix_ref[...], mod3)
    q0_top = _dot(h_top, wa_ref[:, C_Q[1]:C_Q[1] + PROJ_BLOCK])
    h_bot = _modnorm(x3[:, half_rows:ts], gmix_ref[...], mod3)
    q0 = jnp.concatenate([q0_top, _dot(h_bot, wa_ref[:, C_Q[1]:C_Q[1] + PROJ_BLOCK])], axis=0)
    h = jnp.concatenate([h_top, h_bot], axis=0)

    cos, sa, sb = cos_ref[...], sa_ref[...], sb_ref[...]
    res = {}

    def rope_block(raw):
        return [_rope(raw[:, i * LANES:(i + 1) * LANES], cos, sa, sb) for i in range(raw.shape[1] // LANES)]

    def q_block(raw):
        res.setdefault("q", []).extend(p.astype(BF16) for p in rope_block(raw * (HEAD_DIM ** -0.5)))

    def kv_block(raw):
        khist[WINDOW:WINDOW + ts, :] = rope_block(raw[:, 0:KV_W])[0]
        vhist[WINDOW:WINDOW + ts, :] = raw[:, KV_W:2 * KV_W]

    def conv_block(lo, raw):
        cs = slice(lo, lo + PROJ_BLOCK)
        ext = jnp.concatenate([xp[:, cs], raw], axis=0)
        acc = convb_ref[:, cs] + raw * convw_ref[CONV_WIDTH - 1:CONV_WIDTH, cs]
        for j in range(1, CONV_WIDTH):
            acc = acc + pltpu.roll(ext, j, 0)[8:8 + ts] * convw_ref[CONV_WIDTH - 1 - j:CONV_WIDTH - j, cs]
        xp[:, cs] = raw[ts - 8:ts]
        res.setdefault("xact", []).append(_silu_of_half(acc))

    def collect(key, fn):
        return lambda raw: res.setdefault(key, []).append(fn(raw))

    wrefs = (wa_ref, wb_ref, wdt_ref)
    pipe = _Stagger()

    def push_proj(cols, off, width, consume):
        ref, lo = wrefs[cols[0]], cols[1] + off
        pipe.push(lambda: _dot(h, ref[:, lo:lo + width]), consume)

    def blocks(cols, consume):
        return [functools.partial(push_proj, cols, off, PROJ_BLOCK, consume)
                for off in range(0, cols[2] - cols[1], PROJ_BLOCK)]

    conv_blocks = [functools.partial(push_proj, C_XBC, off, PROJ_BLOCK, functools.partial(conv_block, off))
                   for off in range(0, CONV_DIM, PROJ_BLOCK)]
    gate_blocks = iter(blocks(C_G0, collect("g0", _twice_sigmoid_of_half))
                       + blocks(C_G1, collect("g1", _twice_sigmoid_of_half))
                       + blocks(C_G2, collect("g2", _twice_sigmoid_of_half)))
    z_blocks = blocks(C_Z, collect("sz", _silu_of_half))
    gm_blocks = blocks(C_GU, collect("u", _gelu)) + blocks(C_GV, collect("gv", _gelu))

    def cat(key):
        return jnp.concatenate(res[key], axis=1)

    pipe.push(lambda: q0, q_block)
    for blk in blocks(C_Q, q_block)[1:]:
        blk()
    push_proj(C_K, 0, 2 * KV_W, kv_block)
    conv_blocks[0]()
    next(gate_blocks)()
    conv_blocks[1]()
    scores = _attn_scores(cat("q"), khist, s * ts)
    conv_blocks[2]()
    probs = _attn_probs(scores, sinks_ref, ts)
    next(gate_blocks)()
    conv_blocks[3]()
    attn = _attn_values(probs, vhist, ts)
    conv_blocks[4]()
    next(gate_blocks)()
    conv_blocks[5]()
    push_proj(C_DT, 0, LANES, collect("dt", lambda raw: _softplus(raw + dtb_ref[...])))
    next(gate_blocks)()
    a_out = _dot(attn, wao_ref[...])
    z_blocks[0]()
    cum_e, dt_e = _ssd_cumsum(res["dt"][0], -jnp.exp(alog_ref[...]), e_ref[...])
    z_blocks[1]()
    merged = cat("g0") * a_out
    ctx = _ssd_decays(cat("xact"), cum_e, dt_e)
    z_blocks[2]()
    z_blocks[3]()
    y, st_s[...] = _ssd_block(ctx, st_s[...], dskip_ref[...], gm_blocks)
    next(gate_blocks)()
    vn = _layer_norm(cat("gv"), glng_ref[...], glnb_ref[...])
    u = cat("u")
    next(gate_blocks)()
    ssm = _rms(y * cat("sz"), ssmnw_ref[...]).astype(BF16)
    gm = jnp.concatenate(
        [_gmlp_chunk(vn[c * GM_CHUNK:(c + 1) * GM_CHUNK], u[c * GM_CHUNK:(c + 1) * GM_CHUNK], gws_ref, gbst_ref,
                     GM_CHUNK) for c in range(ts // GM_CHUNK)], axis=0).astype(BF16)
    next(gate_blocks)()
    b_out = _dot(ssm, wso_ref[...])
    next(gate_blocks)()
    pipe.flush()
    merged = merged + cat("g1") * b_out
    c_out = _dot(gm, wgo_ref[...])
    for blk in gate_blocks:
        blk()
    pipe.flush()
    merged = merged + cat("g2") * c_out
    o = _dot(merged.astype(BF16), wout_ref[...])
    xo_ref[...] = x3 + mod3[:, 2:3, :] * o.reshape(x3.shape)

    khist[0:WINDOW, :] = khist[ts:ts + WINDOW, :]
    vhist[0:WINDOW, :] = vhist[ts:ts + WINDOW, :]

    @pl.when(s == last)
    def _():
        ko_ref[0] = khist[ts:ts + WINDOW, :]
        vo_ref[0] = vhist[ts:ts + WINDOW, :]
        cvo_ref[0] = xp[8 - (CONV_WIDTH - 1):8, :]
        sto_ref[0] = st_s[...].T


def _sample_attention(q, k_new, v_new, ck_ref, cv_ref, sinks_ref, nb):
    t = q.shape[0] // nb
    groups = [(i, kv) for i in range(nb) for kv in range(N_KV_HEADS)]

    def new_rows(a, i, kv):
        return a[i * t:(i + 1) * t, kv * HEAD_DIM:(kv + 1) * HEAD_DIM].astype(BF16)

    def cached(ref, i, kv):
        return ref[i, kv].T.astype(BF16)

    qg = [_stack_heads(q[i * t:(i + 1) * t], kv, t) for i, kv in groups]
    s_c = [_dot_nt(cached(ck_ref, i, kv), qg[n]) for n, (i, kv) in enumerate(groups)]
    s_n = [_dot_nt(new_rows(k_new, i, kv), qg[n]) for n, (i, kv) in enumerate(groups)]
    sinks = [_sink_row(sinks_ref, kv, t) for kv in range(N_KV_HEADS)]
    probs = []
    for n, (i, kv) in enumerate(groups):
        m = jnp.maximum(jnp.maximum(jnp.max(s_c[n], axis=0, keepdims=True), jnp.max(s_n[n], axis=0, keepdims=True)),
                        sinks[kv])
        p_c = jnp.exp(s_c[n] - m)
        p_n = jnp.exp(s_n[n] - m)
        inv = 1.0 / (jnp.sum(p_c, axis=0, keepdims=True) + jnp.sum(p_n, axis=0, keepdims=True)
                     + jnp.exp(sinks[kv] - m))
        probs.append(((p_c * inv).astype(BF16), (p_n * inv).astype(BF16)))
    outs = [_dot_tn(probs[n][0], cached(cv_ref, i, kv)) + _dot_tn(probs[n][1], new_rows(v_new, i, kv))
            for n, (i, kv) in enumerate(groups)]
    rows = [jnp.concatenate([_unstack_heads(outs[i * N_KV_HEADS + kv], t) for kv in range(N_KV_HEADS)], axis=1)
            for i in range(nb)]
    return jnp.concatenate(rows, axis=0)


def _sample_ssd_setup(xact, dt, a_row, e, L):
    m = xact.shape[0]
    n_seq = m // L
    xs = xact[:, 0:SSM_INNER]
    ri = lax.broadcasted_iota(jnp.int32, (m, m), 0)
    lag = ri - lax.broadcasted_iota(jnp.int32, (m, m), 1)
    tri = jnp.where((lag >= 0) & (lag <= (ri & (L - 1))), 1.0, 0.0).astype(BF16)
    hi, mid, lo = _split3(dt * a_row)
    cum = _dot(tri, hi) + _dot(tri, mid) + _dot(tri, lo)
    hi = cum.astype(BF16)
    mid = (cum - hi.astype(F32)).astype(BF16)
    cum_e = (_dot(hi, e) + _dot(mid, e)).reshape(n_seq, L, SSM_INNER)
    head_of_lane = jnp.right_shift(lax.broadcasted_iota(jnp.int32, (LANES, SSM_HEADS * L), 1), _log2(L))
    e_slots = jnp.where(head_of_lane == lax.broadcasted_iota(jnp.int32, (LANES, SSM_HEADS * L), 0), 1.0, 0.0)
    e_slots = e_slots.astype(BF16)
    cum_s = (_dot(hi, e_slots) + _dot(mid, e_slots)).reshape(n_seq, L, SSM_HEADS * L)
    dt_e = _dot(dt.astype(BF16), e)
    t3 = lax.broadcasted_iota(jnp.int32, cum_s.shape, 1)
    slot3 = lax.broadcasted_iota(jnp.int32, cum_s.shape, 2) & (L - 1)
    cum_src = jnp.sum(jnp.where(t3 == slot3, cum_s, 0.0), axis=1, keepdims=True)
    decay = jnp.exp(jnp.where(t3 >= slot3, cum_s - cum_src, -jnp.inf))
    last = cum_e[:, L - 1:L, :]
    xdt = xs * dt_e
    wx = (xdt.reshape(n_seq, L, SSM_INNER) * jnp.exp(last - cum_e)).astype(BF16)
    per_head = jnp.exp(cum.reshape(n_seq, L, LANES)[:, L - 1:L, :])
    rep = jnp.broadcast_to(per_head, (n_seq, SSM_HEADS, LANES)).reshape(n_seq * SSM_HEADS, LANES)
    own_lane = (lax.broadcasted_iota(jnp.int32, rep.shape, 1)
                == (lax.broadcasted_iota(jnp.int32, rep.shape, 0) & (SSM_HEADS - 1)))
    hi, mid, lo = _split3(jnp.where(own_lane, rep, 0.0))
    ones = jnp.ones((LANES, LANES), BF16)
    splat = (_dot(hi, ones) + _dot(mid, ones) + _dot(lo, ones)).reshape(n_seq, SSM_HEADS, LANES)
    return {
        "L": L, "xs": xs,
        "bm": xact[:, SSM_INNER:SSM_INNER + SSM_GROUPS * SSM_STATE].astype(BF16),
        "cm": xact[:, SSM_INNER + SSM_GROUPS * SSM_STATE:CONV_DIM].astype(BF16),
        "decay": decay, "exp_cum": jnp.exp(cum_e), "wx": wx, "xdt": xdt.astype(BF16), "state_decay": splat,
    }


def _sample_ssd_block(s, h0_ref, sto_ref, dskip_e, nb):
    L = s["L"]
    pairs = [(i, g) for i in range(nb) for g in range(SSM_GROUPS)]
    ri = jnp.right_shift(lax.broadcasted_iota(jnp.int32, (HEADS_PER_GROUP * L, GROUP_INNER), 0), _log2(L))
    li = jnp.right_shift(lax.broadcasted_iota(jnp.int32, (HEADS_PER_GROUP * L, GROUP_INNER), 1), _log2(SSM_HEAD_DIM))

    def rows(a, i, g, width):
        return a[i * L:(i + 1) * L, g * width:(g + 1) * width]

    cb = [_dot_nt(rows(s["cm"], i, g, SSM_STATE), jnp.concatenate([rows(s["bm"], i, g, SSM_STATE)] * HEADS_PER_GROUP,
                                                                  axis=0)) for i, g in pairs]
    upd = [_dot_tn(s["wx"][i][:, g * GROUP_INNER:(g + 1) * GROUP_INNER], rows(s["bm"], i, g, SSM_STATE))
           for i, g in pairs]
    y_inter = [_dot_nt(rows(s["cm"], i, g, SSM_STATE), h0_ref[i][g * GROUP_INNER:(g + 1) * GROUP_INNER, :].astype(BF16))
               for i, g in pairs]
    y_intra = []
    for n, (i, g) in enumerate(pairs):
        gs = slice(g * HEADS_PER_GROUP * L, (g + 1) * HEADS_PER_GROUP * L)
        m_g = (cb[n] * s["decay"][i][:, gs]).astype(BF16)
        tiled = jnp.concatenate([rows(s["xdt"], i, g, GROUP_INNER)] * HEADS_PER_GROUP, axis=0)
        y_intra.append(_dot(m_g, jnp.where(ri == li, tiled, jnp.zeros_like(tiled))))
    ys = []
    for i in range(nb):
        parts = [y_intra[i * SSM_GROUPS + g] + y_inter[i * SSM_GROUPS + g]
                 * s["exp_cum"][i][:, g * GROUP_INNER:(g + 1) * GROUP_INNER] for g in range(SSM_GROUPS)]
        ys.append(jnp.concatenate(parts, axis=1) + dskip_e * s["xs"][i * L:(i + 1) * L])
        decayed = h0_ref[i].reshape(SSM_HEADS, SSM_HEAD_DIM, SSM_STATE) * s["state_decay"][i][:, None, :]
        sto_ref[i] = decayed.reshape(SSM_INNER, SSM_STATE) + jnp.concatenate(
            upd[i * SSM_GROUPS:(i + 1) * SSM_GROUPS], axis=0)
    return jnp.concatenate(ys, axis=0)


def _sample_kernel(x_ref, mod_ref, gmix_ref, wa_ref, wb_ref, wdt_ref, cos_ref, sa_ref, sb_ref, sinks_ref,
                   ck_ref, cv_ref, h0_ref, cs_ref,
                   convw_ref, convb_ref, dtb_ref, alog_ref, dskip_ref, ssmnw_ref, glng_ref, glnb_ref,
                   gws_ref, gbst_ref, wao_ref, wso_ref, wgo_ref, wout_ref, e_ref,
                   xo_ref, ko_ref, vo_ref, sto_ref, cvo_ref, gvo_ref,
                   xp):
    nb = SAMPLE_NB
    t = x_ref.shape[1]
    m = nb * t
    win_ref = (wa_ref, wb_ref, wdt_ref)
    x3 = x_ref[...]
    mod3 = mod_ref[...]
    h = _modnorm(x3, gmix_ref[...], mod3)

    cos, sa, sb = cos_ref[...], sa_ref[...], sb_ref[...]
    q = _proj(h, win_ref, C_Q) * (HEAD_DIM ** -0.5)
    q = jnp.concatenate(
        [_rope(q[:, i * LANES:(i + 1) * LANES], cos, sa, sb) for i in range(Q_W // LANES)], axis=1).astype(BF16)
    k_new = _rope(_proj(h, win_ref, C_K), cos, sa, sb)
    v_new = _proj(h, win_ref, C_V)
    ko_ref[...] = k_new.reshape(nb, t, KV_W)
    vo_ref[...] = v_new.reshape(nb, t, KV_W)

    xp[:, 8 - (CONV_WIDTH - 1):8, :] = cs_ref[...]
    xp[:, 8:8 + t, :] = _proj(h, win_ref, C_XBC).reshape(nb, t, CONV_DIM)
    acc = convb_ref[...] + xp[:, 8:8 + t, :] * convw_ref[CONV_WIDTH - 1:CONV_WIDTH, :]
    for j in range(1, CONV_WIDTH):
        acc = acc + xp[:, 8 - j:8 - j + t, :] * convw_ref[CONV_WIDTH - 1 - j:CONV_WIDTH - j, :]
    xact = _silu_of_half(acc).reshape(m, CONV_DIM)
    cvo_ref[...] = xp[:, t + 8 - (CONV_WIDTH - 1):t + 8, :]
    dt = _softplus(_proj(h, win_ref, C_DT) + dtb_ref[...])
    dskip_e = dskip_ref[...]

    u = _gelu(_proj(h, win_ref, C_GU))
    vn = _layer_norm(_gelu(_proj(h, win_ref, C_GV)), glng_ref[...], glnb_ref[...])
    gvo_ref[...] = vn.reshape(nb, t, GM_WIDTH)

    ssd = _sample_ssd_setup(xact, dt, -jnp.exp(alog_ref[...]), e_ref[...], t)
    attn = _sample_attention(q, k_new, v_new, ck_ref, cv_ref, sinks_ref, nb)
    y = _sample_ssd_block(ssd, h0_ref, sto_ref, dskip_e, nb)
    gm_rows = [_gmlp_chunk(vn[i * t:(i + 1) * t], u[i * t:(i + 1) * t], gws_ref, gbst_ref, t) for i in range(nb)]

    a_out = _dot(attn.astype(BF16), wao_ref[...])
    z = _proj(h, win_ref, C_Z)
    ssm = _rms(y * _silu_of_half(z), ssmnw_ref[...]).astype(BF16)
    b_out = _dot(ssm, wso_ref[...])
    c_out = _dot(jnp.concatenate(gm_rows, axis=0).astype(BF16), wgo_ref[...])
    gates = [_twice_sigmoid_of_half(_proj(h, win_ref, cols)) for cols in (C_G0, C_G1, C_G2)]
    xo_ref[...] = _merge_out(x3, mod3, gates, a_out, b_out, c_out, wout_ref)


def _ffn_block(x_ref, mod_ref, gff_ref, w1_ref, w2_ref, gfin_ref, o_ref, final_norm):
    x3 = x_ref[...]
    mod3 = mod_ref[...]
    nb, t, d = x3.shape
    h3 = _rms(x3, gff_ref[...]) * (1.0 + mod3[:, 4:5, :]) + mod3[:, 3:4, :]
    h = h3.reshape(nb * t, d).astype(BF16)
    a = jnp.maximum(_dot(h, w1_ref[...]), 0.0)
    y = _dot((a * a).astype(BF16), w2_ref[...])
    out = x3 + mod3[:, 5:6, :] * y.reshape(nb, t, d)
    if final_norm:
        out = _rms(out, gfin_ref[...])
    o_ref[...] = out


def _ffn_kernel(xp_ref, modp_ref, xs_ref, mods_ref, gff_ref, w1_ref, w2_ref, gfin_ref, op_ref, os_ref, *,
                final_norm, n_prompt_steps):
    i = pl.program_id(0)

    @pl.when(i < n_prompt_steps)
    def _():
        _ffn_block(xp_ref, modp_ref, gff_ref, w1_ref, w2_ref, gfin_ref, op_ref, final_norm)

    @pl.when(i == n_prompt_steps)
    def _():
        _ffn_block(xs_ref, mods_ref, gff_ref, w1_ref, w2_ref, gfin_ref, os_ref, final_norm)


def _ada_kernel(c_ref, w_ref, b_ref, op_ref, os_ref):
    mod = _dot(_silu(c_ref[...]).astype(BF16), w_ref[...].astype(BF16)) + b_ref[...]
    n_prompt = op_ref.shape[0]
    op_ref[...] = mod[0:n_prompt]
    os_ref[...] = mod[n_prompt:]


def _w_in_piece_kernel(wt_ref, o_ref, *, half_lo, half_hi, n_valid):
    cols = wt_ref.shape[0]
    col = pl.program_id(1) * cols + lax.broadcasted_iota(jnp.int32, (cols, 1), 0)
    scale = jnp.where(col < n_valid, jnp.where((col >= half_lo) & (col < half_hi), 0.5, 1.0), 0.0)
    o_ref[...] = (wt_ref[...] * scale).astype(BF16).T


def _w_in_piece(w_t, first_col, n_cols, half_cols, n_blocks, n_valid=None):
    cols = n_cols // n_blocks
    n_valid = n_cols if n_valid is None else n_valid
    return pl.pallas_call(
        functools.partial(_w_in_piece_kernel, half_lo=half_cols[0], half_hi=half_cols[1], n_valid=n_valid),
        grid=(DEPTH, n_blocks),
        in_specs=[pl.BlockSpec((pl.Squeezed(), pl.Element(cols), pl.Element(D_MODEL)),
                               lambda l, j: (l, pl.multiple_of(first_col + j * cols, SSM_HEADS), 0))],
        out_specs=pl.BlockSpec((None, D_MODEL, cols), lambda l, j: (l, 0, j)),
        out_shape=jax.ShapeDtypeStruct((DEPTH, D_MODEL, n_cols), BF16),
        compiler_params=_params(2),
        name="w_in_piece",
    )(w_t)


def _w_in_split(w_in):
    w_t = jnp.swapaxes(w_in, 1, 2)
    b_lo = N_IN_A + SSM_HEADS
    return (_w_in_piece(w_t, 0, N_IN_A, (C_Z[1], C_Z[2]), W_SPLIT_BLOCKS),
            _w_in_piece(w_t, b_lo, N_IN_B, (C_G0[1], N_IN_B), W_SPLIT_BLOCKS),
            _w_in_piece(w_t, N_IN_A, LANES, (0, 0), 1, n_valid=SSM_HEADS))


def _with_ignored_inputs(body, n_in, n_ignored):
    if n_ignored == 0:
        return body
    return lambda *refs: body(*refs[:n_in], *refs[n_in + n_ignored:])


def _const_spec(shape):
    return pl.BlockSpec(shape, lambda *_: (0,) * len(shape), pipeline_mode=pl.Buffered(1))


def _layer_spec(l, shape):
    return pl.BlockSpec((None,) + shape, lambda *_: (l,) + (0,) * len(shape), pipeline_mode=pl.Buffered(1))


def _any_spec():
    return pl.BlockSpec(memory_space=pl.ANY)


def _smem_spec():
    return pl.BlockSpec(memory_space=pltpu.SMEM)


def _params(n_grid):
    return pltpu.CompilerParams(dimension_semantics=("arbitrary",) * n_grid, vmem_limit_bytes=VMEM_LIMIT)


def _ada_call(c_prompt, c_sample, w_ada, b_ada):
    n_p, n_s = c_prompt.shape[0], c_sample.shape[0]
    c_all = jnp.concatenate([c_prompt, c_sample], axis=0)
    tn = 2 * D_MODEL
    n_tiles = w_ada.shape[2] // tn
    mod_p, mod_s = pl.pallas_call(
        _ada_kernel,
        grid=(DEPTH, n_tiles),
        in_specs=[pl.BlockSpec((n_p + n_s, D_MODEL), lambda l, n: (0, 0)),
                  pl.BlockSpec((None, D_MODEL, tn), lambda l, n: (l, 0, n)),
                  pl.BlockSpec((None, 1, tn), lambda l, n: (l, 0, n))],
        out_specs=(pl.BlockSpec((None, n_p, tn), lambda l, n: (l, 0, n)),
                   pl.BlockSpec((None, n_s, tn), lambda l, n: (l, 0, n))),
        out_shape=(jax.ShapeDtypeStruct((DEPTH, n_p, w_ada.shape[2]), F32),
                   jax.ShapeDtypeStruct((DEPTH, n_s, w_ada.shape[2]), F32)),
        compiler_params=_params(2),
        name="ada_mod",
    )(c_all, w_ada, b_ada.reshape(DEPTH, 1, -1))
    return mod_p.reshape(DEPTH, n_p, 6, D_MODEL), mod_s.reshape(DEPTH, n_s, 6, D_MODEL)


def _w_in_specs(l):
    return [_layer_spec(l, (D_MODEL, N_IN_A)), _layer_spec(l, (D_MODEL, N_IN_B)), _layer_spec(l, (D_MODEL, LANES))]


def _layer_weight_specs(l):
    return [
        _layer_spec(l, (CONV_WIDTH, CONV_DIM)), _layer_spec(l, (1, CONV_DIM)), _layer_spec(l, (1, LANES)),
        _layer_spec(l, (1, LANES)), _layer_spec(l, (1, SSM_INNER)), _layer_spec(l, (1, SSM_INNER)),
        _layer_spec(l, (1, GM_WIDTH)), _layer_spec(l, (1, GM_WIDTH)),
        _layer_spec(l, (GM_GROUPS, GM_CHUNK, GM_CHUNK)), _layer_spec(l, (GM_CHUNK, GM_GROUPS)),
        _layer_spec(l, (Q_W, D_MODEL)), _layer_spec(l, (SSM_INNER, D_MODEL)), _layer_spec(l, (GM_WIDTH, D_MODEL)),
        _layer_spec(l, (D_MODEL, D_MODEL)), _const_spec((LANES, SSM_INNER)),
    ]


def _layer_weight_args(p):
    return (p["conv_w"], p["conv_b"], p["dt_bias"], p["a_log"], p["d_skip"], p["ssm_norm_w"], p["gm_ln_g"],
            p["gm_ln_b"], p["gm_w_s"], p["gm_b_st"], p["w_attn_o"], p["w_ssm_o"], p["w_gm_o"], p["w_out"], p["expand"])


def _prompt_mixer(l, x, mod, p, rope, prev):
    bsz, seq, d = x.shape
    ts = PROMPT_TS
    tab = pl.BlockSpec((ts, LANES), lambda b, s: (s, 0))
    in_specs = [
        pl.BlockSpec((1, ts, d), lambda b, s: (b, s, 0)),
        pl.BlockSpec((None, 1, 6, d), lambda b, s: (l, b, 0, 0)),
        _layer_spec(l, (1, d)), *_w_in_specs(l),
        tab, tab, tab, _smem_spec(),
    ] + _layer_weight_specs(l)
    args = (x, mod, p["g_mix"], *p["w_in"], *rope, p["sinks"][l], *_layer_weight_args(p))
    state_tails = ((WINDOW, KV_W), (WINDOW, KV_W), (SSM_INNER, SSM_STATE), (CONV_WIDTH - 1, CONV_DIM))
    out_shape = (jax.ShapeDtypeStruct((bsz, seq, d), F32),) + tuple(
        jax.ShapeDtypeStruct((DEPTH, bsz) + tail, F32) for tail in state_tails)
    out_specs = (pl.BlockSpec((1, ts, d), lambda b, s: (b, s, 0)),) + tuple(
        pl.BlockSpec((None, 1) + tail, lambda b, s: (l, b, 0, 0)) for tail in state_tails)
    scratch = [
        pltpu.VMEM((WINDOW + ts, KV_W), F32), pltpu.VMEM((WINDOW + ts, KV_W), F32),
        pltpu.VMEM((8, CONV_DIM), F32), pltpu.VMEM((SSM_STATE, SSM_INNER), F32),
    ]
    n_in = len(args)
    prev = () if prev is None else tuple(prev)
    return pl.pallas_call(
        _with_ignored_inputs(_prompt_kernel, n_in, len(prev)),
        grid=(bsz, seq // ts), in_specs=in_specs + [_any_spec()] * len(prev), out_specs=out_specs,
        out_shape=out_shape, scratch_shapes=scratch, compiler_params=_params(2), name="prompt_mixer",
        input_output_aliases={n_in + i: 1 + i for i in range(len(prev))},
    )(*args, *prev)


def _sample_mixer(l, x, mod, cache_k, cache_v, h0, conv_state, p, rope, prev):
    bsz, t, d = x.shape
    nb = SAMPLE_NB
    m = nb * t

    def blk(*tail):
        return pl.BlockSpec((nb,) + tail, lambda i: (i,) + (0,) * len(tail))

    def lblk(*tail):
        return pl.BlockSpec((None, nb) + tail, lambda i: (l, i) + (0,) * len(tail))

    in_specs = [
        blk(t, d), lblk(6, d), _layer_spec(l, (1, d)), *_w_in_specs(l),
        _const_spec((m, LANES)), _const_spec((m, LANES)), _const_spec((m, LANES)), _smem_spec(),
        lblk(N_KV_HEADS, HEAD_DIM, WINDOW), lblk(N_KV_HEADS, HEAD_DIM, WINDOW), lblk(SSM_INNER, SSM_STATE),
        lblk(CONV_WIDTH - 1, CONV_DIM),
    ] + _layer_weight_specs(l)
    args = (x, mod, p["g_mix"], *p["w_in"], *rope, p["sinks"][l], cache_k, cache_v, h0, conv_state,
            *_layer_weight_args(p))
    state_tails = ((t, KV_W), (t, KV_W), (SSM_INNER, SSM_STATE), (CONV_WIDTH - 1, CONV_DIM), (t, GM_WIDTH))
    out_shape = (jax.ShapeDtypeStruct((bsz, t, d), F32),) + tuple(
        jax.ShapeDtypeStruct((DEPTH, bsz) + tail, F32) for tail in state_tails)
    out_specs = (blk(t, d),) + tuple(lblk(*tail) for tail in state_tails)
    scratch = [pltpu.VMEM((nb, 8 + t, CONV_DIM), F32)]
    n_in = len(args)
    prev = () if prev is None else tuple(prev)
    return pl.pallas_call(
        _with_ignored_inputs(_sample_kernel, n_in, len(prev)),
        grid=(bsz // nb,), in_specs=in_specs + [_any_spec()] * len(prev), out_specs=out_specs,
        out_shape=out_shape, scratch_shapes=scratch, compiler_params=_params(1), name="sample_mixer",
        input_output_aliases={n_in + i: 1 + i for i in range(len(prev))},
    )(*args, *prev)


def _ffn(l, x_p, mod_p, x_s, mod_s, p, g_final, final_norm):
    bp, sp, d = x_p.shape
    bs, ss, _ = x_s.shape
    assert bs * ss == FFN_ROWS and sp % FFN_ROWS == 0
    blocks_per_seq = sp // FFN_ROWS
    n_prompt_steps = bp * blocks_per_seq

    def prompt_block(i):
        j = jnp.minimum(i, n_prompt_steps - 1)
        return j // blocks_per_seq, j % blocks_per_seq

    x_spec = pl.BlockSpec((1, FFN_ROWS, d), lambda i: (*prompt_block(i), 0))
    s_spec = pl.BlockSpec((bs, ss, d), lambda i: (0, 0, 0))
    return pl.pallas_call(
        functools.partial(_ffn_kernel, final_norm=final_norm, n_prompt_steps=n_prompt_steps),
        grid=(n_prompt_steps + 1,),
        in_specs=[x_spec, pl.BlockSpec((None, 1, 6, d), lambda i: (l, prompt_block(i)[0], 0, 0)),
                  s_spec, pl.BlockSpec((None, bs, 6, d), lambda i: (l, 0, 0, 0)),
                  _layer_spec(l, (1, d)), _layer_spec(l, (d, D_FF)), _layer_spec(l, (D_FF, d)), _const_spec((1, d))],
        out_specs=(x_spec, s_spec),
        out_shape=(jax.ShapeDtypeStruct(x_p.shape, F32), jax.ShapeDtypeStruct(x_s.shape, F32)),
        compiler_params=_params(1), name="ffn",
    )(x_p, mod_p, x_s, mod_s, p["g_ff"], p["w_ff1"], p["w_ff2"], g_final)


def _rope_tables(pos):
    half = ROT_DIM // 2
    inv_freq = ROPE_THETA ** (-jnp.arange(half, dtype=F32) * (2.0 / ROT_DIM))
    ang = pos.astype(F32)[:, None] * inv_freq[None, :]
    cos, sin = jnp.cos(ang), jnp.sin(ang)
    n = pos.shape[0]
    ones = jnp.ones((n, HEAD_DIM - ROT_DIM), F32)
    zeros = jnp.zeros((n, HEAD_DIM - ROT_DIM), F32)
    zh = jnp.zeros((n, half), F32)
    cos_t = jnp.concatenate([cos, cos, ones], axis=1)
    sin_a = jnp.concatenate([-sin, zh, zeros], axis=1)
    sin_b = jnp.concatenate([zh, sin, zeros], axis=1)
    rep = LANES // HEAD_DIM
    return tuple(jnp.tile(a, (1, rep)) for a in (cos_t, sin_a, sin_b))


def _stacked_params(w_in, g_mix, sinks, conv_w, conv_b, dt_bias, a_log, d_skip, ssm_norm_w, gm_ln_g, gm_ln_b,
                    gm_w_s, gm_b_s, w_attn_o, w_ssm_o, w_gm_o, w_out, g_ff, w_ff1, w_ff2):
    w_r = _w_in_split(w_in)
    pad = jnp.zeros((DEPTH, LANES - SSM_HEADS), F32)
    expand = (jnp.arange(SSM_INNER)[None, :] // SSM_HEAD_DIM == jnp.arange(LANES)[:, None]).astype(BF16)

    def row(a):
        return a[:, None, :]

    return {
        "w_in": w_r, "g_mix": row(g_mix), "sinks": sinks,
        "conv_w": 0.5 * conv_w, "conv_b": row(0.5 * conv_b),
        "dt_bias": row(jnp.concatenate([dt_bias, pad], axis=1)), "a_log": row(jnp.concatenate([a_log, pad], axis=1)),
        "d_skip": row(jnp.repeat(d_skip, SSM_HEAD_DIM, axis=1)), "ssm_norm_w": row(ssm_norm_w),
        "gm_ln_g": row(gm_ln_g), "gm_ln_b": row(gm_ln_b),
        "gm_w_s": gm_w_s, "gm_b_st": jnp.swapaxes(gm_b_s, 1, 2),
        "w_attn_o": (0.5 * w_attn_o).astype(BF16), "w_ssm_o": (0.5 * w_ssm_o).astype(BF16),
        "w_gm_o": (0.5 * w_gm_o).astype(BF16), "w_out": w_out.astype(BF16), "expand": expand,
        "g_ff": row(g_ff), "w_ff1": w_ff1.astype(BF16), "w_ff2": w_ff2.astype(BF16),
    }


def kernel(x_prompt, x_sample, c_prompt, c_sample, cache_attn_k, cache_attn_v, state_ssm, state_conv, w_ada, b_ada, g_mix, w_in, sinks, conv_w, conv_b, dt_bias, a_log, d_skip, ssm_norm_w, gm_ln_g, gm_ln_b, gm_w_s, gm_b_s, w_attn_o, w_ssm_o, w_gm_o, w_out, g_ff, w_ff1, w_ff2, g_final):
    bp, sp, d = x_prompt.shape
    bs, ss, _ = x_sample.shape
    mod_p, mod_s = _ada_call(c_prompt, c_sample, w_ada, b_ada)
    rope_p = _rope_tables(jnp.arange(sp))
    rope_s = tuple(jnp.tile(a, (SAMPLE_NB, 1)) for a in _rope_tables(PAST_LEN + jnp.arange(ss)))
    g_fin = g_final[None]
    p = _stacked_params(w_in, g_mix, sinks, conv_w, conv_b, dt_bias, a_log, d_skip, ssm_norm_w, gm_ln_g,
                        gm_ln_b, gm_w_s, gm_b_s, w_attn_o, w_ssm_o, w_gm_o, w_out, g_ff, w_ff1, w_ff2)
    ck = jnp.transpose(cache_attn_k, (0, 1, 3, 4, 2))
    cv = jnp.transpose(cache_attn_v, (0, 1, 3, 4, 2))
    h0 = state_ssm.reshape(DEPTH, bs, SSM_INNER, SSM_STATE)
    xp, xs = x_prompt, x_sample
    state_p = state_s = None
    for l in range(DEPTH):
        final = l == DEPTH - 1
        xp, *state_p = _prompt_mixer(l, xp, mod_p, p, rope_p, state_p)
        xs, *state_s = _sample_mixer(l, xs, mod_s, ck, cv, h0, state_conv, p, rope_s, state_s)
        xp, xs = _ffn(l, xp, mod_p, xs, mod_s, p, g_fin, final)
    kp, vp, ssm_p, conv_p = state_p
    ks, vs, ssm_s, conv_s, gv_s = state_s
    return (xp, xs,
            kp.reshape(DEPTH, bp, WINDOW, N_KV_HEADS, HEAD_DIM), vp.reshape(DEPTH, bp, WINDOW, N_KV_HEADS, HEAD_DIM),
            ssm_p.reshape(DEPTH, bp, SSM_HEADS, SSM_HEAD_DIM, SSM_STATE), conv_p,
            ks.reshape(DEPTH, bs, ss, N_KV_HEADS, HEAD_DIM), vs.reshape(DEPTH, bs, ss, N_KV_HEADS, HEAD_DIM),
            ssm_s.reshape(DEPTH, bs, SSM_HEADS, SSM_HEAD_DIM, SSM_STATE), conv_s, gv_s)
```

```python
import functools

import jax
import jax.numpy as jnp
from jax import lax
from jax.experimental import pallas as pl
from jax.experimental.pallas import tpu as pltpu

F32 = jnp.float32
BF16 = jnp.bfloat16

D_MODEL = 1024
DEPTH = 2
CHUNK = 64
N_HEADS = 8
N_KV_HEADS = 2
HEAD_DIM = 64
GQA_GROUP = N_HEADS // N_KV_HEADS
ROT_DIM = HEAD_DIM // 4
ROPE_THETA = 500000.0
WINDOW = 128
SSM_HEADS = 16
SSM_HEAD_DIM = 64
SSM_INNER = SSM_HEADS * SSM_HEAD_DIM
SSM_GROUPS = 2
SSM_STATE = 128
SSM_CHUNK = 64
CONV_WIDTH = 4
CONV_DIM = SSM_INNER + 2 * SSM_GROUPS * SSM_STATE
GM_WIDTH = 512
GM_GROUPS = 4
GM_GROUP_DIM = GM_WIDTH // GM_GROUPS
GM_CHUNK = 128
D_FF = 4 * D_MODEL
Q_W = N_HEADS * HEAD_DIM
KV_W = N_KV_HEADS * HEAD_DIM
PAST_LEN = 4096
EPS = 1e-6

LANES = 128
MXU_TILE = 256
HEADS_PER_GROUP = SSM_HEADS // SSM_GROUPS
GROUP_INNER = HEADS_PER_GROUP * SSM_HEAD_DIM

C_Q = (0, 0, Q_W)
C_K = (0, C_Q[2], C_Q[2] + KV_W)
C_V = (0, C_K[2], C_K[2] + KV_W)
C_Z = (0, C_V[2], C_V[2] + SSM_INNER)
C_XBC = (0, C_Z[2], C_Z[2] + CONV_DIM)
N_IN_A = C_XBC[2]
C_GU = (1, 0, GM_WIDTH)
C_GV = (1, C_GU[2], C_GU[2] + GM_WIDTH)
C_G0 = (1, C_GV[2], C_GV[2] + D_MODEL)
C_G1 = (1, C_G0[2], C_G0[2] + D_MODEL)
C_G2 = (1, C_G1[2], C_G1[2] + D_MODEL)
N_IN_B = C_G2[2]
C_DT = (2, 0, LANES)

PROMPT_TS = 512
NORM_SLABS = 4
PROJ_BLOCK = MXU_TILE
SAMPLE_NB = 8
FFN_ROWS = 512
W_SPLIT_BLOCKS = 2
V7X_VMEM_BYTES = 64 * 1024 * 1024
VMEM_LIMIT = V7X_VMEM_BYTES * 7 // 8


def _log2(n):
    assert n & (n - 1) == 0, n
    return n.bit_length() - 1


def _dot(a, b):
    return jnp.dot(a, b, preferred_element_type=F32)


def _dot_nt(a, b):
    return lax.dot_general(a, b, (((1,), (1,)), ((), ())), preferred_element_type=F32)


def _dot_tn(a, b):
    return lax.dot_general(a, b, (((0,), (0,)), ((), ())), preferred_element_type=F32)


def _split3(x):
    hi = x.astype(BF16)
    r = x - hi.astype(F32)
    mid = r.astype(BF16)
    lo = (r - mid.astype(F32)).astype(BF16)
    return hi, mid, lo


def _sigmoid(x):
    return 0.5 * (1.0 + jnp.tanh(0.5 * x))


def _silu(x):
    return x * _sigmoid(x)


def _silu_of_half(xh):
    return xh * (1.0 + jnp.tanh(xh))


def _twice_sigmoid_of_half(xh):
    return 1.0 + jnp.tanh(xh)


def _gelu(x):
    c = 0.7978845608028654
    half_x = 0.5 * x
    return half_x + half_x * jnp.tanh(x * (c + (c * 0.044715) * (x * x)))


def _softplus(x):
    return jnp.maximum(x, 0.0) + jnp.log1p(jnp.exp(-jnp.abs(x)))


def _rms(x, g):
    return x * lax.rsqrt(jnp.mean(x * x, axis=-1, keepdims=True) + EPS) * g


def _rope(x, cos, sin_a, sin_b):
    return x * cos + pltpu.roll(x, LANES - ROT_DIM // 2, 1) * sin_a + pltpu.roll(x, ROT_DIM // 2, 1) * sin_b


def _proj(h, win_refs, cols):
    return _dot(h, win_refs[cols[0]][:, cols[1]:cols[2]])


def _modnorm(x3, g, mod3):
    nb, t, d = x3.shape
    gain = g * (1.0 + mod3[:, 1:2, :])
    h3 = x3 * lax.rsqrt(jnp.mean(x3 * x3, axis=-1, keepdims=True) + EPS) * gain + mod3[:, 0:1, :]
    return h3.reshape(nb * t, d).astype(BF16)


def _sink_row(sinks_ref, kv, cols_per_head):
    c = lax.broadcasted_iota(jnp.int32, (1, GQA_GROUP * cols_per_head), 1)
    row = jnp.full((1, GQA_GROUP * cols_per_head), sinks_ref[kv * GQA_GROUP], F32)
    for i in range(1, GQA_GROUP):
        row = jnp.where(c >= i * cols_per_head, sinks_ref[kv * GQA_GROUP + i], row)
    return row


def _stack_heads(q, kv, rows):
    return jnp.concatenate(
        [q[:, (kv * GQA_GROUP + i) * HEAD_DIM:(kv * GQA_GROUP + i + 1) * HEAD_DIM] for i in range(GQA_GROUP)], axis=0)


def _unstack_heads(o, rows):
    return jnp.concatenate([o[i * rows:(i + 1) * rows, :] for i in range(GQA_GROUP)], axis=1)


def _gmlp_chunk(vn, u, gws_ref, gbst_ref, L):
    ri = lax.broadcasted_iota(jnp.int32, (L, L), 0)
    ci = lax.broadcasted_iota(jnp.int32, (L, L), 1)
    outs = []
    for g in range(GM_GROUPS):
        w = jnp.where(ri >= ci, gws_ref[g, 0:L, 0:L], 0.0).astype(BF16)
        v_g = vn[:, g * GM_GROUP_DIM:(g + 1) * GM_GROUP_DIM].astype(BF16)
        outs.append(_dot(w, v_g) + gbst_ref[0:L, g:g + 1])
    return u * jnp.concatenate(outs, axis=1)


def _layer_norm(x, g, b):
    xc = x - jnp.mean(x, axis=-1, keepdims=True)
    return xc * lax.rsqrt(jnp.mean(xc * xc, axis=-1, keepdims=True) + EPS) * g + b


def _merge_out(x3, mod3, gates, a, b, c, wout_ref):
    merged = gates[0] * a + gates[1] * b + gates[2] * c
    o = _dot(merged.astype(BF16), wout_ref[...])
    nb, t, d = x3.shape
    return x3 + mod3[:, 2:3, :] * o.reshape(nb, t, d)


def _attn_groups(ts):
    return [(c, kv) for c in range(ts // CHUNK) for kv in range(N_KV_HEADS)]


def _attn_scores(q, khist, pos0):
    ts = q.shape[0]
    n_keys = WINDOW + CHUNK
    key_i = lax.broadcasted_iota(jnp.int32, (n_keys, GQA_GROUP * CHUNK), 0)
    scores = []
    for c, kv in _attn_groups(ts):
        k_g = khist[c * CHUNK:c * CHUNK + n_keys, kv * HEAD_DIM:(kv + 1) * HEAD_DIM].astype(BF16)
        sc = _dot_nt(k_g, _stack_heads(q[c * CHUNK:(c + 1) * CHUNK, :], kv, CHUNK))
        if c * CHUNK < WINDOW:
            sc = jnp.where(key_i >= WINDOW - c * CHUNK - pos0, sc, -jnp.inf)
        scores.append(sc)
    return scores


def _attn_probs(scores, sinks_ref, ts):
    sink_rows = [_sink_row(sinks_ref, kv, CHUNK) for kv in range(N_KV_HEADS)]
    probs = []
    for (c, kv), sc in zip(_attn_groups(ts), scores):
        m = jnp.maximum(jnp.max(sc, axis=0, keepdims=True), sink_rows[kv])
        p = jnp.exp(sc - m)
        denom = jnp.sum(p, axis=0, keepdims=True) + jnp.exp(sink_rows[kv] - m)
        probs.append((p * (1.0 / denom)).astype(BF16))
    return probs


def _attn_values(probs, vhist, ts):
    n_keys = WINDOW + CHUNK
    outs = []
    for (c, kv), p in zip(_attn_groups(ts), probs):
        v_g = vhist[c * CHUNK:c * CHUNK + n_keys, kv * HEAD_DIM:(kv + 1) * HEAD_DIM].astype(BF16)
        outs.append(_unstack_heads(_dot_tn(p, v_g), CHUNK))
    rows = [jnp.concatenate(outs[c * N_KV_HEADS:(c + 1) * N_KV_HEADS], axis=1) for c in range(ts // CHUNK)]
    return jnp.concatenate(rows, axis=0).astype(BF16)


def _ssd_cumsum(dt, a_row, e):
    ts = dt.shape[0]
    L = SSM_CHUNK
    span = min(ts, MXU_TILE)
    ri = lax.broadcasted_iota(jnp.int32, (span, span), 0)
    lag = ri - lax.broadcasted_iota(jnp.int32, (span, span), 1)
    tri = jnp.where((lag >= 0) & (lag <= (ri & (L - 1))), 1.0, 0.0).astype(BF16)
    hi, mid, lo = _split3(dt * a_row)
    cum = jnp.concatenate(
        [_dot(tri, hi[r:r + span]) + _dot(tri, mid[r:r + span]) + _dot(tri, lo[r:r + span])
         for r in range(0, ts, span)], axis=0)
    hi = cum.astype(BF16)
    mid = (cum - hi.astype(F32)).astype(BF16)
    cum_e = _dot(hi, e) + _dot(mid, e)
    dt_e = _dot(dt.astype(BF16), e)
    return cum_e, dt_e


def _ssd_decays(xact, cum_e, dt_e):
    ts = xact.shape[0]
    L = SSM_CHUNK
    nc = ts // L
    xs = xact[:, 0:SSM_INNER]
    cum_e = cum_e.reshape(nc, L, SSM_INNER)
    t3 = lax.broadcasted_iota(jnp.int32, cum_e.shape, 1)
    slot3 = lax.broadcasted_iota(jnp.int32, cum_e.shape, 2) & (SSM_HEAD_DIM - 1)
    cum_src = jnp.sum(jnp.where(t3 == slot3, cum_e, 0.0), axis=1, keepdims=True)
    decay = jnp.exp(jnp.where(t3 >= slot3, cum_e - cum_src, -jnp.inf))
    last = cum_e[:, L - 1:L, :]
    exp_cum = jnp.exp(cum_e)
    chunk_decay = jnp.exp(last)
    xdt = xs * dt_e
    wx = (xdt.reshape(nc, L, SSM_INNER) * jnp.exp(last - cum_e)).astype(BF16)
    first_head = (lax.broadcasted_iota(jnp.int32, xdt.shape, 1) & (2 * SSM_HEAD_DIM - 1)) < SSM_HEAD_DIM
    xdt_b16 = xdt.astype(BF16)
    return {
        "xs": xs,
        "bm": xact[:, SSM_INNER:SSM_INNER + SSM_GROUPS * SSM_STATE].astype(BF16),
        "cm": xact[:, SSM_INNER + SSM_GROUPS * SSM_STATE:CONV_DIM].astype(BF16),
        "decay": decay, "exp_cum": exp_cum, "chunk_decay": chunk_decay, "wx": wx,
        "xdt_a": jnp.where(first_head, xdt_b16, jnp.zeros_like(xdt_b16)),
        "xdt_b": jnp.where(first_head, jnp.zeros_like(xdt_b16), xdt_b16),
    }


def _ssd_block(ctx, st, dskip_e, fillers):
    L = SSM_CHUNK
    nc = ctx["xs"].shape[0] // L
    bm, cm = ctx["bm"], ctx["cm"]
    fillers = iter(fillers)

    def grp(a, c, g, width):
        return a[c * L:(c + 1) * L, g * width:(g + 1) * width]

    chunk_state = [jnp.concatenate([_dot_tn(grp(bm, c, g, SSM_STATE), ctx["wx"][c][:, g * GROUP_INNER:(g + 1) * GROUP_INNER])
                                    for g in range(SSM_GROUPS)], axis=1) for c in range(nc)]
    next(fillers)()
    cb = [[_dot_nt(grp(cm, c, g, SSM_STATE), jnp.concatenate([grp(bm, c, g, SSM_STATE)] * 2, axis=0))
           for g in range(SSM_GROUPS)] for c in range(nc)]
    next(fillers)()
    st_in = []
    for c in range(nc):
        st_in.append(st.astype(BF16))
        st = st * ctx["chunk_decay"][c] + chunk_state[c]
    y_intra = []
    for c in range(nc):
        rows = slice(c * L, (c + 1) * L)
        parts = []
        for k in range(SSM_HEADS // 2):
            ls = slice(k * 2 * SSM_HEAD_DIM, (k + 1) * 2 * SSM_HEAD_DIM)
            m_pair = (cb[c][2 * k // HEADS_PER_GROUP] * ctx["decay"][c][:, ls]).astype(BF16)
            block_diag = jnp.concatenate([ctx["xdt_a"][rows, ls], ctx["xdt_b"][rows, ls]], axis=0)
            parts.append(_dot(m_pair, block_diag))
        y_intra.append(jnp.concatenate(parts, axis=1))
    next(fillers)()
    ys = []
    for c in range(nc):
        y_inter = jnp.concatenate(
            [_dot(grp(cm, c, g, SSM_STATE), st_in[c][:, g * GROUP_INNER:(g + 1) * GROUP_INNER])
             for g in range(SSM_GROUPS)], axis=1)
        ys.append(y_intra[c] + y_inter * ctx["exp_cum"][c] + dskip_e * ctx["xs"][c * L:(c + 1) * L])
    next(fillers)()
    return jnp.concatenate(ys, axis=0), st


class _Stagger:
    def __init__(self):
        self._pending = None

    def push(self, produce, consume):
        val = produce()
        self.flush()
        self._pending = (consume, val)

    def flush(self):
        if self._pending is not None:
            consume, val = self._pending
            self._pending = None
            consume(val)


def _prompt_kernel(x_ref, mod_ref, gmix_ref, wa_ref, wb_ref, wdt_ref, cos_ref, sa_ref, sb_ref, sinks_ref,
                   convw_ref, convb_ref, dtb_ref, alog_ref, dskip_ref, ssmnw_ref, glng_ref, glnb_ref,
                   gws_ref, gbst_ref, wao_ref, wso_ref, wgo_ref, wout_ref, e_ref,
                   xo_ref, ko_ref, vo_ref, sto_ref, cvo_ref,
                   khist, vhist, xp, st_s):
    ts = PROMPT_TS
    s = pl.program_id(1)
    last = pl.num_programs(1) - 1

    @pl.when(s == 0)
    def _():
        khist[0:WINDOW, :] = jnp.zeros((WINDOW, KV_W), F32)
        vhist[0:WINDOW, :] = jnp.zeros((WINDOW, KV_W), F32)
        xp[...] = jnp.zeros_like(xp)
        st_s[...] = jnp.zeros_like(st_s)

    x3 = x_ref[...]
    mod3 = mod_ref[...]
    slab = ts // NORM_SLABS
    h_parts, q0_parts = [], []
    for r in range(0, ts, slab):
        h_parts.append(_modnorm(x3[:, r:r + slab], gmix_ref[...], mod3))
        q0_parts.append(_dot(h_parts[-1], wa_ref[:, C_Q[1]:C_Q[1] + PROJ_BLOCK]))
    q0 = jnp.concatenate(q0_parts, axis=0)
    h = jnp.concatenate(h_parts, axis=0)

    cos, sa, sb = cos_ref[...], sa_ref[...], sb_ref[...]
    res = {}

    def rope_block(raw):
        return [_rope(raw[:, i * LANES:(i + 1) * LANES], cos, sa, sb) for i in range(raw.shape[1] // LANES)]

    def q_block(raw):
        res.setdefault("q", []).extend(p.astype(BF16) for p in rope_block(raw * (HEAD_DIM ** -0.5)))

    def kv_block(raw):
        khist[WINDOW:WINDOW + ts, :] = rope_block(raw[:, 0:KV_W])[0]
        vhist[WINDOW:WINDOW + ts, :] = raw[:, KV_W:2 * KV_W]

    def conv_block(lo, raw):
        cs = slice(lo, lo + PROJ_BLOCK)
        ext = jnp.concatenate([xp[:, cs], raw], axis=0)
        acc = convb_ref[:, cs] + raw * convw_ref[CONV_WIDTH - 1:CONV_WIDTH, cs]
        for j in range(1, CONV_WIDTH):
            acc = acc + pltpu.roll(ext, j, 0)[8:8 + ts] * convw_ref[CONV_WIDTH - 1 - j:CONV_WIDTH - j, cs]
        xp[:, cs] = raw[ts - 8:ts]
        res.setdefault("xact", []).append(_silu_of_half(acc))

    def collect(key, fn):
        return lambda raw: res.setdefault(key, []).append(fn(raw))

    wrefs = (wa_ref, wb_ref, wdt_ref)
    pipe = _Stagger()

    def push_proj(cols, off, width, consume):
        ref, lo = wrefs[cols[0]], cols[1] + off
        pipe.push(lambda: _dot(h, ref[:, lo:lo + width]), consume)

    def blocks(cols, consume):
        return [functools.partial(push_proj, cols, off, PROJ_BLOCK, consume)
                for off in range(0, cols[2] - cols[1], PROJ_BLOCK)]

    conv_blocks = [functools.partial(push_proj, C_XBC, off, PROJ_BLOCK, functools.partial(conv_block, off))
                   for off in range(0, CONV_DIM, PROJ_BLOCK)]
    gate_blocks = iter(blocks(C_G0, collect("g0", _twice_sigmoid_of_half))
                       + blocks(C_G1, collect("g1", _twice_sigmoid_of_half))
                       + blocks(C_G2, collect("g2", _twice_sigmoid_of_half)))
    z_blocks = blocks(C_Z, collect("sz", _silu_of_half))
    gm_blocks = blocks(C_GU, collect("u", _gelu)) + blocks(C_GV, collect("gv", _gelu))

    def cat(key):
        return jnp.concatenate(res[key], axis=1)

    pipe.push(lambda: q0, q_block)
    for blk in blocks(C_Q, q_block)[1:]:
        blk()
    push_proj(C_K, 0, 2 * KV_W, kv_block)
    conv_blocks[0]()
    next(gate_blocks)()
    conv_blocks[1]()
    scores = _attn_scores(cat("q"), khist, s * ts)
    conv_blocks[2]()
    probs = _attn_probs(scores, sinks_ref, ts)
    next(gate_blocks)()
    conv_blocks[3]()
    attn = _attn_values(probs, vhist, ts)
    conv_blocks[4]()
    next(gate_blocks)()
    conv_blocks[5]()
    push_proj(C_DT, 0, LANES, collect("dt", lambda raw: _softplus(raw + dtb_ref[...])))
    next(gate_blocks)()
    a_out = _dot(attn, wao_ref[...])
    z_blocks[0]()
    cum_e, dt_e = _ssd_cumsum(res["dt"][0], -jnp.exp(alog_ref[...]), e_ref[...])
    z_blocks[1]()
    merged = cat("g0") * a_out
    ctx = _ssd_decays(cat("xact"), cum_e, dt_e)
    z_blocks[2]()
    z_blocks[3]()
    y, st_s[...] = _ssd_block(ctx, st_s[...], dskip_ref[...], gm_blocks)
    next(gate_blocks)()
    vn = _layer_norm(cat("gv"), glng_ref[...], glnb_ref[...])
    u = cat("u")
    next(gate_blocks)()
    ssm = _rms(y * cat("sz"), ssmnw_ref[...]).astype(BF16)
    gm = jnp.concatenate(
        [_gmlp_chunk(vn[c * GM_CHUNK:(c + 1) * GM_CHUNK], u[c * GM_CHUNK:(c + 1) * GM_CHUNK], gws_ref, gbst_ref,
                     GM_CHUNK) for c in range(ts // GM_CHUNK)], axis=0).astype(BF16)
    next(gate_blocks)()
    b_out = _dot(ssm, wso_ref[...])
    next(gate_blocks)()
    pipe.flush()
    merged = merged + cat("g1") * b_out
    c_out = _dot(gm, wgo_ref[...])
    for blk in gate_blocks:
        blk()
    pipe.flush()
    merged = merged + cat("g2") * c_out
    o = _dot(merged.astype(BF16), wout_ref[...])
    xo_ref[...] = x3 + mod3[:, 2:3, :] * o.reshape(x3.shape)

    khist[0:WINDOW, :] = khist[ts:ts + WINDOW, :]
    vhist[0:WINDOW, :] = vhist[ts:ts + WINDOW, :]

    @pl.when(s == last)
    def _():
        ko_ref[0] = khist[ts:ts + WINDOW, :]
        vo_ref[0] = vhist[ts:ts + WINDOW, :]
        cvo_ref[0] = xp[8 - (CONV_WIDTH - 1):8, :]
        sto_ref[0] = st_s[...].T


def _sample_attention(q, k_new, v_new, ck_ref, cv_ref, sinks_ref, nb):
    t = q.shape[0] // nb
    groups = [(i, kv) for i in range(nb) for kv in range(N_KV_HEADS)]

    def new_rows(a, i, kv):
        return a[i * t:(i + 1) * t, kv * HEAD_DIM:(kv + 1) * HEAD_DIM].astype(BF16)

    def cached(ref, i, kv):
        return ref[i, kv].T.astype(BF16)

    qg = [_stack_heads(q[i * t:(i + 1) * t], kv, t) for i, kv in groups]
    s_c = [_dot_nt(cached(ck_ref, i, kv), qg[n]) for n, (i, kv) in enumerate(groups)]
    s_n = [_dot_nt(new_rows(k_new, i, kv), qg[n]) for n, (i, kv) in enumerate(groups)]
    sinks = [_sink_row(sinks_ref, kv, t) for kv in range(N_KV_HEADS)]
    probs = []
    for n, (i, kv) in enumerate(groups):
        m = jnp.maximum(jnp.maximum(jnp.max(s_c[n], axis=0, keepdims=True), jnp.max(s_n[n], axis=0, keepdims=True)),
                        sinks[kv])
        p_c = jnp.exp(s_c[n] - m)
        p_n = jnp.exp(s_n[n] - m)
        inv = 1.0 / (jnp.sum(p_c, axis=0, keepdims=True) + jnp.sum(p_n, axis=0, keepdims=True)
                     + jnp.exp(sinks[kv] - m))
        probs.append(((p_c * inv).astype(BF16), (p_n * inv).astype(BF16)))
    outs = [_dot_tn(probs[n][0], cached(cv_ref, i, kv)) + _dot_tn(probs[n][1], new_rows(v_new, i, kv))
            for n, (i, kv) in enumerate(groups)]
    rows = [jnp.concatenate([_unstack_heads(outs[i * N_KV_HEADS + kv], t) for kv in range(N_KV_HEADS)], axis=1)
            for i in range(nb)]
    return jnp.concatenate(rows, axis=0)


def _sample_ssd_setup(xact, dt, a_row, e, L):
    m = xact.shape[0]
    n_seq = m // L
    xs = xact[:, 0:SSM_INNER]
    ri = lax.broadcasted_iota(jnp.int32, (m, m), 0)
    lag = ri - lax.broadcasted_iota(jnp.int32, (m, m), 1)
    tri = jnp.where((lag >= 0) & (lag <= (ri & (L - 1))), 1.0, 0.0).astype(BF16)
    hi, mid, lo = _split3(dt * a_row)
    cum = _dot(tri, hi) + _dot(tri, mid) + _dot(tri, lo)
    hi = cum.astype(BF16)
    mid = (cum - hi.astype(F32)).astype(BF16)
    cum_e = (_dot(hi, e) + _dot(mid, e)).reshape(n_seq, L, SSM_INNER)
    head_of_lane = jnp.right_shift(lax.broadcasted_iota(jnp.int32, (LANES, SSM_HEADS * L), 1), _log2(L))
    e_slots = jnp.where(head_of_lane == lax.broadcasted_iota(jnp.int32, (LANES, SSM_HEADS * L), 0), 1.0, 0.0)
    e_slots = e_slots.astype(BF16)
    cum_s = (_dot(hi, e_slots) + _dot(mid, e_slots)).reshape(n_seq, L, SSM_HEADS * L)
    dt_e = _dot(dt.astype(BF16), e)
    t3 = lax.broadcasted_iota(jnp.int32, cum_s.shape, 1)
    slot3 = lax.broadcasted_iota(jnp.int32, cum_s.shape, 2) & (L - 1)
    cum_src = jnp.sum(jnp.where(t3 == slot3, cum_s, 0.0), axis=1, keepdims=True)
    decay = jnp.exp(jnp.where(t3 >= slot3, cum_s - cum_src, -jnp.inf))
    last = cum_e[:, L - 1:L, :]
    xdt = xs * dt_e
    wx = (xdt.reshape(n_seq, L, SSM_INNER) * jnp.exp(last - cum_e)).astype(BF16)
    per_head = jnp.exp(cum.reshape(n_seq, L, LANES)[:, L - 1:L, :])
    rep = jnp.broadcast_to(per_head, (n_seq, SSM_HEADS, LANES)).reshape(n_seq * SSM_HEADS, LANES)
    own_lane = (lax.broadcasted_iota(jnp.int32, rep.shape, 1)
                == (lax.broadcasted_iota(jnp.int32, rep.shape, 0) & (SSM_HEADS - 1)))
    hi, mid, lo = _split3(jnp.where(own_lane, rep, 0.0))
    ones = jnp.ones((LANES, LANES), BF16)
    splat = (_dot(hi, ones) + _dot(mid, ones) + _dot(lo, ones)).reshape(n_seq, SSM_HEADS, LANES)
    return {
        "L": L, "xs": xs,
        "bm": xact[:, SSM_INNER:SSM_INNER + SSM_GROUPS * SSM_STATE].astype(BF16),
        "cm": xact[:, SSM_INNER + SSM_GROUPS * SSM_STATE:CONV_DIM].astype(BF16),
        "decay": decay, "exp_cum": jnp.exp(cum_e), "wx": wx, "xdt": xdt.astype(BF16), "state_decay": splat,
    }


def _sample_ssd_block(s, h0_ref, sto_ref, dskip_e, nb):
    L = s["L"]
    pairs = [(i, g) for i in range(nb) for g in range(SSM_GROUPS)]
    ri = jnp.right_shift(lax.broadcasted_iota(jnp.int32, (HEADS_PER_GROUP * L, GROUP_INNER), 0), _log2(L))
    li = jnp.right_shift(lax.broadcasted_iota(jnp.int32, (HEADS_PER_GROUP * L, GROUP_INNER), 1), _log2(SSM_HEAD_DIM))

    def rows(a, i, g, width):
        return a[i * L:(i + 1) * L, g * width:(g + 1) * width]

    cb = [_dot_nt(rows(s["cm"], i, g, SSM_STATE), jnp.concatenate([rows(s["bm"], i, g, SSM_STATE)] * HEADS_PER_GROUP,
                                                                  axis=0)) for i, g in pairs]
    upd = [_dot_tn(s["wx"][i][:, g * GROUP_INNER:(g + 1) * GROUP_INNER], rows(s["bm"], i, g, SSM_STATE))
           for i, g in pairs]
    y_inter = [_dot_nt(rows(s["cm"], i, g, SSM_STATE), h0_ref[i][g * GROUP_INNER:(g + 1) * GROUP_INNER, :].astype(BF16))
               for i, g in pairs]
    y_intra = []
    for n, (i, g) in enumerate(pairs):
        gs = slice(g * HEADS_PER_GROUP * L, (g + 1) * HEADS_PER_GROUP * L)
        m_g = (cb[n] * s["decay"][i][:, gs]).astype(BF16)
        tiled = jnp.concatenate([rows(s["xdt"], i, g, GROUP_INNER)] * HEADS_PER_GROUP, axis=0)
        y_intra.append(_dot(m_g, jnp.where(ri == li, tiled, jnp.zeros_like(tiled))))
    ys = []
    for i in range(nb):
        parts = [y_intra[i * SSM_GROUPS + g] + y_inter[i * SSM_GROUPS + g]
                 * s["exp_cum"][i][:, g * GROUP_INNER:(g + 1) * GROUP_INNER] for g in range(SSM_GROUPS)]
        ys.append(jnp.concatenate(parts, axis=1) + dskip_e * s["xs"][i * L:(i + 1) * L])
        decayed = h0_ref[i].reshape(SSM_HEADS, SSM_HEAD_DIM, SSM_STATE) * s["state_decay"][i][:, None, :]
        sto_ref[i] = decayed.reshape(SSM_INNER, SSM_STATE) + jnp.concatenate(
            upd[i * SSM_GROUPS:(i + 1) * SSM_GROUPS], axis=0)
    return jnp.concatenate(ys, axis=0)


def _sample_kernel(x_ref, mod_ref, gmix_ref, wa_ref, wb_ref, wdt_ref, cos_ref, sa_ref, sb_ref, sinks_ref,
                   ck_ref, cv_ref, h0_ref, cs_ref,
                   convw_ref, convb_ref, dtb_ref, alog_ref, dskip_ref, ssmnw_ref, glng_ref, glnb_ref,
                   gws_ref, gbst_ref, wao_ref, wso_ref, wgo_ref, wout_ref, e_ref,
                   xo_ref, ko_ref, vo_ref, sto_ref, cvo_ref, gvo_ref,
                   xp):
    nb = SAMPLE_NB
    t = x_ref.shape[1]
    m = nb * t
    win_ref = (wa_ref, wb_ref, wdt_ref)
    x3 = x_ref[...]
    mod3 = mod_ref[...]
    h = _modnorm(x3, gmix_ref[...], mod3)

    cos, sa, sb = cos_ref[...], sa_ref[...], sb_ref[...]
    q = _proj(h, win_ref, C_Q) * (HEAD_DIM ** -0.5)
    q = jnp.concatenate(
        [_rope(q[:, i * LANES:(i + 1) * LANES], cos, sa, sb) for i in range(Q_W // LANES)], axis=1).astype(BF16)
    k_new = _rope(_proj(h, win_ref, C_K), cos, sa, sb)
    v_new = _proj(h, win_ref, C_V)
    ko_ref[...] = k_new.reshape(nb, t, KV_W)
    vo_ref[...] = v_new.reshape(nb, t, KV_W)

    xp[:, 8 - (CONV_WIDTH - 1):8, :] = cs_ref[...]
    xp[:, 8:8 + t, :] = _proj(h, win_ref, C_XBC).reshape(nb, t, CONV_DIM)
    acc = convb_ref[...] + xp[:, 8:8 + t, :] * convw_ref[CONV_WIDTH - 1:CONV_WIDTH, :]
    for j in range(1, CONV_WIDTH):
        acc = acc + xp[:, 8 - j:8 - j + t, :] * convw_ref[CONV_WIDTH - 1 - j:CONV_WIDTH - j, :]
    xact = _silu_of_half(acc).reshape(m, CONV_DIM)
    cvo_ref[...] = xp[:, t + 8 - (CONV_WIDTH - 1):t + 8, :]
    dt = _softplus(_proj(h, win_ref, C_DT) + dtb_ref[...])
    dskip_e = dskip_ref[...]

    u = _gelu(_proj(h, win_ref, C_GU))
    vn = _layer_norm(_gelu(_proj(h, win_ref, C_GV)), glng_ref[...], glnb_ref[...])
    gvo_ref[...] = vn.reshape(nb, t, GM_WIDTH)

    ssd = _sample_ssd_setup(xact, dt, -jnp.exp(alog_ref[...]), e_ref[...], t)
    attn = _sample_attention(q, k_new, v_new, ck_ref, cv_ref, sinks_ref, nb)
    y = _sample_ssd_block(ssd, h0_ref, sto_ref, dskip_e, nb)
    gm_rows = [_gmlp_chunk(vn[i * t:(i + 1) * t], u[i * t:(i + 1) * t], gws_ref, gbst_ref, t) for i in range(nb)]

    a_out = _dot(attn.astype(BF16), wao_ref[...])
    z = _proj(h, win_ref, C_Z)
    ssm = _rms(y * _silu_of_half(z), ssmnw_ref[...]).astype(BF16)
    b_out = _dot(ssm, wso_ref[...])
    c_out = _dot(jnp.concatenate(gm_rows, axis=0).astype(BF16), wgo_ref[...])
    gates = [_twice_sigmoid_of_half(_proj(h, win_ref, cols)) for cols in (C_G0, C_G1, C_G2)]
    xo_ref[...] = _merge_out(x3, mod3, gates, a_out, b_out, c_out, wout_ref)


def _ffn_block(x_ref, mod_ref, gff_ref, w1_ref, w2_ref, gfin_ref, o_ref, final_norm):
    x3 = x_ref[...]
    mod3 = mod_ref[...]
    nb, t, d = x3.shape
    h3 = _rms(x3, gff_ref[...]) * (1.0 + mod3[:, 4:5, :]) + mod3[:, 3:4, :]
    h = h3.reshape(nb * t, d).astype(BF16)
    a = jnp.maximum(_dot(h, w1_ref[...]), 0.0)
    y = _dot((a * a).astype(BF16), w2_ref[...])
    out = x3 + mod3[:, 5:6, :] * y.reshape(nb, t, d)
    if final_norm:
        out = _rms(out, gfin_ref[...])
    o_ref[...] = out


def _ffn_kernel(xp_ref, modp_ref, xs_ref, mods_ref, gff_ref, w1_ref, w2_ref, gfin_ref, op_ref, os_ref, *,
                final_norm, n_prompt_steps):
    i = pl.program_id(0)

    @pl.when(i < n_prompt_steps)
    def _():
        _ffn_block(xp_ref, modp_ref, gff_ref, w1_ref, w2_ref, gfin_ref, op_ref, final_norm)

    @pl.when(i == n_prompt_steps)
    def _():
        _ffn_block(xs_ref, mods_ref, gff_ref, w1_ref, w2_ref, gfin_ref, os_ref, final_norm)


def _ada_kernel(c_ref, w_ref, b_ref, op_ref, os_ref):
    mod = _dot(_silu(c_ref[...]).astype(BF16), w_ref[...].astype(BF16)) + b_ref[...]
    n_prompt = op_ref.shape[0]
    op_ref[...] = mod[0:n_prompt]
    os_ref[...] = mod[n_prompt:]


def _w_in_piece_kernel(wt_ref, o_ref, *, half_lo, half_hi, n_valid):
    cols = wt_ref.shape[0]
    col = pl.program_id(1) * cols + lax.broadcasted_iota(jnp.int32, (cols, 1), 0)
    scale = jnp.where(col < n_valid, jnp.where((col >= half_lo) & (col < half_hi), 0.5, 1.0), 0.0)
    o_ref[...] = (wt_ref[...] * scale).astype(BF16).T


def _w_in_piece(w_t, first_col, n_cols, half_cols, n_blocks, n_valid=None):
    cols = n_cols // n_blocks
    n_valid = n_cols if n_valid is None else n_valid
    return pl.pallas_call(
        functools.partial(_w_in_piece_kernel, half_lo=half_cols[0], half_hi=half_cols[1], n_valid=n_valid),
        grid=(DEPTH, n_blocks),
        in_specs=[pl.BlockSpec((pl.Squeezed(), pl.Element(cols), pl.Element(D_MODEL)),
                               lambda l, j: (l, pl.multiple_of(first_col + j * cols, SSM_HEADS), 0))],
        out_specs=pl.BlockSpec((None, D_MODEL, cols), lambda l, j: (l, 0, j)),
        out_shape=jax.ShapeDtypeStruct((DEPTH, D_MODEL, n_cols), BF16),
        compiler_params=_params(2),
        name="w_in_piece",
    )(w_t)


def _w_in_split(w_in):
    w_t = jnp.swapaxes(w_in, 1, 2)
    b_lo = N_IN_A + SSM_HEADS
    return (_w_in_piece(w_t, 0, N_IN_A, (C_Z[1], C_Z[2]), W_SPLIT_BLOCKS),
            _w_in_piece(w_t, b_lo, N_IN_B, (C_G0[1], N_IN_B), W_SPLIT_BLOCKS),
            _w_in_piece(w_t, N_IN_A, LANES, (0, 0), 1, n_valid=SSM_HEADS))


def _with_ignored_inputs(body, n_in, n_ignored):
    if n_ignored == 0:
        return body
    return lambda *refs: body(*refs[:n_in], *refs[n_in + n_ignored:])


def _const_spec(shape):
    return pl.BlockSpec(shape, lambda *_: (0,) * len(shape), pipeline_mode=pl.Buffered(1))


def _layer_spec(l, shape):
    return pl.BlockSpec((None,) + shape, lambda *_: (l,) + (0,) * len(shape), pipeline_mode=pl.Buffered(1))


def _any_spec():
    return pl.BlockSpec(memory_space=pl.ANY)


def _smem_spec():
    return pl.BlockSpec(memory_space=pltpu.SMEM)


def _params(n_grid):
    return pltpu.CompilerParams(dimension_semantics=("arbitrary",) * n_grid, vmem_limit_bytes=VMEM_LIMIT)


def _ada_call(c_prompt, c_sample, w_ada, b_ada):
    n_p, n_s = c_prompt.shape[0], c_sample.shape[0]
    c_all = jnp.concatenate([c_prompt, c_sample], axis=0)
    tn = 2 * D_MODEL
    n_tiles = w_ada.shape[2] // tn
    mod_p, mod_s = pl.pallas_call(
        _ada_kernel,
        grid=(DEPTH, n_tiles),
        in_specs=[pl.BlockSpec((n_p + n_s, D_MODEL), lambda l, n: (0, 0)),
                  pl.BlockSpec((None, D_MODEL, tn), lambda l, n: (l, 0, n)),
                  pl.BlockSpec((None, 1, tn), lambda l, n: (l, 0, n))],
        out_specs=(pl.BlockSpec((None, n_p, tn), lambda l, n: (l, 0, n)),
                   pl.BlockSpec((None, n_s, tn), lambda l, n: (l, 0, n))),
        out_shape=(jax.ShapeDtypeStruct((DEPTH, n_p, w_ada.shape[2]), F32),
                   jax.ShapeDtypeStruct((DEPTH, n_s, w_ada.shape[2]), F32)),
        compiler_params=_params(2),
        name="ada_mod",
    )(c_all, w_ada, b_ada.reshape(DEPTH, 1, -1))
    return mod_p.reshape(DEPTH, n_p, 6, D_MODEL), mod_s.reshape(DEPTH, n_s, 6, D_MODEL)


def _w_in_specs(l):
    return [_layer_spec(l, (D_MODEL, N_IN_A)), _layer_spec(l, (D_MODEL, N_IN_B)), _layer_spec(l, (D_MODEL, LANES))]


def _layer_weight_specs(l):
    return [
        _layer_spec(l, (CONV_WIDTH, CONV_DIM)), _layer_spec(l, (1, CONV_DIM)), _layer_spec(l, (1, LANES)),
        _layer_spec(l, (1, LANES)), _layer_spec(l, (1, SSM_INNER)), _layer_spec(l, (1, SSM_INNER)),
        _layer_spec(l, (1, GM_WIDTH)), _layer_spec(l, (1, GM_WIDTH)),
        _layer_spec(l, (GM_GROUPS, GM_CHUNK, GM_CHUNK)), _layer_spec(l, (GM_CHUNK, GM_GROUPS)),
        _layer_spec(l, (Q_W, D_MODEL)), _layer_spec(l, (SSM_INNER, D_MODEL)), _layer_spec(l, (GM_WIDTH, D_MODEL)),
        _layer_spec(l, (D_MODEL, D_MODEL)), _const_spec((LANES, SSM_INNER)),
    ]


def _layer_weight_args(p):
    return (p["conv_w"], p["conv_b"], p["dt_bias"], p["a_log"], p["d_skip"], p["ssm_norm_w"], p["gm_ln_g"],
            p["gm_ln_b"], p["gm_w_s"], p["gm_b_st"], p["w_attn_o"], p["w_ssm_o"], p["w_gm_o"], p["w_out"], p["expand"])


def _prompt_mixer(l, x, mod, p, rope, prev):
    bsz, seq, d = x.shape
    ts = PROMPT_TS
    tab = pl.BlockSpec((ts, LANES), lambda b, s: (s, 0))
    in_specs = [
        pl.BlockSpec((1, ts, d), lambda b, s: (b, s, 0)),
        pl.BlockSpec((None, 1, 6, d), lambda b, s: (l, b, 0, 0)),
        _layer_spec(l, (1, d)), *_w_in_specs(l),
        tab, tab, tab, _smem_spec(),
    ] + _layer_weight_specs(l)
    args = (x, mod, p["g_mix"], *p["w_in"], *rope, p["sinks"][l], *_layer_weight_args(p))
    state_tails = ((WINDOW, KV_W), (WINDOW, KV_W), (SSM_INNER, SSM_STATE), (CONV_WIDTH - 1, CONV_DIM))
    out_shape = (jax.ShapeDtypeStruct((bsz, seq, d), F32),) + tuple(
        jax.ShapeDtypeStruct((DEPTH, bsz) + tail, F32) for tail in state_tails)
    out_specs = (pl.BlockSpec((1, ts, d), lambda b, s: (b, s, 0)),) + tuple(
        pl.BlockSpec((None, 1) + tail, lambda b, s: (l, b, 0, 0)) for tail in state_tails)
    scratch = [
        pltpu.VMEM((WINDOW + ts, KV_W), F32), pltpu.VMEM((WINDOW + ts, KV_W), F32),
        pltpu.VMEM((8, CONV_DIM), F32), pltpu.VMEM((SSM_STATE, SSM_INNER), F32),
    ]
    n_in = len(args)
    prev = () if prev is None else tuple(prev)
    return pl.pallas_call(
        _with_ignored_inputs(_prompt_kernel, n_in, len(prev)),
        grid=(bsz, seq // ts), in_specs=in_specs + [_any_spec()] * len(prev), out_specs=out_specs,
        out_shape=out_shape, scratch_shapes=scratch, compiler_params=_params(2), name="prompt_mixer",
        input_output_aliases={n_in + i: 1 + i for i in range(len(prev))},
    )(*args, *prev)


def _sample_mixer(l, x, mod, cache_k, cache_v, h0, conv_state, p, rope, prev):
    bsz, t, d = x.shape
    nb = SAMPLE_NB
    m = nb * t

    def blk(*tail):
        return pl.BlockSpec((nb,) + tail, lambda i: (i,) + (0,) * len(tail))

    def lblk(*tail):
        return pl.BlockSpec((None, nb) + tail, lambda i: (l, i) + (0,) * len(tail))

    in_specs = [
        blk(t, d), lblk(6, d), _layer_spec(l, (1, d)), *_w_in_specs(l),
        _const_spec((m, LANES)), _const_spec((m, LANES)), _const_spec((m, LANES)), _smem_spec(),
        lblk(N_KV_HEADS, HEAD_DIM, WINDOW), lblk(N_KV_HEADS, HEAD_DIM, WINDOW), lblk(SSM_INNER, SSM_STATE),
        lblk(CONV_WIDTH - 1, CONV_DIM),
    ] + _layer_weight_specs(l)
    args = (x, mod, p["g_mix"], *p["w_in"], *rope, p["sinks"][l], cache_k, cache_v, h0, conv_state,
            *_layer_weight_args(p))
    state_tails = ((t, KV_W), (t, KV_W), (SSM_INNER, SSM_STATE), (CONV_WIDTH - 1, CONV_DIM), (t, GM_WIDTH))
    out_shape = (jax.ShapeDtypeStruct((bsz, t, d), F32),) + tuple(
        jax.ShapeDtypeStruct((DEPTH, bsz) + tail, F32) for tail in state_tails)
    out_specs = (blk(t, d),) + tuple(lblk(*tail) for tail in state_tails)
    scratch = [pltpu.VMEM((nb, 8 + t, CONV_DIM), F32)]
    n_in = len(args)
    prev = () if prev is None else tuple(prev)
    return pl.pallas_call(
        _with_ignored_inputs(_sample_kernel, n_in, len(prev)),
        grid=(bsz // nb,), in_specs=in_specs + [_any_spec()] * len(prev), out_specs=out_specs,
        out_shape=out_shape, scratch_shapes=scratch, compiler_params=_params(1), name="sample_mixer",
        input_output_aliases={n_in + i: 1 + i for i in range(len(prev))},
    )(*args, *prev)


def _ffn(l, x_p, mod_p, x_s, mod_s, p, g_final, final_norm):
    bp, sp, d = x_p.shape
    bs, ss, _ = x_s.shape
    assert bs * ss == FFN_ROWS and sp % FFN_ROWS == 0
    blocks_per_seq = sp // FFN_ROWS
    n_prompt_steps = bp * blocks_per_seq

    def prompt_block(i):
        j = jnp.minimum(i, n_prompt_steps - 1)
        return j // blocks_per_seq, j % blocks_per_seq

    x_spec = pl.BlockSpec((1, FFN_ROWS, d), lambda i: (*prompt_block(i), 0))
    s_spec = pl.BlockSpec((bs, ss, d), lambda i: (0, 0, 0))
    return pl.pallas_call(
        functools.partial(_ffn_kernel, final_norm=final_norm, n_prompt_steps=n_prompt_steps),
        grid=(n_prompt_steps + 1,),
        in_specs=[x_spec, pl.BlockSpec((None, 1, 6, d), lambda i: (l, prompt_block(i)[0], 0, 0)),
                  s_spec, pl.BlockSpec((None, bs, 6, d), lambda i: (l, 0, 0, 0)),
                  _layer_spec(l, (1, d)), _layer_spec(l, (d, D_FF)), _layer_spec(l, (D_FF, d)), _const_spec((1, d))],
        out_specs=(x_spec, s_spec),
        out_shape=(jax.ShapeDtypeStruct(x_p.shape, F32), jax.ShapeDtypeStruct(x_s.shape, F32)),
        compiler_params=_params(1), name="ffn",
    )(x_p, mod_p, x_s, mod_s, p["g_ff"], p["w_ff1"], p["w_ff2"], g_final)


def _rope_tables(pos):
    half = ROT_DIM // 2
    inv_freq = ROPE_THETA ** (-jnp.arange(half, dtype=F32) * (2.0 / ROT_DIM))
    ang = pos.astype(F32)[:, None] * inv_freq[None, :]
    cos, sin = jnp.cos(ang), jnp.sin(ang)
    n = pos.shape[0]
    ones = jnp.ones((n, HEAD_DIM - ROT_DIM), F32)
    zeros = jnp.zeros((n, HEAD_DIM - ROT_DIM), F32)
    zh = jnp.zeros((n, half), F32)
    cos_t = jnp.concatenate([cos, cos, ones], axis=1)
    sin_a = jnp.concatenate([-sin, zh, zeros], axis=1)
    sin_b = jnp.concatenate([zh, sin, zeros], axis=1)
    rep = LANES // HEAD_DIM
    return tuple(jnp.tile(a, (1, rep)) for a in (cos_t, sin_a, sin_b))


def _stacked_params(w_in, g_mix, sinks, conv_w, conv_b, dt_bias, a_log, d_skip, ssm_norm_w, gm_ln_g, gm_ln_b,
                    gm_w_s, gm_b_s, w_attn_o, w_ssm_o, w_gm_o, w_out, g_ff, w_ff1, w_ff2):
    w_r = _w_in_split(w_in)
    pad = jnp.zeros((DEPTH, LANES - SSM_HEADS), F32)
    expand = (jnp.arange(SSM_INNER)[None, :] // SSM_HEAD_DIM == jnp.arange(LANES)[:, None]).astype(BF16)

    def row(a):
        return a[:, None, :]

    return {
        "w_in": w_r, "g_mix": row(g_mix), "sinks": sinks,
        "conv_w": 0.5 * conv_w, "conv_b": row(0.5 * conv_b),
        "dt_bias": row(jnp.concatenate([dt_bias, pad], axis=1)), "a_log": row(jnp.concatenate([a_log, pad], axis=1)),
        "d_skip": row(jnp.repeat(d_skip, SSM_HEAD_DIM, axis=1)), "ssm_norm_w": row(ssm_norm_w),
        "gm_ln_g": row(gm_ln_g), "gm_ln_b": row(gm_ln_b),
        "gm_w_s": gm_w_s, "gm_b_st": jnp.swapaxes(gm_b_s, 1, 2),
        "w_attn_o": (0.5 * w_attn_o).astype(BF16), "w_ssm_o": (0.5 * w_ssm_o).astype(BF16),
        "w_gm_o": (0.5 * w_gm_o).astype(BF16), "w_out": w_out.astype(BF16), "expand": expand,
        "g_ff": row(g_ff), "w_ff1": w_ff1.astype(BF16), "w_ff2": w_ff2.astype(BF16),
    }


def kernel(x_prompt, x_sample, c_prompt, c_sample, cache_attn_k, cache_attn_v, state_ssm, state_conv, w_ada, b_ada, g_mix, w_in, sinks, conv_w, conv_b, dt_bias, a_log, d_skip, ssm_norm_w, gm_ln_g, gm_ln_b, gm_w_s, gm_b_s, w_attn_o, w_ssm_o, w_gm_o, w_out, g_ff, w_ff1, w_ff2, g_final):
    bp, sp, d = x_prompt.shape
    bs, ss, _ = x_sample.shape
    mod_p, mod_s = _ada_call(c_prompt, c_sample, w_ada, b_ada)
    rope_p = _rope_tables(jnp.arange(sp))
    rope_s = tuple(jnp.tile(a, (SAMPLE_NB, 1)) for a in _rope_tables(PAST_LEN + jnp.arange(ss)))
    g_fin = g_final[None]
    p = _stacked_params(w_in, g_mix, sinks, conv_w, conv_b, dt_bias, a_log, d_skip, ssm_norm_w, gm_ln_g,
                        gm_ln_b, gm_w_s, gm_b_s, w_attn_o, w_ssm_o, w_gm_o, w_out, g_ff, w_ff1, w_ff2)
    ck = jnp.transpose(cache_attn_k, (0, 1, 3, 4, 2))
    cv = jnp.transpose(cache_attn_v, (0, 1, 3, 4, 2))
    h0 = state_ssm.reshape(DEPTH, bs, SSM_INNER, SSM_STATE)
    xp, xs = x_prompt, x_sample
    state_p = state_s = None
    for l in range(DEPTH):
        final = l == DEPTH - 1
        xp, *state_p = _prompt_mixer(l, xp, mod_p, p, rope_p, state_p)
        xs, *state_s = _sample_mixer(l, xs, mod_s, ck, cv, h0, state_conv, p, rope_s, state_s)
        xp, xs = _ffn(l, xp, mod_p, xs, mod_s, p, g_fin, final)
    kp, vp, ssm_p, conv_p = state_p
    ks, vs, ssm_s, conv_s, gv_s = state_s
    return (xp, xs,
            kp.reshape(DEPTH, bp, WINDOW, N_KV_HEADS, HEAD_DIM), vp.reshape(DEPTH, bp, WINDOW, N_KV_HEADS, HEAD_DIM),
            ssm_p.reshape(DEPTH, bp, SSM_HEADS, SSM_HEAD_DIM, SSM_STATE), conv_p,
            ks.reshape(DEPTH, bs, ss, N_KV_HEADS, HEAD_DIM), vs.reshape(DEPTH, bs, ss, N_KV_HEADS, HEAD_DIM),
            ssm_s.reshape(DEPTH, bs, SSM_HEADS, SSM_HEAD_DIM, SSM_STATE), conv_s, gv_s)
```

```python
import functools

import jax
import jax.numpy as jnp
from jax import lax
from jax.experimental import pallas as pl
from jax.experimental.pallas import tpu as pltpu

F32 = jnp.float32
BF16 = jnp.bfloat16

D_MODEL = 1024
DEPTH = 2
CHUNK = 64
N_HEADS = 8
N_KV_HEADS = 2
HEAD_DIM = 64
GQA_GROUP = N_HEADS // N_KV_HEADS
ROT_DIM = HEAD_DIM // 4
ROPE_THETA = 500000.0
WINDOW = 128
SSM_HEADS = 16
SSM_HEAD_DIM = 64
SSM_INNER = SSM_HEADS * SSM_HEAD_DIM
SSM_GROUPS = 2
SSM_STATE = 128
SSM_CHUNK = 64
CONV_WIDTH = 4
CONV_DIM = SSM_INNER + 2 * SSM_GROUPS * SSM_STATE
GM_WIDTH = 512
GM_GROUPS = 4
GM_GROUP_DIM = GM_WIDTH // GM_GROUPS
GM_CHUNK = 128
D_FF = 4 * D_MODEL
Q_W = N_HEADS * HEAD_DIM
KV_W = N_KV_HEADS * HEAD_DIM
PAST_LEN = 4096
EPS = 1e-6

LANES = 128
MXU_TILE = 256
HEADS_PER_GROUP = SSM_HEADS // SSM_GROUPS
GROUP_INNER = HEADS_PER_GROUP * SSM_HEAD_DIM

C_Q = (0, 0, Q_W)
C_K = (0, C_Q[2], C_Q[2] + KV_W)
C_V = (0, C_K[2], C_K[2] + KV_W)
C_Z = (0, C_V[2], C_V[2] + SSM_INNER)
C_XBC = (0, C_Z[2], C_Z[2] + CONV_DIM)
N_IN_A = C_XBC[2]
C_GU = (1, 0, GM_WIDTH)
C_GV = (1, C_GU[2], C_GU[2] + GM_WIDTH)
C_G0 = (1, C_GV[2], C_GV[2] + D_MODEL)
C_G1 = (1, C_G0[2], C_G0[2] + D_MODEL)
C_G2 = (1, C_G1[2], C_G1[2] + D_MODEL)
N_IN_B = C_G2[2]
C_DT = (2, 0, LANES)

PROMPT_TS = 512
NORM_SLABS = 4
PROJ_BLOCK = MXU_TILE
SAMPLE_NB = 8
FFN_ROWS = 512
W_SPLIT_BLOCKS = 2
V7X_VMEM_BYTES = 64 * 1024 * 1024
VMEM_LIMIT = V7X_VMEM_BYTES * 7 // 8


def _log2(n):
    assert n & (n - 1) == 0, n
    return n.bit_length() - 1


def _dot(a, b):
    return jnp.dot(a, b, preferred_element_type=F32)


def _dot_nt(a, b):
    return lax.dot_general(a, b, (((1,), (1,)), ((), ())), preferred_element_type=F32)


def _dot_tn(a, b):
    return lax.dot_general(a, b, (((0,), (0,)), ((), ())), preferred_element_type=F32)


def _split3(x):
    hi = x.astype(BF16)
    r = x - hi.astype(F32)
    mid = r.astype(BF16)
    lo = (r - mid.astype(F32)).astype(BF16)
    return hi, mid, lo


def _sigmoid(x):
    return 0.5 * (1.0 + jnp.tanh(0.5 * x))


def _silu(x):
    return x * _sigmoid(x)


def _silu_of_half(xh):
    return xh * (1.0 + jnp.tanh(xh))


def _twice_sigmoid_of_half(xh):
    return 1.0 + jnp.tanh(xh)


def _gelu(x):
    c = 0.7978845608028654
    half_x = 0.5 * x
    return half_x + half_x * jnp.tanh(x * (c + (c * 0.044715) * (x * x)))


def _softplus(x):
    return jnp.maximum(x, 0.0) + jnp.log1p(jnp.exp(-jnp.abs(x)))


def _rms(x, g):
    return x * lax.rsqrt(jnp.mean(x * x, axis=-1, keepdims=True) + EPS) * g


def _rope(x, cos, sin_a, sin_b):
    return x * cos + pltpu.roll(x, LANES - ROT_DIM // 2, 1) * sin_a + pltpu.roll(x, ROT_DIM // 2, 1) * sin_b


def _proj(h, win_refs, cols):
    return _dot(h, win_refs[cols[0]][:, cols[1]:cols[2]])


def _modnorm(x3, g, mod3):
    nb, t, d = x3.shape
    gain = g * (1.0 + mod3[:, 1:2, :])
    h3 = x3 * lax.rsqrt(jnp.mean(x3 * x3, axis=-1, keepdims=True) + EPS) * gain + mod3[:, 0:1, :]
    return h3.reshape(nb * t, d).astype(BF16)


def _sink_row(sinks_ref, kv, cols_per_head):
    c = lax.broadcasted_iota(jnp.int32, (1, GQA_GROUP * cols_per_head), 1)
    row = jnp.full((1, GQA_GROUP * cols_per_head), sinks_ref[kv * GQA_GROUP], F32)
    for i in range(1, GQA_GROUP):
        row = jnp.where(c >= i * cols_per_head, sinks_ref[kv * GQA_GROUP + i], row)
    return row


def _stack_heads(q, kv, rows):
    return jnp.concatenate(
        [q[:, (kv * GQA_GROUP + i) * HEAD_DIM:(kv * GQA_GROUP + i + 1) * HEAD_DIM] for i in range(GQA_GROUP)], axis=0)


def _unstack_heads(o, rows):
    return jnp.concatenate([o[i * rows:(i + 1) * rows, :] for i in range(GQA_GROUP)], axis=1)


def _gmlp_chunk(vn, u, gws_ref, gbst_ref, L):
    ri = lax.broadcasted_iota(jnp.int32, (L, L), 0)
    ci = lax.broadcasted_iota(jnp.int32, (L, L), 1)
    outs = []
    for g in range(GM_GROUPS):
        w = jnp.where(ri >= ci, gws_ref[g, 0:L, 0:L], 0.0).astype(BF16)
        v_g = vn[:, g * GM_GROUP_DIM:(g + 1) * GM_GROUP_DIM].astype(BF16)
        outs.append(_dot(w, v_g) + gbst_ref[0:L, g:g + 1])
    return u * jnp.concatenate(outs, axis=1)


def _layer_norm(x, g, b):
    xc = x - jnp.mean(x, axis=-1, keepdims=True)
    return xc * lax.rsqrt(jnp.mean(xc * xc, axis=-1, keepdims=True) + EPS) * g + b


def _merge_out(x3, mod3, gates, a, b, c, wout_ref):
    merged = gates[0] * a + gates[1] * b + gates[2] * c
    o = _dot(merged.astype(BF16), wout_ref[...])
    nb, t, d = x3.shape
    return x3 + mod3[:, 2:3, :] * o.reshape(nb, t, d)


def _attn_groups(ts):
    return [(c, kv) for c in range(ts // CHUNK) for kv in range(N_KV_HEADS)]


def _attn_scores(q, khist, pos0):
    ts = q.shape[0]
    n_keys = WINDOW + CHUNK
    key_i = lax.broadcasted_iota(jnp.int32, (n_keys, GQA_GROUP * CHUNK), 0)
    scores = []
    for c, kv in _attn_groups(ts):
        k_g = khist[c * CHUNK:c * CHUNK + n_keys, kv * HEAD_DIM:(kv + 1) * HEAD_DIM].astype(BF16)
        sc = _dot_nt(k_g, _stack_heads(q[c * CHUNK:(c + 1) * CHUNK, :], kv, CHUNK))
        if c * CHUNK < WINDOW:
            sc = jnp.where(key_i >= WINDOW - c * CHUNK - pos0, sc, -jnp.inf)
        scores.append(sc)
    return scores


def _attn_probs(scores, sinks_ref, ts):
    sink_rows = [_sink_row(sinks_ref, kv, CHUNK) for kv in range(N_KV_HEADS)]
    probs = []
    for (c, kv), sc in zip(_attn_groups(ts), scores):
        m = jnp.maximum(jnp.max(sc, axis=0, keepdims=True), sink_rows[kv])
        p = jnp.exp(sc - m)
        denom = jnp.sum(p, axis=0, keepdims=True) + jnp.exp(sink_rows[kv] - m)
        probs.append((p * (1.0 / denom)).astype(BF16))
    return probs


def _attn_values(probs, vhist, ts):
    n_keys = WINDOW + CHUNK
    outs = []
    for (c, kv), p in zip(_attn_groups(ts), probs):
        v_g = vhist[c * CHUNK:c * CHUNK + n_keys, kv * HEAD_DIM:(kv + 1) * HEAD_DIM].astype(BF16)
        outs.append(_unstack_heads(_dot_tn(p, v_g), CHUNK))
    rows = [jnp.concatenate(outs[c * N_KV_HEADS:(c + 1) * N_KV_HEADS], axis=1) for c in range(ts // CHUNK)]
    return jnp.concatenate(rows, axis=0).astype(BF16)


def _ssd_cumsum(dt, a_row, e):
    ts = dt.shape[0]
    L = SSM_CHUNK
    span = min(ts, MXU_TILE)
    ri = lax.broadcasted_iota(jnp.int32, (span, span), 0)
    lag = ri - lax.broadcasted_iota(jnp.int32, (span, span), 1)
    tri = jnp.where((lag >= 0) & (lag <= (ri & (L - 1))), 1.0, 0.0).astype(BF16)
    hi, mid, lo = _split3(dt * a_row)
    cum = jnp.concatenate(
        [_dot(tri, hi[r:r + span]) + _dot(tri, mid[r:r + span]) + _dot(tri, lo[r:r + span])
         for r in range(0, ts, span)], axis=0)
    hi = cum.astype(BF16)
    mid = (cum - hi.astype(F32)).astype(BF16)
    cum_e = _dot(hi, e) + _dot(mid, e)
    dt_e = _dot(dt.astype(BF16), e)
    return cum_e, dt_e


def _ssd_decays(xact, cum_e, dt_e):
    ts = xact.shape[0]
    L = SSM_CHUNK
    nc = ts // L
    xs = xact[:, 0:SSM_INNER]
    cum_e = cum_e.reshape(nc, L, SSM_INNER)
    t3 = lax.broadcasted_iota(jnp.int32, cum_e.shape, 1)
    slot3 = lax.broadcasted_iota(jnp.int32, cum_e.shape, 2) & (SSM_HEAD_DIM - 1)
    cum_src = jnp.sum(jnp.where(t3 == slot3, cum_e, 0.0), axis=1, keepdims=True)
    decay = jnp.exp(jnp.where(t3 >= slot3, cum_e - cum_src, -jnp.inf))
    last = cum_e[:, L - 1:L, :]
    exp_cum = jnp.exp(cum_e)
    chunk_decay = jnp.exp(last)
    xdt = xs * dt_e
    wx = (xdt.reshape(nc, L, SSM_INNER) * jnp.exp(last - cum_e)).astype(BF16)
    first_head = (lax.broadcasted_iota(jnp.int32, xdt.shape, 1) & (2 * SSM_HEAD_DIM - 1)) < SSM_HEAD_DIM
    xdt_b16 = xdt.astype(BF16)
    return {
        "xs": xs,
        "bm": xact[:, SSM_INNER:SSM_INNER + SSM_GROUPS * SSM_STATE].astype(BF16),
        "cm": xact[:, SSM_INNER + SSM_GROUPS * SSM_STATE:CONV_DIM].astype(BF16),
        "decay": decay, "exp_cum": exp_cum, "chunk_decay": chunk_decay, "wx": wx,
        "xdt_a": jnp.where(first_head, xdt_b16, jnp.zeros_like(xdt_b16)),
        "xdt_b": jnp.where(first_head, jnp.zeros_like(xdt_b16), xdt_b16),
    }


def _ssd_block(ctx, st, dskip_e, fillers):
    L = SSM_CHUNK
    nc = ctx["xs"].shape[0] // L
    bm, cm = ctx["bm"], ctx["cm"]
    fillers = iter(fillers)

    def grp(a, c, g, width):
        return a[c * L:(c + 1) * L, g * width:(g + 1) * width]

    chunk_state = [jnp.concatenate([_dot_tn(grp(bm, c, g, SSM_STATE), ctx["wx"][c][:, g * GROUP_INNER:(g + 1) * GROUP_INNER])
                                    for g in range(SSM_GROUPS)], axis=1) for c in range(nc)]
    next(fillers)()
    cb = [[_dot_nt(grp(cm, c, g, SSM_STATE), jnp.concatenate([grp(bm, c, g, SSM_STATE)] * 2, axis=0))
           for g in range(SSM_GROUPS)] for c in range(nc)]
    next(fillers)()
    st_in = []
    for c in range(nc):
        st_in.append(st.astype(BF16))
        st = st * ctx["chunk_decay"][c] + chunk_state[c]
    y_intra = []
    for c in range(nc):
        rows = slice(c * L, (c + 1) * L)
        parts = []
        for k in range(SSM_HEADS // 2):
            ls = slice(k * 2 * SSM_HEAD_DIM, (k + 1) * 2 * SSM_HEAD_DIM)
            m_pair = (cb[c][2 * k // HEADS_PER_GROUP] * ctx["decay"][c][:, ls]).astype(BF16)
            block_diag = jnp.concatenate([ctx["xdt_a"][rows, ls], ctx["xdt_b"][rows, ls]], axis=0)
            parts.append(_dot(m_pair, block_diag))
        y_intra.append(jnp.concatenate(parts, axis=1))
    next(fillers)()
    ys = []
    for c in range(nc):
        y_inter = jnp.concatenate(
            [_dot(grp(cm, c, g, SSM_STATE), st_in[c][:, g * GROUP_INNER:(g + 1) * GROUP_INNER])
             for g in range(SSM_GROUPS)], axis=1)
        ys.append(y_intra[c] + y_inter * ctx["exp_cum"][c] + dskip_e * ctx["xs"][c * L:(c + 1) * L])
    next(fillers)()
    return jnp.concatenate(ys, axis=0), st


class _Stagger:
    def __init__(self):
        self._pending = None

    def push(self, produce, consume):
        val = produce()
        self.flush()
        self._pending = (consume, val)

    def flush(self):
        if self._pending is not None:
            consume, val = self._pending
            self._pending = None
            consume(val)


def _prompt_kernel(x_ref, mod_ref, gmix_ref, wa_ref, wb_ref, wdt_ref, cos_ref, sa_ref, sb_ref, sinks_ref,
                   convw_ref, convb_ref, dtb_ref, alog_ref, dskip_ref, ssmnw_ref, glng_ref, glnb_ref,
                   gws_ref, gbst_ref, wao_ref, wso_ref, wgo_ref, wout_ref, e_ref,
                   xo_ref, ko_ref, vo_ref, sto_ref, cvo_ref,
                   khist, vhist, xp, st_s):
    ts = PROMPT_TS
    s = pl.program_id(1)
    last = pl.num_programs(1) - 1

    @pl.when(s == 0)
    def _():
        khist[0:WINDOW, :] = jnp.zeros((WINDOW, KV_W), F32)
        vhist[0:WINDOW, :] = jnp.zeros((WINDOW, KV_W), F32)
        xp[...] = jnp.zeros_like(xp)
        st_s[...] = jnp.zeros_like(st_s)

    x3 = x_ref[...]
    mod3 = mod_ref[...]
    slab = ts // NORM_SLABS
    h_parts, q0_parts = [], []
    for r in range(0, ts, slab):
        h_parts.append(_modnorm(x3[:, r:r + slab], gmix_ref[...], mod3))
        q0_parts.append(_dot(h_parts[-1], wa_ref[:, C_Q[1]:C_Q[1] + PROJ_BLOCK]))
    q0 = jnp.concatenate(q0_parts, axis=0)
    h = jnp.concatenate(h_parts, axis=0)

    cos, sa, sb = cos_ref[...], sa_ref[...], sb_ref[...]
    res = {}

    def rope_block(raw):
        return [_rope(raw[:, i * LANES:(i + 1) * LANES], cos, sa, sb) for i in range(raw.shape[1] // LANES)]

    def q_block(raw):
        res.setdefault("q", []).extend(p.astype(BF16) for p in rope_block(raw * (HEAD_DIM ** -0.5)))

    def kv_block(raw):
        khist[WINDOW:WINDOW + ts, :] = rope_block(raw[:, 0:KV_W])[0]
        vhist[WINDOW:WINDOW + ts, :] = raw[:, KV_W:2 * KV_W]

    def conv_block(lo, raw):
        cs = slice(lo, lo + PROJ_BLOCK)
        ext = jnp.concatenate([xp[:, cs], raw], axis=0)
        acc = convb_ref[:, cs] + raw * convw_ref[CONV_WIDTH - 1:CONV_WIDTH, cs]
        for j in range(1, CONV_WIDTH):
            acc = acc + pltpu.roll(ext, j, 0)[8:8 + ts] * convw_ref[CONV_WIDTH - 1 - j:CONV_WIDTH - j, cs]
        xp[:, cs] = raw[ts - 8:ts]
        res.setdefault("xact", []).append(_silu_of_half(acc))

    def collect(key, fn):
        return lambda raw: res.setdefault(key, []).append(fn(raw))

    wrefs = (wa_ref, wb_ref, wdt_ref)
    pipe = _Stagger()

    def push_proj(cols, off, width, consume):
        ref, lo = wrefs[cols[0]], cols[1] + off
        pipe.push(lambda: _dot(h, ref[:, lo:lo + width]), consume)

    def blocks(cols, consume):
        return [functools.partial(push_proj, cols, off, PROJ_BLOCK, consume)
                for off in range(0, cols[2] - cols[1], PROJ_BLOCK)]

    conv_blocks = [functools.partial(push_proj, C_XBC, off, PROJ_BLOCK, functools.partial(conv_block, off))
                   for off in range(0, CONV_DIM, PROJ_BLOCK)]
    gate_blocks = iter(blocks(C_G0, collect("g0", _twice_sigmoid_of_half))
                       + blocks(C_G1, collect("g1", _twice_sigmoid_of_half))
                       + blocks(C_G2, collect("g2", _twice_sigmoid_of_half)))
    z_blocks = blocks(C_Z, collect("sz", _silu_of_half))
    gm_blocks = blocks(C_GU, collect("u", _gelu)) + blocks(C_GV, collect("gv", _gelu))

    def cat(key):
        return jnp.concatenate(res[key], axis=1)

    pipe.push(lambda: q0, q_block)
    for blk in blocks(C_Q, q_block)[1:]:
        blk()
    push_proj(C_K, 0, 2 * KV_W, kv_block)
    conv_blocks[0]()
    next(gate_blocks)()
    conv_blocks[1]()
    scores = _attn_scores(cat("q"), khist, s * ts)
    conv_blocks[2]()
    probs = _attn_probs(scores, sinks_ref, ts)
    next(gate_blocks)()
    conv_blocks[3]()
    attn = _attn_values(probs, vhist, ts)
    conv_blocks[4]()
    next(gate_blocks)()
    conv_blocks[5]()
    push_proj(C_DT, 0, LANES, collect("dt", lambda raw: _softplus(raw + dtb_ref[...])))
    next(gate_blocks)()
    a_out = _dot(attn, wao_ref[...])
    z_blocks[0]()
    cum_e, dt_e = _ssd_cumsum(res["dt"][0], -jnp.exp(alog_ref[...]), e_ref[...])
    z_blocks[1]()
    merged = cat("g0") * a_out
    ctx = _ssd_decays(cat("xact"), cum_e, dt_e)
    z_blocks[2]()
    z_blocks[3]()
    y, st_s[...] = _ssd_block(ctx, st_s[...], dskip_ref[...], gm_blocks)
    next(gate_blocks)()
    vn = _layer_norm(cat("gv"), glng_ref[...], glnb_ref[...])
    u = cat("u")
    next(gate_blocks)()
    ssm = _rms(y * cat("sz"), ssmnw_ref[...]).astype(BF16)
    gm = jnp.concatenate(
        [_gmlp_chunk(vn[c * GM_CHUNK:(c + 1) * GM_CHUNK], u[c * GM_CHUNK:(c + 1) * GM_CHUNK], gws_ref, gbst_ref,
                     GM_CHUNK) for c in range(ts // GM_CHUNK)], axis=0).astype(BF16)
    next(gate_blocks)()
    b_out = _dot(ssm, wso_ref[...])
    next(gate_blocks)()
    pipe.flush()
    merged = merged + cat("g1") * b_out
    c_out = _dot(gm, wgo_ref[...])
    for blk in gate_blocks:
        blk()
    pipe.flush()
    merged = merged + cat("g2") * c_out
    o = _dot(merged.astype(BF16), wout_ref[...])
    xo_ref[...] = x3 + mod3[:, 2:3, :] * o.reshape(x3.shape)

    khist[0:WINDOW, :] = khist[ts:ts + WINDOW, :]
    vhist[0:WINDOW, :] = vhist[ts:ts + WINDOW, :]

    @pl.when(s == last)
    def _():
        ko_ref[0] = khist[ts:ts + WINDOW, :]
        vo_ref[0] = vhist[ts:ts + WINDOW, :]
        cvo_ref[0] = xp[8 - (CONV_WIDTH - 1):8, :]
        sto_ref[0] = st_s[...].T


def _sample_attention(q, k_new, v_new, ck_ref, cv_ref, sinks_ref, nb):
    t = q.shape[0] // nb
    groups = [(i, kv) for i in range(nb) for kv in range(N_KV_HEADS)]

    def new_rows(a, i, kv):
        return a[i * t:(i + 1) * t, kv * HEAD_DIM:(kv + 1) * HEAD_DIM].astype(BF16)

    def cached(ref, i, kv):
        return ref[i, kv].T.astype(BF16)

    qg = [_stack_heads(q[i * t:(i + 1) * t], kv, t) for i, kv in groups]
    s_c = [_dot_nt(cached(ck_ref, i, kv), qg[n]) for n, (i, kv) in enumerate(groups)]
    s_n = [_dot_nt(new_rows(k_new, i, kv), qg[n]) for n, (i, kv) in enumerate(groups)]
    sinks = [_sink_row(sinks_ref, kv, t) for kv in range(N_KV_HEADS)]
    probs = []
    for n, (i, kv) in enumerate(groups):
        m = jnp.maximum(jnp.maximum(jnp.max(s_c[n], axis=0, keepdims=True), jnp.max(s_n[n], axis=0, keepdims=True)),
                        sinks[kv])
        p_c = jnp.exp(s_c[n] - m)
        p_n = jnp.exp(s_n[n] - m)
        inv = 1.0 / (jnp.sum(p_c, axis=0, keepdims=True) + jnp.sum(p_n, axis=0, keepdims=True)
                     + jnp.exp(sinks[kv] - m))
        probs.append(((p_c * inv).astype(BF16), (p_n * inv).astype(BF16)))
    outs = [_dot_tn(probs[n][0], cached(cv_ref, i, kv)) + _dot_tn(probs[n][1], new_rows(v_new, i, kv))
            for n, (i, kv) in enumerate(groups)]
    rows = [jnp.concatenate([_unstack_heads(outs[i * N_KV_HEADS + kv], t) for kv in range(N_KV_HEADS)], axis=1)
            for i in range(nb)]
    return jnp.concatenate(rows, axis=0)


def _sample_ssd_setup(xact, dt, a_row, e, L):
    m = xact.shape[0]
    n_seq = m // L
    xs = xact[:, 0:SSM_INNER]
    ri = lax.broadcasted_iota(jnp.int32, (m, m), 0)
    lag = ri - lax.broadcasted_iota(jnp.int32, (m, m), 1)
    tri = jnp.where((lag >= 0) & (lag <= (ri & (L - 1))), 1.0, 0.0).astype(BF16)
    hi, mid, lo = _split3(dt * a_row)
    cum = _dot(tri, hi) + _dot(tri, mid) + _dot(tri, lo)
    hi = cum.astype(BF16)
    mid = (cum - hi.astype(F32)).astype(BF16)
    cum_e = (_dot(hi, e) + _dot(mid, e)).reshape(n_seq, L, SSM_INNER)
    head_of_lane = jnp.right_shift(lax.broadcasted_iota(jnp.int32, (LANES, SSM_HEADS * L), 1), _log2(L))
    e_slots = jnp.where(head_of_lane == lax.broadcasted_iota(jnp.int32, (LANES, SSM_HEADS * L), 0), 1.0, 0.0)
    e_slots = e_slots.astype(BF16)
    cum_s = (_dot(hi, e_slots) + _dot(mid, e_slots)).reshape(n_seq, L, SSM_HEADS * L)
    dt_e = _dot(dt.astype(BF16), e)
    t3 = lax.broadcasted_iota(jnp.int32, cum_s.shape, 1)
    slot3 = lax.broadcasted_iota(jnp.int32, cum_s.shape, 2) & (L - 1)
    cum_src = jnp.sum(jnp.where(t3 == slot3, cum_s, 0.0), axis=1, keepdims=True)
    decay = jnp.exp(jnp.where(t3 >= slot3, cum_s - cum_src, -jnp.inf))
    last = cum_e[:, L - 1:L, :]
    xdt = xs * dt_e
    wx = (xdt.reshape(n_seq, L, SSM_INNER) * jnp.exp(last - cum_e)).astype(BF16)
    per_head = jnp.exp(cum.reshape(n_seq, L, LANES)[:, L - 1:L, :])
    rep = jnp.broadcast_to(per_head, (n_seq, SSM_HEADS, LANES)).reshape(n_seq * SSM_HEADS, LANES)
    own_lane = (lax.broadcasted_iota(jnp.int32, rep.shape, 1)
                == (lax.broadcasted_iota(jnp.int32, rep.shape, 0) & (SSM_HEADS - 1)))
    hi, mid, lo = _split3(jnp.where(own_lane, rep, 0.0))
    ones = jnp.ones((LANES, LANES), BF16)
    splat = (_dot(hi, ones) + _dot(mid, ones) + _dot(lo, ones)).reshape(n_seq, SSM_HEADS, LANES)
    return {
        "L": L, "xs": xs,
        "bm": xact[:, SSM_INNER:SSM_INNER + SSM_GROUPS * SSM_STATE].astype(BF16),
        "cm": xact[:, SSM_INNER + SSM_GROUPS * SSM_STATE:CONV_DIM].astype(BF16),
        "decay": decay, "exp_cum": jnp.exp(cum_e), "wx": wx, "xdt": xdt.astype(BF16), "state_decay": splat,
    }


def _sample_ssd_block(s, h0_ref, sto_ref, dskip_e, nb):
    L = s["L"]
    pairs = [(i, g) for i in range(nb) for g in range(SSM_GROUPS)]
    ri = jnp.right_shift(lax.broadcasted_iota(jnp.int32, (HEADS_PER_GROUP * L, GROUP_INNER), 0), _log2(L))
    li = jnp.right_shift(lax.broadcasted_iota(jnp.int32, (HEADS_PER_GROUP * L, GROUP_INNER), 1), _log2(SSM_HEAD_DIM))

    def rows(a, i, g, width):
        return a[i * L:(i + 1) * L, g * width:(g + 1) * width]

    cb = [_dot_nt(rows(s["cm"], i, g, SSM_STATE), jnp.concatenate([rows(s["bm"], i, g, SSM_STATE)] * HEADS_PER_GROUP,
                                                                  axis=0)) for i, g in pairs]
    upd = [_dot_tn(s["wx"][i][:, g * GROUP_INNER:(g + 1) * GROUP_INNER], rows(s["bm"], i, g, SSM_STATE))
           for i, g in pairs]
    y_inter = [_dot_nt(rows(s["cm"], i, g, SSM_STATE), h0_ref[i][g * GROUP_INNER:(g + 1) * GROUP_INNER, :].astype(BF16))
               for i, g in pairs]
    y_intra = []
    for n, (i, g) in enumerate(pairs):
        gs = slice(g * HEADS_PER_GROUP * L, (g + 1) * HEADS_PER_GROUP * L)
        m_g = (cb[n] * s["decay"][i][:, gs]).astype(BF16)
        tiled = jnp.concatenate([rows(s["xdt"], i, g, GROUP_INNER)] * HEADS_PER_GROUP, axis=0)
        y_intra.append(_dot(m_g, jnp.where(ri == li, tiled, jnp.zeros_like(tiled))))
    ys = []
    for i in range(nb):
        parts = [y_intra[i * SSM_GROUPS + g] + y_inter[i * SSM_GROUPS + g]
                 * s["exp_cum"][i][:, g * GROUP_INNER:(g + 1) * GROUP_INNER] for g in range(SSM_GROUPS)]
        ys.append(jnp.concatenate(parts, axis=1) + dskip_e * s["xs"][i * L:(i + 1) * L])
        decayed = h0_ref[i].reshape(SSM_HEADS, SSM_HEAD_DIM, SSM_STATE) * s["state_decay"][i][:, None, :]
        sto_ref[i] = decayed.reshape(SSM_INNER, SSM_STATE) + jnp.concatenate(
            upd[i * SSM_GROUPS:(i + 1) * SSM_GROUPS], axis=0)
    return jnp.concatenate(ys, axis=0)


def _sample_kernel(x_ref, mod_ref, gmix_ref, wa_ref, wb_ref, wdt_ref, cos_ref, sa_ref, sb_ref, sinks_ref,
                   ck_ref, cv_ref, h0_ref, cs_ref,
                   convw_ref, convb_ref, dtb_ref, alog_ref, dskip_ref, ssmnw_ref, glng_ref, glnb_ref,
                   gws_ref, gbst_ref, wao_ref, wso_ref, wgo_ref, wout_ref, e_ref,
                   xo_ref, ko_ref, vo_ref, sto_ref, cvo_ref, gvo_ref,
                   xp):
    nb = SAMPLE_NB
    t = x_ref.shape[1]
    m = nb * t
    win_ref = (wa_ref, wb_ref, wdt_ref)
    x3 = x_ref[...]
    mod3 = mod_ref[...]
    h = _modnorm(x3, gmix_ref[...], mod3)

    cos, sa, sb = cos_ref[...], sa_ref[...], sb_ref[...]
    q = _proj(h, win_ref, C_Q) * (HEAD_DIM ** -0.5)
    q = jnp.concatenate(
        [_rope(q[:, i * LANES:(i + 1) * LANES], cos, sa, sb) for i in range(Q_W // LANES)], axis=1).astype(BF16)
    k_new = _rope(_proj(h, win_ref, C_K), cos, sa, sb)
    v_new = _proj(h, win_ref, C_V)
    ko_ref[...] = k_new.reshape(nb, t, KV_W)
    vo_ref[...] = v_new.reshape(nb, t, KV_W)

    xp[:, 8 - (CONV_WIDTH - 1):8, :] = cs_ref[...]
    xp[:, 8:8 + t, :] = _proj(h, win_ref, C_XBC).reshape(nb, t, CONV_DIM)
    acc = convb_ref[...] + xp[:, 8:8 + t, :] * convw_ref[CONV_WIDTH - 1:CONV_WIDTH, :]
    for j in range(1, CONV_WIDTH):
        acc = acc + xp[:, 8 - j:8 - j + t, :] * convw_ref[CONV_WIDTH - 1 - j:CONV_WIDTH - j, :]
    xact = _silu_of_half(acc).reshape(m, CONV_DIM)
    cvo_ref[...] = xp[:, t + 8 - (CONV_WIDTH - 1):t + 8, :]
    dt = _softplus(_proj(h, win_ref, C_DT) + dtb_ref[...])
    dskip_e = dskip_ref[...]

    u = _gelu(_proj(h, win_ref, C_GU))
    vn = _layer_norm(_gelu(_proj(h, win_ref, C_GV)), glng_ref[...], glnb_ref[...])
    gvo_ref[...] = vn.reshape(nb, t, GM_WIDTH)

    ssd = _sample_ssd_setup(xact, dt, -jnp.exp(alog_ref[...]), e_ref[...], t)
    attn = _sample_attention(q, k_new, v_new, ck_ref, cv_ref, sinks_ref, nb)
    y = _sample_ssd_block(ssd, h0_ref, sto_ref, dskip_e, nb)
    gm_rows = [_gmlp_chunk(vn[i * t:(i + 1) * t], u[i * t:(i + 1) * t], gws_ref, gbst_ref, t) for i in range(nb)]

    a_out = _dot(attn.astype(BF16), wao_ref[...])
    z = _proj(h, win_ref, C_Z)
    ssm = _rms(y * _silu_of_half(z), ssmnw_ref[...]).astype(BF16)
    b_out = _dot(ssm, wso_ref[...])
    c_out = _dot(jnp.concatenate(gm_rows, axis=0).astype(BF16), wgo_ref[...])
    gates = [_twice_sigmoid_of_half(_proj(h, win_ref, cols)) for cols in (C_G0, C_G1, C_G2)]
    xo_ref[...] = _merge_out(x3, mod3, gates, a_out, b_out, c_out, wout_ref)


def _ffn_block(x_ref, mod_ref, gff_ref, w1_ref, w2_ref, gfin_ref, o_ref, final_norm):
    x3 = x_ref[...]
    mod3 = mod_ref[...]
    nb, t, d = x3.shape
    h3 = _rms(x3, gff_ref[...]) * (1.0 + mod3[:, 4:5, :]) + mod3[:, 3:4, :]
    h = h3.reshape(nb * t, d).astype(BF16)
    a = jnp.maximum(_dot(h, w1_ref[...]), 0.0)
    y = _dot((a * a).astype(BF16), w2_ref[...])
    out = x3 + mod3[:, 5:6, :] * y.reshape(nb, t, d)
    if final_norm:
        out = _rms(out, gfin_ref[...])
    o_ref[...] = out


def _ffn_kernel(xp_ref, modp_ref, xs_ref, mods_ref, gff_ref, w1_ref, w2_ref, gfin_ref, op_ref, os_ref, *,
                final_norm, n_prompt_steps):
    i = pl.program_id(0)

    @pl.when(i < n_prompt_steps)
    def _():
        _ffn_block(xp_ref, modp_ref, gff_ref, w1_ref, w2_ref, gfin_ref, op_ref, final_norm)

    @pl.when(i == n_prompt_steps)
    def _():
        _ffn_block(xs_ref, mods_ref, gff_ref, w1_ref, w2_ref, gfin_ref, os_ref, final_norm)


def _ada_kernel(c_ref, w_ref, b_ref, op_ref, os_ref):
    mod = _dot(_silu(c_ref[...]).astype(BF16), w_ref[...].astype(BF16)) + b_ref[...]
    n_prompt = op_ref.shape[0]
    op_ref[...] = mod[0:n_prompt]
    os_ref[...] = mod[n_prompt:]


def _w_in_piece_kernel(wt_ref, o_ref, *, half_lo, half_hi, n_valid):
    cols = wt_ref.shape[0]
    col = pl.program_id(1) * cols + lax.broadcasted_iota(jnp.int32, (cols, 1), 0)
    scale = jnp.where(col < n_valid, jnp.where((col >= half_lo) & (col < half_hi), 0.5, 1.0), 0.0)
    o_ref[...] = (wt_ref[...] * scale).astype(BF16).T


def _w_in_piece(w_t, first_col, n_cols, half_cols, n_blocks, n_valid=None):
    cols = n_cols // n_blocks
    n_valid = n_cols if n_valid is None else n_valid
    return pl.pallas_call(
        functools.partial(_w_in_piece_kernel, half_lo=half_cols[0], half_hi=half_cols[1], n_valid=n_valid),
        grid=(DEPTH, n_blocks),
        in_specs=[pl.BlockSpec((pl.Squeezed(), pl.Element(cols), pl.Element(D_MODEL)),
                               lambda l, j: (l, pl.multiple_of(first_col + j * cols, SSM_HEADS), 0))],
        out_specs=pl.BlockSpec((None, D_MODEL, cols), lambda l, j: (l, 0, j)),
        out_shape=jax.ShapeDtypeStruct((DEPTH, D_MODEL, n_cols), BF16),
        compiler_params=_params(2),
        name="w_in_piece",
    )(w_t)


def _w_in_split(w_in):
    w_t = jnp.swapaxes(w_in, 1, 2)
    b_lo = N_IN_A + SSM_HEADS
    return (_w_in_piece(w_t, 0, N_IN_A, (C_Z[1], C_Z[2]), W_SPLIT_BLOCKS),
            _w_in_piece(w_t, b_lo, N_IN_B, (C_G0[1], N_IN_B), W_SPLIT_BLOCKS),
            _w_in_piece(w_t, N_IN_A, LANES, (0, 0), 1, n_valid=SSM_HEADS))


def _with_ignored_inputs(body, n_in, n_ignored):
    if n_ignored == 0:
        return body
    return lambda *refs: body(*refs[:n_in], *refs[n_in + n_ignored:])


def _const_spec(shape):
    return pl.BlockSpec(shape, lambda *_: (0,) * len(shape), pipeline_mode=pl.Buffered(1))


def _layer_spec(l, shape):
    return pl.BlockSpec((None,) + shape, lambda *_: (l,) + (0,) * len(shape), pipeline_mode=pl.Buffered(1))


def _any_spec():
    return pl.BlockSpec(memory_space=pl.ANY)


def _smem_spec():
    return pl.BlockSpec(memory_space=pltpu.SMEM)


def _params(n_grid):
    return pltpu.CompilerParams(dimension_semantics=("arbitrary",) * n_grid, vmem_limit_bytes=VMEM_LIMIT)


def _ada_call(c_prompt, c_sample, w_ada, b_ada):
    n_p, n_s = c_prompt.shape[0], c_sample.shape[0]
    c_all = jnp.concatenate([c_prompt, c_sample], axis=0)
    tn = 2 * D_MODEL
    n_tiles = w_ada.shape[2] // tn
    mod_p, mod_s = pl.pallas_call(
        _ada_kernel,
        grid=(DEPTH, n_tiles),
        in_specs=[pl.BlockSpec((n_p + n_s, D_MODEL), lambda l, n: (0, 0)),
                  pl.BlockSpec((None, D_MODEL, tn), lambda l, n: (l, 0, n)),
                  pl.BlockSpec((None, 1, tn), lambda l, n: (l, 0, n))],
        out_specs=(pl.BlockSpec((None, n_p, tn), lambda l, n: (l, 0, n)),
                   pl.BlockSpec((None, n_s, tn), lambda l, n: (l, 0, n))),
        out_shape=(jax.ShapeDtypeStruct((DEPTH, n_p, w_ada.shape[2]), F32),
                   jax.ShapeDtypeStruct((DEPTH, n_s, w_ada.shape[2]), F32)),
        compiler_params=_params(2),
        name="ada_mod",
    )(c_all, w_ada, b_ada.reshape(DEPTH, 1, -1))
    return mod_p.reshape(DEPTH, n_p, 6, D_MODEL), mod_s.reshape(DEPTH, n_s, 6, D_MODEL)


def _w_in_specs(l):
    return [_layer_spec(l, (D_MODEL, N_IN_A)), _layer_spec(l, (D_MODEL, N_IN_B)), _layer_spec(l, (D_MODEL, LANES))]


def _layer_weight_specs(l):
    return [
        _layer_spec(l, (CONV_WIDTH, CONV_DIM)), _layer_spec(l, (1, CONV_DIM)), _layer_spec(l, (1, LANES)),
        _layer_spec(l, (1, LANES)), _layer_spec(l, (1, SSM_INNER)), _layer_spec(l, (1, SSM_INNER)),
        _layer_spec(l, (1, GM_WIDTH)), _layer_spec(l, (1, GM_WIDTH)),
        _layer_spec(l, (GM_GROUPS, GM_CHUNK, GM_CHUNK)), _layer_spec(l, (GM_CHUNK, GM_GROUPS)),
        _layer_spec(l, (Q_W, D_MODEL)), _layer_spec(l, (SSM_INNER, D_MODEL)), _layer_spec(l, (GM_WIDTH, D_MODEL)),
        _layer_spec(l, (D_MODEL, D_MODEL)), _const_spec((LANES, SSM_INNER)),
    ]


def _layer_weight_args(p):
    return (p["conv_w"], p["conv_b"], p["dt_bias"], p["a_log"], p["d_skip"], p["ssm_norm_w"], p["gm_ln_g"],
            p["gm_ln_b"], p["gm_w_s"], p["gm_b_st"], p["w_attn_o"], p["w_ssm_o"], p["w_gm_o"], p["w_out"], p["expand"])


def _prompt_mixer(l, x, mod, p, rope, prev):
    bsz, seq, d = x.shape
    ts = PROMPT_TS
    tab = pl.BlockSpec((ts, LANES), lambda b, s: (s, 0))
    in_specs = [
        pl.BlockSpec((1, ts, d), lambda b, s: (b, s, 0)),
        pl.BlockSpec((None, 1, 6, d), lambda b, s: (l, b, 0, 0)),
        _layer_spec(l, (1, d)), *_w_in_specs(l),
        tab, tab, tab, _smem_spec(),
    ] + _layer_weight_specs(l)
    args = (x, mod, p["g_mix"], *p["w_in"], *rope, p["sinks"][l], *_layer_weight_args(p))
    state_tails = ((WINDOW, KV_W), (WINDOW, KV_W), (SSM_INNER, SSM_STATE), (CONV_WIDTH - 1, CONV_DIM))
    out_shape = (jax.ShapeDtypeStruct((bsz, seq, d), F32),) + tuple(
        jax.ShapeDtypeStruct((DEPTH, bsz) + tail, F32) for tail in state_tails)
    out_specs = (pl.BlockSpec((1, ts, d), lambda b, s: (b, s, 0)),) + tuple(
        pl.BlockSpec((None, 1) + tail, lambda b, s: (l, b, 0, 0)) for tail in state_tails)
    scratch = [
        pltpu.VMEM((WINDOW + ts, KV_W), F32), pltpu.VMEM((WINDOW + ts, KV_W), F32),
        pltpu.VMEM((8, CONV_DIM), F32), pltpu.VMEM((SSM_STATE, SSM_INNER), F32),
    ]
    n_in = len(args)
    prev = () if prev is None else tuple(prev)
    return pl.pallas_call(
        _with_ignored_inputs(_prompt_kernel, n_in, len(prev)),
        grid=(bsz, seq // ts), in_specs=in_specs + [_any_spec()] * len(prev), out_specs=out_specs,
        out_shape=out_shape, scratch_shapes=scratch, compiler_params=_params(2), name="prompt_mixer",
        input_output_aliases={n_in + i: 1 + i for i in range(len(prev))},
    )(*args, *prev)


def _sample_mixer(l, x, mod, cache_k, cache_v, h0, conv_state, p, rope, prev):
    bsz, t, d = x.shape
    nb = SAMPLE_NB
    m = nb * t

    def blk(*tail):
        return pl.BlockSpec((nb,) + tail, lambda i: (i,) + (0,) * len(tail))

    def lblk(*tail):
        return pl.BlockSpec((None, nb) + tail, lambda i: (l, i) + (0,) * len(tail))

    in_specs = [
        blk(t, d), lblk(6, d), _layer_spec(l, (1, d)), *_w_in_specs(l),
        _const_spec((m, LANES)), _const_spec((m, LANES)), _const_spec((m, LANES)), _smem_spec(),
        lblk(N_KV_HEADS, HEAD_DIM, WINDOW), lblk(N_KV_HEADS, HEAD_DIM, WINDOW), lblk(SSM_INNER, SSM_STATE),
        lblk(CONV_WIDTH - 1, CONV_DIM),
    ] + _layer_weight_specs(l)
    args = (x, mod, p["g_mix"], *p["w_in"], *rope, p["sinks"][l], cache_k, cache_v, h0, conv_state,
            *_layer_weight_args(p))
    state_tails = ((t, KV_W), (t, KV_W), (SSM_INNER, SSM_STATE), (CONV_WIDTH - 1, CONV_DIM), (t, GM_WIDTH))
    out_shape = (jax.ShapeDtypeStruct((bsz, t, d), F32),) + tuple(
        jax.ShapeDtypeStruct((DEPTH, bsz) + tail, F32) for tail in state_tails)
    out_specs = (blk(t, d),) + tuple(lblk(*tail) for tail in state_tails)
    scratch = [pltpu.VMEM((nb, 8 + t, CONV_DIM), F32)]
    n_in = len(args)
    prev = () if prev is None else tuple(prev)
    return pl.pallas_call(
        _with_ignored_inputs(_sample_kernel, n_in, len(prev)),
        grid=(bsz // nb,), in_specs=in_specs + [_any_spec()] * len(prev), out_specs=out_specs,
        out_shape=out_shape, scratch_shapes=scratch, compiler_params=_params(1), name="sample_mixer",
        input_output_aliases={n_in + i: 1 + i for i in range(len(prev))},
    )(*args, *prev)


def _ffn(l, x_p, mod_p, x_s, mod_s, p, g_final, final_norm):
    bp, sp, d = x_p.shape
    bs, ss, _ = x_s.shape
    assert bs * ss == FFN_ROWS and sp % FFN_ROWS == 0
    blocks_per_seq = sp // FFN_ROWS
    n_prompt_steps = bp * blocks_per_seq

    def prompt_block(i):
        j = jnp.minimum(i, n_prompt_steps - 1)
        return j // blocks_per_seq, j % blocks_per_seq

    x_spec = pl.BlockSpec((1, FFN_ROWS, d), lambda i: (*prompt_block(i), 0))
    s_spec = pl.BlockSpec((bs, ss, d), lambda i: (0, 0, 0))
    return pl.pallas_call(
        functools.partial(_ffn_kernel, final_norm=final_norm, n_prompt_steps=n_prompt_steps),
        grid=(n_prompt_steps + 1,),
        in_specs=[x_spec, pl.BlockSpec((None, 1, 6, d), lambda i: (l, prompt_block(i)[0], 0, 0)),
                  s_spec, pl.BlockSpec((None, bs, 6, d), lambda i: (l, 0, 0, 0)),
                  _layer_spec(l, (1, d)), _layer_spec(l, (d, D_FF)), _layer_spec(l, (D_FF, d)), _const_spec((1, d))],
        out_specs=(x_spec, s_spec),
        out_shape=(jax.ShapeDtypeStruct(x_p.shape, F32), jax.ShapeDtypeStruct(x_s.shape, F32)),
        compiler_params=pltpu.CompilerParams(dimension_semantics=("arbitrary",), vmem_limit_bytes=VMEM_LIMIT,
                                             allow_input_fusion=[False] * 5 + [True, True, False]),
        name="ffn",
    )(x_p, mod_p, x_s, mod_s, p["g_ff"], p["w_ff1"], p["w_ff2"], g_final)


def _rope_tables(pos):
    half = ROT_DIM // 2
    inv_freq = ROPE_THETA ** (-jnp.arange(half, dtype=F32) * (2.0 / ROT_DIM))
    ang = pos.astype(F32)[:, None] * inv_freq[None, :]
    cos, sin = jnp.cos(ang), jnp.sin(ang)
    n = pos.shape[0]
    ones = jnp.ones((n, HEAD_DIM - ROT_DIM), F32)
    zeros = jnp.zeros((n, HEAD_DIM - ROT_DIM), F32)
    zh = jnp.zeros((n, half), F32)
    cos_t = jnp.concatenate([cos, cos, ones], axis=1)
    sin_a = jnp.concatenate([-sin, zh, zeros], axis=1)
    sin_b = jnp.concatenate([zh, sin, zeros], axis=1)
    rep = LANES // HEAD_DIM
    return tuple(jnp.tile(a, (1, rep)) for a in (cos_t, sin_a, sin_b))


def _stacked_params(w_in, g_mix, sinks, conv_w, conv_b, dt_bias, a_log, d_skip, ssm_norm_w, gm_ln_g, gm_ln_b,
                    gm_w_s, gm_b_s, w_attn_o, w_ssm_o, w_gm_o, w_out, g_ff, w_ff1, w_ff2):
    w_r = _w_in_split(w_in)
    pad = jnp.zeros((DEPTH, LANES - SSM_HEADS), F32)
    expand = (jnp.arange(SSM_INNER)[None, :] // SSM_HEAD_DIM == jnp.arange(LANES)[:, None]).astype(BF16)

    def row(a):
        return a[:, None, :]

    return {
        "w_in": w_r, "g_mix": row(g_mix), "sinks": sinks,
        "conv_w": 0.5 * conv_w, "conv_b": row(0.5 * conv_b),
        "dt_bias": row(jnp.concatenate([dt_bias, pad], axis=1)), "a_log": row(jnp.concatenate([a_log, pad], axis=1)),
        "d_skip": row(jnp.repeat(d_skip, SSM_HEAD_DIM, axis=1)), "ssm_norm_w": row(ssm_norm_w),
        "gm_ln_g": row(gm_ln_g), "gm_ln_b": row(gm_ln_b),
        "gm_w_s": gm_w_s, "gm_b_st": jnp.swapaxes(gm_b_s, 1, 2),
        "w_attn_o": (0.5 * w_attn_o).astype(BF16), "w_ssm_o": (0.5 * w_ssm_o).astype(BF16),
        "w_gm_o": (0.5 * w_gm_o).astype(BF16), "w_out": w_out.astype(BF16), "expand": expand,
        "g_ff": row(g_ff), "w_ff1": w_ff1.astype(BF16), "w_ff2": w_ff2.astype(BF16),
    }


def kernel(x_prompt, x_sample, c_prompt, c_sample, cache_attn_k, cache_attn_v, state_ssm, state_conv, w_ada, b_ada, g_mix, w_in, sinks, conv_w, conv_b, dt_bias, a_log, d_skip, ssm_norm_w, gm_ln_g, gm_ln_b, gm_w_s, gm_b_s, w_attn_o, w_ssm_o, w_gm_o, w_out, g_ff, w_ff1, w_ff2, g_final):
    bp, sp, d = x_prompt.shape
    bs, ss, _ = x_sample.shape
    mod_p, mod_s = _ada_call(c_prompt, c_sample, w_ada, b_ada)
    rope_p = _rope_tables(jnp.arange(sp))
    rope_s = tuple(jnp.tile(a, (SAMPLE_NB, 1)) for a in _rope_tables(PAST_LEN + jnp.arange(ss)))
    g_fin = g_final[None]
    p = _stacked_params(w_in, g_mix, sinks, conv_w, conv_b, dt_bias, a_log, d_skip, ssm_norm_w, gm_ln_g,
                        gm_ln_b, gm_w_s, gm_b_s, w_attn_o, w_ssm_o, w_gm_o, w_out, g_ff, w_ff1, w_ff2)
    ck = jnp.transpose(cache_attn_k, (0, 1, 3, 4, 2))
    cv = jnp.transpose(cache_attn_v, (0, 1, 3, 4, 2))
    h0 = state_ssm.reshape(DEPTH, bs, SSM_INNER, SSM_STATE)
    xp, xs = x_prompt, x_sample
    state_p = state_s = None
    for l in range(DEPTH):
        final = l == DEPTH - 1
        xp, *state_p = _prompt_mixer(l, xp, mod_p, p, rope_p, state_p)
        xs, *state_s = _sample_mixer(l, xs, mod_s, ck, cv, h0, state_conv, p, rope_s, state_s)
        xp, xs = _ffn(l, xp, mod_p, xs, mod_s, p, g_fin, final)
    kp, vp, ssm_p, conv_p = state_p
    ks, vs, ssm_s, conv_s, gv_s = state_s
    return (xp, xs,
            kp.reshape(DEPTH, bp, WINDOW, N_KV_HEADS, HEAD_DIM), vp.reshape(DEPTH, bp, WINDOW, N_KV_HEADS, HEAD_DIM),
            ssm_p.reshape(DEPTH, bp, SSM_HEADS, SSM_HEAD_DIM, SSM_STATE), conv_p,
            ks.reshape(DEPTH, bs, ss, N_KV_HEADS, HEAD_DIM), vs.reshape(DEPTH, bs, ss, N_KV_HEADS, HEAD_DIM),
            ssm_s.reshape(DEPTH, bs, SSM_HEADS, SSM_HEAD_DIM, SSM_STATE), conv_s, gv_s)
```
